```python
import jax
import jax.numpy as jnp
from jax import lax
import numpy as np

D_MODEL = 1024
BATCH = 2
SEQ = 8192
DEPTH = 2

GRID_W = 64
CTX_LEN = 256
N_MIXERS = 2
N_FOURIER_LAYERS = (DEPTH + 1) // 2
N_HGRN_LAYERS = DEPTH // 2
FNET_GROUPS = 4
FNET_GROUP_DIM = D_MODEL // FNET_GROUPS
HGRN_HEADS = 8
HGRN_KEY_DIM = 128
HGRN_VALUE_DIM = D_MODEL // HGRN_HEADS
HGRN_F_DIM = HGRN_HEADS * HGRN_KEY_DIM
HGRN_CHUNK = 64
N_EXPERTS = 32
TOP_K = 4
D_FF = 1024
SWIGLU_ALPHA = 1.702
SWIGLU_LIMIT = 7.0
MOE_BLOCK = 256
NORM_EPS = 1e-6

kernel_name = 'hybrid_fnet_hgrn2_moe_dit'


def rms_norm(x, g):
    xf = x.astype(jnp.float32)
    y = xf * lax.rsqrt(jnp.mean(xf * xf, axis=-1, keepdims=True) + NORM_EPS)
    return (y * g.astype(jnp.float32)).astype(x.dtype)


def adaln(cond, w, b):
    m = jax.nn.silu(cond) @ w + b
    return jnp.split(m[..., None, :], 6, axis=-1)


def modulate(x, g, shift, scale):
    return rms_norm(x, g) * (1 + scale) + shift


def fourier_tokens(u, grid):
    bsz, length, _ = u.shape
    uf = u.astype(jnp.float32)
    if grid is None:
        ug = uf.reshape(bsz, length, FNET_GROUPS, FNET_GROUP_DIM)
        y = jnp.fft.fftn(ug, axes=(1, 3), norm='ortho').real
    else:
        rows, cols = grid
        ug = uf.reshape(bsz, rows, cols, FNET_GROUPS, FNET_GROUP_DIM)
        y = jnp.fft.fftn(ug, axes=(1, 2, 4), norm='ortho').real
    return y.reshape(bsz, length, D_MODEL).astype(u.dtype)


def fourier_mixer(h_ctx, h_lat, w_in, w_out, grid, need_ctx):
    y_lat = fourier_tokens(h_lat @ w_in, grid) @ w_out
    y_ctx = fourier_tokens(h_ctx @ w_in, None) @ w_out if need_ctx else None
    return y_ctx, y_lat


def gla_chunk_scan(q, k, v, log_f, s0):
    bsz, length, heads, _ = q.shape
    n_chunks = length // HGRN_CHUNK

    def chunks(t):
        return t.reshape(bsz, n_chunks, HGRN_CHUNK, heads, t.shape[-1]).transpose(1, 0, 3, 2, 4)

    lower_tri = jnp.tril(jnp.ones((HGRN_CHUNK, HGRN_CHUNK), dtype=bool))

    def step(s, blk):
        qc, kc, vc, ac = blk
        b = jnp.cumsum(ac, axis=2)
        b_last = b[:, :, -1:, :]
        diff = b[:, :, :, None, :] - b[:, :, None, :, :]
        decay = jnp.where(lower_tri[:, :, None], jnp.exp(jnp.minimum(diff, 0.0)), 0.0)
        scores = jnp.einsum('bhtk,bhsk,bhtsk->bhts', qc, kc, decay)
        o = (jnp.einsum('bhts,bhsv->bhtv', scores, vc)
             + jnp.einsum('bhtk,bhkv->bhtv', qc * jnp.exp(b), s))
        s_new = (jnp.exp(b_last[:, :, 0, :])[..., None] * s
                 + jnp.einsum('bhsk,bhsv->bhkv', kc * jnp.exp(b_last - b), vc))
        return s_new, o

    s_fin, o = lax.scan(step, s0, (chunks(q), chunks(k), chunks(v), chunks(log_f)))
    o = o.transpose(1, 0, 3, 2, 4).reshape(bsz, length, heads, v.shape[-1])
    return o, s_fin


def bidir_scan(q, k_fw, a_fw, k_bw, a_bw, v, s_fw, s_bw):
    o_fw, s_fw = gla_chunk_scan(q, k_fw, v, a_fw, s_fw)
    flip = lambda t: t[:, ::-1]
    o_bw, s_bw = gla_chunk_scan(flip(q), flip(k_bw), flip(v), flip(a_bw), s_bw)
    return o_fw + flip(o_bw), s_fw, s_bw


def hgrn2_mixer(h_ctx, h_lat, w_in, lb, norm_g, w_out, need_ctx):
    splits = [HGRN_F_DIM, 2 * HGRN_F_DIM, 3 * HGRN_F_DIM, 3 * HGRN_F_DIM + D_MODEL]

    def project(h):
        bsz, length, _ = h.shape
        q, f_fw, f_bw, i_in, g = jnp.split(h @ w_in, splits, axis=-1)
        heads = lambda t: t.reshape(bsz, length, HGRN_HEADS, -1)
        fg_fw = lb[0] + (1 - lb[0]) * jax.nn.sigmoid(f_fw.astype(jnp.float32))
        fg_bw = lb[1] + (1 - lb[1]) * jax.nn.sigmoid(f_bw.astype(jnp.float32))
        scan_in = (heads(jax.nn.silu(q).astype(jnp.float32)),
                   heads(1 - fg_fw), heads(jnp.log(fg_fw)),
                   heads(1 - fg_bw), heads(jnp.log(fg_bw)),
                   heads(i_in.astype(jnp.float32)))
        return scan_in, g

    def readout(o, g):
        bsz, length = o.shape[:2]
        on = (o * lax.rsqrt(jnp.mean(o * o, axis=-1, keepdims=True) + NORM_EPS)
              * norm_g.astype(jnp.float32).reshape(HGRN_HEADS, HGRN_VALUE_DIM))
        y = on.reshape(bsz, length, D_MODEL).astype(g.dtype) * jax.nn.silu(g)
        return y @ w_out

    ctx_in, g_ctx = project(h_ctx)
    lat_in, g_lat = project(h_lat)
    s0 = jnp.zeros((h_lat.shape[0], HGRN_HEADS, HGRN_KEY_DIM, HGRN_VALUE_DIM), jnp.float32)
    o_ctx, s_fw, s_bw = bidir_scan(*ctx_in, s0, s0)
    o_lat, _, _ = bidir_scan(*lat_in, s_fw, s_bw)
    y_lat = readout(o_lat, g_lat)
    y_ctx = readout(o_ctx, g_ctx) if need_ctx else None
    return y_ctx, y_lat


def moe_ffn(h, w_r, b_r, w1, b1, w2, b2):
    n_tok = h.shape[0]
    logits = (h @ w_r + b_r).astype(jnp.float32)
    top_val, top_idx = lax.top_k(logits, TOP_K)
    gate = jax.nn.softmax(top_val, axis=-1)
    n_assign = n_tok * TOP_K
    flat_e = top_idx.reshape(-1)
    order = jnp.argsort(flat_e)
    e_sorted = flat_e[order]
    tok_sorted = order // TOP_K
    gate_sorted = gate.reshape(-1)[order]
    counts = jnp.bincount(flat_e, length=N_EXPERTS)
    padded = (counts + MOE_BLOCK - 1) // MOE_BLOCK * MOE_BLOCK
    pad_end = jnp.cumsum(padded)
    pad_start = pad_end - padded
    start = jnp.cumsum(counts) - counts
    dest = pad_start[e_sorted] + jnp.arange(n_assign) - start[e_sorted]
    n_blocks = -(-n_assign // MOE_BLOCK) + N_EXPERTS
    n_rows = n_blocks * MOE_BLOCK
    row_tok = jnp.zeros((n_rows,), jnp.int32).at[dest].set(tok_sorted.astype(jnp.int32))
    row_gate = jnp.zeros((n_rows,), jnp.float32).at[dest].set(gate_sorted)
    block_exp = jnp.minimum(
        jnp.searchsorted(pad_end, jnp.arange(n_blocks) * MOE_BLOCK, side='right'), N_EXPERTS - 1)
    xb = h[row_tok].reshape(n_blocks, MOE_BLOCK, h.shape[-1])

    def expert_block(args):
        xblk, e = args
        u = xblk @ w1[e] + b1[e]
        glu, lin = jnp.split(u, 2, axis=-1)
        glu = jnp.minimum(glu, SWIGLU_LIMIT)
        lin = jnp.clip(lin, -SWIGLU_LIMIT, SWIGLU_LIMIT)
        act = glu * jax.nn.sigmoid(SWIGLU_ALPHA * glu) * (lin + 1)
        return act @ w2[e] + b2[e]

    yb = lax.map(expert_block, (xb, block_exp)).reshape(n_rows, -1)
    y = yb * row_gate[:, None].astype(yb.dtype)
    return jax.ops.segment_sum(y, row_tok, num_segments=n_tok)


def setup_inputs(seed: int = 0) -> dict:
    key = jax.random.key(seed)
    ks = jax.random.split(key, 21)
    nrm = lambda k, shape, scale: jax.random.normal(k, shape, jnp.float32) * scale
    D = D_MODEL
    return {
        'x': nrm(ks[0], (BATCH, SEQ, D), 1.0),
        'c': nrm(ks[1], (BATCH, D), 1.0),
        'ctx': nrm(ks[2], (BATCH, CTX_LEN, D), 1.0),
        'c_ctx': nrm(ks[3], (D,), 1.0),
        'mod_w': nrm(ks[4], (DEPTH, D, 6 * D), 0.5 * D ** -0.5),
        'mod_b': nrm(ks[5], (DEPTH, 6 * D), 0.02),
        'norm1_g': 1.0 + nrm(ks[6], (DEPTH, D), 0.02),
        'norm2_g': 1.0 + nrm(ks[7], (DEPTH, D), 0.02),
        'fourier_w_in': nrm(ks[8], (N_FOURIER_LAYERS, D, D), D ** -0.5),
        'fourier_w_out': nrm(ks[9], (N_FOURIER_LAYERS, D, D), D ** -0.5),
        'hgrn_w_in': nrm(ks[10], (N_HGRN_LAYERS, D, 3 * HGRN_F_DIM + 2 * D), D ** -0.5),
        'hgrn_lower_bounds': nrm(ks[11], (DEPTH, 2, HGRN_F_DIM), 0.5),
        'hgrn_norm_g': 1.0 + nrm(ks[12], (N_HGRN_LAYERS, D), 0.02),
        'hgrn_w_out': nrm(ks[13], (N_HGRN_LAYERS, D, D), D ** -0.5),
        'router_w': nrm(ks[14], (DEPTH, D, N_EXPERTS), D ** -0.5),
        'router_b': nrm(ks[15], (DEPTH, N_EXPERTS), 0.01),
        'expert_w1': nrm(ks[16], (DEPTH, N_EXPERTS, D, 2 * D_FF), D ** -0.5),
        'expert_b1': nrm(ks[17], (DEPTH, N_EXPERTS, 2 * D_FF), 0.01),
        'expert_w2': nrm(ks[18], (DEPTH, N_EXPERTS, D_FF, D), D_FF ** -0.5),
        'expert_b2': nrm(ks[19], (DEPTH, N_EXPERTS, D), 0.01),
        'final_norm_g': 1.0 + nrm(ks[20], (D,), 0.02),
    }


def reference(x, c, ctx, c_ctx, mod_w, mod_b, norm1_g, norm2_g, fourier_w_in, fourier_w_out,
              hgrn_w_in, hgrn_lower_bounds, hgrn_norm_g, hgrn_w_out, router_w, router_b,
              expert_w1, expert_b1, expert_w2, expert_b2, final_norm_g):
    lb_soft = jax.nn.softmax(hgrn_lower_bounds.astype(jnp.float32), axis=0)
    lower_bounds = jnp.cumsum(lb_soft, axis=0) - lb_soft[0]
    rows = x.shape[1] // GRID_W
    grid = (rows, GRID_W)
    x_lat, x_ctx = x, ctx
    for i in range(DEPTH):
        need_ctx = i < DEPTH - 1
        sh1, sc1, g1, sh2, sc2, g2 = adaln(c, mod_w[i], mod_b[i])
        csh1, csc1, cg1, csh2, csc2, cg2 = adaln(c_ctx, mod_w[i], mod_b[i])
        h_lat = modulate(x_lat, norm1_g[i], sh1, sc1)
        h_ctx = modulate(x_ctx, norm1_g[i], csh1, csc1)
        j = i // N_MIXERS
        if i % N_MIXERS == 0:
            y_ctx, y_lat = fourier_mixer(h_ctx, h_lat, fourier_w_in[j], fourier_w_out[j],
                                         grid, need_ctx)
        else:
            y_ctx, y_lat = hgrn2_mixer(h_ctx, h_lat, hgrn_w_in[j], lower_bounds[i],
                                       hgrn_norm_g[j], hgrn_w_out[j], need_ctx)
        x_lat = x_lat + g1 * y_lat
        h_lat = modulate(x_lat, norm2_g[i], sh2, sc2)
        moe_args = (router_w[i], router_b[i], expert_w1[i], expert_b1[i],
                    expert_w2[i], expert_b2[i])
        if need_ctx:
            x_ctx = x_ctx + cg1 * y_ctx
            h_ctx = modulate(x_ctx, norm2_g[i], csh2, csc2)
            n_ctx = x_ctx.shape[0] * x_ctx.shape[1]
            tokens = jnp.concatenate([h_ctx.reshape(-1, D_MODEL), h_lat.reshape(-1, D_MODEL)], axis=0)
            out = moe_ffn(tokens, *moe_args)
            x_ctx = x_ctx + cg2 * out[:n_ctx].reshape(x_ctx.shape)
            x_lat = x_lat + g2 * out[n_ctx:].reshape(x_lat.shape)
        else:
            out = moe_ffn(h_lat.reshape(-1, D_MODEL), *moe_args)
            x_lat = x_lat + g2 * out.reshape(x_lat.shape)
    return rms_norm(x_lat, final_norm_g)
```

```python
import functools

import numpy as np
import jax
import jax.numpy as jnp
from jax import lax
from jax.experimental import pallas as pl
from jax.experimental.pallas import tpu as pltpu

F32 = jnp.float32
BF16 = jnp.bfloat16
HIGHEST = lax.Precision.HIGHEST

D = 1024
BATCH = 2
SEQ = 8192
CTX_LEN = 256
GRID_W = 64
GRID_H = SEQ // GRID_W
NLAT = BATCH * SEQ
NCTX = BATCH * CTX_LEN
NTOK = NLAT + NCTX
TM = 256
LAT_TILES = NLAT // TM
ALL_TILES = NTOK // TM
TILES_PER_BATCH = SEQ // TM
FGROUPS = 4
FGDIM = D // FGROUPS
HEADS = 8
HDIM = D // HEADS
CHUNK = 128
N_EXPERTS = 32
TOP_K = 4
D_FF = 1024
SWIGLU_ALPHA = 1.702
SWIGLU_LIMIT = 7.0
BM = 256
EPAD = 128
NORM_EPS = 1e-6
VMEM_LIMIT = 56 * 1024 * 1024


def _cparams(sem, vmem=None, **kw):
    return pltpu.CompilerParams(dimension_semantics=sem, vmem_limit_bytes=vmem, **kw)


def _sigmoid(x):
    return 1.0 / (1.0 + jnp.exp(-x))


def _rms(x, g):
    return x * lax.rsqrt(jnp.mean(x * x, axis=-1, keepdims=True) + NORM_EPS) * g


def _tile_cond(i):
    return jnp.where(i < LAT_TILES, i // TILES_PER_BATCH, 2)


def _adaln_kernel(cond_ref, w_ref, b_ref, o_ref):
    c = cond_ref[...]
    s = c * _sigmoid(c)
    o_ref[...] = jnp.dot(s, w_ref[...], precision=HIGHEST,
                         preferred_element_type=F32) + b_ref[...]


def _adaln(cond8, mod_w, mod_b):
    depth = mod_w.shape[0]
    nb = 1536
    out = pl.pallas_call(
        _adaln_kernel,
        grid=(depth, 6 * D // nb),
        in_specs=[pl.BlockSpec((8, D), lambda l, j: (0, 0)),
                  pl.BlockSpec((None, D, nb), lambda l, j: (l, 0, j)),
                  pl.BlockSpec((None, 1, nb), lambda l, j: (l, 0, j))],
        out_specs=pl.BlockSpec((None, 8, nb), lambda l, j: (l, 0, j)),
        out_shape=jax.ShapeDtypeStruct((depth, 8, 6 * D), F32),
        compiler_params=_cparams(("arbitrary", "arbitrary"), VMEM_LIMIT),
        name="adaln",
    )(cond8, mod_w, mod_b.reshape(depth, 1, 6 * D))
    return out.reshape(depth * 8, 6, D)


def _dft_cs(n):
    k = np.arange(n)
    ang = 2.0 * np.pi * np.outer(k, k) / n
    s = 1.0 / np.sqrt(n)
    return np.cos(ang) * s, np.sin(ang) * s


def _fourier_constants():
    cd, sd = _dft_cs(FGDIM)
    dch = np.concatenate([cd, sd], axis=1)
    cc, sc = _dft_cs(GRID_W)
    eye = np.eye(TM // GRID_W)
    kc, ks = np.kron(eye, cc), np.kron(eye, sc)
    m_lat = np.block([[kc, -ks], [ks, kc]])
    cp, sp = _dft_cs(CTX_LEN)
    m_ctx = np.block([[cp, -sp], [sp, cp]])
    mpos = np.stack([m_lat, m_ctx])
    cr, sr = _dft_cs(GRID_H)
    mrow = np.concatenate([cr, -sr], axis=1)
    return (jnp.asarray(dch, BF16), jnp.asarray(mpos, BF16), jnp.asarray(mrow, BF16))


def _fourier_in_kernel(x_ref, mod_ref, g_ref, win_ref, dch_ref, mpos_ref, vr_ref, vi_ref):
    x = x_ref[...]
    h = _rms(x, g_ref[...]) * (1.0 + mod_ref[1:2, :]) + mod_ref[0:1, :]
    u = jnp.dot(h.astype(BF16), win_ref[...], preferred_element_type=F32).astype(BF16)
    parts = [jnp.dot(u[:, g * FGDIM:(g + 1) * FGDIM], dch_ref[...],
                     preferred_element_type=F32) for g in range(FGROUPS)]
    uc = jnp.concatenate([p[:, :FGDIM] for p in parts], axis=1)
    us = jnp.concatenate([p[:, FGDIM:] for p in parts], axis=1)
    st = jnp.concatenate([uc, us], axis=0).astype(BF16)
    v = jnp.dot(mpos_ref[...], st, preferred_element_type=F32)
    vr_ref[...] = v[:TM]
    vi_ref[...] = v[TM:]


def _fourier_in(x_all, mods, layer, norm_g, w_in, dch, mpos):
    tile = pl.BlockSpec((TM, D), lambda i: (i, 0))
    return pl.pallas_call(
        _fourier_in_kernel,
        grid=(ALL_TILES,),
        in_specs=[tile,
                  pl.BlockSpec((None, 6, D), lambda i: (layer * 8 + _tile_cond(i), 0, 0)),
                  pl.BlockSpec((1, D), lambda i: (0, 0)),
                  pl.BlockSpec((D, D), lambda i: (0, 0)),
                  pl.BlockSpec((FGDIM, 2 * FGDIM), lambda i: (0, 0)),
                  pl.BlockSpec((None, 2 * TM, 2 * TM), lambda i: (jnp.where(i < LAT_TILES, 0, 1), 0, 0))],
        out_specs=[tile, tile],
        out_shape=[jax.ShapeDtypeStruct((NTOK, D), F32)] * 2,
        compiler_params=_cparams(("arbitrary",), VMEM_LIMIT),
        name="fourier_in",
    )(x_all, mods, norm_g.reshape(1, D), w_in.astype(BF16), dch, mpos)


CB = 8


def _fourier_out_lat_kernel(vr_ref, vi_ref, x_ref, mrow_ref, wout_ref, mod_ref, o_ref):
    g1 = mod_ref[2:3, :]
    for c in range(CB):
        st = jnp.concatenate([vr_ref[:, c, :], vi_ref[:, c, :]], axis=0).astype(BF16)
        yf = jnp.dot(mrow_ref[...], st, preferred_element_type=F32)
        y = jnp.dot(yf.astype(BF16), wout_ref[...], preferred_element_type=F32)
        o_ref[:, c, :] = x_ref[:, c, :] + g1 * y


def _fourier_out_ctx_kernel(yr_ref, x_ref, wout_ref, mod_ref, o_ref):
    y = jnp.dot(yr_ref[...].astype(BF16), wout_ref[...], preferred_element_type=F32)
    o_ref[...] = x_ref[...] + mod_ref[2:3, :] * y


def _fourier_out(vr, vi, x_all, mods, layer, w_out, mrow):
    wout = w_out.astype(BF16)
    rows = NTOK // GRID_W
    v3 = lambda a: a.reshape(rows, GRID_W, D)
    blk = pl.BlockSpec((GRID_H, CB, D), lambda b, c: (b, c, 0))
    x_new = pl.pallas_call(
        _fourier_out_lat_kernel,
        grid=(BATCH, GRID_W // CB),
        in_specs=[blk, blk, blk,
                  pl.BlockSpec((GRID_H, 2 * GRID_H), lambda b, c: (0, 0)),
                  pl.BlockSpec((D, D), lambda b, c: (0, 0)),
                  pl.BlockSpec((None, 6, D), lambda b, c: (layer * 8 + b, 0, 0))],
        out_specs=blk,
        out_shape=jax.ShapeDtypeStruct((rows, GRID_W, D), F32),
        input_output_aliases={2: 0},
        compiler_params=_cparams(("arbitrary", "arbitrary"), VMEM_LIMIT),
        name="fourier_out_lat",
    )(v3(vr), v3(vi), v3(x_all), mrow, wout, mods).reshape(NTOK, D)
    ctile = pl.BlockSpec((TM, D), lambda i: (LAT_TILES + i, 0))
    return pl.pallas_call(
        _fourier_out_ctx_kernel,
        grid=(NCTX // TM,),
        in_specs=[ctile, ctile,
                  pl.BlockSpec((D, D), lambda i: (0, 0)),
                  pl.BlockSpec((None, 6, D), lambda i: (layer * 8 + 2, 0, 0))],
        out_specs=ctile,
        out_shape=jax.ShapeDtypeStruct((NTOK, D), F32),
        input_output_aliases={1: 0},
        compiler_params=_cparams(("arbitrary",), VMEM_LIMIT),
        name="fourier_out_ctx",
    )(vr, x_new, wout, mods)


def _hgrn_in_kernel(x_ref, mod_ref, g_ref, win_ref, hlb_ref, q_ref, ff_ref, fb_ref, v_ref, gs_ref,
                    *, layer):
    x = x_ref[...]
    h = (_rms(x, g_ref[...]) * (1.0 + mod_ref[1:2, :]) + mod_ref[0:1, :]).astype(BF16)
    raw = [hlb_ref[l] for l in range(hlb_ref.shape[0])]
    mx = functools.reduce(jnp.maximum, raw)
    ex = [jnp.exp(r - mx) for r in raw]
    den = functools.reduce(lambda a, b: a + b, ex)
    soft = [e / den for e in ex]
    lb = functools.reduce(lambda a, b: a + b, soft[:layer + 1]) - soft[0]

    def proj(j):
        return jnp.dot(h, win_ref[:, j * D:(j + 1) * D], preferred_element_type=F32)

    q = proj(0)
    q_ref[...] = q * _sigmoid(q)
    ff_ref[...] = lb[0:1, :] + (1.0 - lb[0:1, :]) * _sigmoid(proj(1))
    fb_ref[...] = lb[1:2, :] + (1.0 - lb[1:2, :]) * _sigmoid(proj(2))
    v_ref[...] = proj(3)
    g = proj(4)
    gs_ref[...] = g * _sigmoid(g)


def _hgrn_in(x_all, mods, layer, norm_g, w_in, hlb):
    tile = pl.BlockSpec((TM, D), lambda i: (i, 0))
    depth = hlb.shape[0]
    return pl.pallas_call(
        functools.partial(_hgrn_in_kernel, layer=layer),
        grid=(ALL_TILES,),
        in_specs=[tile,
                  pl.BlockSpec((None, 6, D), lambda i: (layer * 8 + _tile_cond(i), 0, 0)),
                  pl.BlockSpec((1, D), lambda i: (0, 0)),
                  pl.BlockSpec((D, 5 * D), lambda i: (0, 0)),
                  pl.BlockSpec((depth, 2, D), lambda i: (0, 0, 0))],
        out_specs=[tile] * 5,
        out_shape=[jax.ShapeDtypeStruct((NTOK, D), F32)] * 5,
        compiler_params=_cparams(("arbitrary",), VMEM_LIMIT),
        name="hgrn_in",
    )(x_all, mods, norm_g.reshape(1, D), w_in.astype(BF16), hlb)


N_LEVELS = 7
_NT = (((1,), (1,)), ((), ()))
_TN = (((0,), (0,)), ((), ()))


def _scan_kernel(q_ref, fg_ref, v_ref, o_ref, s_ref, *, rev):
    @pl.when(pl.program_id(1) == 0)
    def _():
        s_ref[...] = jnp.zeros_like(s_ref)

    q = q_ref[...]
    fg = fg_ref[...]
    k = 1.0 - fg
    vb = v_ref[...].astype(BF16)
    t = lax.broadcasted_iota(jnp.int32, (CHUNK, D), 0)
    ti = lax.broadcasted_iota(jnp.int32, (CHUNK, CHUNK), 0)
    si = lax.broadcasted_iota(jnp.int32, (CHUNK, CHUNK), 1)
    scores = [None] * HEADS

    def add_level(qs, ks, mask):
        qb, kb = qs.astype(BF16), ks.astype(BF16)
        for h in range(HEADS):
            sl = slice(h * HDIM, (h + 1) * HDIM)
            sc = lax.dot_general(qb[:, sl], kb[:, sl], _NT, preferred_element_type=F32)
            sc = jnp.where(mask, sc, 0.0)
            scores[h] = sc if scores[h] is None else scores[h] + sc

    add_level(q, k, ti == si)
    run = fg
    rest = jnp.ones_like(fg)
    tot = fg
    for l in range(N_LEVELS):
        hbit = 1 << l
        odd = (t & hbit) != 0
        far = (t & hbit) == 0 if rev else odd
        add_level(jnp.where(far, run, 0.0) * q, jnp.where(far, 0.0, rest) * k,
                  (ti >> (l + 1)) == (si >> (l + 1)))
        sib = jnp.where(odd, pltpu.roll(tot, hbit, 0), pltpu.roll(tot, CHUNK - hbit, 0))
        run = jnp.where(far, run * sib, run)
        rest = jnp.where(far, rest, rest * sib)
        tot = tot * sib
    qin = (q * run).astype(BF16)
    kst = (k * rest).astype(BF16)
    for h in range(HEADS):
        sl = slice(h * HDIM, (h + 1) * HDIM)
        st = s_ref[h]
        o_ref[:, sl] = (jnp.dot(scores[h].astype(BF16), vb[:, sl], preferred_element_type=F32)
                        + lax.dot_general(qin[:, sl], st.astype(BF16), _NT,
                                          preferred_element_type=F32))
        s_ref[h] = st * tot[0:1, sl] + lax.dot_general(vb[:, sl], kst[:, sl], _TN,
                                                       preferred_element_type=F32)


LAT_CHUNKS = SEQ // CHUNK
CTX_CHUNKS = CTX_LEN // CHUNK
SCAN_STEPS = CTX_CHUNKS + LAT_CHUNKS


def _scan(q, fg, v, rev):
    def idx(b, s):
        if rev:
            c = jnp.where(s < CTX_CHUNKS, NLAT // CHUNK + CTX_CHUNKS * b + (CTX_CHUNKS - 1 - s),
                          LAT_CHUNKS * b + (SCAN_STEPS - 1 - s))
        else:
            c = jnp.where(s < CTX_CHUNKS, NLAT // CHUNK + CTX_CHUNKS * b + s,
                          LAT_CHUNKS * b + (s - CTX_CHUNKS))
        return (c, 0)
    blk = pl.BlockSpec((CHUNK, D), idx)
    return pl.pallas_call(
        functools.partial(_scan_kernel, rev=rev),
        grid=(BATCH, SCAN_STEPS),
        in_specs=[blk, blk, blk],
        out_specs=blk,
        out_shape=jax.ShapeDtypeStruct((NTOK, D), F32),
        scratch_shapes=[pltpu.VMEM((HEADS, HDIM, HDIM), F32)],
        compiler_params=_cparams(("arbitrary", "arbitrary"), VMEM_LIMIT),
        name="scan_bw" if rev else "scan_fw",
    )(q, fg, v)


def _hgrn_out_kernel(of_ref, ob_ref, gs_ref, x_ref, ng_ref, wout_ref, mod_ref, o_ref):
    o = of_ref[...] + ob_ref[...]
    parts = []
    for h in range(HEADS):
        oh = o[:, h * HDIM:(h + 1) * HDIM]
        parts.append(oh * lax.rsqrt(jnp.mean(oh * oh, axis=-1, keepdims=True) + NORM_EPS))
    on = jnp.concatenate(parts, axis=1) * ng_ref[...]
    y = jnp.dot((on * gs_ref[...]).astype(BF16), wout_ref[...], preferred_element_type=F32)
    o_ref[...] = x_ref[...] + mod_ref[2:3, :] * y


def _hgrn_out(o_fw, o_bw, gs, x_all, mods, layer, norm_g, w_out, n_tiles):
    tile = pl.BlockSpec((TM, D), lambda i: (i, 0))
    return pl.pallas_call(
        _hgrn_out_kernel,
        grid=(n_tiles,),
        in_specs=[tile, tile, tile, tile,
                  pl.BlockSpec((1, D), lambda i: (0, 0)),
                  pl.BlockSpec((D, D), lambda i: (0, 0)),
                  pl.BlockSpec((None, 6, D), lambda i: (layer * 8 + _tile_cond(i), 0, 0))],
        out_specs=tile,
        out_shape=jax.ShapeDtypeStruct((n_tiles * TM, D), F32),
        compiler_params=_cparams(("arbitrary",), VMEM_LIMIT),
        name="hgrn_out",
    )(o_fw, o_bw, gs, x_all, norm_g.reshape(1, D), w_out.astype(BF16), mods)


def _route_kernel(x_ref, mod_ref, g_ref, wr_ref, br_ref, h_ref, e_ref, rank_ref, gate_ref, cnt_ref):
    @pl.when(pl.program_id(0) == 0)
    def _():
        cnt_ref[...] = jnp.zeros_like(cnt_ref)

    h = _rms(x_ref[...], g_ref[...]) * (1.0 + mod_ref[4:5, :]) + mod_ref[3:4, :]
    h_ref[...] = h
    logits = jnp.dot(h, wr_ref[...], precision=HIGHEST, preferred_element_type=F32) + br_ref[...]
    lane = lax.broadcasted_iota(jnp.int32, (TM, EPAD), 1).astype(F32)
    vals = logits
    sel = jnp.zeros((TM, EPAD), F32)
    tops, idxs = [], []
    for _ in range(TOP_K):
        m = jnp.max(vals, axis=1, keepdims=True)
        idx = jnp.min(jnp.where(vals == m, lane, float(EPAD)), axis=1, keepdims=True)
        hit = lane == idx
        vals = jnp.where(hit, -jnp.inf, vals)
        sel = jnp.where(hit, 1.0, sel)
        tops.append(m)
        idxs.append(idx)
    ex = [jnp.exp(m - tops[0]) for m in tops]
    den = ex[0] + ex[1] + ex[2] + ex[3]
    r = lax.broadcasted_iota(jnp.int32, (TM, TM), 0)
    c = lax.broadcasted_iota(jnp.int32, (TM, TM), 1)
    before = jnp.where(r > c, 1.0, 0.0).astype(BF16)
    pref = jnp.dot(before, sel.astype(BF16), preferred_element_type=F32) + cnt_ref[...]
    e_out = jnp.zeros((TM, EPAD), F32)
    rank_out = jnp.zeros((TM, EPAD), F32)
    gate_out = jnp.zeros((TM, EPAD), F32)
    for kk in range(TOP_K):
        rank = jnp.sum(jnp.where(lane == idxs[kk], pref, 0.0), axis=1, keepdims=True)
        slot = lane == float(kk)
        e_out = jnp.where(slot, idxs[kk], e_out)
        rank_out = jnp.where(slot, rank, rank_out)
        gate_out = jnp.where(slot, ex[kk] / den, gate_out)
    e_ref[...] = e_out.astype(jnp.int32)
    rank_ref[...] = rank_out.astype(jnp.int32)
    gate_ref[...] = gate_out
    cnt_ref[...] += jnp.sum(sel, axis=0, keepdims=True)


def _route(x_all, mods, layer, norm_g, w_r, b_r, n_tiles):
    tile = pl.BlockSpec((TM, D), lambda i: (i, 0))
    small = pl.BlockSpec((TM, EPAD), lambda i: (i, 0))
    wr = jnp.zeros((D, EPAD), F32).at[:, :N_EXPERTS].set(w_r)
    br = jnp.full((1, EPAD), -1e30, F32).at[0, :N_EXPERTS].set(b_r)
    n = n_tiles * TM
    return pl.pallas_call(
        _route_kernel,
        grid=(n_tiles,),
        in_specs=[tile,
                  pl.BlockSpec((None, 6, D), lambda i: (layer * 8 + _tile_cond(i), 0, 0)),
                  pl.BlockSpec((1, D), lambda i: (0, 0)),
                  pl.BlockSpec((D, EPAD), lambda i: (0, 0)),
                  pl.BlockSpec((1, EPAD), lambda i: (0, 0))],
        out_specs=[tile, small, small, small, pl.BlockSpec((1, EPAD), lambda i: (0, 0))],
        out_shape=[jax.ShapeDtypeStruct((n, D), F32),
                   jax.ShapeDtypeStruct((n, EPAD), jnp.int32),
                   jax.ShapeDtypeStruct((n, EPAD), jnp.int32),
                   jax.ShapeDtypeStruct((n, EPAD), F32),
                   jax.ShapeDtypeStruct((1, EPAD), F32)],
        compiler_params=_cparams(("arbitrary",), VMEM_LIMIT),
        name="route",
    )(x_all, mods, norm_g.reshape(1, D), wr, br)


def _row_copy(src_hbm, row, dst, slot, r, sem):
    return pltpu.make_async_copy(src_hbm.at[pl.ds(row, 1), :], dst.at[slot, pl.ds(r, 1), :],
                                 sem.at[slot])


def _expert_kernel(bexp_ref, nused_ref, tok_ref, tok_next_ref, h_hbm, w1_ref, b1_ref, w2_ref, b2_ref,
                   gate_ref, y_ref, xb, w1b, w2b, sem):
    i = pl.program_id(0)
    n_used = nused_ref[0]
    slot = i % 2

    def gather(tok, to_slot):
        def body(r, carry):
            _row_copy(h_hbm, tok[0, 0, r], xb, to_slot, r, sem).start()
            return carry
        lax.fori_loop(0, BM, body, 0)

    @pl.when(i == 0)
    def _():
        gather(tok_ref, 0)

    @pl.when(i + 1 < n_used)
    def _():
        gather(tok_next_ref, 1 - slot)

    @pl.when(jnp.logical_or(i == 0, bexp_ref[i] != bexp_ref[jnp.maximum(i - 1, 0)]))
    def _():
        w1b[...] = w1_ref[...].astype(BF16)
        w2b[...] = w2_ref[...].astype(BF16)

    @pl.when(i < n_used)
    def _():
        pltpu.make_async_copy(h_hbm.at[pl.ds(0, BM), :], xb.at[slot], sem.at[slot]).wait()
        x = xb[slot].astype(BF16)
        u = jnp.dot(x, w1b[...], preferred_element_type=F32) + b1_ref[...]
        glu = jnp.minimum(u[:, :D_FF], SWIGLU_LIMIT)
        lin = jnp.clip(u[:, D_FF:], -SWIGLU_LIMIT, SWIGLU_LIMIT)
        act = glu * _sigmoid(SWIGLU_ALPHA * glu) * (lin + 1.0)
        y = jnp.dot(act.astype(BF16), w2b[...], preferred_element_type=F32) + b2_ref[...]
        y_ref[...] = y * gate_ref[...]

    @pl.when(i >= n_used)
    def _():
        y_ref[...] = jnp.zeros_like(y_ref)


def _experts(h, row_tok, row_gate, block_exp, n_used, w1, b1, w2, b2):
    n_blocks = row_tok.shape[0] // BM
    tok3 = row_tok.reshape(n_blocks, 1, BM)
    last = n_blocks - 1
    grid_spec = pltpu.PrefetchScalarGridSpec(
        num_scalar_prefetch=2,
        grid=(n_blocks,),
        in_specs=[
            pl.BlockSpec((1, 1, BM), lambda i, be, nu: (i, 0, 0), memory_space=pltpu.SMEM),
            pl.BlockSpec((1, 1, BM), lambda i, be, nu: (jnp.minimum(i + 1, last), 0, 0),
                         memory_space=pltpu.SMEM),
            pl.BlockSpec(memory_space=pl.ANY),
            pl.BlockSpec((None, D, 2 * D_FF), lambda i, be, nu: (be[i], 0, 0)),
            pl.BlockSpec((None, 1, 2 * D_FF), lambda i, be, nu: (be[i], 0, 0)),
            pl.BlockSpec((None, D_FF, D), lambda i, be, nu: (be[i], 0, 0)),
            pl.BlockSpec((None, 1, D), lambda i, be, nu: (be[i], 0, 0)),
            pl.BlockSpec((BM, 1), lambda i, be, nu: (i, 0)),
        ],
        out_specs=pl.BlockSpec((BM, D), lambda i, be, nu: (i, 0)),
        scratch_shapes=[pltpu.VMEM((2, BM, D), F32),
                        pltpu.VMEM((D, 2 * D_FF), BF16),
                        pltpu.VMEM((D_FF, D), BF16),
                        pltpu.SemaphoreType.DMA((2,))],
    )
    return pl.pallas_call(
        _expert_kernel,
        grid_spec=grid_spec,
        out_shape=jax.ShapeDtypeStruct((n_blocks * BM, D), F32),
        compiler_params=_cparams(("arbitrary",), VMEM_LIMIT, disable_bounds_checks=True),
        name="experts",
    )(block_exp, n_used, tok3, tok3, h, w1, b1.reshape(N_EXPERTS, 1, 2 * D_FF),
      w2, b2.reshape(N_EXPERTS, 1, D), row_gate.reshape(-1, 1))


def _combine_kernel(dest_ref, dest_next_ref, y_hbm, x_ref, mod_ref, fg_ref, o_ref, buf, sem,
                    *, final_norm):
    i = pl.program_id(0)
    n = pl.num_programs(0)
    slot = i % 2
    rows = TOP_K * TM

    def gather(dest, to_slot):
        def body(r, carry):
            _row_copy(y_hbm, dest[0, 0, r], buf, to_slot, r, sem).start()
            return carry
        lax.fori_loop(0, rows, body, 0)

    @pl.when(i == 0)
    def _():
        gather(dest_ref, 0)

    @pl.when(i + 1 < n)
    def _():
        gather(dest_next_ref, 1 - slot)

    pltpu.make_async_copy(y_hbm.at[pl.ds(0, rows), :], buf.at[slot], sem.at[slot]).wait()
    acc = buf[slot, 0:TM, :]
    for kk in range(1, TOP_K):
        acc = acc + buf[slot, kk * TM:(kk + 1) * TM, :]
    out = x_ref[...] + mod_ref[5:6, :] * acc
    if final_norm:
        out = _rms(out, fg_ref[...])
    o_ref[...] = out


def _combine(yb, dest, x_all, mods, layer, final_g, n_tiles, final_norm):
    dest3 = dest.reshape(n_tiles, TM, TOP_K).transpose(0, 2, 1).reshape(n_tiles, 1, TOP_K * TM)
    last = n_tiles - 1
    tile = pl.BlockSpec((TM, D), lambda i: (i, 0))
    return pl.pallas_call(
        functools.partial(_combine_kernel, final_norm=final_norm),
        grid=(n_tiles,),
        in_specs=[pl.BlockSpec((1, 1, TOP_K * TM), lambda i: (i, 0, 0), memory_space=pltpu.SMEM),
                  pl.BlockSpec((1, 1, TOP_K * TM), lambda i: (jnp.minimum(i + 1, last), 0, 0),
                               memory_space=pltpu.SMEM),
                  pl.BlockSpec(memory_space=pl.ANY),
                  tile,
                  pl.BlockSpec((None, 6, D), lambda i: (layer * 8 + _tile_cond(i), 0, 0)),
                  pl.BlockSpec((1, D), lambda i: (0, 0))],
        out_specs=tile,
        out_shape=jax.ShapeDtypeStruct((n_tiles * TM, D), F32),
        scratch_shapes=[pltpu.VMEM((2, TOP_K * TM, D), F32),
                        pltpu.SemaphoreType.DMA((2,))],
        compiler_params=_cparams(("arbitrary",), VMEM_LIMIT, disable_bounds_checks=True),
        name="combine",
    )(dest3, dest3, yb, x_all, mods, final_g.reshape(1, D))


def _moe(x_all, mods, layer, norm_g, w_r, b_r, w1, b1, w2, b2, final_g, n_tiles, final_norm):
    n = n_tiles * TM
    h, e128, rank128, gate128, cnt = _route(x_all, mods, layer, norm_g, w_r, b_r, n_tiles)
    e, rank, gate = e128[:, :TOP_K], rank128[:, :TOP_K], gate128[:, :TOP_K]
    counts = cnt[0, :N_EXPERTS].astype(jnp.int32)
    padded = (counts + BM - 1) // BM * BM
    pad_end = jnp.cumsum(padded)
    pad_start = pad_end - padded
    dest = pad_start[e] + rank
    n_blocks = -(-(n * TOP_K) // BM) + N_EXPERTS
    n_rows = n_blocks * BM
    tok = jnp.broadcast_to(jnp.arange(n, dtype=jnp.int32)[:, None], (n, TOP_K))
    row_tok = jnp.zeros((n_rows,), jnp.int32).at[dest.reshape(-1)].set(tok.reshape(-1))
    row_gate = jnp.zeros((n_rows,), F32).at[dest.reshape(-1)].set(gate.reshape(-1))
    block_exp = jnp.minimum(
        jnp.searchsorted(pad_end, jnp.arange(n_blocks, dtype=jnp.int32) * BM, side='right'),
        N_EXPERTS - 1).astype(jnp.int32)
    n_used = (pad_end[-1:] // BM).astype(jnp.int32)
    yb = _experts(h, row_tok, row_gate, block_exp, n_used, w1, b1, w2, b2)
    return _combine(yb, dest, x_all, mods, layer, final_g, n_tiles, final_norm)


def kernel(x, c, ctx, c_ctx, mod_w, mod_b, norm1_g, norm2_g, fourier_w_in, fourier_w_out,
           hgrn_w_in, hgrn_lower_bounds, hgrn_norm_g, hgrn_w_out, router_w, router_b,
           expert_w1, expert_b1, expert_w2, expert_b2, final_norm_g):
    assert x.shape == (BATCH, SEQ, D) and ctx.shape == (BATCH, CTX_LEN, D)
    cond8 = jnp.zeros((8, D), F32).at[:BATCH].set(c).at[BATCH].set(c_ctx)
    mods = _adaln(cond8, mod_w, mod_b)
    x_all = jnp.concatenate([x.reshape(NLAT, D), ctx.reshape(NCTX, D)], axis=0)
    dch, mpos, mrow = _fourier_constants()

    vr, vi = _fourier_in(x_all, mods, 0, norm1_g[0], fourier_w_in[0], dch, mpos)
    x_all = _fourier_out(vr, vi, x_all, mods, 0, fourier_w_out[0], mrow)
    x_all = _moe(x_all, mods, 0, norm2_g[0], router_w[0], router_b[0], expert_w1[0], expert_b1[0],
                 expert_w2[0], expert_b2[0], final_norm_g, ALL_TILES, False)

    q, ff, fb, v, gs = _hgrn_in(x_all, mods, 1, norm1_g[1], hgrn_w_in[0], hgrn_lower_bounds)
    o_fw = _scan(q, ff, v, False)
    o_bw = _scan(q, fb, v, True)
    x_lat = _hgrn_out(o_fw, o_bw, gs, x_all, mods, 1, hgrn_norm_g[0], hgrn_w_out[0], LAT_TILES)
    out = _moe(x_lat, mods, 1, norm2_g[1], router_w[1], router_b[1], expert_w1[1], expert_b1[1],
               expert_w2[1], expert_b2[1], final_norm_g, LAT_TILES, True)
    return out.reshape(BATCH, SEQ, D)
```

```python
import functools

import numpy as np
import jax
import jax.numpy as jnp
from jax import lax
from jax.experimental import pallas as pl
from jax.experimental.pallas import tpu as pltpu

F32 = jnp.float32
BF16 = jnp.bfloat16
HIGHEST = lax.Precision.HIGHEST

D = 1024
BATCH = 2
SEQ = 8192
CTX_LEN = 256
GRID_W = 64
GRID_H = SEQ // GRID_W
NLAT = BATCH * SEQ
NCTX = BATCH * CTX_LEN
NTOK = NLAT + NCTX
TM = 256
LAT_TILES = NLAT // TM
ALL_TILES = NTOK // TM
TILES_PER_BATCH = SEQ // TM
FGROUPS = 4
FGDIM = D // FGROUPS
HEADS = 8
HDIM = D // HEADS
CHUNK = 128
N_EXPERTS = 32
TOP_K = 4
D_FF = 1024
SWIGLU_ALPHA = 1.702
SWIGLU_LIMIT = 7.0
BM = 256
LANES = 128
RT = D // LANES
NORM_EPS = 1e-6
VMEM_LIMIT = 56 * 1024 * 1024

_NT = (((1,), (1,)), ((), ()))
_TN = (((0,), (0,)), ((), ()))


def _cparams(sem, vmem=None, **kw):
    return pltpu.CompilerParams(dimension_semantics=sem, vmem_limit_bytes=vmem, **kw)


def _sigmoid(x):
    return 1.0 / (1.0 + jnp.exp(-x))


def _rms(x, g):
    return x * lax.rsqrt(jnp.mean(x * x, axis=-1, keepdims=True) + NORM_EPS) * g


def _tile_cond(i):
    return jnp.where(i < LAT_TILES, i // TILES_PER_BATCH, 2)


def _adaln_kernel(cond_ref, w_ref, b_ref, o_ref):
    c = cond_ref[...]
    s = c * _sigmoid(c)
    o_ref[...] = jnp.dot(s, w_ref[...], precision=HIGHEST,
                         preferred_element_type=F32) + b_ref[...]


def _adaln(cond8, mod_w, mod_b):
    depth = mod_w.shape[0]
    nb = 1536
    out = pl.pallas_call(
        _adaln_kernel,
        grid=(depth, 6 * D // nb),
        in_specs=[pl.BlockSpec((8, D), lambda l, j: (0, 0)),
                  pl.BlockSpec((None, D, nb), lambda l, j: (l, 0, j)),
                  pl.BlockSpec((None, 1, nb), lambda l, j: (l, 0, j))],
        out_specs=pl.BlockSpec((None, 8, nb), lambda l, j: (l, 0, j)),
        out_shape=jax.ShapeDtypeStruct((depth, 8, 6 * D), F32),
        compiler_params=_cparams(("arbitrary", "arbitrary"), VMEM_LIMIT),
        name="adaln",
    )(cond8, mod_w, mod_b.reshape(depth, 1, 6 * D))
    return out.reshape(depth * 8, 6, D)


def _dft_cs(n):
    k = np.arange(n)
    ang = 2.0 * np.pi * np.outer(k, k) / n
    s = 1.0 / np.sqrt(n)
    return np.cos(ang) * s, np.sin(ang) * s


def _fourier_constants():
    cd, sd = _dft_cs(FGDIM)
    dch = np.concatenate([cd, sd], axis=1)
    cc, sc = _dft_cs(GRID_W)
    eye = np.eye(TM // GRID_W)
    kc, ks = np.kron(eye, cc), np.kron(eye, sc)
    m_lat = np.block([[kc, -ks], [ks, kc]])
    cp, sp = _dft_cs(CTX_LEN)
    m_ctx = np.block([[cp, -sp], [sp, cp]])
    mpos = np.stack([m_lat, m_ctx])
    cr, sr = _dft_cs(GRID_H)
    mrow = np.concatenate([cr, -sr], axis=1)
    return (jnp.asarray(dch, BF16), jnp.asarray(mpos, BF16), jnp.asarray(mrow, BF16))


def _fourier_in_kernel(x_ref, mod_ref, g_ref, win_ref, dch_ref, mpos_ref, vr_ref, vi_ref):
    x = x_ref[...]
    h = _rms(x, g_ref[...]) * (1.0 + mod_ref[1:2, :]) + mod_ref[0:1, :]
    u = jnp.dot(h.astype(BF16), win_ref[...], preferred_element_type=F32).astype(BF16)
    parts = [jnp.dot(u[:, g * FGDIM:(g + 1) * FGDIM], dch_ref[...],
                     preferred_element_type=F32) for g in range(FGROUPS)]
    uc = jnp.concatenate([p[:, :FGDIM] for p in parts], axis=1)
    us = jnp.concatenate([p[:, FGDIM:] for p in parts], axis=1)
    st = jnp.concatenate([uc, us], axis=0).astype(BF16)
    v = jnp.dot(mpos_ref[...], st, preferred_element_type=F32)
    vr_ref[...] = v[:TM]
    vi_ref[...] = v[TM:]


def _fourier_in(x_all, mods, layer, norm_g, w_in, dch, mpos):
    tile = pl.BlockSpec((TM, D), lambda i: (i, 0))
    return pl.pallas_call(
        _fourier_in_kernel,
        grid=(ALL_TILES,),
        in_specs=[tile,
                  pl.BlockSpec((None, 6, D), lambda i: (layer * 8 + _tile_cond(i), 0, 0)),
                  pl.BlockSpec((1, D), lambda i: (0, 0)),
                  pl.BlockSpec((D, D), lambda i: (0, 0)),
                  pl.BlockSpec((FGDIM, 2 * FGDIM), lambda i: (0, 0)),
                  pl.BlockSpec((None, 2 * TM, 2 * TM), lambda i: (jnp.where(i < LAT_TILES, 0, 1), 0, 0))],
        out_specs=[tile, tile],
        out_shape=[jax.ShapeDtypeStruct((NTOK, D), F32)] * 2,
        compiler_params=_cparams(("arbitrary",), VMEM_LIMIT),
        name="fourier_in",
    )(x_all, mods, norm_g.reshape(1, D), w_in.astype(BF16), dch, mpos)


CB = 8


def _fourier_out_lat_kernel(vr_ref, vi_ref, x_ref, mrow_ref, wout_ref, mod_ref, o_ref):
    g1 = mod_ref[2:3, :]
    for c in range(CB):
        st = jnp.concatenate([vr_ref[:, c, :], vi_ref[:, c, :]], axis=0).astype(BF16)
        yf = jnp.dot(mrow_ref[...], st, preferred_element_type=F32)
        y = jnp.dot(yf.astype(BF16), wout_ref[...], preferred_element_type=F32)
        o_ref[:, c, :] = x_ref[:, c, :] + g1 * y


def _fourier_out_ctx_kernel(yr_ref, x_ref, wout_ref, mod_ref, o_ref):
    y = jnp.dot(yr_ref[...].astype(BF16), wout_ref[...], preferred_element_type=F32)
    o_ref[...] = x_ref[...] + mod_ref[2:3, :] * y


def _fourier_out(vr, vi, x_all, mods, layer, w_out, mrow):
    wout = w_out.astype(BF16)
    rows = NTOK // GRID_W
    v3 = lambda a: a.reshape(rows, GRID_W, D)
    blk = pl.BlockSpec((GRID_H, CB, D), lambda b, c: (b, c, 0))
    x_new = pl.pallas_call(
        _fourier_out_lat_kernel,
        grid=(BATCH, GRID_W // CB),
        in_specs=[blk, blk, blk,
                  pl.BlockSpec((GRID_H, 2 * GRID_H), lambda b, c: (0, 0)),
                  pl.BlockSpec((D, D), lambda b, c: (0, 0)),
                  pl.BlockSpec((None, 6, D), lambda b, c: (layer * 8 + b, 0, 0))],
        out_specs=blk,
        out_shape=jax.ShapeDtypeStruct((rows, GRID_W, D), F32),
        input_output_aliases={2: 0},
        compiler_params=_cparams(("arbitrary", "arbitrary"), VMEM_LIMIT),
        name="fourier_out_lat",
    )(v3(vr), v3(vi), v3(x_all), mrow, wout, mods).reshape(NTOK, D)
    ctile = pl.BlockSpec((TM, D), lambda i: (LAT_TILES + i, 0))
    return pl.pallas_call(
        _fourier_out_ctx_kernel,
        grid=(NCTX // TM,),
        in_specs=[ctile, ctile,
                  pl.BlockSpec((D, D), lambda i: (0, 0)),
                  pl.BlockSpec((None, 6, D), lambda i: (layer * 8 + 2, 0, 0))],
        out_specs=ctile,
        out_shape=jax.ShapeDtypeStruct((NTOK, D), F32),
        input_output_aliases={1: 0},
        compiler_params=_cparams(("arbitrary",), VMEM_LIMIT),
        name="fourier_out_ctx",
    )(vr, x_new, wout, mods)


def _hgrn_in_kernel(x_ref, mod_ref, g_ref, win_ref, hlb_ref, q_ref, ff_ref, fb_ref, v_ref, gs_ref,
                    *, layer):
    x = x_ref[...]
    h = (_rms(x, g_ref[...]) * (1.0 + mod_ref[1:2, :]) + mod_ref[0:1, :]).astype(BF16)
    raw = [hlb_ref[l] for l in range(hlb_ref.shape[0])]
    mx = functools.reduce(jnp.maximum, raw)
    ex = [jnp.exp(r - mx) for r in raw]
    den = functools.reduce(lambda a, b: a + b, ex)
    soft = [e / den for e in ex]
    lb = functools.reduce(lambda a, b: a + b, soft[:layer + 1]) - soft[0]

    def proj(j):
        return jnp.dot(h, win_ref[:, j * D:(j + 1) * D], preferred_element_type=F32)

    q = proj(0)
    q_ref[...] = q * _sigmoid(q)
    ff_ref[...] = lb[0:1, :] + (1.0 - lb[0:1, :]) * _sigmoid(proj(1))
    fb_ref[...] = lb[1:2, :] + (1.0 - lb[1:2, :]) * _sigmoid(proj(2))
    v_ref[...] = proj(3)
    g = proj(4)
    gs_ref[...] = g * _sigmoid(g)


def _hgrn_in(x_all, mods, layer, norm_g, w_in, hlb):
    tile = pl.BlockSpec((TM, D), lambda i: (i, 0))
    depth = hlb.shape[0]
    return pl.pallas_call(
        functools.partial(_hgrn_in_kernel, layer=layer),
        grid=(ALL_TILES,),
        in_specs=[tile,
                  pl.BlockSpec((None, 6, D), lambda i: (layer * 8 + _tile_cond(i), 0, 0)),
                  pl.BlockSpec((1, D), lambda i: (0, 0)),
                  pl.BlockSpec((D, 5 * D), lambda i: (0, 0)),
                  pl.BlockSpec((depth, 2, D), lambda i: (0, 0, 0))],
        out_specs=[tile] * 5,
        out_shape=[jax.ShapeDtypeStruct((NTOK, D), F32)] * 5,
        compiler_params=_cparams(("arbitrary",), VMEM_LIMIT),
        name="hgrn_in",
    )(x_all, mods, norm_g.reshape(1, D), w_in.astype(BF16), hlb)


N_LEVELS = 7


def _scan_kernel(q_ref, fg_ref, v_ref, o_ref, s_ref, *, rev):
    @pl.when(pl.program_id(1) == 0)
    def _():
        s_ref[...] = jnp.zeros_like(s_ref)

    q = q_ref[...]
    fg = fg_ref[...]
    k = 1.0 - fg
    vb = v_ref[...].astype(BF16)
    t = lax.broadcasted_iota(jnp.int32, (CHUNK, D), 0)
    ti = lax.broadcasted_iota(jnp.int32, (CHUNK, CHUNK), 0)
    si = lax.broadcasted_iota(jnp.int32, (CHUNK, CHUNK), 1)
    scores = [None] * HEADS

    def add_level(qs, ks, mask):
        qb, kb = qs.astype(BF16), ks.astype(BF16)
        for h in range(HEADS):
            sl = slice(h * HDIM, (h + 1) * HDIM)
            sc = lax.dot_general(qb[:, sl], kb[:, sl], _NT, preferred_element_type=F32)
            sc = jnp.where(mask, sc, 0.0)
            scores[h] = sc if scores[h] is None else scores[h] + sc

    add_level(q, k, ti == si)
    run = fg
    rest = jnp.ones_like(fg)
    tot = fg
    for l in range(N_LEVELS):
        hbit = 1 << l
        odd = (t & hbit) != 0
        far = (t & hbit) == 0 if rev else odd
        add_level(jnp.where(far, run, 0.0) * q, jnp.where(far, 0.0, rest) * k,
                  (ti >> (l + 1)) == (si >> (l + 1)))
        sib = jnp.where(odd, pltpu.roll(tot, hbit, 0), pltpu.roll(tot, CHUNK - hbit, 0))
        run = jnp.where(far, run * sib, run)
        rest = jnp.where(far, rest, rest * sib)
        tot = tot * sib
    qin = (q * run).astype(BF16)
    kst = (k * rest).astype(BF16)
    for h in range(HEADS):
        sl = slice(h * HDIM, (h + 1) * HDIM)
        st = s_ref[h]
        o_ref[:, sl] = (jnp.dot(scores[h].astype(BF16), vb[:, sl], preferred_element_type=F32)
                        + lax.dot_general(qin[:, sl], st.astype(BF16), _NT,
                                          preferred_element_type=F32))
        s_ref[h] = st * tot[0:1, sl] + lax.dot_general(vb[:, sl], kst[:, sl], _TN,
                                                       preferred_element_type=F32)


LAT_CHUNKS = SEQ // CHUNK
CTX_CHUNKS = CTX_LEN // CHUNK
SCAN_STEPS = CTX_CHUNKS + LAT_CHUNKS


def _scan(q, fg, v, rev):
    def idx(b, s):
        if rev:
            c = jnp.where(s < CTX_CHUNKS, NLAT // CHUNK + CTX_CHUNKS * b + (CTX_CHUNKS - 1 - s),
                          LAT_CHUNKS * b + (SCAN_STEPS - 1 - s))
        else:
            c = jnp.where(s < CTX_CHUNKS, NLAT // CHUNK + CTX_CHUNKS * b + s,
                          LAT_CHUNKS * b + (s - CTX_CHUNKS))
        return (c, 0)
    blk = pl.BlockSpec((CHUNK, D), idx)
    return pl.pallas_call(
        functools.partial(_scan_kernel, rev=rev),
        grid=(BATCH, SCAN_STEPS),
        in_specs=[blk, blk, blk],
        out_specs=blk,
        out_shape=jax.ShapeDtypeStruct((NTOK, D), F32),
        scratch_shapes=[pltpu.VMEM((HEADS, HDIM, HDIM), F32)],
        compiler_params=_cparams(("arbitrary", "arbitrary"), VMEM_LIMIT),
        name="scan_bw" if rev else "scan_fw",
    )(q, fg, v)


def _hgrn_out_kernel(of_ref, ob_ref, gs_ref, x_ref, ng_ref, wout_ref, mod_ref, o_ref):
    o = of_ref[...] + ob_ref[...]
    parts = []
    for h in range(HEADS):
        oh = o[:, h * HDIM:(h + 1) * HDIM]
        parts.append(oh * lax.rsqrt(jnp.mean(oh * oh, axis=-1, keepdims=True) + NORM_EPS))
    on = jnp.concatenate(parts, axis=1) * ng_ref[...]
    y = jnp.dot((on * gs_ref[...]).astype(BF16), wout_ref[...], preferred_element_type=F32)
    o_ref[...] = x_ref[...] + mod_ref[2:3, :] * y


def _hgrn_out(o_fw, o_bw, gs, x_all, mods, layer, norm_g, w_out, n_tiles):
    tile = pl.BlockSpec((TM, D), lambda i: (i, 0))
    return pl.pallas_call(
        _hgrn_out_kernel,
        grid=(n_tiles,),
        in_specs=[tile, tile, tile, tile,
                  pl.BlockSpec((1, D), lambda i: (0, 0)),
                  pl.BlockSpec((D, D), lambda i: (0, 0)),
                  pl.BlockSpec((None, 6, D), lambda i: (layer * 8 + _tile_cond(i), 0, 0))],
        out_specs=tile,
        out_shape=jax.ShapeDtypeStruct((n_tiles * TM, D), F32),
        compiler_params=_cparams(("arbitrary",), VMEM_LIMIT),
        name="hgrn_out",
    )(o_fw, o_bw, gs, x_all, norm_g.reshape(1, D), w_out.astype(BF16), mods)


def _to_row_tiled(ref, val):
    for j in range(RT):
        ref[pl.ds(j, val.shape[0], stride=RT), :] = val[:, j * LANES:(j + 1) * LANES]


def _from_row_tiled(ref, n, base=0):
    return [ref[pl.ds(base + j, n, stride=RT), :] for j in range(RT)]


def _route_kernel(x_ref, mod_ref, g_ref, wrt_ref, brt_ref, h_ref, e_ref, rank_ref, gate_ref, cnt_ref):
    @pl.when(pl.program_id(0) == 0)
    def _():
        cnt_ref[...] = jnp.zeros_like(cnt_ref)

    h = _rms(x_ref[...], g_ref[...]) * (1.0 + mod_ref[4:5, :]) + mod_ref[3:4, :]
    _to_row_tiled(h_ref, h)
    logits = lax.dot_general(wrt_ref[...], h, _NT, precision=HIGHEST,
                             preferred_element_type=F32) + brt_ref[:, 0:1]
    row = lax.broadcasted_iota(jnp.int32, (N_EXPERTS, TM), 0).astype(F32)
    vals = logits
    sel = jnp.zeros((N_EXPERTS, TM), F32)
    tops, idxs = [], []
    for _ in range(TOP_K):
        m = jnp.max(vals, axis=0, keepdims=True)
        idx = jnp.min(jnp.where(vals == m, row, float(N_EXPERTS)), axis=0, keepdims=True)
        hit = row == idx
        vals = jnp.where(hit, -jnp.inf, vals)
        sel = jnp.where(hit, 1.0, sel)
        tops.append(m)
        idxs.append(idx)
    ex = [jnp.exp(m - tops[0]) for m in tops]
    den = ex[0] + ex[1] + ex[2] + ex[3]
    r = lax.broadcasted_iota(jnp.int32, (TM, TM), 0)
    c = lax.broadcasted_iota(jnp.int32, (TM, TM), 1)
    before = jnp.where(r < c, 1.0, 0.0).astype(BF16)
    pref = jnp.dot(sel.astype(BF16), before, preferred_element_type=F32) + cnt_ref[:, 0:1]
    slot = lax.broadcasted_iota(jnp.int32, (8, TM), 0)
    e_out = jnp.zeros((8, TM), F32)
    rank_out = jnp.zeros((8, TM), F32)
    gate_out = jnp.zeros((8, TM), F32)
    for kk in range(TOP_K):
        rank = jnp.sum(jnp.where(row == idxs[kk], pref, 0.0), axis=0, keepdims=True)
        e_out = jnp.where(slot == kk, idxs[kk], e_out)
        rank_out = jnp.where(slot == kk, rank, rank_out)
        gate_out = jnp.where(slot == kk, ex[kk] / den, gate_out)
    e_ref[...] = e_out.astype(jnp.int32)
    rank_ref[...] = rank_out.astype(jnp.int32)
    gate_ref[...] = gate_out
    cnt_ref[...] += jnp.sum(sel, axis=1, keepdims=True)


def _route(x_all, mods, layer, norm_g, w_r, b_r, n_tiles):
    tile = pl.BlockSpec((TM, D), lambda i: (i, 0))
    small = pl.BlockSpec((None, 8, TM), lambda i: (i, 0, 0))
    n = n_tiles * TM
    return pl.pallas_call(
        _route_kernel,
        grid=(n_tiles,),
        in_specs=[tile,
                  pl.BlockSpec((None, 6, D), lambda i: (layer * 8 + _tile_cond(i), 0, 0)),
                  pl.BlockSpec((1, D), lambda i: (0, 0)),
                  pl.BlockSpec((N_EXPERTS, D), lambda i: (0, 0)),
                  pl.BlockSpec((N_EXPERTS, LANES), lambda i: (0, 0))],
        out_specs=[pl.BlockSpec((TM * RT, LANES), lambda i: (i, 0)), small, small, small,
                   pl.BlockSpec((N_EXPERTS, LANES), lambda i: (0, 0))],
        out_shape=[jax.ShapeDtypeStruct((n * RT, LANES), F32),
                   jax.ShapeDtypeStruct((n_tiles, 8, TM), jnp.int32),
                   jax.ShapeDtypeStruct((n_tiles, 8, TM), jnp.int32),
                   jax.ShapeDtypeStruct((n_tiles, 8, TM), F32),
                   jax.ShapeDtypeStruct((N_EXPERTS, LANES), F32)],
        compiler_params=_cparams(("arbitrary",), VMEM_LIMIT),
        name="route",
    )(x_all, mods, norm_g.reshape(1, D), w_r.T, jnp.broadcast_to(b_r[:, None], (N_EXPERTS, LANES)))


def _dest_kernel(ps_ref, e_ref, rank_ref, d_ref):
    e = e_ref[...]
    acc = rank_ref[...]
    for j in range(N_EXPERTS):
        acc = acc + jnp.where(e == j, ps_ref[j], 0)
    d_ref[...] = acc


def _dest(pad_start, e, rank):
    full = pl.BlockSpec(e.shape, lambda i, ps: (0, 0, 0))
    return pl.pallas_call(
        _dest_kernel,
        grid_spec=pltpu.PrefetchScalarGridSpec(num_scalar_prefetch=1, grid=(1,),
                                               in_specs=[full, full], out_specs=full),
        out_shape=jax.ShapeDtypeStruct(e.shape, jnp.int32),
        name="dest",
    )(pad_start, e, rank)


def _dispatch_kernel(dest_ref, h_hbm, zero_hbm, xb_hbm, sem):
    del zero_hbm
    i = pl.program_id(0)
    n = pl.num_programs(0)
    slot = i % 2

    def body(t, carry):
        src = h_hbm.at[pl.ds((i * TM + t) * RT, RT), :]
        for kk in range(TOP_K):
            d = dest_ref[0, 0, kk * TM + t]
            pltpu.make_async_copy(src, xb_hbm.at[pl.ds(d * RT, RT), :], sem.at[slot]).start()
        return carry
    lax.fori_loop(0, TM, body, 0, unroll=4)

    def drain(s):
        rows = TOP_K * TM * RT
        pltpu.make_async_copy(h_hbm.at[pl.ds(0, rows), :], xb_hbm.at[pl.ds(0, rows), :],
                              sem.at[s]).wait()

    @pl.when(i > 0)
    def _():
        drain(1 - slot)

    @pl.when(i == n - 1)
    def _():
        drain(slot)


def _dispatch(h_rt, dest, n_rows):
    n_tiles = dest.shape[0]
    dest3 = dest[:, :TOP_K, :].reshape(n_tiles, 1, TOP_K * TM)
    xb = pl.pallas_call(
        _dispatch_kernel,
        grid=(n_tiles,),
        in_specs=[pl.BlockSpec((1, 1, TOP_K * TM), lambda i: (i, 0, 0), memory_space=pltpu.SMEM),
                  pl.BlockSpec(memory_space=pl.ANY),
                  pl.BlockSpec(memory_space=pl.ANY)],
        out_specs=pl.BlockSpec(memory_space=pl.ANY),
        out_shape=jax.ShapeDtypeStruct((n_rows * RT, LANES), F32),
        scratch_shapes=[pltpu.SemaphoreType.DMA((2,))],
        input_output_aliases={2: 0},
        compiler_params=_cparams(("arbitrary",), VMEM_LIMIT, disable_bounds_checks=True),
        name="dispatch",
    )(dest3, h_rt, jnp.zeros((n_rows * RT, LANES), F32))
    return xb, dest3


def _expert_kernel(bexp_ref, nused_ref, xb_ref, w1_ref, b1_ref, w2_ref, b2_ref, y_ref, w1b, w2b):
    i = pl.program_id(0)
    n_used = nused_ref[0]

    @pl.when(jnp.logical_or(i == 0, bexp_ref[i] != bexp_ref[jnp.maximum(i - 1, 0)]))
    def _():
        w1b[...] = w1_ref[...].astype(BF16)
        w2b[...] = w2_ref[...].astype(BF16)

    @pl.when(i < n_used)
    def _():
        x = jnp.concatenate(_from_row_tiled(xb_ref, BM), axis=1).astype(BF16)
        u = jnp.dot(x, w1b[...], preferred_element_type=F32) + b1_ref[...]
        glu = jnp.minimum(u[:, :D_FF], SWIGLU_LIMIT)
        lin = jnp.clip(u[:, D_FF:], -SWIGLU_LIMIT, SWIGLU_LIMIT)
        act = glu * _sigmoid(SWIGLU_ALPHA * glu) * (lin + 1.0)
        y = jnp.dot(act.astype(BF16), w2b[...], preferred_element_type=F32) + b2_ref[...]
        _to_row_tiled(y_ref, y)

    @pl.when(i >= n_used)
    def _():
        y_ref[...] = jnp.zeros_like(y_ref)


def _experts(xb, block_exp, n_used, layer, w1, b1, w2, b2):
    n_blocks = xb.shape[0] // (BM * RT)
    grid_spec = pltpu.PrefetchScalarGridSpec(
        num_scalar_prefetch=2,
        grid=(n_blocks,),
        in_specs=[
            pl.BlockSpec((BM * RT, LANES), lambda i, be, nu: (jnp.minimum(i, nu[0] - 1), 0)),
            pl.BlockSpec((None, None, D, 2 * D_FF), lambda i, be, nu: (layer, be[i], 0, 0)),
            pl.BlockSpec((None, 1, 2 * D_FF), lambda i, be, nu: (layer * N_EXPERTS + be[i], 0, 0)),
            pl.BlockSpec((None, None, D_FF, D), lambda i, be, nu: (layer, be[i], 0, 0)),
            pl.BlockSpec((None, 1, D), lambda i, be, nu: (layer * N_EXPERTS + be[i], 0, 0)),
        ],
        out_specs=pl.BlockSpec((BM * RT, LANES), lambda i, be, nu: (i, 0)),
        scratch_shapes=[pltpu.VMEM((D, 2 * D_FF), BF16),
                        pltpu.VMEM((D_FF, D), BF16)],
    )
    return pl.pallas_call(
        _expert_kernel,
        grid_spec=grid_spec,
        out_shape=jax.ShapeDtypeStruct(xb.shape, F32),
        compiler_params=_cparams(("arbitrary",), VMEM_LIMIT),
        name="experts",
    )(block_exp, n_used, xb, w1, b1.reshape(-1, 1, 2 * D_FF), w2, b2.reshape(-1, 1, D))


def _combine_kernel(dest_ref, dest_next_ref, y_hbm, gate_ref, x_ref, mod_ref, fg_ref, o_ref, buf, sem,
                    *, final_norm):
    i = pl.program_id(0)
    n = pl.num_programs(0)
    slot = i % 2
    rows = TOP_K * TM

    def gather(dest, to_slot):
        def body(r, carry):
            pltpu.make_async_copy(y_hbm.at[pl.ds(dest[0, 0, r] * RT, RT), :],
                                  buf.at[to_slot, pl.ds(r * RT, RT), :], sem.at[to_slot]).start()
            return carry
        lax.fori_loop(0, rows, body, 0, unroll=8)

    @pl.when(i == 0)
    def _():
        gather(dest_ref, 0)

    @pl.when(i + 1 < n)
    def _():
        gather(dest_next_ref, 1 - slot)

    pltpu.make_async_copy(y_hbm.at[pl.ds(0, rows * RT), :], buf.at[slot], sem.at[slot]).wait()
    eye = (lax.broadcasted_iota(jnp.int32, (8, 8), 0)
           == lax.broadcasted_iota(jnp.int32, (8, 8), 1)).astype(F32)
    gate = lax.dot_general(gate_ref[...], eye, _TN, precision=HIGHEST,
                           preferred_element_type=F32)
    cur = buf.at[slot]
    outs = []
    for j in range(RT):
        acc = None
        for kk in range(TOP_K):
            piece = gate[:, kk:kk + 1] * cur[pl.ds(kk * TM * RT + j, TM, stride=RT), :]
            acc = piece if acc is None else acc + piece
        outs.append(acc)
    out = x_ref[...] + mod_ref[5:6, :] * jnp.concatenate(outs, axis=1)
    if final_norm:
        out = _rms(out, fg_ref[...])
    o_ref[...] = out


def _combine(yb, dest3, gate, x_all, mods, layer, final_g, n_tiles, final_norm):
    last = n_tiles - 1
    tile = pl.BlockSpec((TM, D), lambda i: (i, 0))
    return pl.pallas_call(
        functools.partial(_combine_kernel, final_norm=final_norm),
        grid=(n_tiles,),
        in_specs=[pl.BlockSpec((1, 1, TOP_K * TM), lambda i: (i, 0, 0), memory_space=pltpu.SMEM),
                  pl.BlockSpec((1, 1, TOP_K * TM), lambda i: (jnp.minimum(i + 1, last), 0, 0),
                               memory_space=pltpu.SMEM),
                  pl.BlockSpec(memory_space=pl.ANY),
                  pl.BlockSpec((None, 8, TM), lambda i: (i, 0, 0)),
                  tile,
                  pl.BlockSpec((None, 6, D), lambda i: (layer * 8 + _tile_cond(i), 0, 0)),
                  pl.BlockSpec((1, D), lambda i: (0, 0))],
        out_specs=tile,
        out_shape=jax.ShapeDtypeStruct((n_tiles * TM, D), F32),
        scratch_shapes=[pltpu.VMEM((2, TOP_K * TM * RT, LANES), F32),
                        pltpu.SemaphoreType.DMA((2,))],
        compiler_params=_cparams(("arbitrary",), VMEM_LIMIT, disable_bounds_checks=True),
        name="combine",
    )(dest3, dest3, yb, gate, x_all, mods, final_g.reshape(1, D))


def _moe(x_all, mods, layer, norm_g, w_r, b_r, w1, b1, w2, b2, final_g, n_tiles, final_norm):
    n = n_tiles * TM
    h_rt, e, rank, gate, cnt = _route(x_all, mods, layer, norm_g, w_r, b_r, n_tiles)
    counts = cnt[:, 0].astype(jnp.int32)
    padded = (counts + BM - 1) // BM * BM
    pad_end = jnp.cumsum(padded)
    pad_start = pad_end - padded
    n_blocks = -(-(n * TOP_K) // BM) + N_EXPERTS
    starts = jnp.arange(n_blocks, dtype=jnp.int32) * BM
    block_exp = jnp.minimum(jnp.sum((pad_end[None, :] <= starts[:, None]).astype(jnp.int32), axis=1),
                            N_EXPERTS - 1)
    n_used = pad_end[-1:] // BM
    dest = _dest(pad_start, e, rank)
    xb, dest3 = _dispatch(h_rt, dest, n_blocks * BM)
    yb = _experts(xb, block_exp, n_used, layer, w1, b1, w2, b2)
    return _combine(yb, dest3, gate, x_all, mods, layer, final_g, n_tiles, final_norm)


def kernel(x, c, ctx, c_ctx, mod_w, mod_b, norm1_g, norm2_g, fourier_w_in, fourier_w_out,
           hgrn_w_in, hgrn_lower_bounds, hgrn_norm_g, hgrn_w_out, router_w, router_b,
           expert_w1, expert_b1, expert_w2, expert_b2, final_norm_g):
    assert x.shape == (BATCH, SEQ, D) and ctx.shape == (BATCH, CTX_LEN, D)
    cond8 = jnp.zeros((8, D), F32).at[:BATCH].set(c).at[BATCH].set(c_ctx)
    mods = _adaln(cond8, mod_w, mod_b)
    x_all = jnp.concatenate([x.reshape(NLAT, D), ctx.reshape(NCTX, D)], axis=0)
    dch, mpos, mrow = _fourier_constants()
    experts = (expert_w1, expert_b1, expert_w2, expert_b2)

    vr, vi = _fourier_in(x_all, mods, 0, norm1_g[0], fourier_w_in[0], dch, mpos)
    x_all = _fourier_out(vr, vi, x_all, mods, 0, fourier_w_out[0], mrow)
    x_all = _moe(x_all, mods, 0, norm2_g[0], router_w[0], router_b[0], *experts,
                 final_norm_g, ALL_TILES, False)

    q, ff, fb, v, gs = _hgrn_in(x_all, mods, 1, norm1_g[1], hgrn_w_in[0], hgrn_lower_bounds)
    o_fw = _scan(q, ff, v, False)
    o_bw = _scan(q, fb, v, True)
    x_lat = _hgrn_out(o_fw, o_bw, gs, x_all, mods, 1, hgrn_norm_g[0], hgrn_w_out[0], LAT_TILES)
    out = _moe(x_lat, mods, 1, norm2_g[1], router_w[1], router_b[1], *experts,
               final_norm_g, LAT_TILES, True)
    return out.reshape(BATCH, SEQ, D)
```

```python
import functools

import numpy as np
import jax
import jax.numpy as jnp
from jax import lax
from jax.experimental import pallas as pl
from jax.experimental.pallas import tpu as pltpu

F32 = jnp.float32
BF16 = jnp.bfloat16
HIGHEST = lax.Precision.HIGHEST

D = 1024
BATCH = 2
SEQ = 8192
CTX_LEN = 256
GRID_W = 64
GRID_H = SEQ // GRID_W
NLAT = BATCH * SEQ
NCTX = BATCH * CTX_LEN
NTOK = NLAT + NCTX
TM = 256
LAT_TILES = NLAT // TM
ALL_TILES = NTOK // TM
TILES_PER_BATCH = SEQ // TM
FGROUPS = 4
FGDIM = D // FGROUPS
HEADS = 8
HDIM = D // HEADS
CHUNK = 128
N_EXPERTS = 32
TOP_K = 4
D_FF = 1024
SWIGLU_ALPHA = 1.702
SWIGLU_LIMIT = 7.0
BM = 512
LANES = 128
RT = D // LANES
NORM_EPS = 1e-6
VMEM_LIMIT = 56 * 1024 * 1024

_NT = (((1,), (1,)), ((), ()))
_TN = (((0,), (0,)), ((), ()))


def _cparams(sem, vmem=None, **kw):
    return pltpu.CompilerParams(dimension_semantics=sem, vmem_limit_bytes=vmem, **kw)


def _sigmoid(x):
    return 1.0 / (1.0 + jnp.exp(-x))


def _rms(x, g):
    return x * lax.rsqrt(jnp.mean(x * x, axis=-1, keepdims=True) + NORM_EPS) * g


def _tile_cond(i):
    return jnp.where(i < LAT_TILES, i // TILES_PER_BATCH, 2)


def _adaln_kernel(cond_ref, w_ref, b_ref, o_ref):
    c = cond_ref[...]
    s = c * _sigmoid(c)
    o_ref[...] = jnp.dot(s, w_ref[...], precision=HIGHEST,
                         preferred_element_type=F32) + b_ref[...]


def _adaln(cond8, mod_w, mod_b):
    depth = mod_w.shape[0]
    nb = 1536
    out = pl.pallas_call(
        _adaln_kernel,
        grid=(depth, 6 * D // nb),
        in_specs=[pl.BlockSpec((8, D), lambda l, j: (0, 0)),
                  pl.BlockSpec((None, D, nb), lambda l, j: (l, 0, j)),
                  pl.BlockSpec((None, 1, nb), lambda l, j: (l, 0, j))],
        out_specs=pl.BlockSpec((None, 8, nb), lambda l, j: (l, 0, j)),
        out_shape=jax.ShapeDtypeStruct((depth, 8, 6 * D), F32),
        compiler_params=_cparams(("arbitrary", "arbitrary"), VMEM_LIMIT),
        name="adaln",
    )(cond8, mod_w, mod_b.reshape(depth, 1, 6 * D))
    return out.reshape(depth * 8, 6, D)


def _dft_cs(n):
    k = np.arange(n)
    ang = 2.0 * np.pi * np.outer(k, k) / n
    s = 1.0 / np.sqrt(n)
    return np.cos(ang) * s, np.sin(ang) * s


def _fourier_constants():
    cd, sd = _dft_cs(FGDIM)
    dch = np.concatenate([cd, sd], axis=1)
    cc, sc = _dft_cs(GRID_W)
    eye = np.eye(TM // GRID_W)
    kc, ks = np.kron(eye, cc), np.kron(eye, sc)
    m_lat = np.block([[kc, -ks], [ks, kc]])
    cp, sp = _dft_cs(CTX_LEN)
    m_ctx = np.block([[cp, -sp], [sp, cp]])
    mpos = np.stack([m_lat, m_ctx])
    cr, sr = _dft_cs(GRID_H)
    mrow = np.concatenate([cr, -sr], axis=1)
    return (jnp.asarray(dch, BF16), jnp.asarray(mpos, BF16), jnp.asarray(mrow, BF16))


def _fourier_in_kernel(x_ref, mod_ref, g_ref, win_ref, dch_ref, mpos_ref, vr_ref, vi_ref):
    x = x_ref[...]
    h = _rms(x, g_ref[...]) * (1.0 + mod_ref[1:2, :]) + mod_ref[0:1, :]
    u = jnp.dot(h.astype(BF16), win_ref[...], preferred_element_type=F32).astype(BF16)
    parts = [jnp.dot(u[:, g * FGDIM:(g + 1) * FGDIM], dch_ref[...],
                     preferred_element_type=F32) for g in range(FGROUPS)]
    uc = jnp.concatenate([p[:, :FGDIM] for p in parts], axis=1)
    us = jnp.concatenate([p[:, FGDIM:] for p in parts], axis=1)
    st = jnp.concatenate([uc, us], axis=0).astype(BF16)
    v = jnp.dot(mpos_ref[...], st, preferred_element_type=F32)
    vr_ref[...] = v[:TM]
    vi_ref[...] = v[TM:]


def _fourier_in(x_all, mods, layer, norm_g, w_in, dch, mpos):
    tile = pl.BlockSpec((TM, D), lambda i: (i, 0))
    return pl.pallas_call(
        _fourier_in_kernel,
        grid=(ALL_TILES,),
        in_specs=[tile,
                  pl.BlockSpec((None, 6, D), lambda i: (layer * 8 + _tile_cond(i), 0, 0)),
                  pl.BlockSpec((1, D), lambda i: (0, 0)),
                  pl.BlockSpec((D, D), lambda i: (0, 0)),
                  pl.BlockSpec((FGDIM, 2 * FGDIM), lambda i: (0, 0)),
                  pl.BlockSpec((None, 2 * TM, 2 * TM), lambda i: (jnp.where(i < LAT_TILES, 0, 1), 0, 0))],
        out_specs=[tile, tile],
        out_shape=[jax.ShapeDtypeStruct((NTOK, D), F32)] * 2,
        compiler_params=_cparams(("arbitrary",), VMEM_LIMIT),
        name="fourier_in",
    )(x_all, mods, norm_g.reshape(1, D), w_in.astype(BF16), dch, mpos)


CB = 8


def _fourier_out_lat_kernel(vr_ref, vi_ref, x_ref, mrow_ref, wout_ref, mod_ref, o_ref):
    g1 = mod_ref[2:3, :]
    for c in range(CB):
        st = jnp.concatenate([vr_ref[:, c, :], vi_ref[:, c, :]], axis=0).astype(BF16)
        yf = jnp.dot(mrow_ref[...], st, preferred_element_type=F32)
        y = jnp.dot(yf.astype(BF16), wout_ref[...], preferred_element_type=F32)
        o_ref[:, c, :] = x_ref[:, c, :] + g1 * y


def _fourier_out_ctx_kernel(yr_ref, x_ref, wout_ref, mod_ref, o_ref):
    y = jnp.dot(yr_ref[...].astype(BF16), wout_ref[...], preferred_element_type=F32)
    o_ref[...] = x_ref[...] + mod_ref[2:3, :] * y


def _fourier_out(vr, vi, x_all, mods, layer, w_out, mrow):
    wout = w_out.astype(BF16)
    rows = NTOK // GRID_W
    v3 = lambda a: a.reshape(rows, GRID_W, D)
    blk = pl.BlockSpec((GRID_H, CB, D), lambda b, c: (b, c, 0))
    x_new = pl.pallas_call(
        _fourier_out_lat_kernel,
        grid=(BATCH, GRID_W // CB),
        in_specs=[blk, blk, blk,
                  pl.BlockSpec((GRID_H, 2 * GRID_H), lambda b, c: (0, 0)),
                  pl.BlockSpec((D, D), lambda b, c: (0, 0)),
                  pl.BlockSpec((None, 6, D), lambda b, c: (layer * 8 + b, 0, 0))],
        out_specs=blk,
        out_shape=jax.ShapeDtypeStruct((rows, GRID_W, D), F32),
        input_output_aliases={2: 0},
        compiler_params=_cparams(("arbitrary", "arbitrary"), VMEM_LIMIT),
        name="fourier_out_lat",
    )(v3(vr), v3(vi), v3(x_all), mrow, wout, mods).reshape(NTOK, D)
    ctile = pl.BlockSpec((TM, D), lambda i: (LAT_TILES + i, 0))
    return pl.pallas_call(
        _fourier_out_ctx_kernel,
        grid=(NCTX // TM,),
        in_specs=[ctile, ctile,
                  pl.BlockSpec((D, D), lambda i: (0, 0)),
                  pl.BlockSpec((None, 6, D), lambda i: (layer * 8 + 2, 0, 0))],
        out_specs=ctile,
        out_shape=jax.ShapeDtypeStruct((NTOK, D), F32),
        input_output_aliases={1: 0},
        compiler_params=_cparams(("arbitrary",), VMEM_LIMIT),
        name="fourier_out_ctx",
    )(vr, x_new, wout, mods)


def _hgrn_in_kernel(x_ref, mod_ref, g_ref, win_ref, hlb_ref, q_ref, ff_ref, fb_ref, v_ref, gs_ref,
                    *, layer):
    x = x_ref[...]
    h = (_rms(x, g_ref[...]) * (1.0 + mod_ref[1:2, :]) + mod_ref[0:1, :]).astype(BF16)
    raw = [hlb_ref[l] for l in range(hlb_ref.shape[0])]
    mx = functools.reduce(jnp.maximum, raw)
    ex = [jnp.exp(r - mx) for r in raw]
    den = functools.reduce(lambda a, b: a + b, ex)
    soft = [e / den for e in ex]
    lb = functools.reduce(lambda a, b: a + b, soft[:layer + 1]) - soft[0]

    def proj(j):
        return jnp.dot(h, win_ref[:, j * D:(j + 1) * D], preferred_element_type=F32)

    q = proj(0)
    q_ref[...] = q * _sigmoid(q)
    ff_ref[...] = lb[0:1, :] + (1.0 - lb[0:1, :]) * _sigmoid(proj(1))
    fb_ref[...] = lb[1:2, :] + (1.0 - lb[1:2, :]) * _sigmoid(proj(2))
    v_ref[...] = proj(3)
    g = proj(4)
    gs_ref[...] = g * _sigmoid(g)


def _hgrn_in(x_all, mods, layer, norm_g, w_in, hlb):
    tile = pl.BlockSpec((TM, D), lambda i: (i, 0))
    depth = hlb.shape[0]
    return pl.pallas_call(
        functools.partial(_hgrn_in_kernel, layer=layer),
        grid=(ALL_TILES,),
        in_specs=[tile,
                  pl.BlockSpec((None, 6, D), lambda i: (layer * 8 + _tile_cond(i), 0, 0)),
                  pl.BlockSpec((1, D), lambda i: (0, 0)),
                  pl.BlockSpec((D, 5 * D), lambda i: (0, 0)),
                  pl.BlockSpec((depth, 2, D), lambda i: (0, 0, 0))],
        out_specs=[tile] * 5,
        out_shape=[jax.ShapeDtypeStruct((NTOK, D), F32)] * 5,
        compiler_params=_cparams(("arbitrary",), VMEM_LIMIT),
        name="hgrn_in",
    )(x_all, mods, norm_g.reshape(1, D), w_in.astype(BF16), hlb)


N_LEVELS = 7


def _scan_kernel(q_ref, fg_ref, v_ref, o_ref, s_ref, *, rev):
    @pl.when(pl.program_id(1) == 0)
    def _():
        s_ref[...] = jnp.zeros_like(s_ref)

    q = q_ref[...]
    fg = fg_ref[...]
    k = 1.0 - fg
    vb = v_ref[...].astype(BF16)
    t = lax.broadcasted_iota(jnp.int32, (CHUNK, D), 0)
    ti = lax.broadcasted_iota(jnp.int32, (CHUNK, CHUNK), 0)
    si = lax.broadcasted_iota(jnp.int32, (CHUNK, CHUNK), 1)
    scores = [None] * HEADS

    def add_level(qs, ks, mask):
        qb, kb = qs.astype(BF16), ks.astype(BF16)
        for h in range(HEADS):
            sl = slice(h * HDIM, (h + 1) * HDIM)
            sc = lax.dot_general(qb[:, sl], kb[:, sl], _NT, preferred_element_type=F32)
            sc = jnp.where(mask, sc, 0.0)
            scores[h] = sc if scores[h] is None else scores[h] + sc

    add_level(q, k, ti == si)
    run = fg
    rest = jnp.ones_like(fg)
    tot = fg
    for l in range(N_LEVELS):
        hbit = 1 << l
        odd = (t & hbit) != 0
        far = (t & hbit) == 0 if rev else odd
        add_level(jnp.where(far, run, 0.0) * q, jnp.where(far, 0.0, rest) * k,
                  (ti >> (l + 1)) == (si >> (l + 1)))
        sib = jnp.where(odd, pltpu.roll(tot, hbit, 0), pltpu.roll(tot, CHUNK - hbit, 0))
        run = jnp.where(far, run * sib, run)
        rest = jnp.where(far, rest, rest * sib)
        tot = tot * sib
    qin = (q * run).astype(BF16)
    kst = (k * rest).astype(BF16)
    for h in range(HEADS):
        sl = slice(h * HDIM, (h + 1) * HDIM)
        st = s_ref[h]
        o_ref[:, sl] = (jnp.dot(scores[h].astype(BF16), vb[:, sl], preferred_element_type=F32)
                        + lax.dot_general(qin[:, sl], st.astype(BF16), _NT,
                                          preferred_element_type=F32))
        s_ref[h] = st * tot[0:1, sl] + lax.dot_general(vb[:, sl], kst[:, sl], _TN,
                                                       preferred_element_type=F32)


LAT_CHUNKS = SEQ // CHUNK
CTX_CHUNKS = CTX_LEN // CHUNK
SCAN_STEPS = CTX_CHUNKS + LAT_CHUNKS


def _scan(q, fg, v, rev):
    def idx(b, s):
        if rev:
            c = jnp.where(s < CTX_CHUNKS, NLAT // CHUNK + CTX_CHUNKS * b + (CTX_CHUNKS - 1 - s),
                          LAT_CHUNKS * b + (SCAN_STEPS - 1 - s))
        else:
            c = jnp.where(s < CTX_CHUNKS, NLAT // CHUNK + CTX_CHUNKS * b + s,
                          LAT_CHUNKS * b + (s - CTX_CHUNKS))
        return (c, 0)
    blk = pl.BlockSpec((CHUNK, D), idx)
    return pl.pallas_call(
        functools.partial(_scan_kernel, rev=rev),
        grid=(BATCH, SCAN_STEPS),
        in_specs=[blk, blk, blk],
        out_specs=blk,
        out_shape=jax.ShapeDtypeStruct((NTOK, D), F32),
        scratch_shapes=[pltpu.VMEM((HEADS, HDIM, HDIM), F32)],
        compiler_params=_cparams(("arbitrary", "arbitrary"), VMEM_LIMIT),
        name="scan_bw" if rev else "scan_fw",
    )(q, fg, v)


def _hgrn_out_kernel(of_ref, ob_ref, gs_ref, x_ref, ng_ref, wout_ref, mod_ref, o_ref):
    o = of_ref[...] + ob_ref[...]
    parts = []
    for h in range(HEADS):
        oh = o[:, h * HDIM:(h + 1) * HDIM]
        parts.append(oh * lax.rsqrt(jnp.mean(oh * oh, axis=-1, keepdims=True) + NORM_EPS))
    on = jnp.concatenate(parts, axis=1) * ng_ref[...]
    y = jnp.dot((on * gs_ref[...]).astype(BF16), wout_ref[...], preferred_element_type=F32)
    o_ref[...] = x_ref[...] + mod_ref[2:3, :] * y


def _hgrn_out(o_fw, o_bw, gs, x_all, mods, layer, norm_g, w_out, n_tiles):
    tile = pl.BlockSpec((TM, D), lambda i: (i, 0))
    return pl.pallas_call(
        _hgrn_out_kernel,
        grid=(n_tiles,),
        in_specs=[tile, tile, tile, tile,
                  pl.BlockSpec((1, D), lambda i: (0, 0)),
                  pl.BlockSpec((D, D), lambda i: (0, 0)),
                  pl.BlockSpec((None, 6, D), lambda i: (layer * 8 + _tile_cond(i), 0, 0))],
        out_specs=tile,
        out_shape=jax.ShapeDtypeStruct((n_tiles * TM, D), F32),
        compiler_params=_cparams(("arbitrary",), VMEM_LIMIT),
        name="hgrn_out",
    )(o_fw, o_bw, gs, x_all, norm_g.reshape(1, D), w_out.astype(BF16), mods)


def _to_row_tiled(ref, val):
    for j in range(RT):
        ref[pl.ds(j, val.shape[0], stride=RT), :] = val[:, j * LANES:(j + 1) * LANES]


def _from_row_tiled(ref, n, base=0):
    return [ref[pl.ds(base + j, n, stride=RT), :] for j in range(RT)]


def _route_kernel(x_ref, mod_ref, g_ref, wrt_ref, brt_ref, h_ref, e_ref, rank_ref, gate_ref, cnt_ref):
    @pl.when(pl.program_id(0) == 0)
    def _():
        cnt_ref[...] = jnp.zeros_like(cnt_ref)

    h = _rms(x_ref[...], g_ref[...]) * (1.0 + mod_ref[4:5, :]) + mod_ref[3:4, :]
    _to_row_tiled(h_ref, h)
    logits = lax.dot_general(wrt_ref[...], h, _NT, precision=HIGHEST,
                             preferred_element_type=F32) + brt_ref[:, 0:1]
    row = lax.broadcasted_iota(jnp.int32, (N_EXPERTS, TM), 0).astype(F32)
    vals = logits
    sel = jnp.zeros((N_EXPERTS, TM), F32)
    tops, idxs = [], []
    for _ in range(TOP_K):
        m = jnp.max(vals, axis=0, keepdims=True)
        idx = jnp.min(jnp.where(vals == m, row, float(N_EXPERTS)), axis=0, keepdims=True)
        hit = row == idx
        vals = jnp.where(hit, -jnp.inf, vals)
        sel = jnp.where(hit, 1.0, sel)
        tops.append(m)
        idxs.append(idx)
    ex = [jnp.exp(m - tops[0]) for m in tops]
    den = ex[0] + ex[1] + ex[2] + ex[3]
    r = lax.broadcasted_iota(jnp.int32, (TM, TM), 0)
    c = lax.broadcasted_iota(jnp.int32, (TM, TM), 1)
    before = jnp.where(r < c, 1.0, 0.0).astype(BF16)
    pref = jnp.dot(sel.astype(BF16), before, preferred_element_type=F32) + cnt_ref[:, 0:1]
    slot = lax.broadcasted_iota(jnp.int32, (8, TM), 0)
    e_out = jnp.zeros((8, TM), F32)
    rank_out = jnp.zeros((8, TM), F32)
    gate_out = jnp.zeros((8, TM), F32)
    for kk in range(TOP_K):
        rank = jnp.sum(jnp.where(row == idxs[kk], pref, 0.0), axis=0, keepdims=True)
        e_out = jnp.where(slot == kk, idxs[kk], e_out)
        rank_out = jnp.where(slot == kk, rank, rank_out)
        gate_out = jnp.where(slot == kk, ex[kk] / den, gate_out)
    e_ref[...] = e_out.astype(jnp.int32)
    rank_ref[...] = rank_out.astype(jnp.int32)
    gate_ref[...] = gate_out
    cnt_ref[...] += jnp.sum(sel, axis=1, keepdims=True)


def _route(x_all, mods, layer, norm_g, w_r, b_r, n_tiles):
    tile = pl.BlockSpec((TM, D), lambda i: (i, 0))
    small = pl.BlockSpec((None, 8, TM), lambda i: (i, 0, 0))
    n = n_tiles * TM
    return pl.pallas_call(
        _route_kernel,
        grid=(n_tiles,),
        in_specs=[tile,
                  pl.BlockSpec((None, 6, D), lambda i: (layer * 8 + _tile_cond(i), 0, 0)),
                  pl.BlockSpec((1, D), lambda i: (0, 0)),
                  pl.BlockSpec((N_EXPERTS, D), lambda i: (0, 0)),
                  pl.BlockSpec((N_EXPERTS, LANES), lambda i: (0, 0))],
        out_specs=[pl.BlockSpec((TM * RT, LANES), lambda i: (i, 0)), small, small, small,
                   pl.BlockSpec((N_EXPERTS, LANES), lambda i: (0, 0))],
        out_shape=[jax.ShapeDtypeStruct((n * RT, LANES), F32),
                   jax.ShapeDtypeStruct((n_tiles, 8, TM), jnp.int32),
                   jax.ShapeDtypeStruct((n_tiles, 8, TM), jnp.int32),
                   jax.ShapeDtypeStruct((n_tiles, 8, TM), F32),
                   jax.ShapeDtypeStruct((N_EXPERTS, LANES), F32)],
        compiler_params=_cparams(("arbitrary",), VMEM_LIMIT),
        name="route",
    )(x_all, mods, norm_g.reshape(1, D), w_r.T, jnp.broadcast_to(b_r[:, None], (N_EXPERTS, LANES)))


def _dest_kernel(ps_ref, e_ref, rank_ref, d_ref):
    e = e_ref[...]
    acc = rank_ref[...]
    for j in range(N_EXPERTS):
        acc = acc + jnp.where(e == j, ps_ref[j], 0)
    d_ref[...] = acc


def _dest(pad_start, e, rank):
    full = pl.BlockSpec(e.shape, lambda i, ps: (0, 0, 0))
    return pl.pallas_call(
        _dest_kernel,
        grid_spec=pltpu.PrefetchScalarGridSpec(num_scalar_prefetch=1, grid=(1,),
                                               in_specs=[full, full], out_specs=full),
        out_shape=jax.ShapeDtypeStruct(e.shape, jnp.int32),
        name="dest",
    )(pad_start, e, rank)


def _dispatch_kernel(dest_ref, h_ref, zero_hbm, xb_hbm, sem):
    del zero_hbm

    def body(t, carry):
        src = h_ref.at[pl.ds(t * RT, RT), :]
        for kk in range(TOP_K):
            d = dest_ref[0, 0, kk * TM + t]
            pltpu.make_async_copy(src, xb_hbm.at[pl.ds(d * RT, RT), :],
                                  sem.at[0]).start(priority=kk % 2)
        return carry
    lax.fori_loop(0, TM, body, 0, unroll=4)
    for kk in range(TOP_K):
        pltpu.make_async_copy(h_ref, xb_hbm.at[pl.ds(0, TM * RT), :], sem.at[0]).wait()


def _dispatch(h_rt, dest, n_rows):
    n_tiles = dest.shape[0]
    dest3 = dest[:, :TOP_K, :].reshape(n_tiles, 1, TOP_K * TM)
    xb = pl.pallas_call(
        _dispatch_kernel,
        grid=(n_tiles,),
        in_specs=[pl.BlockSpec((1, 1, TOP_K * TM), lambda i: (i, 0, 0), memory_space=pltpu.SMEM),
                  pl.BlockSpec((TM * RT, LANES), lambda i: (i, 0)),
                  pl.BlockSpec(memory_space=pl.ANY)],
        out_specs=pl.BlockSpec(memory_space=pl.ANY),
        out_shape=jax.ShapeDtypeStruct((n_rows * RT, LANES), F32),
        scratch_shapes=[pltpu.SemaphoreType.DMA((1,))],
        input_output_aliases={2: 0},
        compiler_params=_cparams(("arbitrary",), VMEM_LIMIT, disable_bounds_checks=True),
        name="dispatch",
    )(dest3, h_rt, jnp.zeros((n_rows * RT, LANES), F32))
    return xb, dest3


def _expert_kernel(bexp_ref, nused_ref, xb_ref, w1_ref, b1_ref, w2_ref, b2_ref, y_ref, w1b, w2b):
    i = pl.program_id(0)
    n_used = nused_ref[0]

    @pl.when(jnp.logical_or(i == 0, bexp_ref[i] != bexp_ref[jnp.maximum(i - 1, 0)]))
    def _():
        w1b[...] = w1_ref[...].astype(BF16)
        w2b[...] = w2_ref[...].astype(BF16)

    @pl.when(i < n_used)
    def _():
        x = jnp.concatenate(_from_row_tiled(xb_ref, BM), axis=1).astype(BF16)
        u = jnp.dot(x, w1b[...], preferred_element_type=F32) + b1_ref[...]
        glu = jnp.minimum(u[:, :D_FF], SWIGLU_LIMIT)
        lin = jnp.clip(u[:, D_FF:], -SWIGLU_LIMIT, SWIGLU_LIMIT)
        act = glu * _sigmoid(SWIGLU_ALPHA * glu) * (lin + 1.0)
        y = jnp.dot(act.astype(BF16), w2b[...], preferred_element_type=F32) + b2_ref[...]
        _to_row_tiled(y_ref, y)

    @pl.when(i >= n_used)
    def _():
        y_ref[...] = jnp.zeros_like(y_ref)


def _experts(xb, block_exp, n_used, layer, w1, b1, w2, b2):
    n_blocks = xb.shape[0] // (BM * RT)
    grid_spec = pltpu.PrefetchScalarGridSpec(
        num_scalar_prefetch=2,
        grid=(n_blocks,),
        in_specs=[
            pl.BlockSpec((BM * RT, LANES), lambda i, be, nu: (jnp.minimum(i, nu[0] - 1), 0)),
            pl.BlockSpec((None, None, D, 2 * D_FF), lambda i, be, nu: (layer, be[i], 0, 0)),
            pl.BlockSpec((None, 1, 2 * D_FF), lambda i, be, nu: (layer * N_EXPERTS + be[i], 0, 0)),
            pl.BlockSpec((None, None, D_FF, D), lambda i, be, nu: (layer, be[i], 0, 0)),
            pl.BlockSpec((None, 1, D), lambda i, be, nu: (layer * N_EXPERTS + be[i], 0, 0)),
        ],
        out_specs=pl.BlockSpec((BM * RT, LANES), lambda i, be, nu: (i, 0)),
        scratch_shapes=[pltpu.VMEM((D, 2 * D_FF), BF16),
                        pltpu.VMEM((D_FF, D), BF16)],
    )
    return pl.pallas_call(
        _expert_kernel,
        grid_spec=grid_spec,
        out_shape=jax.ShapeDtypeStruct(xb.shape, F32),
        compiler_params=_cparams(("arbitrary",), VMEM_LIMIT),
        name="experts",
    )(block_exp, n_used, xb, w1, b1.reshape(-1, 1, 2 * D_FF), w2, b2.reshape(-1, 1, D))


def _combine_kernel(dest_ref, dest_next_ref, y_hbm, gate_ref, x_ref, mod_ref, fg_ref, o_ref, buf, sem,
                    *, final_norm):
    i = pl.program_id(0)
    n = pl.num_programs(0)
    slot = i % 2
    rows = TOP_K * TM

    def gather(dest, to_slot):
        def body(p, carry):
            for half in range(2):
                r = 2 * p + half
                pltpu.make_async_copy(y_hbm.at[pl.ds(dest[0, 0, r] * RT, RT), :],
                                      buf.at[to_slot, pl.ds(r * RT, RT), :],
                                      sem.at[to_slot]).start(priority=half)
            return carry
        lax.fori_loop(0, rows // 2, body, 0, unroll=4)

    @pl.when(i == 0)
    def _():
        gather(dest_ref, 0)

    @pl.when(i + 1 < n)
    def _():
        gather(dest_next_ref, 1 - slot)

    pltpu.make_async_copy(y_hbm.at[pl.ds(0, rows * RT), :], buf.at[slot], sem.at[slot]).wait()
    eye = (lax.broadcasted_iota(jnp.int32, (8, 8), 0)
           == lax.broadcasted_iota(jnp.int32, (8, 8), 1)).astype(F32)
    gate = lax.dot_general(gate_ref[...], eye, _TN, precision=HIGHEST,
                           preferred_element_type=F32)
    cur = buf.at[slot]
    outs = []
    for j in range(RT):
        acc = None
        for kk in range(TOP_K):
            piece = gate[:, kk:kk + 1] * cur[pl.ds(kk * TM * RT + j, TM, stride=RT), :]
            acc = piece if acc is None else acc + piece
        outs.append(acc)
    out = x_ref[...] + mod_ref[5:6, :] * jnp.concatenate(outs, axis=1)
    if final_norm:
        out = _rms(out, fg_ref[...])
    o_ref[...] = out


def _combine(yb, dest3, gate, x_all, mods, layer, final_g, n_tiles, final_norm):
    last = n_tiles - 1
    tile = pl.BlockSpec((TM, D), lambda i: (i, 0))
    return pl.pallas_call(
        functools.partial(_combine_kernel, final_norm=final_norm),
        grid=(n_tiles,),
        in_specs=[pl.BlockSpec((1, 1, TOP_K * TM), lambda i: (i, 0, 0), memory_space=pltpu.SMEM),
                  pl.BlockSpec((1, 1, TOP_K * TM), lambda i: (jnp.minimum(i + 1, last), 0, 0),
                               memory_space=pltpu.SMEM),
                  pl.BlockSpec(memory_space=pl.ANY),
                  pl.BlockSpec((None, 8, TM), lambda i: (i, 0, 0)),
                  tile,
                  pl.BlockSpec((None, 6, D), lambda i: (layer * 8 + _tile_cond(i), 0, 0)),
                  pl.BlockSpec((1, D), lambda i: (0, 0))],
        out_specs=tile,
        out_shape=jax.ShapeDtypeStruct((n_tiles * TM, D), F32),
        scratch_shapes=[pltpu.VMEM((2, TOP_K * TM * RT, LANES), F32),
                        pltpu.SemaphoreType.DMA((2,))],
        compiler_params=_cparams(("arbitrary",), VMEM_LIMIT, disable_bounds_checks=True),
        name="combine",
    )(dest3, dest3, yb, gate, x_all, mods, final_g.reshape(1, D))


def _moe(x_all, mods, layer, norm_g, w_r, b_r, w1, b1, w2, b2, final_g, n_tiles, final_norm):
    n = n_tiles * TM
    h_rt, e, rank, gate, cnt = _route(x_all, mods, layer, norm_g, w_r, b_r, n_tiles)
    counts = cnt[:, 0].astype(jnp.int32)
    padded = (counts + BM - 1) // BM * BM
    pad_end = jnp.cumsum(padded)
    pad_start = pad_end - padded
    n_blocks = -(-(n * TOP_K) // BM) + N_EXPERTS
    starts = jnp.arange(n_blocks, dtype=jnp.int32) * BM
    block_exp = jnp.minimum(jnp.sum((pad_end[None, :] <= starts[:, None]).astype(jnp.int32), axis=1),
                            N_EXPERTS - 1)
    n_used = pad_end[-1:] // BM
    dest = _dest(pad_start, e, rank)
    xb, dest3 = _dispatch(h_rt, dest, n_blocks * BM)
    yb = _experts(xb, block_exp, n_used, layer, w1, b1, w2, b2)
    return _combine(yb, dest3, gate, x_all, mods, layer, final_g, n_tiles, final_norm)


def kernel(x, c, ctx, c_ctx, mod_w, mod_b, norm1_g, norm2_g, fourier_w_in, fourier_w_out,
           hgrn_w_in, hgrn_lower_bounds, hgrn_norm_g, hgrn_w_out, router_w, router_b,
           expert_w1, expert_b1, expert_w2, expert_b2, final_norm_g):
    assert x.shape == (BATCH, SEQ, D) and ctx.shape == (BATCH, CTX_LEN, D)
    cond8 = jnp.zeros((8, D), F32).at[:BATCH].set(c).at[BATCH].set(c_ctx)
    mods = _adaln(cond8, mod_w, mod_b)
    x_all = jnp.concatenate([x.reshape(NLAT, D), ctx.reshape(NCTX, D)], axis=0)
    dch, mpos, mrow = _fourier_constants()
    experts = (expert_w1, expert_b1, expert_w2, expert_b2)

    vr, vi = _fourier_in(x_all, mods, 0, norm1_g[0], fourier_w_in[0], dch, mpos)
    x_all = _fourier_out(vr, vi, x_all, mods, 0, fourier_w_out[0], mrow)
    x_all = _moe(x_all, mods, 0, norm2_g[0], router_w[0], router_b[0], *experts,
                 final_norm_g, ALL_TILES, False)

    q, ff, fb, v, gs = _hgrn_in(x_all, mods, 1, norm1_g[1], hgrn_w_in[0], hgrn_lower_bounds)
    o_fw = _scan(q, ff, v, False)
    o_bw = _scan(q, fb, v, True)
    x_lat = _hgrn_out(o_fw, o_bw, gs, x_all, mods, 1, hgrn_norm_g[0], hgrn_w_out[0], LAT_TILES)
    out = _moe(x_lat, mods, 1, norm2_g[1], router_w[1], router_b[1], *experts,
               final_norm_g, LAT_TILES, True)
    return out.reshape(BATCH, SEQ, D)
```

```python
import functools

import numpy as np
import jax
import jax.numpy as jnp
from jax import lax
from jax.experimental import pallas as pl
from jax.experimental.pallas import tpu as pltpu

F32 = jnp.float32
BF16 = jnp.bfloat16
HIGHEST = lax.Precision.HIGHEST

D = 1024
BATCH = 2
SEQ = 8192
CTX_LEN = 256
GRID_W = 64
GRID_H = SEQ // GRID_W
NLAT = BATCH * SEQ
NCTX = BATCH * CTX_LEN
NTOK = NLAT + NCTX
TM = 256
LAT_TILES = NLAT // TM
ALL_TILES = NTOK // TM
TILES_PER_BATCH = SEQ // TM
FGROUPS = 4
FGDIM = D // FGROUPS
HEADS = 8
HDIM = D // HEADS
CHUNK = 128
N_EXPERTS = 32
TOP_K = 4
D_FF = 1024
SWIGLU_ALPHA = 1.702
SWIGLU_LIMIT = 7.0
BM = 512
LANES = 128
RT = D // LANES
NORM_EPS = 1e-6
VMEM_LIMIT = 56 * 1024 * 1024

_NT = (((1,), (1,)), ((), ()))
_TN = (((0,), (0,)), ((), ()))


def _cparams(sem, vmem=None, **kw):
    return pltpu.CompilerParams(dimension_semantics=sem, vmem_limit_bytes=vmem, **kw)


def _sigmoid(x):
    return 1.0 / (1.0 + jnp.exp(-x))


def _rms(x, g):
    return x * lax.rsqrt(jnp.mean(x * x, axis=-1, keepdims=True) + NORM_EPS) * g


def _tile_cond(i):
    return jnp.where(i < LAT_TILES, i // TILES_PER_BATCH, 2)


def _adaln_kernel(cond_ref, w_ref, b_ref, o_ref):
    c = cond_ref[...]
    s = c * _sigmoid(c)
    o_ref[...] = jnp.dot(s, w_ref[...], precision=HIGHEST,
                         preferred_element_type=F32) + b_ref[...]


def _adaln(cond8, mod_w, mod_b):
    depth = mod_w.shape[0]
    nb = 1536
    out = pl.pallas_call(
        _adaln_kernel,
        grid=(depth, 6 * D // nb),
        in_specs=[pl.BlockSpec((8, D), lambda l, j: (0, 0)),
                  pl.BlockSpec((None, D, nb), lambda l, j: (l, 0, j)),
                  pl.BlockSpec((None, 1, nb), lambda l, j: (l, 0, j))],
        out_specs=pl.BlockSpec((None, 8, nb), lambda l, j: (l, 0, j)),
        out_shape=jax.ShapeDtypeStruct((depth, 8, 6 * D), F32),
        compiler_params=_cparams(("arbitrary", "arbitrary"), VMEM_LIMIT),
        name="adaln",
    )(cond8, mod_w, mod_b.reshape(depth, 1, 6 * D))
    return out.reshape(depth * 8, 6, D)


def _dft_cs(n):
    k = np.arange(n)
    ang = 2.0 * np.pi * np.outer(k, k) / n
    s = 1.0 / np.sqrt(n)
    return np.cos(ang) * s, np.sin(ang) * s


def _fourier_constants():
    cd, sd = _dft_cs(FGDIM)
    dch = np.concatenate([cd, sd], axis=1)
    cc, sc = _dft_cs(GRID_W)
    eye = np.eye(TM // GRID_W)
    kc, ks = np.kron(eye, cc), np.kron(eye, sc)
    m_lat = np.block([[kc, -ks], [ks, kc]])
    cp, sp = _dft_cs(CTX_LEN)
    m_ctx = np.block([[cp, -sp], [sp, cp]])
    mpos = np.stack([m_lat, m_ctx])
    cr, sr = _dft_cs(GRID_H)
    mrow = np.concatenate([cr, -sr], axis=1)
    return (jnp.asarray(dch, BF16), jnp.asarray(mpos, BF16), jnp.asarray(mrow, BF16))


def _fourier_in_kernel(x_ref, mod_ref, g_ref, win_ref, dch_ref, mpos_ref, vr_ref, vi_ref):
    x = x_ref[...]
    h = _rms(x, g_ref[...]) * (1.0 + mod_ref[1:2, :]) + mod_ref[0:1, :]
    u = jnp.dot(h.astype(BF16), win_ref[...], preferred_element_type=F32).astype(BF16)
    parts = [jnp.dot(u[:, g * FGDIM:(g + 1) * FGDIM], dch_ref[...],
                     preferred_element_type=F32) for g in range(FGROUPS)]
    uc = jnp.concatenate([p[:, :FGDIM] for p in parts], axis=1)
    us = jnp.concatenate([p[:, FGDIM:] for p in parts], axis=1)
    st = jnp.concatenate([uc, us], axis=0).astype(BF16)
    v = jnp.dot(mpos_ref[...], st, preferred_element_type=F32)
    vr_ref[...] = v[:TM]
    vi_ref[...] = v[TM:]


def _fourier_in(x_all, mods, layer, norm_g, w_in, dch, mpos):
    tile = pl.BlockSpec((TM, D), lambda i: (i, 0))
    return pl.pallas_call(
        _fourier_in_kernel,
        grid=(ALL_TILES,),
        in_specs=[tile,
                  pl.BlockSpec((None, 6, D), lambda i: (layer * 8 + _tile_cond(i), 0, 0)),
                  pl.BlockSpec((1, D), lambda i: (0, 0)),
                  pl.BlockSpec((D, D), lambda i: (0, 0)),
                  pl.BlockSpec((FGDIM, 2 * FGDIM), lambda i: (0, 0)),
                  pl.BlockSpec((None, 2 * TM, 2 * TM), lambda i: (jnp.where(i < LAT_TILES, 0, 1), 0, 0))],
        out_specs=[tile, tile],
        out_shape=[jax.ShapeDtypeStruct((NTOK, D), F32)] * 2,
        compiler_params=_cparams(("arbitrary",), VMEM_LIMIT),
        name="fourier_in",
    )(x_all, mods, norm_g.reshape(1, D), w_in.astype(BF16), dch, mpos)


CB = 8


def _fourier_out_lat_kernel(vr_ref, vi_ref, x_ref, mrow_ref, wout_ref, mod_ref, o_ref):
    g1 = mod_ref[2:3, :]
    for c in range(CB):
        st = jnp.concatenate([vr_ref[:, c, :], vi_ref[:, c, :]], axis=0).astype(BF16)
        yf = jnp.dot(mrow_ref[...], st, preferred_element_type=F32)
        y = jnp.dot(yf.astype(BF16), wout_ref[...], preferred_element_type=F32)
        o_ref[:, c, :] = x_ref[:, c, :] + g1 * y


def _fourier_out_ctx_kernel(yr_ref, x_ref, wout_ref, mod_ref, o_ref):
    y = jnp.dot(yr_ref[...].astype(BF16), wout_ref[...], preferred_element_type=F32)
    o_ref[...] = x_ref[...] + mod_ref[2:3, :] * y


def _fourier_out(vr, vi, x_all, mods, layer, w_out, mrow):
    wout = w_out.astype(BF16)
    rows = NTOK // GRID_W
    v3 = lambda a: a.reshape(rows, GRID_W, D)
    blk = pl.BlockSpec((GRID_H, CB, D), lambda b, c: (b, c, 0))
    x_new = pl.pallas_call(
        _fourier_out_lat_kernel,
        grid=(BATCH, GRID_W // CB),
        in_specs=[blk, blk, blk,
                  pl.BlockSpec((GRID_H, 2 * GRID_H), lambda b, c: (0, 0)),
                  pl.BlockSpec((D, D), lambda b, c: (0, 0)),
                  pl.BlockSpec((None, 6, D), lambda b, c: (layer * 8 + b, 0, 0))],
        out_specs=blk,
        out_shape=jax.ShapeDtypeStruct((rows, GRID_W, D), F32),
        input_output_aliases={2: 0},
        compiler_params=_cparams(("arbitrary", "arbitrary"), VMEM_LIMIT),
        name="fourier_out_lat",
    )(v3(vr), v3(vi), v3(x_all), mrow, wout, mods).reshape(NTOK, D)
    ctile = pl.BlockSpec((TM, D), lambda i: (LAT_TILES + i, 0))
    return pl.pallas_call(
        _fourier_out_ctx_kernel,
        grid=(NCTX // TM,),
        in_specs=[ctile, ctile,
                  pl.BlockSpec((D, D), lambda i: (0, 0)),
                  pl.BlockSpec((None, 6, D), lambda i: (layer * 8 + 2, 0, 0))],
        out_specs=ctile,
        out_shape=jax.ShapeDtypeStruct((NTOK, D), F32),
        input_output_aliases={1: 0},
        compiler_params=_cparams(("arbitrary",), VMEM_LIMIT),
        name="fourier_out_ctx",
    )(vr, x_new, wout, mods)


def _hgrn_in_kernel(x_ref, mod_ref, g_ref, win_ref, hlb_ref, q_ref, ff_ref, fb_ref, v_ref, gs_ref,
                    *, layer):
    x = x_ref[...]
    h = (_rms(x, g_ref[...]) * (1.0 + mod_ref[1:2, :]) + mod_ref[0:1, :]).astype(BF16)
    raw = [hlb_ref[l] for l in range(hlb_ref.shape[0])]
    mx = functools.reduce(jnp.maximum, raw)
    ex = [jnp.exp(r - mx) for r in raw]
    den = functools.reduce(lambda a, b: a + b, ex)
    soft = [e / den for e in ex]
    lb = functools.reduce(lambda a, b: a + b, soft[:layer + 1]) - soft[0]

    def proj(j):
        return jnp.dot(h, win_ref[:, j * D:(j + 1) * D], preferred_element_type=F32)

    q = proj(0)
    q_ref[...] = q * _sigmoid(q)
    ff_ref[...] = lb[0:1, :] + (1.0 - lb[0:1, :]) * _sigmoid(proj(1))
    fb_ref[...] = lb[1:2, :] + (1.0 - lb[1:2, :]) * _sigmoid(proj(2))
    v_ref[...] = proj(3)
    g = proj(4)
    gs_ref[...] = g * _sigmoid(g)


def _hgrn_in(x_all, mods, layer, norm_g, w_in, hlb):
    tile = pl.BlockSpec((TM, D), lambda i: (i, 0))
    depth = hlb.shape[0]
    return pl.pallas_call(
        functools.partial(_hgrn_in_kernel, layer=layer),
        grid=(ALL_TILES,),
        in_specs=[tile,
                  pl.BlockSpec((None, 6, D), lambda i: (layer * 8 + _tile_cond(i), 0, 0)),
                  pl.BlockSpec((1, D), lambda i: (0, 0)),
                  pl.BlockSpec((D, 5 * D), lambda i: (0, 0)),
                  pl.BlockSpec((depth, 2, D), lambda i: (0, 0, 0))],
        out_specs=[tile] * 5,
        out_shape=[jax.ShapeDtypeStruct((NTOK, D), F32)] * 5,
        compiler_params=_cparams(("arbitrary",), VMEM_LIMIT),
        name="hgrn_in",
    )(x_all, mods, norm_g.reshape(1, D), w_in.astype(BF16), hlb)


N_LEVELS = 7
SUB = 8
NGRP = CHUNK // SUB
FINE_LEVELS = 3


def _scan_kernel(q_ref, fg_ref, v_ref, o_ref, s_ref, *, rev):
    @pl.when(pl.program_id(1) == 0)
    def _():
        s_ref[...] = jnp.zeros_like(s_ref)

    q = q_ref[...]
    fg = fg_ref[...]
    k = 1.0 - fg
    vb = v_ref[...].astype(BF16)
    t = lax.broadcasted_iota(jnp.int32, (CHUNK, D), 0)
    ti = lax.broadcasted_iota(jnp.int32, (CHUNK, CHUNK), 0)
    si = lax.broadcasted_iota(jnp.int32, (CHUNK, CHUNK), 1)
    scores = [None] * HEADS

    def add_level(qs, ks, mask):
        qb, kb = qs.astype(BF16), ks.astype(BF16)
        for h in range(HEADS):
            sl = slice(h * HDIM, (h + 1) * HDIM)
            sc = lax.dot_general(qb[:, sl], kb[:, sl], _NT, preferred_element_type=F32)
            sc = jnp.where(mask, sc, 0.0)
            scores[h] = sc if scores[h] is None else scores[h] + sc

    add_level(q, k, ti == si)
    run = fg
    rest = jnp.ones_like(fg)
    tot = fg
    for l in range(FINE_LEVELS):
        hbit = 1 << l
        odd = (t & hbit) != 0
        far = (t & hbit) == 0 if rev else odd
        add_level(jnp.where(far, run, 0.0) * q, jnp.where(far, 0.0, rest) * k,
                  (ti >> (l + 1)) == (si >> (l + 1)))
        sib = jnp.where(odd, pltpu.roll(tot, hbit, 0), pltpu.roll(tot, CHUNK - hbit, 0))
        run = jnp.where(far, run * sib, run)
        rest = jnp.where(far, rest, rest * sib)
        tot = tot * sib

    def groups(a):
        return [a[b * SUB:(b + 1) * SUB, :] for b in range(NGRP)]

    q_g, k_g, run_g, rest_g, tot_g = (groups(a) for a in (q, k, run, rest, tot))
    sc_g = [groups(scores[h]) for h in range(HEADS)]
    lane = lax.broadcasted_iota(jnp.int32, (SUB, CHUNK), 1)
    zero_g = jnp.zeros((SUB, D), F32)
    for l in range(FINE_LEVELS, N_LEVELS):
        bit = 1 << (l - FINE_LEVELS)
        is_far = [((b & bit) == 0) == rev for b in range(NGRP)]
        far_groups = [b for b in range(NGRP) if is_far[b]]
        qb = jnp.concatenate([run_g[b] * q_g[b] for b in far_groups], axis=0).astype(BF16)
        kb = jnp.concatenate([zero_g if is_far[b] else rest_g[b] * k_g[b] for b in range(NGRP)],
                             axis=0).astype(BF16)
        span = 2 << l
        keep = [None if span == CHUNK else
                (lane >= b * SUB // span * span) & (lane < b * SUB // span * span + span)
                for b in far_groups]
        for h in range(HEADS):
            sl = slice(h * HDIM, (h + 1) * HDIM)
            sc = lax.dot_general(qb[:, sl], kb[:, sl], _NT, preferred_element_type=F32)
            for i, b in enumerate(far_groups):
                piece = sc[i * SUB:(i + 1) * SUB, :]
                if keep[i] is not None:
                    piece = jnp.where(keep[i], piece, 0.0)
                sc_g[h][b] = sc_g[h][b] + piece
        sib_g = [tot_g[b ^ bit] for b in range(NGRP)]
        run_g = [run_g[b] * sib_g[b] if is_far[b] else run_g[b] for b in range(NGRP)]
        rest_g = [rest_g[b] if is_far[b] else rest_g[b] * sib_g[b] for b in range(NGRP)]
        tot_g = [tot_g[b] * sib_g[b] for b in range(NGRP)]
    qin = (q * jnp.concatenate(run_g, axis=0)).astype(BF16)
    kst = (k * jnp.concatenate(rest_g, axis=0)).astype(BF16)
    tot_row = tot_g[0][0:1, :]
    for h in range(HEADS):
        sl = slice(h * HDIM, (h + 1) * HDIM)
        st = s_ref[h]
        sc = jnp.concatenate(sc_g[h], axis=0).astype(BF16)
        o_ref[:, sl] = (jnp.dot(sc, vb[:, sl], preferred_element_type=F32)
                        + lax.dot_general(qin[:, sl], st.astype(BF16), _NT,
                                          preferred_element_type=F32))
        s_ref[h] = st * tot_row[:, sl] + lax.dot_general(vb[:, sl], kst[:, sl], _TN,
                                                         preferred_element_type=F32)


LAT_CHUNKS = SEQ // CHUNK
CTX_CHUNKS = CTX_LEN // CHUNK
SCAN_STEPS = CTX_CHUNKS + LAT_CHUNKS


def _scan(q, fg, v, rev):
    def idx(b, s):
        if rev:
            c = jnp.where(s < CTX_CHUNKS, NLAT // CHUNK + CTX_CHUNKS * b + (CTX_CHUNKS - 1 - s),
                          LAT_CHUNKS * b + (SCAN_STEPS - 1 - s))
        else:
            c = jnp.where(s < CTX_CHUNKS, NLAT // CHUNK + CTX_CHUNKS * b + s,
                          LAT_CHUNKS * b + (s - CTX_CHUNKS))
        return (c, 0)
    blk = pl.BlockSpec((CHUNK, D), idx)
    return pl.pallas_call(
        functools.partial(_scan_kernel, rev=rev),
        grid=(BATCH, SCAN_STEPS),
        in_specs=[blk, blk, blk],
        out_specs=blk,
        out_shape=jax.ShapeDtypeStruct((NTOK, D), F32),
        scratch_shapes=[pltpu.VMEM((HEADS, HDIM, HDIM), F32)],
        compiler_params=_cparams(("arbitrary", "arbitrary"), VMEM_LIMIT),
        name="scan_bw" if rev else "scan_fw",
    )(q, fg, v)


def _hgrn_out_kernel(of_ref, ob_ref, gs_ref, x_ref, ng_ref, wout_ref, mod_ref, o_ref):
    o = of_ref[...] + ob_ref[...]
    parts = []
    for h in range(HEADS):
        oh = o[:, h * HDIM:(h + 1) * HDIM]
        parts.append(oh * lax.rsqrt(jnp.mean(oh * oh, axis=-1, keepdims=True) + NORM_EPS))
    on = jnp.concatenate(parts, axis=1) * ng_ref[...]
    y = jnp.dot((on * gs_ref[...]).astype(BF16), wout_ref[...], preferred_element_type=F32)
    o_ref[...] = x_ref[...] + mod_ref[2:3, :] * y


def _hgrn_out(o_fw, o_bw, gs, x_all, mods, layer, norm_g, w_out, n_tiles):
    tile = pl.BlockSpec((TM, D), lambda i: (i, 0))
    return pl.pallas_call(
        _hgrn_out_kernel,
        grid=(n_tiles,),
        in_specs=[tile, tile, tile, tile,
                  pl.BlockSpec((1, D), lambda i: (0, 0)),
                  pl.BlockSpec((D, D), lambda i: (0, 0)),
                  pl.BlockSpec((None, 6, D), lambda i: (layer * 8 + _tile_cond(i), 0, 0))],
        out_specs=tile,
        out_shape=jax.ShapeDtypeStruct((n_tiles * TM, D), F32),
        compiler_params=_cparams(("arbitrary",), VMEM_LIMIT),
        name="hgrn_out",
    )(o_fw, o_bw, gs, x_all, norm_g.reshape(1, D), w_out.astype(BF16), mods)


def _to_row_tiled(ref, val):
    for j in range(RT):
        ref[pl.ds(j, val.shape[0], stride=RT), :] = val[:, j * LANES:(j + 1) * LANES]


def _from_row_tiled(ref, n, base=0):
    return [ref[pl.ds(base + j, n, stride=RT), :] for j in range(RT)]


def _route_kernel(x_ref, mod_ref, g_ref, wrt_ref, brt_ref, h_ref, e_ref, rank_ref, gate_ref, cnt_ref):
    @pl.when(pl.program_id(0) == 0)
    def _():
        cnt_ref[...] = jnp.zeros_like(cnt_ref)

    h = _rms(x_ref[...], g_ref[...]) * (1.0 + mod_ref[4:5, :]) + mod_ref[3:4, :]
    _to_row_tiled(h_ref, h)
    logits = lax.dot_general(wrt_ref[...], h, _NT, precision=HIGHEST,
                             preferred_element_type=F32) + brt_ref[:, 0:1]
    row = lax.broadcasted_iota(jnp.int32, (N_EXPERTS, TM), 0).astype(F32)
    vals = logits
    sel = jnp.zeros((N_EXPERTS, TM), F32)
    tops, idxs = [], []
    for _ in range(TOP_K):
        m = jnp.max(vals, axis=0, keepdims=True)
        idx = jnp.min(jnp.where(vals == m, row, float(N_EXPERTS)), axis=0, keepdims=True)
        hit = row == idx
        vals = jnp.where(hit, -jnp.inf, vals)
        sel = jnp.where(hit, 1.0, sel)
        tops.append(m)
        idxs.append(idx)
    ex = [jnp.exp(m - tops[0]) for m in tops]
    den = ex[0] + ex[1] + ex[2] + ex[3]
    r = lax.broadcasted_iota(jnp.int32, (TM, TM), 0)
    c = lax.broadcasted_iota(jnp.int32, (TM, TM), 1)
    before = jnp.where(r < c, 1.0, 0.0).astype(BF16)
    pref = jnp.dot(sel.astype(BF16), before, preferred_element_type=F32) + cnt_ref[:, 0:1]
    slot = lax.broadcasted_iota(jnp.int32, (8, TM), 0)
    e_out = jnp.zeros((8, TM), F32)
    rank_out = jnp.zeros((8, TM), F32)
    gate_out = jnp.zeros((8, TM), F32)
    for kk in range(TOP_K):
        rank = jnp.sum(jnp.where(row == idxs[kk], pref, 0.0), axis=0, keepdims=True)
        e_out = jnp.where(slot == kk, idxs[kk], e_out)
        rank_out = jnp.where(slot == kk, rank, rank_out)
        gate_out = jnp.where(slot == kk, ex[kk] / den, gate_out)
    e_ref[...] = e_out.astype(jnp.int32)
    rank_ref[...] = rank_out.astype(jnp.int32)
    gate_ref[...] = gate_out
    cnt_ref[...] += jnp.sum(sel, axis=1, keepdims=True)


def _route(x_all, mods, layer, norm_g, w_r, b_r, n_tiles):
    tile = pl.BlockSpec((TM, D), lambda i: (i, 0))
    small = pl.BlockSpec((None, 8, TM), lambda i: (i, 0, 0))
    n = n_tiles * TM
    return pl.pallas_call(
        _route_kernel,
        grid=(n_tiles,),
        in_specs=[tile,
                  pl.BlockSpec((None, 6, D), lambda i: (layer * 8 + _tile_cond(i), 0, 0)),
                  pl.BlockSpec((1, D), lambda i: (0, 0)),
                  pl.BlockSpec((N_EXPERTS, D), lambda i: (0, 0)),
                  pl.BlockSpec((N_EXPERTS, LANES), lambda i: (0, 0))],
        out_specs=[pl.BlockSpec((TM * RT, LANES), lambda i: (i, 0)), small, small, small,
                   pl.BlockSpec((N_EXPERTS, LANES), lambda i: (0, 0))],
        out_shape=[jax.ShapeDtypeStruct((n * RT, LANES), F32),
                   jax.ShapeDtypeStruct((n_tiles, 8, TM), jnp.int32),
                   jax.ShapeDtypeStruct((n_tiles, 8, TM), jnp.int32),
                   jax.ShapeDtypeStruct((n_tiles, 8, TM), F32),
                   jax.ShapeDtypeStruct((N_EXPERTS, LANES), F32)],
        compiler_params=_cparams(("arbitrary",), VMEM_LIMIT),
        name="route",
    )(x_all, mods, norm_g.reshape(1, D), w_r.T, jnp.broadcast_to(b_r[:, None], (N_EXPERTS, LANES)))


def _dest_kernel(ps_ref, e_ref, rank_ref, d_ref):
    e = e_ref[...]
    acc = rank_ref[...]
    for j in range(N_EXPERTS):
        acc = acc + jnp.where(e == j, ps_ref[j], 0)
    d_ref[...] = acc


def _dest(pad_start, e, rank):
    full = pl.BlockSpec(e.shape, lambda i, ps: (0, 0, 0))
    return pl.pallas_call(
        _dest_kernel,
        grid_spec=pltpu.PrefetchScalarGridSpec(num_scalar_prefetch=1, grid=(1,),
                                               in_specs=[full, full], out_specs=full),
        out_shape=jax.ShapeDtypeStruct(e.shape, jnp.int32),
        name="dest",
    )(pad_start, e, rank)


PAD_PIECES = tuple(1 << p for p in reversed(range(BM.bit_length() - 1)))


def _zero_fill(lo_ref, hi_ref, xb_hbm, zeros, sem, n_blocks, wait):
    def go(rows, row0):
        cp = pltpu.make_async_copy(zeros.at[pl.ds(0, rows * RT), :],
                                   xb_hbm.at[pl.ds(row0 * RT, rows * RT), :], sem.at[1])
        cp.wait() if wait else cp.start()

    def per_expert(e, carry):
        row = lo_ref[e]
        n = hi_ref[e] - row
        for piece in PAD_PIECES:
            @pl.when((n & piece) != 0)
            def _():
                go(piece, row)
            row = row + (n & piece)
        return carry
    lax.fori_loop(0, N_EXPERTS, per_expert, 0)

    def per_block(b, carry):
        go(BM, b * BM)
        return carry
    lax.fori_loop(hi_ref[N_EXPERTS - 1] // BM, n_blocks, per_block, 0)


def _dispatch_kernel(lo_ref, hi_ref, dest_ref, h_ref, xb_hbm, zeros, sem, *, n_blocks):
    i = pl.program_id(0)

    @pl.when(i == 0)
    def _():
        zeros[...] = jnp.zeros_like(zeros)
        _zero_fill(lo_ref, hi_ref, xb_hbm, zeros, sem, n_blocks, wait=False)

    def body(t, carry):
        src = h_ref.at[pl.ds(t * RT, RT), :]
        for kk in range(TOP_K):
            d = dest_ref[0, 0, kk * TM + t]
            pltpu.make_async_copy(src, xb_hbm.at[pl.ds(d * RT, RT), :],
                                  sem.at[0]).start(priority=kk % 2)
        return carry
    lax.fori_loop(0, TM, body, 0, unroll=4)
    for kk in range(TOP_K):
        pltpu.make_async_copy(h_ref, xb_hbm.at[pl.ds(0, TM * RT), :], sem.at[0]).wait()

    @pl.when(i == pl.num_programs(0) - 1)
    def _():
        _zero_fill(lo_ref, hi_ref, xb_hbm, zeros, sem, n_blocks, wait=True)


def _dispatch(h_rt, dest, pad_lo, pad_hi, n_blocks):
    n_tiles = dest.shape[0]
    dest3 = dest[:, :TOP_K, :].reshape(n_tiles, 1, TOP_K * TM)
    grid_spec = pltpu.PrefetchScalarGridSpec(
        num_scalar_prefetch=2,
        grid=(n_tiles,),
        in_specs=[pl.BlockSpec((1, 1, TOP_K * TM), lambda i, lo, hi: (i, 0, 0),
                               memory_space=pltpu.SMEM),
                  pl.BlockSpec((TM * RT, LANES), lambda i, lo, hi: (i, 0))],
        out_specs=pl.BlockSpec(memory_space=pl.ANY),
        scratch_shapes=[pltpu.VMEM((BM * RT, LANES), F32),
                        pltpu.SemaphoreType.DMA((2,))],
    )
    xb = pl.pallas_call(
        functools.partial(_dispatch_kernel, n_blocks=n_blocks),
        grid_spec=grid_spec,
        out_shape=jax.ShapeDtypeStruct((n_blocks * BM * RT, LANES), F32),
        compiler_params=_cparams(("arbitrary",), VMEM_LIMIT, disable_bounds_checks=True),
        name="dispatch",
    )(pad_lo, pad_hi, dest3, h_rt)
    return xb, dest3


def _expert_kernel(bexp_ref, nused_ref, xb_ref, w1_ref, b1_ref, w2_ref, b2_ref, y_ref, w1b, w2b):
    i = pl.program_id(0)
    n_used = nused_ref[0]

    @pl.when(jnp.logical_or(i == 0, bexp_ref[i] != bexp_ref[jnp.maximum(i - 1, 0)]))
    def _():
        w1b[...] = w1_ref[...].astype(BF16)
        w2b[...] = w2_ref[...].astype(BF16)

    @pl.when(i < n_used)
    def _():
        x = jnp.concatenate(_from_row_tiled(xb_ref, BM), axis=1).astype(BF16)
        u = jnp.dot(x, w1b[...], preferred_element_type=F32) + b1_ref[...]
        glu = jnp.minimum(u[:, :D_FF], SWIGLU_LIMIT)
        lin = jnp.clip(u[:, D_FF:], -SWIGLU_LIMIT, SWIGLU_LIMIT)
        act = glu * _sigmoid(SWIGLU_ALPHA * glu) * (lin + 1.0)
        y = jnp.dot(act.astype(BF16), w2b[...], preferred_element_type=F32) + b2_ref[...]
        _to_row_tiled(y_ref, y)

    @pl.when(i >= n_used)
    def _():
        y_ref[...] = jnp.zeros_like(y_ref)


def _experts(xb, block_exp, n_used, layer, w1, b1, w2, b2):
    n_blocks = xb.shape[0] // (BM * RT)
    grid_spec = pltpu.PrefetchScalarGridSpec(
        num_scalar_prefetch=2,
        grid=(n_blocks,),
        in_specs=[
            pl.BlockSpec((BM * RT, LANES), lambda i, be, nu: (jnp.minimum(i, nu[0] - 1), 0)),
            pl.BlockSpec((None, None, D, 2 * D_FF), lambda i, be, nu: (layer, be[i], 0, 0)),
            pl.BlockSpec((None, 1, 2 * D_FF), lambda i, be, nu: (layer * N_EXPERTS + be[i], 0, 0)),
            pl.BlockSpec((None, None, D_FF, D), lambda i, be, nu: (layer, be[i], 0, 0)),
            pl.BlockSpec((None, 1, D), lambda i, be, nu: (layer * N_EXPERTS + be[i], 0, 0)),
        ],
        out_specs=pl.BlockSpec((BM * RT, LANES), lambda i, be, nu: (i, 0)),
        scratch_shapes=[pltpu.VMEM((D, 2 * D_FF), BF16),
                        pltpu.VMEM((D_FF, D), BF16)],
    )
    return pl.pallas_call(
        _expert_kernel,
        grid_spec=grid_spec,
        out_shape=jax.ShapeDtypeStruct(xb.shape, F32),
        compiler_params=_cparams(("arbitrary",), VMEM_LIMIT),
        name="experts",
    )(block_exp, n_used, xb, w1, b1.reshape(-1, 1, 2 * D_FF), w2, b2.reshape(-1, 1, D))


def _combine_kernel(dest_ref, dest_next_ref, y_hbm, gate_ref, x_ref, mod_ref, fg_ref, o_ref, buf, sem,
                    *, final_norm):
    i = pl.program_id(0)
    n = pl.num_programs(0)
    slot = i % 2
    rows = TOP_K * TM

    def gather(dest, to_slot):
        def body(p, carry):
            for half in range(2):
                r = 2 * p + half
                pltpu.make_async_copy(y_hbm.at[pl.ds(dest[0, 0, r] * RT, RT), :],
                                      buf.at[to_slot, pl.ds(r * RT, RT), :],
                                      sem.at[to_slot]).start(priority=half)
            return carry
        lax.fori_loop(0, rows // 2, body, 0, unroll=4)

    @pl.when(i == 0)
    def _():
        gather(dest_ref, 0)

    @pl.when(i + 1 < n)
    def _():
        gather(dest_next_ref, 1 - slot)

    pltpu.make_async_copy(y_hbm.at[pl.ds(0, rows * RT), :], buf.at[slot], sem.at[slot]).wait()
    eye = (lax.broadcasted_iota(jnp.int32, (8, 8), 0)
           == lax.broadcasted_iota(jnp.int32, (8, 8), 1)).astype(F32)
    gate = lax.dot_general(gate_ref[...], eye, _TN, precision=HIGHEST,
                           preferred_element_type=F32)
    cur = buf.at[slot]
    outs = []
    for j in range(RT):
        acc = None
        for kk in range(TOP_K):
            piece = gate[:, kk:kk + 1] * cur[pl.ds(kk * TM * RT + j, TM, stride=RT), :]
            acc = piece if acc is None else acc + piece
        outs.append(acc)
    out = x_ref[...] + mod_ref[5:6, :] * jnp.concatenate(outs, axis=1)
    if final_norm:
        out = _rms(out, fg_ref[...])
    o_ref[...] = out


def _combine(yb, dest3, gate, x_all, mods, layer, final_g, n_tiles, final_norm):
    last = n_tiles - 1
    tile = pl.BlockSpec((TM, D), lambda i: (i, 0))
    return pl.pallas_call(
        functools.partial(_combine_kernel, final_norm=final_norm),
        grid=(n_tiles,),
        in_specs=[pl.BlockSpec((1, 1, TOP_K * TM), lambda i: (i, 0, 0), memory_space=pltpu.SMEM),
                  pl.BlockSpec((1, 1, TOP_K * TM), lambda i: (jnp.minimum(i + 1, last), 0, 0),
                               memory_space=pltpu.SMEM),
                  pl.BlockSpec(memory_space=pl.ANY),
                  pl.BlockSpec((None, 8, TM), lambda i: (i, 0, 0)),
                  tile,
                  pl.BlockSpec((None, 6, D), lambda i: (layer * 8 + _tile_cond(i), 0, 0)),
                  pl.BlockSpec((1, D), lambda i: (0, 0))],
        out_specs=tile,
        out_shape=jax.ShapeDtypeStruct((n_tiles * TM, D), F32),
        scratch_shapes=[pltpu.VMEM((2, TOP_K * TM * RT, LANES), F32),
                        pltpu.SemaphoreType.DMA((2,))],
        compiler_params=_cparams(("arbitrary",), VMEM_LIMIT, disable_bounds_checks=True),
        name="combine",
    )(dest3, dest3, yb, gate, x_all, mods, final_g.reshape(1, D))


def _moe(x_all, mods, layer, norm_g, w_r, b_r, w1, b1, w2, b2, final_g, n_tiles, final_norm):
    n = n_tiles * TM
    h_rt, e, rank, gate, cnt = _route(x_all, mods, layer, norm_g, w_r, b_r, n_tiles)
    counts = cnt[:, 0].astype(jnp.int32)
    padded = (counts + BM - 1) // BM * BM
    pad_end = jnp.cumsum(padded)
    pad_start = pad_end - padded
    n_blocks = -(-(n * TOP_K) // BM) + N_EXPERTS
    starts = jnp.arange(n_blocks, dtype=jnp.int32) * BM
    block_exp = jnp.minimum(jnp.sum((pad_end[None, :] <= starts[:, None]).astype(jnp.int32), axis=1),
                            N_EXPERTS - 1)
    n_used = pad_end[-1:] // BM
    dest = _dest(pad_start, e, rank)
    xb, dest3 = _dispatch(h_rt, dest, pad_start + counts, pad_end, n_blocks)
    yb = _experts(xb, block_exp, n_used, layer, w1, b1, w2, b2)
    return _combine(yb, dest3, gate, x_all, mods, layer, final_g, n_tiles, final_norm)


def kernel(x, c, ctx, c_ctx, mod_w, mod_b, norm1_g, norm2_g, fourier_w_in, fourier_w_out,
           hgrn_w_in, hgrn_lower_bounds, hgrn_norm_g, hgrn_w_out, router_w, router_b,
           expert_w1, expert_b1, expert_w2, expert_b2, final_norm_g):
    assert x.shape == (BATCH, SEQ, D) and ctx.shape == (BATCH, CTX_LEN, D)
    cond8 = jnp.zeros((8, D), F32).at[:BATCH].set(c).at[BATCH].set(c_ctx)
    mods = _adaln(cond8, mod_w, mod_b)
    x_all = jnp.concatenate([x.reshape(NLAT, D), ctx.reshape(NCTX, D)], axis=0)
    dch, mpos, mrow = _fourier_constants()
    experts = (expert_w1, expert_b1, expert_w2, expert_b2)

    vr, vi = _fourier_in(x_all, mods, 0, norm1_g[0], fourier_w_in[0], dch, mpos)
    x_all = _fourier_out(vr, vi, x_all, mods, 0, fourier_w_out[0], mrow)
    x_all = _moe(x_all, mods, 0, norm2_g[0], router_w[0], router_b[0], *experts,
                 final_norm_g, ALL_TILES, False)

    q, ff, fb, v, gs = _hgrn_in(x_all, mods, 1, norm1_g[1], hgrn_w_in[0], hgrn_lower_bounds)
    o_fw = _scan(q, ff, v, False)
    o_bw = _scan(q, fb, v, True)
    x_lat = _hgrn_out(o_fw, o_bw, gs, x_all, mods, 1, hgrn_norm_g[0], hgrn_w_out[0], LAT_TILES)
    out = _moe(x_lat, mods, 1, norm2_g[1], router_w[1], router_b[1], *experts,
               final_norm_g, LAT_TILES, True)
    return out.reshape(BATCH, SEQ, D)
```

```python
import functools

import numpy as np
import jax
import jax.numpy as jnp
from jax import lax
from jax.experimental import pallas as pl
from jax.experimental.pallas import tpu as pltpu

F32 = jnp.float32
BF16 = jnp.bfloat16
HIGHEST = lax.Precision.HIGHEST

D = 1024
BATCH = 2
SEQ = 8192
CTX_LEN = 256
GRID_W = 64
GRID_H = SEQ // GRID_W
NLAT = BATCH * SEQ
NCTX = BATCH * CTX_LEN
NTOK = NLAT + NCTX
TM = 256
LAT_TILES = NLAT // TM
ALL_TILES = NTOK // TM
TILES_PER_BATCH = SEQ // TM
FGROUPS = 4
FGDIM = D // FGROUPS
HEADS = 8
HDIM = D // HEADS
CHUNK = 128
N_EXPERTS = 32
TOP_K = 4
D_FF = 1024
SWIGLU_ALPHA = 1.702
SWIGLU_LIMIT = 7.0
BM = 512
LANES = 128
RT = D // LANES
NORM_EPS = 1e-6
VMEM_LIMIT = 56 * 1024 * 1024

_NT = (((1,), (1,)), ((), ()))
_TN = (((0,), (0,)), ((), ()))


def _cparams(sem, vmem=None, **kw):
    return pltpu.CompilerParams(dimension_semantics=sem, vmem_limit_bytes=vmem, **kw)


def _sigmoid(x):
    return 1.0 / (1.0 + jnp.exp(-x))


def _rms(x, g):
    return x * lax.rsqrt(jnp.mean(x * x, axis=-1, keepdims=True) + NORM_EPS) * g


def _tile_cond(i):
    return jnp.where(i < LAT_TILES, i // TILES_PER_BATCH, 2)


def _adaln_kernel(cond_ref, w_ref, b_ref, o_ref):
    c = cond_ref[...]
    s = c * _sigmoid(c)
    o_ref[...] = jnp.dot(s, w_ref[...], precision=HIGHEST,
                         preferred_element_type=F32) + b_ref[...]


def _adaln(cond8, mod_w, mod_b):
    depth = mod_w.shape[0]
    nb = 1536
    out = pl.pallas_call(
        _adaln_kernel,
        grid=(depth, 6 * D // nb),
        in_specs=[pl.BlockSpec((8, D), lambda l, j: (0, 0)),
                  pl.BlockSpec((None, D, nb), lambda l, j: (l, 0, j)),
                  pl.BlockSpec((None, 1, nb), lambda l, j: (l, 0, j))],
        out_specs=pl.BlockSpec((None, 8, nb), lambda l, j: (l, 0, j)),
        out_shape=jax.ShapeDtypeStruct((depth, 8, 6 * D), F32),
        compiler_params=_cparams(("arbitrary", "arbitrary"), VMEM_LIMIT),
        name="adaln",
    )(cond8, mod_w, mod_b.reshape(depth, 1, 6 * D))
    return out.reshape(depth * 8, 6, D)


def _dft_cs(n):
    k = np.arange(n)
    ang = 2.0 * np.pi * np.outer(k, k) / n
    s = 1.0 / np.sqrt(n)
    return np.cos(ang) * s, np.sin(ang) * s


def _fourier_constants():
    cd, sd = _dft_cs(FGDIM)
    dch = np.concatenate([cd, sd], axis=1)
    cc, sc = _dft_cs(GRID_W)
    eye = np.eye(TM // GRID_W)
    kc, ks = np.kron(eye, cc), np.kron(eye, sc)
    m_lat = np.block([[kc, -ks], [ks, kc]])
    cp, sp = _dft_cs(CTX_LEN)
    m_ctx = np.block([[cp, -sp], [sp, cp]])
    mpos = np.stack([m_lat, m_ctx])
    cr, sr = _dft_cs(GRID_H)
    mrow = np.concatenate([cr, -sr], axis=1)
    return (jnp.asarray(dch, BF16), jnp.asarray(mpos, BF16), jnp.asarray(mrow, BF16))


def _fourier_in_kernel(x_ref, mod_ref, g_ref, win_ref, dch_ref, mpos_ref, vr_ref, vi_ref):
    x = x_ref[...]
    h = _rms(x, g_ref[...]) * (1.0 + mod_ref[1:2, :]) + mod_ref[0:1, :]
    u = jnp.dot(h.astype(BF16), win_ref[...], preferred_element_type=F32).astype(BF16)
    parts = [jnp.dot(u[:, g * FGDIM:(g + 1) * FGDIM], dch_ref[...],
                     preferred_element_type=F32) for g in range(FGROUPS)]
    uc = jnp.concatenate([p[:, :FGDIM] for p in parts], axis=1)
    us = jnp.concatenate([p[:, FGDIM:] for p in parts], axis=1)
    st = jnp.concatenate([uc, us], axis=0).astype(BF16)
    v = jnp.dot(mpos_ref[...], st, preferred_element_type=F32)
    vr_ref[...] = v[:TM]
    vi_ref[...] = v[TM:]


def _fourier_in(x_all, mods, layer, norm_g, w_in, dch, mpos):
    tile = pl.BlockSpec((TM, D), lambda i: (i, 0))
    return pl.pallas_call(
        _fourier_in_kernel,
        grid=(ALL_TILES,),
        in_specs=[tile,
                  pl.BlockSpec((None, 6, D), lambda i: (layer * 8 + _tile_cond(i), 0, 0)),
                  pl.BlockSpec((1, D), lambda i: (0, 0)),
                  pl.BlockSpec((D, D), lambda i: (0, 0)),
                  pl.BlockSpec((FGDIM, 2 * FGDIM), lambda i: (0, 0)),
                  pl.BlockSpec((None, 2 * TM, 2 * TM), lambda i: (jnp.where(i < LAT_TILES, 0, 1), 0, 0))],
        out_specs=[tile, tile],
        out_shape=[jax.ShapeDtypeStruct((NTOK, D), F32)] * 2,
        compiler_params=_cparams(("arbitrary",), VMEM_LIMIT),
        name="fourier_in",
    )(x_all, mods, norm_g.reshape(1, D), w_in.astype(BF16), dch, mpos)


CB = 8


def _fourier_out_lat_kernel(vr_ref, vi_ref, x_ref, mrow_ref, wout_ref, mod_ref, o_ref):
    g1 = mod_ref[2:3, :]
    for c in range(CB):
        st = jnp.concatenate([vr_ref[:, c, :], vi_ref[:, c, :]], axis=0).astype(BF16)
        yf = jnp.dot(mrow_ref[...], st, preferred_element_type=F32)
        y = jnp.dot(yf.astype(BF16), wout_ref[...], preferred_element_type=F32)
        o_ref[:, c, :] = x_ref[:, c, :] + g1 * y


def _fourier_out_ctx_kernel(yr_ref, x_ref, wout_ref, mod_ref, o_ref):
    y = jnp.dot(yr_ref[...].astype(BF16), wout_ref[...], preferred_element_type=F32)
    o_ref[...] = x_ref[...] + mod_ref[2:3, :] * y


def _fourier_out(vr, vi, x_all, mods, layer, w_out, mrow):
    wout = w_out.astype(BF16)
    rows = NTOK // GRID_W
    v3 = lambda a: a.reshape(rows, GRID_W, D)
    blk = pl.BlockSpec((GRID_H, CB, D), lambda b, c: (b, c, 0))
    x_new = pl.pallas_call(
        _fourier_out_lat_kernel,
        grid=(BATCH, GRID_W // CB),
        in_specs=[blk, blk, blk,
                  pl.BlockSpec((GRID_H, 2 * GRID_H), lambda b, c: (0, 0)),
                  pl.BlockSpec((D, D), lambda b, c: (0, 0)),
                  pl.BlockSpec((None, 6, D), lambda b, c: (layer * 8 + b, 0, 0))],
        out_specs=blk,
        out_shape=jax.ShapeDtypeStruct((rows, GRID_W, D), F32),
        input_output_aliases={2: 0},
        compiler_params=_cparams(("arbitrary", "arbitrary"), VMEM_LIMIT),
        name="fourier_out_lat",
    )(v3(vr), v3(vi), v3(x_all), mrow, wout, mods).reshape(NTOK, D)
    ctile = pl.BlockSpec((TM, D), lambda i: (LAT_TILES + i, 0))
    return pl.pallas_call(
        _fourier_out_ctx_kernel,
        grid=(NCTX // TM,),
        in_specs=[ctile, ctile,
                  pl.BlockSpec((D, D), lambda i: (0, 0)),
                  pl.BlockSpec((None, 6, D), lambda i: (layer * 8 + 2, 0, 0))],
        out_specs=ctile,
        out_shape=jax.ShapeDtypeStruct((NTOK, D), F32),
        input_output_aliases={1: 0},
        compiler_params=_cparams(("arbitrary",), VMEM_LIMIT),
        name="fourier_out_ctx",
    )(vr, x_new, wout, mods)


def _hgrn_in_kernel(x_ref, mod_ref, g_ref, win_ref, hlb_ref, q_ref, ff_ref, fb_ref, v_ref, gs_ref,
                    *, layer):
    x = x_ref[...]
    h = (_rms(x, g_ref[...]) * (1.0 + mod_ref[1:2, :]) + mod_ref[0:1, :]).astype(BF16)
    raw = [hlb_ref[l] for l in range(hlb_ref.shape[0])]
    mx = functools.reduce(jnp.maximum, raw)
    ex = [jnp.exp(r - mx) for r in raw]
    den = functools.reduce(lambda a, b: a + b, ex)
    soft = [e / den for e in ex]
    lb = functools.reduce(lambda a, b: a + b, soft[:layer + 1]) - soft[0]

    def proj(j):
        return jnp.dot(h, win_ref[:, j * D:(j + 1) * D], preferred_element_type=F32)

    q = proj(0)
    q_ref[...] = q * _sigmoid(q)
    ff_ref[...] = lb[0:1, :] + (1.0 - lb[0:1, :]) * _sigmoid(proj(1))
    fb_ref[...] = lb[1:2, :] + (1.0 - lb[1:2, :]) * _sigmoid(proj(2))
    v_ref[...] = proj(3)
    g = proj(4)
    gs_ref[...] = g * _sigmoid(g)


def _hgrn_in(x_all, mods, layer, norm_g, w_in, hlb):
    tile = pl.BlockSpec((TM, D), lambda i: (i, 0))
    depth = hlb.shape[0]
    return pl.pallas_call(
        functools.partial(_hgrn_in_kernel, layer=layer),
        grid=(ALL_TILES,),
        in_specs=[tile,
                  pl.BlockSpec((None, 6, D), lambda i: (layer * 8 + _tile_cond(i), 0, 0)),
                  pl.BlockSpec((1, D), lambda i: (0, 0)),
                  pl.BlockSpec((D, 5 * D), lambda i: (0, 0)),
                  pl.BlockSpec((depth, 2, D), lambda i: (0, 0, 0))],
        out_specs=[tile] * 5,
        out_shape=[jax.ShapeDtypeStruct((NTOK, D), F32)] * 5,
        compiler_params=_cparams(("arbitrary",), VMEM_LIMIT),
        name="hgrn_in",
    )(x_all, mods, norm_g.reshape(1, D), w_in.astype(BF16), hlb)


N_LEVELS = 7
SUB = 8
NGRP = CHUNK // SUB
FINE_LEVELS = 3


def _scan_kernel(q_ref, fg_ref, v_ref, o_ref, s_ref, *, rev):
    @pl.when(pl.program_id(1) == 0)
    def _():
        s_ref[...] = jnp.zeros_like(s_ref)

    q = q_ref[...]
    fg = fg_ref[...]
    k = 1.0 - fg
    vb = v_ref[...].astype(BF16)
    t = lax.broadcasted_iota(jnp.int32, (CHUNK, D), 0)
    ti = lax.broadcasted_iota(jnp.int32, (CHUNK, CHUNK), 0)
    si = lax.broadcasted_iota(jnp.int32, (CHUNK, CHUNK), 1)
    scores = [None] * HEADS

    def add_level(qs, ks, mask):
        qb, kb = qs.astype(BF16), ks.astype(BF16)
        for h in range(HEADS):
            sl = slice(h * HDIM, (h + 1) * HDIM)
            sc = lax.dot_general(qb[:, sl], kb[:, sl], _NT, preferred_element_type=F32)
            sc = jnp.where(mask, sc, 0.0)
            scores[h] = sc if scores[h] is None else scores[h] + sc

    add_level(q, k, ti == si)
    run = fg
    rest = jnp.ones_like(fg)
    tot = fg
    for l in range(FINE_LEVELS):
        hbit = 1 << l
        odd = (t & hbit) != 0
        far = (t & hbit) == 0 if rev else odd
        t_far = (ti & hbit) == 0 if rev else (ti & hbit) != 0
        s_near = (si & hbit) != 0 if rev else (si & hbit) == 0
        add_level(run * q, rest * k,
                  ((ti >> (l + 1)) == (si >> (l + 1))) & t_far & s_near)
        sib = jnp.where(odd, pltpu.roll(tot, hbit, 0), pltpu.roll(tot, CHUNK - hbit, 0))
        run = jnp.where(far, run * sib, run)
        rest = jnp.where(far, rest, rest * sib)
        tot = tot * sib

    def groups(a):
        return [a[b * SUB:(b + 1) * SUB, :] for b in range(NGRP)]

    q_g, k_g, run_g, rest_g, tot_g = (groups(a) for a in (q, k, run, rest, tot))
    sc_g = [groups(scores[h]) for h in range(HEADS)]
    lane = lax.broadcasted_iota(jnp.int32, (SUB, CHUNK), 1)
    zero_g = jnp.zeros((SUB, D), F32)
    for l in range(FINE_LEVELS, N_LEVELS):
        bit = 1 << (l - FINE_LEVELS)
        is_far = [((b & bit) == 0) == rev for b in range(NGRP)]
        far_groups = [b for b in range(NGRP) if is_far[b]]
        qb = jnp.concatenate([run_g[b] * q_g[b] for b in far_groups], axis=0).astype(BF16)
        kb = jnp.concatenate([zero_g if is_far[b] else rest_g[b] * k_g[b] for b in range(NGRP)],
                             axis=0).astype(BF16)
        span = 2 << l
        keep = [None if span == CHUNK else
                (lane >= b * SUB // span * span) & (lane < b * SUB // span * span + span)
                for b in far_groups]
        for h in range(HEADS):
            sl = slice(h * HDIM, (h + 1) * HDIM)
            sc = lax.dot_general(qb[:, sl], kb[:, sl], _NT, preferred_element_type=F32)
            for i, b in enumerate(far_groups):
                piece = sc[i * SUB:(i + 1) * SUB, :]
                if keep[i] is not None:
                    piece = jnp.where(keep[i], piece, 0.0)
                sc_g[h][b] = sc_g[h][b] + piece
        sib_g = [tot_g[b ^ bit] for b in range(NGRP)]
        run_g = [run_g[b] * sib_g[b] if is_far[b] else run_g[b] for b in range(NGRP)]
        rest_g = [rest_g[b] if is_far[b] else rest_g[b] * sib_g[b] for b in range(NGRP)]
        tot_g = [tot_g[b] * sib_g[b] for b in range(NGRP)]
    qin = (q * jnp.concatenate(run_g, axis=0)).astype(BF16)
    kst = (k * jnp.concatenate(rest_g, axis=0)).astype(BF16)
    tot_row = tot_g[0][0:1, :]
    for h in range(HEADS):
        sl = slice(h * HDIM, (h + 1) * HDIM)
        st = s_ref[h]
        sc = jnp.concatenate(sc_g[h], axis=0).astype(BF16)
        o_ref[:, sl] = (jnp.dot(sc, vb[:, sl], preferred_element_type=F32)
                        + lax.dot_general(qin[:, sl], st.astype(BF16), _NT,
                                          preferred_element_type=F32))
        s_ref[h] = st * tot_row[:, sl] + lax.dot_general(vb[:, sl], kst[:, sl], _TN,
                                                         preferred_element_type=F32)


LAT_CHUNKS = SEQ // CHUNK
CTX_CHUNKS = CTX_LEN // CHUNK
SCAN_STEPS = CTX_CHUNKS + LAT_CHUNKS


def _scan(q, fg, v, rev):
    def idx(b, s):
        if rev:
            c = jnp.where(s < CTX_CHUNKS, NLAT // CHUNK + CTX_CHUNKS * b + (CTX_CHUNKS - 1 - s),
                          LAT_CHUNKS * b + (SCAN_STEPS - 1 - s))
        else:
            c = jnp.where(s < CTX_CHUNKS, NLAT // CHUNK + CTX_CHUNKS * b + s,
                          LAT_CHUNKS * b + (s - CTX_CHUNKS))
        return (c, 0)
    blk = pl.BlockSpec((CHUNK, D), idx)
    return pl.pallas_call(
        functools.partial(_scan_kernel, rev=rev),
        grid=(BATCH, SCAN_STEPS),
        in_specs=[blk, blk, blk],
        out_specs=blk,
        out_shape=jax.ShapeDtypeStruct((NTOK, D), F32),
        scratch_shapes=[pltpu.VMEM((HEADS, HDIM, HDIM), F32)],
        compiler_params=_cparams(("arbitrary", "arbitrary"), VMEM_LIMIT),
        name="scan_bw" if rev else "scan_fw",
    )(q, fg, v)


def _hgrn_out_kernel(of_ref, ob_ref, gs_ref, x_ref, ng_ref, wout_ref, mod_ref, o_ref):
    o = of_ref[...] + ob_ref[...]
    parts = []
    for h in range(HEADS):
        oh = o[:, h * HDIM:(h + 1) * HDIM]
        parts.append(oh * lax.rsqrt(jnp.mean(oh * oh, axis=-1, keepdims=True) + NORM_EPS))
    on = jnp.concatenate(parts, axis=1) * ng_ref[...]
    y = jnp.dot((on * gs_ref[...]).astype(BF16), wout_ref[...], preferred_element_type=F32)
    o_ref[...] = x_ref[...] + mod_ref[2:3, :] * y


def _hgrn_out(o_fw, o_bw, gs, x_all, mods, layer, norm_g, w_out, n_tiles):
    tile = pl.BlockSpec((TM, D), lambda i: (i, 0))
    return pl.pallas_call(
        _hgrn_out_kernel,
        grid=(n_tiles,),
        in_specs=[tile, tile, tile, tile,
                  pl.BlockSpec((1, D), lambda i: (0, 0)),
                  pl.BlockSpec((D, D), lambda i: (0, 0)),
                  pl.BlockSpec((None, 6, D), lambda i: (layer * 8 + _tile_cond(i), 0, 0))],
        out_specs=tile,
        out_shape=jax.ShapeDtypeStruct((n_tiles * TM, D), F32),
        compiler_params=_cparams(("arbitrary",), VMEM_LIMIT),
        name="hgrn_out",
    )(o_fw, o_bw, gs, x_all, norm_g.reshape(1, D), w_out.astype(BF16), mods)


def _to_row_tiled(ref, val):
    for j in range(RT):
        ref[pl.ds(j, val.shape[0], stride=RT), :] = val[:, j * LANES:(j + 1) * LANES]


def _from_row_tiled(ref, n, base=0):
    return [ref[pl.ds(base + j, n, stride=RT), :] for j in range(RT)]


def _route_kernel(x_ref, mod_ref, g_ref, wrt_ref, brt_ref, h_ref, e_ref, rank_ref, gate_ref, cnt_ref):
    @pl.when(pl.program_id(0) == 0)
    def _():
        cnt_ref[...] = jnp.zeros_like(cnt_ref)

    h = _rms(x_ref[...], g_ref[...]) * (1.0 + mod_ref[4:5, :]) + mod_ref[3:4, :]
    _to_row_tiled(h_ref, h)
    logits = lax.dot_general(wrt_ref[...], h, _NT, precision=HIGHEST,
                             preferred_element_type=F32) + brt_ref[:, 0:1]
    row = lax.broadcasted_iota(jnp.int32, (N_EXPERTS, TM), 0).astype(F32)
    vals = logits
    sel = jnp.zeros((N_EXPERTS, TM), F32)
    tops, idxs = [], []
    for _ in range(TOP_K):
        m = jnp.max(vals, axis=0, keepdims=True)
        idx = jnp.min(jnp.where(vals == m, row, float(N_EXPERTS)), axis=0, keepdims=True)
        hit = row == idx
        vals = jnp.where(hit, -jnp.inf, vals)
        sel = jnp.where(hit, 1.0, sel)
        tops.append(m)
        idxs.append(idx)
    ex = [jnp.exp(m - tops[0]) for m in tops]
    den = ex[0] + ex[1] + ex[2] + ex[3]
    r = lax.broadcasted_iota(jnp.int32, (TM, TM), 0)
    c = lax.broadcasted_iota(jnp.int32, (TM, TM), 1)
    before = jnp.where(r < c, 1.0, 0.0).astype(BF16)
    pref = jnp.dot(sel.astype(BF16), before, preferred_element_type=F32) + cnt_ref[:, 0:1]
    slot = lax.broadcasted_iota(jnp.int32, (8, TM), 0)
    e_out = jnp.zeros((8, TM), F32)
    rank_out = jnp.zeros((8, TM), F32)
    gate_out = jnp.zeros((8, TM), F32)
    for kk in range(TOP_K):
        rank = jnp.sum(jnp.where(row == idxs[kk], pref, 0.0), axis=0, keepdims=True)
        e_out = jnp.where(slot == kk, idxs[kk], e_out)
        rank_out = jnp.where(slot == kk, rank, rank_out)
        gate_out = jnp.where(slot == kk, ex[kk] / den, gate_out)
    e_ref[...] = e_out.astype(jnp.int32)
    rank_ref[...] = rank_out.astype(jnp.int32)
    gate_ref[...] = gate_out
    cnt_ref[...] += jnp.sum(sel, axis=1, keepdims=True)


def _route(x_all, mods, layer, norm_g, w_r, b_r, n_tiles):
    tile = pl.BlockSpec((TM, D), lambda i: (i, 0))
    small = pl.BlockSpec((None, 8, TM), lambda i: (i, 0, 0))
    n = n_tiles * TM
    return pl.pallas_call(
        _route_kernel,
        grid=(n_tiles,),
        in_specs=[tile,
                  pl.BlockSpec((None, 6, D), lambda i: (layer * 8 + _tile_cond(i), 0, 0)),
                  pl.BlockSpec((1, D), lambda i: (0, 0)),
                  pl.BlockSpec((N_EXPERTS, D), lambda i: (0, 0)),
                  pl.BlockSpec((N_EXPERTS, LANES), lambda i: (0, 0))],
        out_specs=[pl.BlockSpec((TM * RT, LANES), lambda i: (i, 0)), small, small, small,
                   pl.BlockSpec((N_EXPERTS, LANES), lambda i: (0, 0))],
        out_shape=[jax.ShapeDtypeStruct((n * RT, LANES), F32),
                   jax.ShapeDtypeStruct((n_tiles, 8, TM), jnp.int32),
                   jax.ShapeDtypeStruct((n_tiles, 8, TM), jnp.int32),
                   jax.ShapeDtypeStruct((n_tiles, 8, TM), F32),
                   jax.ShapeDtypeStruct((N_EXPERTS, LANES), F32)],
        compiler_params=_cparams(("arbitrary",), VMEM_LIMIT),
        name="route",
    )(x_all, mods, norm_g.reshape(1, D), w_r.T, jnp.broadcast_to(b_r[:, None], (N_EXPERTS, LANES)))


def _dest_kernel(ps_ref, e_ref, rank_ref, d_ref):
    e = e_ref[...]
    acc = rank_ref[...]
    for j in range(N_EXPERTS):
        acc = acc + jnp.where(e == j, ps_ref[j], 0)
    d_ref[...] = acc


def _dest(pad_start, e, rank):
    full = pl.BlockSpec(e.shape, lambda i, ps: (0, 0, 0))
    return pl.pallas_call(
        _dest_kernel,
        grid_spec=pltpu.PrefetchScalarGridSpec(num_scalar_prefetch=1, grid=(1,),
                                               in_specs=[full, full], out_specs=full),
        out_shape=jax.ShapeDtypeStruct(e.shape, jnp.int32),
        name="dest",
    )(pad_start, e, rank)


PAD_PIECES = tuple(1 << p for p in reversed(range(BM.bit_length() - 1)))


def _zero_fill(lo_ref, hi_ref, xb_hbm, zeros, sem, n_blocks, wait):
    def go(rows, row0):
        cp = pltpu.make_async_copy(zeros.at[pl.ds(0, rows * RT), :],
                                   xb_hbm.at[pl.ds(row0 * RT, rows * RT), :], sem.at[1])
        cp.wait() if wait else cp.start()

    def per_expert(e, carry):
        row = lo_ref[e]
        n = hi_ref[e] - row
        for piece in PAD_PIECES:
            @pl.when((n & piece) != 0)
            def _():
                go(piece, row)
            row = row + (n & piece)
        return carry
    lax.fori_loop(0, N_EXPERTS, per_expert, 0)

    def per_block(b, carry):
        go(BM, b * BM)
        return carry
    lax.fori_loop(hi_ref[N_EXPERTS - 1] // BM, n_blocks, per_block, 0)


def _dispatch_kernel(lo_ref, hi_ref, dest_ref, h_ref, xb_hbm, zeros, sem, *, n_blocks):
    i = pl.program_id(0)

    @pl.when(i == 0)
    def _():
        zeros[...] = jnp.zeros_like(zeros)
        _zero_fill(lo_ref, hi_ref, xb_hbm, zeros, sem, n_blocks, wait=False)

    def body(t, carry):
        src = h_ref.at[pl.ds(t * RT, RT), :]
        for kk in range(TOP_K):
            d = dest_ref[0, 0, kk * TM + t]
            pltpu.make_async_copy(src, xb_hbm.at[pl.ds(d * RT, RT), :],
                                  sem.at[0]).start(priority=kk % 2)
        return carry
    lax.fori_loop(0, TM, body, 0, unroll=4)
    for kk in range(TOP_K):
        pltpu.make_async_copy(h_ref, xb_hbm.at[pl.ds(0, TM * RT), :], sem.at[0]).wait()

    @pl.when(i == pl.num_programs(0) - 1)
    def _():
        _zero_fill(lo_ref, hi_ref, xb_hbm, zeros, sem, n_blocks, wait=True)


def _dispatch(h_rt, dest, pad_lo, pad_hi, n_blocks):
    n_tiles = dest.shape[0]
    dest3 = dest[:, :TOP_K, :].reshape(n_tiles, 1, TOP_K * TM)
    grid_spec = pltpu.PrefetchScalarGridSpec(
        num_scalar_prefetch=2,
        grid=(n_tiles,),
        in_specs=[pl.BlockSpec((1, 1, TOP_K * TM), lambda i, lo, hi: (i, 0, 0),
                               memory_space=pltpu.SMEM),
                  pl.BlockSpec((TM * RT, LANES), lambda i, lo, hi: (i, 0))],
        out_specs=pl.BlockSpec(memory_space=pl.ANY),
        scratch_shapes=[pltpu.VMEM((BM * RT, LANES), F32),
                        pltpu.SemaphoreType.DMA((2,))],
    )
    xb = pl.pallas_call(
        functools.partial(_dispatch_kernel, n_blocks=n_blocks),
        grid_spec=grid_spec,
        out_shape=jax.ShapeDtypeStruct((n_blocks * BM * RT, LANES), F32),
        compiler_params=_cparams(("arbitrary",), VMEM_LIMIT, disable_bounds_checks=True),
        name="dispatch",
    )(pad_lo, pad_hi, dest3, h_rt)
    return xb, dest3


def _expert_kernel(bexp_ref, nused_ref, first_ref, wslot_ref, next_ref, xb_ref, w1_hbm, b1_ref,
                   w2_hbm, b2_ref, y_ref, w1s, w2s, w1b, w2b, sem, *, layer):
    i = pl.program_id(0)
    n_used = nused_ref[0]

    def weight_copies(e, slot):
        return (pltpu.make_async_copy(w1_hbm.at[layer, e], w1s.at[slot], sem.at[0, slot]),
                pltpu.make_async_copy(w2_hbm.at[layer, e], w2s.at[slot], sem.at[1, slot]))

    @pl.when(i == 0)
    def _():
        for cp in weight_copies(bexp_ref[0], 0):
            cp.start()

    @pl.when(first_ref[i] == 1)
    def _():
        slot = wslot_ref[i]
        for cp in weight_copies(bexp_ref[i], slot):
            cp.wait()
        w1b[...] = w1s[slot].astype(BF16)
        w2b[...] = w2s[slot].astype(BF16)

        @pl.when(next_ref[i] >= 0)
        def _():
            for cp in weight_copies(next_ref[i], 1 - slot):
                cp.start()

    @pl.when(i < n_used)
    def _():
        x = jnp.concatenate(_from_row_tiled(xb_ref, BM), axis=1).astype(BF16)
        u = jnp.dot(x, w1b[...], preferred_element_type=F32) + b1_ref[...]
        glu = jnp.minimum(u[:, :D_FF], SWIGLU_LIMIT)
        lin = jnp.clip(u[:, D_FF:], -SWIGLU_LIMIT, SWIGLU_LIMIT)
        act = glu * _sigmoid(SWIGLU_ALPHA * glu) * (lin + 1.0)
        y = jnp.dot(act.astype(BF16), w2b[...], preferred_element_type=F32) + b2_ref[...]
        _to_row_tiled(y_ref, y)

    @pl.when(i >= n_used)
    def _():
        y_ref[...] = jnp.zeros_like(y_ref)


def _experts(xb, block_exp, n_used, counts, layer, w1, b1, w2, b2):
    n_blocks = xb.shape[0] // (BM * RT)
    used = jnp.arange(n_blocks, dtype=jnp.int32) < n_used[0]
    changed = jnp.concatenate([jnp.ones((1,), bool), block_exp[1:] != block_exp[:-1]])
    first = jnp.logical_and(used, changed).astype(jnp.int32)
    wslot = (jnp.cumsum(first) - 1) % 2
    ids = jnp.where(counts > 0, jnp.arange(N_EXPERTS, dtype=jnp.int32), N_EXPERTS)
    later = jnp.concatenate([lax.cummin(ids, reverse=True)[1:],
                             jnp.full((1,), N_EXPERTS, jnp.int32)])
    next_exp = jnp.where(later == N_EXPERTS, -1, later)[block_exp]
    smap = lambda f: (lambda i, be, nu, fi, ws, nx: f(i, be, nu))
    grid_spec = pltpu.PrefetchScalarGridSpec(
        num_scalar_prefetch=5,
        grid=(n_blocks,),
        in_specs=[
            pl.BlockSpec((BM * RT, LANES), smap(lambda i, be, nu: (jnp.minimum(i, nu[0] - 1), 0))),
            pl.BlockSpec(memory_space=pl.ANY),
            pl.BlockSpec((None, 1, 2 * D_FF), smap(lambda i, be, nu: (layer * N_EXPERTS + be[i], 0, 0))),
            pl.BlockSpec(memory_space=pl.ANY),
            pl.BlockSpec((None, 1, D), smap(lambda i, be, nu: (layer * N_EXPERTS + be[i], 0, 0))),
        ],
        out_specs=pl.BlockSpec((BM * RT, LANES), smap(lambda i, be, nu: (i, 0))),
        scratch_shapes=[pltpu.VMEM((2, D, 2 * D_FF), F32),
                        pltpu.VMEM((2, D_FF, D), F32),
                        pltpu.VMEM((D, 2 * D_FF), BF16),
                        pltpu.VMEM((D_FF, D), BF16),
                        pltpu.SemaphoreType.DMA((2, 2))],
    )
    return pl.pallas_call(
        functools.partial(_expert_kernel, layer=layer),
        grid_spec=grid_spec,
        out_shape=jax.ShapeDtypeStruct(xb.shape, F32),
        compiler_params=_cparams(("arbitrary",), VMEM_LIMIT),
        name="experts",
    )(block_exp, n_used, first, wslot.astype(jnp.int32), next_exp.astype(jnp.int32), xb, w1,
      b1.reshape(-1, 1, 2 * D_FF), w2, b2.reshape(-1, 1, D))


def _combine_kernel(dest_ref, dest_next_ref, y_hbm, gate_ref, x_ref, mod_ref, fg_ref, o_ref, buf, sem,
                    *, final_norm):
    i = pl.program_id(0)
    n = pl.num_programs(0)
    slot = i % 2
    rows = TOP_K * TM

    def gather(dest, to_slot):
        def body(p, carry):
            for half in range(2):
                r = 2 * p + half
                pltpu.make_async_copy(y_hbm.at[pl.ds(dest[0, 0, r] * RT, RT), :],
                                      buf.at[to_slot, pl.ds(r * RT, RT), :],
                                      sem.at[to_slot]).start(priority=half)
            return carry
        lax.fori_loop(0, rows // 2, body, 0, unroll=4)

    @pl.when(i == 0)
    def _():
        gather(dest_ref, 0)

    @pl.when(i + 1 < n)
    def _():
        gather(dest_next_ref, 1 - slot)

    pltpu.make_async_copy(y_hbm.at[pl.ds(0, rows * RT), :], buf.at[slot], sem.at[slot]).wait()
    eye = (lax.broadcasted_iota(jnp.int32, (8, 8), 0)
           == lax.broadcasted_iota(jnp.int32, (8, 8), 1)).astype(F32)
    gate = lax.dot_general(gate_ref[...], eye, _TN, precision=HIGHEST,
                           preferred_element_type=F32)
    cur = buf.at[slot]
    outs = []
    for j in range(RT):
        acc = None
        for kk in range(TOP_K):
            piece = gate[:, kk:kk + 1] * cur[pl.ds(kk * TM * RT + j, TM, stride=RT), :]
            acc = piece if acc is None else acc + piece
        outs.append(acc)
    out = x_ref[...] + mod_ref[5:6, :] * jnp.concatenate(outs, axis=1)
    if final_norm:
        out = _rms(out, fg_ref[...])
    o_ref[...] = out


def _combine(yb, dest3, gate, x_all, mods, layer, final_g, n_tiles, final_norm):
    last = n_tiles - 1
    tile = pl.BlockSpec((TM, D), lambda i: (i, 0))
    return pl.pallas_call(
        functools.partial(_combine_kernel, final_norm=final_norm),
        grid=(n_tiles,),
        in_specs=[pl.BlockSpec((1, 1, TOP_K * TM), lambda i: (i, 0, 0), memory_space=pltpu.SMEM),
                  pl.BlockSpec((1, 1, TOP_K * TM), lambda i: (jnp.minimum(i + 1, last), 0, 0),
                               memory_space=pltpu.SMEM),
                  pl.BlockSpec(memory_space=pl.ANY),
                  pl.BlockSpec((None, 8, TM), lambda i: (i, 0, 0)),
                  tile,
                  pl.BlockSpec((None, 6, D), lambda i: (layer * 8 + _tile_cond(i), 0, 0)),
                  pl.BlockSpec((1, D), lambda i: (0, 0))],
        out_specs=tile,
        out_shape=jax.ShapeDtypeStruct((n_tiles * TM, D), F32),
        scratch_shapes=[pltpu.VMEM((2, TOP_K * TM * RT, LANES), F32),
                        pltpu.SemaphoreType.DMA((2,))],
        compiler_params=_cparams(("arbitrary",), VMEM_LIMIT, disable_bounds_checks=True),
        name="combine",
    )(dest3, dest3, yb, gate, x_all, mods, final_g.reshape(1, D))


def _moe(x_all, mods, layer, norm_g, w_r, b_r, w1, b1, w2, b2, final_g, n_tiles, final_norm):
    n = n_tiles * TM
    h_rt, e, rank, gate, cnt = _route(x_all, mods, layer, norm_g, w_r, b_r, n_tiles)
    counts = cnt[:, 0].astype(jnp.int32)
    padded = (counts + BM - 1) // BM * BM
    pad_end = jnp.cumsum(padded)
    pad_start = pad_end - padded
    n_blocks = -(-(n * TOP_K) // BM) + N_EXPERTS
    starts = jnp.arange(n_blocks, dtype=jnp.int32) * BM
    block_exp = jnp.minimum(jnp.sum((pad_end[None, :] <= starts[:, None]).astype(jnp.int32), axis=1),
                            N_EXPERTS - 1)
    n_used = pad_end[-1:] // BM
    dest = _dest(pad_start, e, rank)
    xb, dest3 = _dispatch(h_rt, dest, pad_start + counts, pad_end, n_blocks)
    yb = _experts(xb, block_exp, n_used, counts, layer, w1, b1, w2, b2)
    return _combine(yb, dest3, gate, x_all, mods, layer, final_g, n_tiles, final_norm)


def kernel(x, c, ctx, c_ctx, mod_w, mod_b, norm1_g, norm2_g, fourier_w_in, fourier_w_out,
           hgrn_w_in, hgrn_lower_bounds, hgrn_norm_g, hgrn_w_out, router_w, router_b,
           expert_w1, expert_b1, expert_w2, expert_b2, final_norm_g):
    assert x.shape == (BATCH, SEQ, D) and ctx.shape == (BATCH, CTX_LEN, D)
    cond8 = jnp.zeros((8, D), F32).at[:BATCH].set(c).at[BATCH].set(c_ctx)
    mods = _adaln(cond8, mod_w, mod_b)
    x_all = jnp.concatenate([x.reshape(NLAT, D), ctx.reshape(NCTX, D)], axis=0)
    dch, mpos, mrow = _fourier_constants()
    experts = (expert_w1, expert_b1, expert_w2, expert_b2)

    vr, vi = _fourier_in(x_all, mods, 0, norm1_g[0], fourier_w_in[0], dch, mpos)
    x_all = _fourier_out(vr, vi, x_all, mods, 0, fourier_w_out[0], mrow)
    x_all = _moe(x_all, mods, 0, norm2_g[0], router_w[0], router_b[0], *experts,
                 final_norm_g, ALL_TILES, False)

    q, ff, fb, v, gs = _hgrn_in(x_all, mods, 1, norm1_g[1], hgrn_w_in[0], hgrn_lower_bounds)
    o_fw = _scan(q, ff, v, False)
    o_bw = _scan(q, fb, v, True)
    x_lat = _hgrn_out(o_fw, o_bw, gs, x_all, mods, 1, hgrn_norm_g[0], hgrn_w_out[0], LAT_TILES)
    out = _moe(x_lat, mods, 1, norm2_g[1], router_w[1], router_b[1], *experts,
               final_norm_g, LAT_TILES, True)
    return out.reshape(BATCH, SEQ, D)
```

```python
import functools

import numpy as np
import jax
import jax.numpy as jnp
from jax import lax
from jax.experimental import pallas as pl
from jax.experimental.pallas import tpu as pltpu

F32 = jnp.float32
BF16 = jnp.bfloat16
HIGHEST = lax.Precision.HIGHEST

D = 1024
BATCH = 2
SEQ = 8192
CTX_LEN = 256
GRID_W = 64
GRID_H = SEQ // GRID_W
NLAT = BATCH * SEQ
NCTX = BATCH * CTX_LEN
NTOK = NLAT + NCTX
TM = 256
LAT_TILES = NLAT // TM
ALL_TILES = NTOK // TM
TILES_PER_BATCH = SEQ // TM
FGROUPS = 4
FGDIM = D // FGROUPS
HEADS = 8
HDIM = D // HEADS
CHUNK = 128
N_EXPERTS = 32
TOP_K = 4
D_FF = 1024
SWIGLU_ALPHA = 1.702
SWIGLU_LIMIT = 7.0
BM = 512
HB = BM // 2
LANES = 128
RT = D // LANES
NORM_EPS = 1e-6
VMEM_LIMIT = 56 * 1024 * 1024

_NT = (((1,), (1,)), ((), ()))
_TN = (((0,), (0,)), ((), ()))


def _cparams(sem, vmem=None, **kw):
    return pltpu.CompilerParams(dimension_semantics=sem, vmem_limit_bytes=vmem, **kw)


def _sigmoid(x):
    return 1.0 / (1.0 + jnp.exp(-x))


def _rms(x, g):
    return x * lax.rsqrt(jnp.mean(x * x, axis=-1, keepdims=True) + NORM_EPS) * g


def _tile_cond(i):
    return jnp.where(i < LAT_TILES, i // TILES_PER_BATCH, 2)


def _adaln_kernel(cond_ref, w_ref, b_ref, o_ref):
    c = cond_ref[...]
    s = c * _sigmoid(c)
    o_ref[...] = jnp.dot(s, w_ref[...], precision=HIGHEST,
                         preferred_element_type=F32) + b_ref[...]


def _adaln(cond8, mod_w, mod_b):
    depth = mod_w.shape[0]
    nb = 1536
    out = pl.pallas_call(
        _adaln_kernel,
        grid=(depth, 6 * D // nb),
        in_specs=[pl.BlockSpec((8, D), lambda l, j: (0, 0)),
                  pl.BlockSpec((None, D, nb), lambda l, j: (l, 0, j)),
                  pl.BlockSpec((None, 1, nb), lambda l, j: (l, 0, j))],
        out_specs=pl.BlockSpec((None, 8, nb), lambda l, j: (l, 0, j)),
        out_shape=jax.ShapeDtypeStruct((depth, 8, 6 * D), F32),
        compiler_params=_cparams(("arbitrary", "arbitrary"), VMEM_LIMIT),
        name="adaln",
    )(cond8, mod_w, mod_b.reshape(depth, 1, 6 * D))
    return out.reshape(depth * 8, 6, D)


def _dft_cs(n):
    k = np.arange(n)
    ang = 2.0 * np.pi * np.outer(k, k) / n
    s = 1.0 / np.sqrt(n)
    return np.cos(ang) * s, np.sin(ang) * s


def _fourier_constants():
    cd, sd = _dft_cs(FGDIM)
    dch = np.concatenate([cd, sd], axis=1)
    cc, sc = _dft_cs(GRID_W)
    eye = np.eye(TM // GRID_W)
    kc, ks = np.kron(eye, cc), np.kron(eye, sc)
    m_lat = np.block([[kc, -ks], [ks, kc]])
    cp, sp = _dft_cs(CTX_LEN)
    m_ctx = np.block([[cp, -sp], [sp, cp]])
    mpos = np.stack([m_lat, m_ctx])
    cr, sr = _dft_cs(GRID_H)
    mrow = np.concatenate([cr, -sr], axis=1)
    return (jnp.asarray(dch, BF16), jnp.asarray(mpos, BF16), jnp.asarray(mrow, BF16))


def _fourier_in_kernel(x_ref, mod_ref, g_ref, win_ref, dch_ref, mpos_ref, vr_ref, vi_ref):
    x = x_ref[...]
    h = _rms(x, g_ref[...]) * (1.0 + mod_ref[1:2, :]) + mod_ref[0:1, :]
    u = jnp.dot(h.astype(BF16), win_ref[...], preferred_element_type=F32).astype(BF16)
    parts = [jnp.dot(u[:, g * FGDIM:(g + 1) * FGDIM], dch_ref[...],
                     preferred_element_type=F32) for g in range(FGROUPS)]
    uc = jnp.concatenate([p[:, :FGDIM] for p in parts], axis=1)
    us = jnp.concatenate([p[:, FGDIM:] for p in parts], axis=1)
    st = jnp.concatenate([uc, us], axis=0).astype(BF16)
    v = jnp.dot(mpos_ref[...], st, preferred_element_type=F32)
    vr_ref[...] = v[:TM]
    vi_ref[...] = v[TM:]


def _fourier_in(x_all, mods, layer, norm_g, w_in, dch, mpos):
    tile = pl.BlockSpec((TM, D), lambda i: (i, 0))
    return pl.pallas_call(
        _fourier_in_kernel,
        grid=(ALL_TILES,),
        in_specs=[tile,
                  pl.BlockSpec((None, 6, D), lambda i: (layer * 8 + _tile_cond(i), 0, 0)),
                  pl.BlockSpec((1, D), lambda i: (0, 0)),
                  pl.BlockSpec((D, D), lambda i: (0, 0)),
                  pl.BlockSpec((FGDIM, 2 * FGDIM), lambda i: (0, 0)),
                  pl.BlockSpec((None, 2 * TM, 2 * TM), lambda i: (jnp.where(i < LAT_TILES, 0, 1), 0, 0))],
        out_specs=[tile, tile],
        out_shape=[jax.ShapeDtypeStruct((NTOK, D), F32)] * 2,
        compiler_params=_cparams(("arbitrary",), VMEM_LIMIT),
        name="fourier_in",
    )(x_all, mods, norm_g.reshape(1, D), w_in.astype(BF16), dch, mpos)


CB = 8


def _fourier_out_lat_kernel(vr_ref, vi_ref, x_ref, mrow_ref, wout_ref, mod_ref, o_ref):
    g1 = mod_ref[2:3, :]
    for c in range(CB):
        st = jnp.concatenate([vr_ref[:, c, :], vi_ref[:, c, :]], axis=0).astype(BF16)
        yf = jnp.dot(mrow_ref[...], st, preferred_element_type=F32)
        y = jnp.dot(yf.astype(BF16), wout_ref[...], preferred_element_type=F32)
        o_ref[:, c, :] = x_ref[:, c, :] + g1 * y


def _fourier_out_ctx_kernel(yr_ref, x_ref, wout_ref, mod_ref, o_ref):
    y = jnp.dot(yr_ref[...].astype(BF16), wout_ref[...], preferred_element_type=F32)
    o_ref[...] = x_ref[...] + mod_ref[2:3, :] * y


def _fourier_out(vr, vi, x_all, mods, layer, w_out, mrow):
    wout = w_out.astype(BF16)
    rows = NTOK // GRID_W
    v3 = lambda a: a.reshape(rows, GRID_W, D)
    blk = pl.BlockSpec((GRID_H, CB, D), lambda b, c: (b, c, 0))
    x_new = pl.pallas_call(
        _fourier_out_lat_kernel,
        grid=(BATCH, GRID_W // CB),
        in_specs=[blk, blk, blk,
                  pl.BlockSpec((GRID_H, 2 * GRID_H), lambda b, c: (0, 0)),
                  pl.BlockSpec((D, D), lambda b, c: (0, 0)),
                  pl.BlockSpec((None, 6, D), lambda b, c: (layer * 8 + b, 0, 0))],
        out_specs=blk,
        out_shape=jax.ShapeDtypeStruct((rows, GRID_W, D), F32),
        input_output_aliases={2: 0},
        compiler_params=_cparams(("arbitrary", "arbitrary"), VMEM_LIMIT),
        name="fourier_out_lat",
    )(v3(vr), v3(vi), v3(x_all), mrow, wout, mods).reshape(NTOK, D)
    ctile = pl.BlockSpec((TM, D), lambda i: (LAT_TILES + i, 0))
    return pl.pallas_call(
        _fourier_out_ctx_kernel,
        grid=(NCTX // TM,),
        in_specs=[ctile, ctile,
                  pl.BlockSpec((D, D), lambda i: (0, 0)),
                  pl.BlockSpec((None, 6, D), lambda i: (layer * 8 + 2, 0, 0))],
        out_specs=ctile,
        out_shape=jax.ShapeDtypeStruct((NTOK, D), F32),
        input_output_aliases={1: 0},
        compiler_params=_cparams(("arbitrary",), VMEM_LIMIT),
        name="fourier_out_ctx",
    )(vr, x_new, wout, mods)


def _hgrn_in_kernel(x_ref, mod_ref, g_ref, win_ref, hlb_ref, q_ref, ff_ref, fb_ref, v_ref, gs_ref,
                    *, layer):
    x = x_ref[...]
    h = (_rms(x, g_ref[...]) * (1.0 + mod_ref[1:2, :]) + mod_ref[0:1, :]).astype(BF16)
    raw = [hlb_ref[l] for l in range(hlb_ref.shape[0])]
    mx = functools.reduce(jnp.maximum, raw)
    ex = [jnp.exp(r - mx) for r in raw]
    den = functools.reduce(lambda a, b: a + b, ex)
    soft = [e / den for e in ex]
    lb = functools.reduce(lambda a, b: a + b, soft[:layer + 1]) - soft[0]

    def proj(j):
        return jnp.dot(h, win_ref[:, j * D:(j + 1) * D], preferred_element_type=F32)

    q = proj(0)
    q_ref[...] = q * _sigmoid(q)
    ff_ref[...] = lb[0:1, :] + (1.0 - lb[0:1, :]) * _sigmoid(proj(1))
    fb_ref[...] = lb[1:2, :] + (1.0 - lb[1:2, :]) * _sigmoid(proj(2))
    v_ref[...] = proj(3)
    g = proj(4)
    gs_ref[...] = g * _sigmoid(g)


def _hgrn_in(x_all, mods, layer, norm_g, w_in, hlb):
    tile = pl.BlockSpec((TM, D), lambda i: (i, 0))
    depth = hlb.shape[0]
    return pl.pallas_call(
        functools.partial(_hgrn_in_kernel, layer=layer),
        grid=(ALL_TILES,),
        in_specs=[tile,
                  pl.BlockSpec((None, 6, D), lambda i: (layer * 8 + _tile_cond(i), 0, 0)),
                  pl.BlockSpec((1, D), lambda i: (0, 0)),
                  pl.BlockSpec((D, 5 * D), lambda i: (0, 0)),
                  pl.BlockSpec((depth, 2, D), lambda i: (0, 0, 0))],
        out_specs=[tile] * 5,
        out_shape=[jax.ShapeDtypeStruct((NTOK, D), F32)] * 5,
        compiler_params=_cparams(("arbitrary",), VMEM_LIMIT),
        name="hgrn_in",
    )(x_all, mods, norm_g.reshape(1, D), w_in.astype(BF16), hlb)


N_LEVELS = 7
SUB = 8
NGRP = CHUNK // SUB
FINE_LEVELS = 3


def _scan_pair_kernel(qf_ref, ff_ref, vf_ref, qb_ref, fb_ref, vb_ref, of_ref, ob_ref, sf_ref, sb_ref):
    @pl.when(pl.program_id(1) == 0)
    def _():
        sf_ref[...] = jnp.zeros_like(sf_ref)
        sb_ref[...] = jnp.zeros_like(sb_ref)

    _scan_chunk(qf_ref, ff_ref, vf_ref, of_ref, sf_ref, rev=False)
    _scan_chunk(qb_ref, fb_ref, vb_ref, ob_ref, sb_ref, rev=True)


def _scan_chunk(q_ref, fg_ref, v_ref, o_ref, s_ref, *, rev):
    q = q_ref[...]
    fg = fg_ref[...]
    k = 1.0 - fg
    vb = v_ref[...].astype(BF16)
    t = lax.broadcasted_iota(jnp.int32, (CHUNK, D), 0)
    ti = lax.broadcasted_iota(jnp.int32, (CHUNK, CHUNK), 0)
    si = lax.broadcasted_iota(jnp.int32, (CHUNK, CHUNK), 1)
    scores = [None] * HEADS

    def add_level(qs, ks, mask):
        qb, kb = qs.astype(BF16), ks.astype(BF16)
        for h in range(HEADS):
            sl = slice(h * HDIM, (h + 1) * HDIM)
            sc = lax.dot_general(qb[:, sl], kb[:, sl], _NT, preferred_element_type=F32)
            sc = jnp.where(mask, sc, 0.0)
            scores[h] = sc if scores[h] is None else scores[h] + sc

    add_level(q, k, ti == si)
    run = fg
    rest = jnp.ones_like(fg)
    tot = fg
    for l in range(FINE_LEVELS):
        hbit = 1 << l
        odd = (t & hbit) != 0
        far = (t & hbit) == 0 if rev else odd
        t_far = (ti & hbit) == 0 if rev else (ti & hbit) != 0
        s_near = (si & hbit) != 0 if rev else (si & hbit) == 0
        add_level(run * q, rest * k,
                  ((ti >> (l + 1)) == (si >> (l + 1))) & t_far & s_near)
        sib = jnp.where(odd, pltpu.roll(tot, hbit, 0), pltpu.roll(tot, CHUNK - hbit, 0))
        run = jnp.where(far, run * sib, run)
        rest = jnp.where(far, rest, rest * sib)
        tot = tot * sib

    def groups(a):
        return [a[b * SUB:(b + 1) * SUB, :] for b in range(NGRP)]

    q_g, k_g, run_g, rest_g, tot_g = (groups(a) for a in (q, k, run, rest, tot))
    sc_g = [groups(scores[h]) for h in range(HEADS)]
    lane = lax.broadcasted_iota(jnp.int32, (SUB, CHUNK), 1)
    zero_g = jnp.zeros((SUB, D), F32)
    for l in range(FINE_LEVELS, N_LEVELS):
        bit = 1 << (l - FINE_LEVELS)
        is_far = [((b & bit) == 0) == rev for b in range(NGRP)]
        far_groups = [b for b in range(NGRP) if is_far[b]]
        qb = jnp.concatenate([run_g[b] * q_g[b] for b in far_groups], axis=0).astype(BF16)
        kb = jnp.concatenate([zero_g if is_far[b] else rest_g[b] * k_g[b] for b in range(NGRP)],
                             axis=0).astype(BF16)
        span = 2 << l
        keep = [None if span == CHUNK else
                (lane >= b * SUB // span * span) & (lane < b * SUB // span * span + span)
                for b in far_groups]
        for h in range(HEADS):
            sl = slice(h * HDIM, (h + 1) * HDIM)
            sc = lax.dot_general(qb[:, sl], kb[:, sl], _NT, preferred_element_type=F32)
            for i, b in enumerate(far_groups):
                piece = sc[i * SUB:(i + 1) * SUB, :]
                if keep[i] is not None:
                    piece = jnp.where(keep[i], piece, 0.0)
                sc_g[h][b] = sc_g[h][b] + piece
        sib_g = [tot_g[b ^ bit] for b in range(NGRP)]
        run_g = [run_g[b] * sib_g[b] if is_far[b] else run_g[b] for b in range(NGRP)]
        rest_g = [rest_g[b] if is_far[b] else rest_g[b] * sib_g[b] for b in range(NGRP)]
        tot_g = [tot_g[b] * sib_g[b] for b in range(NGRP)]
    qin = (q * jnp.concatenate(run_g, axis=0)).astype(BF16)
    kst = (k * jnp.concatenate(rest_g, axis=0)).astype(BF16)
    tot_row = tot_g[0][0:1, :]
    for h in range(HEADS):
        sl = slice(h * HDIM, (h + 1) * HDIM)
        st = s_ref[h]
        sc = jnp.concatenate(sc_g[h], axis=0).astype(BF16)
        o_ref[:, sl] = (jnp.dot(sc, vb[:, sl], preferred_element_type=F32)
                        + lax.dot_general(qin[:, sl], st.astype(BF16), _NT,
                                          preferred_element_type=F32))
        s_ref[h] = st * tot_row[:, sl] + lax.dot_general(vb[:, sl], kst[:, sl], _TN,
                                                         preferred_element_type=F32)


LAT_CHUNKS = SEQ // CHUNK
CTX_CHUNKS = CTX_LEN // CHUNK
SCAN_STEPS = CTX_CHUNKS + LAT_CHUNKS


def _scan(q, f_fw, f_bw, v):
    def idx_fw(b, s):
        return (jnp.where(s < CTX_CHUNKS, NLAT // CHUNK + CTX_CHUNKS * b + s,
                          LAT_CHUNKS * b + (s - CTX_CHUNKS)), 0)

    def idx_bw(b, s):
        return (jnp.where(s < CTX_CHUNKS, NLAT // CHUNK + CTX_CHUNKS * b + (CTX_CHUNKS - 1 - s),
                          LAT_CHUNKS * b + (SCAN_STEPS - 1 - s)), 0)
    fw = pl.BlockSpec((CHUNK, D), idx_fw)
    bw = pl.BlockSpec((CHUNK, D), idx_bw)
    state = pltpu.VMEM((HEADS, HDIM, HDIM), F32)
    return pl.pallas_call(
        _scan_pair_kernel,
        grid=(BATCH, SCAN_STEPS),
        in_specs=[fw, fw, fw, bw, bw, bw],
        out_specs=[fw, bw],
        out_shape=[jax.ShapeDtypeStruct((NTOK, D), F32)] * 2,
        scratch_shapes=[state, state],
        compiler_params=_cparams(("arbitrary", "arbitrary"), VMEM_LIMIT),
        name="scan",
    )(q, f_fw, v, q, f_bw, v)


def _hgrn_out_kernel(of_ref, ob_ref, gs_ref, x_ref, ng_ref, wout_ref, mod_ref, o_ref):
    o = of_ref[...] + ob_ref[...]
    parts = []
    for h in range(HEADS):
        oh = o[:, h * HDIM:(h + 1) * HDIM]
        parts.append(oh * lax.rsqrt(jnp.mean(oh * oh, axis=-1, keepdims=True) + NORM_EPS))
    on = jnp.concatenate(parts, axis=1) * ng_ref[...]
    y = jnp.dot((on * gs_ref[...]).astype(BF16), wout_ref[...], preferred_element_type=F32)
    o_ref[...] = x_ref[...] + mod_ref[2:3, :] * y


def _hgrn_out(o_fw, o_bw, gs, x_all, mods, layer, norm_g, w_out, n_tiles):
    tile = pl.BlockSpec((TM, D), lambda i: (i, 0))
    return pl.pallas_call(
        _hgrn_out_kernel,
        grid=(n_tiles,),
        in_specs=[tile, tile, tile, tile,
                  pl.BlockSpec((1, D), lambda i: (0, 0)),
                  pl.BlockSpec((D, D), lambda i: (0, 0)),
                  pl.BlockSpec((None, 6, D), lambda i: (layer * 8 + _tile_cond(i), 0, 0))],
        out_specs=tile,
        out_shape=jax.ShapeDtypeStruct((n_tiles * TM, D), F32),
        compiler_params=_cparams(("arbitrary",), VMEM_LIMIT),
        name="hgrn_out",
    )(o_fw, o_bw, gs, x_all, norm_g.reshape(1, D), w_out.astype(BF16), mods)


def _to_row_tiled(ref, val):
    for j in range(RT):
        ref[pl.ds(j, val.shape[0], stride=RT), :] = val[:, j * LANES:(j + 1) * LANES]


def _from_row_tiled(ref, n, base=0):
    return [ref[pl.ds(base + j, n, stride=RT), :] for j in range(RT)]


def _route_kernel(x_ref, mod_ref, g_ref, wrt_ref, brt_ref, h_ref, e_ref, rank_ref, gate_ref, cnt_ref):
    @pl.when(pl.program_id(0) == 0)
    def _():
        cnt_ref[...] = jnp.zeros_like(cnt_ref)

    h = _rms(x_ref[...], g_ref[...]) * (1.0 + mod_ref[4:5, :]) + mod_ref[3:4, :]
    _to_row_tiled(h_ref, h)
    w = wrt_ref[...]
    w_hi = w.astype(BF16)
    w_lo = (w - w_hi.astype(F32)).astype(BF16)
    h_hi = h.astype(BF16)
    h_lo = (h - h_hi.astype(F32)).astype(BF16)
    part = lax.dot_general(jnp.concatenate([w_hi, w_lo], axis=0), h_hi, _NT,
                           preferred_element_type=F32)
    logits = (part[:N_EXPERTS] + part[N_EXPERTS:]
              + lax.dot_general(w_hi, h_lo, _NT, preferred_element_type=F32)
              + brt_ref[:, 0:1])
    row = lax.broadcasted_iota(jnp.int32, (N_EXPERTS, TM), 0).astype(F32)
    vals = logits
    sel = jnp.zeros((N_EXPERTS, TM), F32)
    tops, idxs = [], []
    for _ in range(TOP_K):
        m = jnp.max(vals, axis=0, keepdims=True)
        idx = jnp.min(jnp.where(vals == m, row, float(N_EXPERTS)), axis=0, keepdims=True)
        hit = row == idx
        vals = jnp.where(hit, -jnp.inf, vals)
        sel = jnp.where(hit, 1.0, sel)
        tops.append(m)
        idxs.append(idx)
    ex = [jnp.exp(m - tops[0]) for m in tops]
    den = ex[0] + ex[1] + ex[2] + ex[3]
    r = lax.broadcasted_iota(jnp.int32, (TM, TM), 0)
    c = lax.broadcasted_iota(jnp.int32, (TM, TM), 1)
    before = jnp.where(r < c, 1.0, 0.0).astype(BF16)
    pref = jnp.dot(sel.astype(BF16), before, preferred_element_type=F32) + cnt_ref[:, 0:1]
    slot = lax.broadcasted_iota(jnp.int32, (8, TM), 0)
    e_out = jnp.zeros((8, TM), F32)
    rank_out = jnp.zeros((8, TM), F32)
    gate_out = jnp.zeros((8, TM), F32)
    for kk in range(TOP_K):
        rank = jnp.sum(jnp.where(row == idxs[kk], pref, 0.0), axis=0, keepdims=True)
        e_out = jnp.where(slot == kk, idxs[kk], e_out)
        rank_out = jnp.where(slot == kk, rank, rank_out)
        gate_out = jnp.where(slot == kk, ex[kk] / den, gate_out)
    e_ref[...] = e_out.astype(jnp.int32)
    rank_ref[...] = rank_out.astype(jnp.int32)
    gate_ref[...] = gate_out
    cnt_ref[...] += jnp.sum(sel, axis=1, keepdims=True)


def _route(x_all, mods, layer, norm_g, w_r, b_r, n_tiles):
    tile = pl.BlockSpec((TM, D), lambda i: (i, 0))
    small = pl.BlockSpec((None, 8, TM), lambda i: (i, 0, 0))
    n = n_tiles * TM
    return pl.pallas_call(
        _route_kernel,
        grid=(n_tiles,),
        in_specs=[tile,
                  pl.BlockSpec((None, 6, D), lambda i: (layer * 8 + _tile_cond(i), 0, 0)),
                  pl.BlockSpec((1, D), lambda i: (0, 0)),
                  pl.BlockSpec((N_EXPERTS, D), lambda i: (0, 0)),
                  pl.BlockSpec((N_EXPERTS, LANES), lambda i: (0, 0))],
        out_specs=[pl.BlockSpec((TM * RT, LANES), lambda i: (i, 0)), small, small, small,
                   pl.BlockSpec((N_EXPERTS, LANES), lambda i: (0, 0))],
        out_shape=[jax.ShapeDtypeStruct((n * RT, LANES), F32),
                   jax.ShapeDtypeStruct((n_tiles, 8, TM), jnp.int32),
                   jax.ShapeDtypeStruct((n_tiles, 8, TM), jnp.int32),
                   jax.ShapeDtypeStruct((n_tiles, 8, TM), F32),
                   jax.ShapeDtypeStruct((N_EXPERTS, LANES), F32)],
        compiler_params=_cparams(("arbitrary",), VMEM_LIMIT),
        name="route",
    )(x_all, mods, norm_g.reshape(1, D), w_r.T, jnp.broadcast_to(b_r[:, None], (N_EXPERTS, LANES)))


def _dest_kernel(ps_ref, e_ref, rank_ref, d_ref):
    e = e_ref[...]
    acc = rank_ref[...]
    for j in range(N_EXPERTS):
        acc = acc + jnp.where(e == j, ps_ref[j], 0)
    d_ref[...] = acc


def _dest(pad_start, e, rank):
    full = pl.BlockSpec(e.shape, lambda i, ps: (0, 0, 0))
    return pl.pallas_call(
        _dest_kernel,
        grid_spec=pltpu.PrefetchScalarGridSpec(num_scalar_prefetch=1, grid=(1,),
                                               in_specs=[full, full], out_specs=full),
        out_shape=jax.ShapeDtypeStruct(e.shape, jnp.int32),
        name="dest",
    )(pad_start, e, rank)


PAD_PIECES = tuple(1 << p for p in reversed(range(BM.bit_length() - 1)))


def _zero_fill(lo_ref, hi_ref, xb_hbm, zeros, sem, n_blocks, wait):
    def go(rows, row0):
        cp = pltpu.make_async_copy(zeros.at[pl.ds(0, rows * RT), :],
                                   xb_hbm.at[pl.ds(row0 * RT, rows * RT), :], sem.at[1])
        cp.wait() if wait else cp.start()

    def per_expert(e, carry):
        row = lo_ref[e]
        n = hi_ref[e] - row
        for piece in PAD_PIECES:
            @pl.when((n & piece) != 0)
            def _():
                go(piece, row)
            row = row + (n & piece)
        return carry
    lax.fori_loop(0, N_EXPERTS, per_expert, 0)

    def per_block(b, carry):
        go(BM, b * BM)
        return carry
    lax.fori_loop(hi_ref[N_EXPERTS - 1] // BM, n_blocks, per_block, 0)


def _dispatch_kernel(lo_ref, hi_ref, dest_ref, h_ref, xb_hbm, zeros, sem, *, n_blocks):
    i = pl.program_id(0)

    @pl.when(i == 0)
    def _():
        zeros[...] = jnp.zeros_like(zeros)
        _zero_fill(lo_ref, hi_ref, xb_hbm, zeros, sem, n_blocks, wait=False)

    def body(t, carry):
        src = h_ref.at[pl.ds(t * RT, RT), :]
        for kk in range(TOP_K):
            d = dest_ref[0, 0, kk * TM + t]
            pltpu.make_async_copy(src, xb_hbm.at[pl.ds(d * RT, RT), :],
                                  sem.at[0]).start(priority=kk % 2)
        return carry
    lax.fori_loop(0, TM, body, 0, unroll=4)
    for kk in range(TOP_K):
        pltpu.make_async_copy(h_ref, xb_hbm.at[pl.ds(0, TM * RT), :], sem.at[0]).wait()

    @pl.when(i == pl.num_programs(0) - 1)
    def _():
        _zero_fill(lo_ref, hi_ref, xb_hbm, zeros, sem, n_blocks, wait=True)


def _dispatch(h_rt, dest, pad_lo, pad_hi, n_blocks):
    n_tiles = dest.shape[0]
    dest3 = dest[:, :TOP_K, :].reshape(n_tiles, 1, TOP_K * TM)
    grid_spec = pltpu.PrefetchScalarGridSpec(
        num_scalar_prefetch=2,
        grid=(n_tiles,),
        in_specs=[pl.BlockSpec((1, 1, TOP_K * TM), lambda i, lo, hi: (i, 0, 0),
                               memory_space=pltpu.SMEM),
                  pl.BlockSpec((TM * RT, LANES), lambda i, lo, hi: (i, 0))],
        out_specs=pl.BlockSpec(memory_space=pl.ANY),
        scratch_shapes=[pltpu.VMEM((BM * RT, LANES), F32),
                        pltpu.SemaphoreType.DMA((2,))],
    )
    xb = pl.pallas_call(
        functools.partial(_dispatch_kernel, n_blocks=n_blocks),
        grid_spec=grid_spec,
        out_shape=jax.ShapeDtypeStruct((n_blocks * BM * RT, LANES), F32),
        compiler_params=_cparams(("arbitrary",), VMEM_LIMIT, disable_bounds_checks=True),
        name="dispatch",
    )(pad_lo, pad_hi, dest3, h_rt)
    return xb, dest3


def _expert_kernel(bexp_ref, nused_ref, first_ref, wslot_ref, next_ref, valid_ref, xb_ref, w1_hbm, b1_ref,
                   w2_hbm, b2_ref, y_ref, w1s, w2s, w1b, w2b, sem, *, layer):
    i = pl.program_id(0)
    del nused_ref

    def weight_copies(e, slot):
        return (pltpu.make_async_copy(w1_hbm.at[layer, e], w1s.at[slot], sem.at[0, slot]),
                pltpu.make_async_copy(w2_hbm.at[layer, e], w2s.at[slot], sem.at[1, slot]))

    @pl.when(i == 0)
    def _():
        for cp in weight_copies(bexp_ref[0], 0):
            cp.start()

    @pl.when(first_ref[i] == 1)
    def _():
        slot = wslot_ref[i]
        for cp in weight_copies(bexp_ref[i], slot):
            cp.wait()
        w1b[...] = w1s[slot].astype(BF16)
        w2b[...] = w2s[slot].astype(BF16)

        @pl.when(next_ref[i] >= 0)
        def _():
            for cp in weight_copies(next_ref[i], 1 - slot):
                cp.start()

    for half in range(2):
        rows = y_ref.at[pl.ds(half * HB * RT, HB * RT), :]

        @pl.when(valid_ref[i] > half * HB)
        def _():
            x = jnp.concatenate(_from_row_tiled(xb_ref, HB, base=half * HB * RT),
                                axis=1).astype(BF16)
            u = jnp.dot(x, w1b[...], preferred_element_type=F32) + b1_ref[...]
            glu = jnp.minimum(u[:, :D_FF], SWIGLU_LIMIT)
            lin = jnp.clip(u[:, D_FF:], -SWIGLU_LIMIT, SWIGLU_LIMIT)
            act = glu * _sigmoid(SWIGLU_ALPHA * glu) * (lin + 1.0)
            y = jnp.dot(act.astype(BF16), w2b[...], preferred_element_type=F32) + b2_ref[...]
            _to_row_tiled(rows, y)

        @pl.when(valid_ref[i] <= half * HB)
        def _():
            rows[...] = jnp.zeros_like(rows)


def _experts(xb, block_exp, n_used, counts, pad_lo, layer, w1, b1, w2, b2):
    n_blocks = xb.shape[0] // (BM * RT)
    used = jnp.arange(n_blocks, dtype=jnp.int32) < n_used[0]
    changed = jnp.concatenate([jnp.ones((1,), bool), block_exp[1:] != block_exp[:-1]])
    first = jnp.logical_and(used, changed).astype(jnp.int32)
    wslot = (jnp.cumsum(first) - 1) % 2
    ids = jnp.where(counts > 0, jnp.arange(N_EXPERTS, dtype=jnp.int32), N_EXPERTS)
    later = jnp.concatenate([lax.cummin(ids, reverse=True)[1:],
                             jnp.full((1,), N_EXPERTS, jnp.int32)])
    next_exp = jnp.where(later == N_EXPERTS, -1, later)[block_exp]
    starts = jnp.arange(n_blocks, dtype=jnp.int32) * BM
    valid = jnp.clip(pad_lo[block_exp] - starts, 0, BM)
    smap = lambda f: (lambda i, be, nu, fi, ws, nx, va: f(i, be, nu))
    grid_spec = pltpu.PrefetchScalarGridSpec(
        num_scalar_prefetch=6,
        grid=(n_blocks,),
        in_specs=[
            pl.BlockSpec((BM * RT, LANES), smap(lambda i, be, nu: (jnp.minimum(i, nu[0] - 1), 0))),
            pl.BlockSpec(memory_space=pl.ANY),
            pl.BlockSpec((None, 1, 2 * D_FF), smap(lambda i, be, nu: (layer * N_EXPERTS + be[i], 0, 0))),
            pl.BlockSpec(memory_space=pl.ANY),
            pl.BlockSpec((None, 1, D), smap(lambda i, be, nu: (layer * N_EXPERTS + be[i], 0, 0))),
        ],
        out_specs=pl.BlockSpec((BM * RT, LANES), smap(lambda i, be, nu: (i, 0))),
        scratch_shapes=[pltpu.VMEM((2, D, 2 * D_FF), F32),
                        pltpu.VMEM((2, D_FF, D), F32),
                        pltpu.VMEM((D, 2 * D_FF), BF16),
                        pltpu.VMEM((D_FF, D), BF16),
                        pltpu.SemaphoreType.DMA((2, 2))],
    )
    return pl.pallas_call(
        functools.partial(_expert_kernel, layer=layer),
        grid_spec=grid_spec,
        out_shape=jax.ShapeDtypeStruct(xb.shape, F32),
        compiler_params=_cparams(("arbitrary",), VMEM_LIMIT),
        name="experts",
    )(block_exp, n_used, first, wslot.astype(jnp.int32), next_exp.astype(jnp.int32),
      valid.astype(jnp.int32), xb, w1,
      b1.reshape(-1, 1, 2 * D_FF), w2, b2.reshape(-1, 1, D))


def _combine_kernel(dest_ref, dest_next_ref, y_hbm, gate_ref, x_ref, mod_ref, fg_ref, o_ref, buf, sem,
                    *, final_norm):
    i = pl.program_id(0)
    n = pl.num_programs(0)
    slot = i % 2
    rows = TOP_K * TM

    def gather(dest, to_slot):
        def body(p, carry):
            for half in range(2):
                r = 2 * p + half
                pltpu.make_async_copy(y_hbm.at[pl.ds(dest[0, 0, r] * RT, RT), :],
                                      buf.at[to_slot, pl.ds(r * RT, RT), :],
                                      sem.at[to_slot]).start(priority=half)
            return carry
        lax.fori_loop(0, rows // 2, body, 0, unroll=4)

    @pl.when(i == 0)
    def _():
        gather(dest_ref, 0)

    @pl.when(i + 1 < n)
    def _():
        gather(dest_next_ref, 1 - slot)

    pltpu.make_async_copy(y_hbm.at[pl.ds(0, rows * RT), :], buf.at[slot], sem.at[slot]).wait()
    eye = (lax.broadcasted_iota(jnp.int32, (8, 8), 0)
           == lax.broadcasted_iota(jnp.int32, (8, 8), 1)).astype(F32)
    gate = lax.dot_general(gate_ref[...], eye, _TN, precision=HIGHEST,
                           preferred_element_type=F32)
    cur = buf.at[slot]
    outs = []
    for j in range(RT):
        acc = None
        for kk in range(TOP_K):
            piece = gate[:, kk:kk + 1] * cur[pl.ds(kk * TM * RT + j, TM, stride=RT), :]
            acc = piece if acc is None else acc + piece
        outs.append(acc)
    out = x_ref[...] + mod_ref[5:6, :] * jnp.concatenate(outs, axis=1)
    if final_norm:
        out = _rms(out, fg_ref[...])
    o_ref[...] = out


def _combine(yb, dest3, gate, x_all, mods, layer, final_g, n_tiles, final_norm):
    last = n_tiles - 1
    tile = pl.BlockSpec((TM, D), lambda i: (i, 0))
    return pl.pallas_call(
        functools.partial(_combine_kernel, final_norm=final_norm),
        grid=(n_tiles,),
        in_specs=[pl.BlockSpec((1, 1, TOP_K * TM), lambda i: (i, 0, 0), memory_space=pltpu.SMEM),
                  pl.BlockSpec((1, 1, TOP_K * TM), lambda i: (jnp.minimum(i + 1, last), 0, 0),
                               memory_space=pltpu.SMEM),
                  pl.BlockSpec(memory_space=pl.ANY),
                  pl.BlockSpec((None, 8, TM), lambda i: (i, 0, 0)),
                  tile,
                  pl.BlockSpec((None, 6, D), lambda i: (layer * 8 + _tile_cond(i), 0, 0)),
                  pl.BlockSpec((1, D), lambda i: (0, 0))],
        out_specs=tile,
        out_shape=jax.ShapeDtypeStruct((n_tiles * TM, D), F32),
        scratch_shapes=[pltpu.VMEM((2, TOP_K * TM * RT, LANES), F32),
                        pltpu.SemaphoreType.DMA((2,))],
        compiler_params=_cparams(("arbitrary",), VMEM_LIMIT, disable_bounds_checks=True),
        name="combine",
    )(dest3, dest3, yb, gate, x_all, mods, final_g.reshape(1, D))


def _moe(x_all, mods, layer, norm_g, w_r, b_r, w1, b1, w2, b2, final_g, n_tiles, final_norm):
    n = n_tiles * TM
    h_rt, e, rank, gate, cnt = _route(x_all, mods, layer, norm_g, w_r, b_r, n_tiles)
    counts = cnt[:, 0].astype(jnp.int32)
    padded = (counts + BM - 1) // BM * BM
    pad_end = jnp.cumsum(padded)
    pad_start = pad_end - padded
    n_blocks = -(-(n * TOP_K) // BM) + N_EXPERTS
    starts = jnp.arange(n_blocks, dtype=jnp.int32) * BM
    block_exp = jnp.minimum(jnp.sum((pad_end[None, :] <= starts[:, None]).astype(jnp.int32), axis=1),
                            N_EXPERTS - 1)
    n_used = pad_end[-1:] // BM
    dest = _dest(pad_start, e, rank)
    xb, dest3 = _dispatch(h_rt, dest, pad_start + counts, pad_end, n_blocks)
    yb = _experts(xb, block_exp, n_used, counts, pad_start + counts, layer, w1, b1, w2, b2)
    return _combine(yb, dest3, gate, x_all, mods, layer, final_g, n_tiles, final_norm)


def kernel(x, c, ctx, c_ctx, mod_w, mod_b, norm1_g, norm2_g, fourier_w_in, fourier_w_out,
           hgrn_w_in, hgrn_lower_bounds, hgrn_norm_g, hgrn_w_out, router_w, router_b,
           expert_w1, expert_b1, expert_w2, expert_b2, final_norm_g):
    assert x.shape == (BATCH, SEQ, D) and ctx.shape == (BATCH, CTX_LEN, D)
    cond8 = jnp.zeros((8, D), F32).at[:BATCH].set(c).at[BATCH].set(c_ctx)
    mods = _adaln(cond8, mod_w, mod_b)
    x_all = jnp.concatenate([x.reshape(NLAT, D), ctx.reshape(NCTX, D)], axis=0)
    dch, mpos, mrow = _fourier_constants()
    experts = (expert_w1, expert_b1, expert_w2, expert_b2)

    vr, vi = _fourier_in(x_all, mods, 0, norm1_g[0], fourier_w_in[0], dch, mpos)
    x_all = _fourier_out(vr, vi, x_all, mods, 0, fourier_w_out[0], mrow)
    x_all = _moe(x_all, mods, 0, norm2_g[0], router_w[0], router_b[0], *experts,
                 final_norm_g, ALL_TILES, False)

    q, ff, fb, v, gs = _hgrn_in(x_all, mods, 1, norm1_g[1], hgrn_w_in[0], hgrn_lower_bounds)
    o_fw, o_bw = _scan(q, ff, fb, v)
    x_lat = _hgrn_out(o_fw, o_bw, gs, x_all, mods, 1, hgrn_norm_g[0], hgrn_w_out[0], LAT_TILES)
    out = _moe(x_lat, mods, 1, norm2_g[1], router_w[1], router_b[1], *experts,
               final_norm_g, LAT_TILES, True)
    return out.reshape(BATCH, SEQ, D)
```

```python
import functools

import numpy as np
import jax
import jax.numpy as jnp
from jax import lax
from jax.experimental import pallas as pl
from jax.experimental.pallas import tpu as pltpu

F32 = jnp.float32
BF16 = jnp.bfloat16
HIGHEST = lax.Precision.HIGHEST

D = 1024
BATCH = 2
SEQ = 8192
CTX_LEN = 256
GRID_W = 64
GRID_H = SEQ // GRID_W
NLAT = BATCH * SEQ
NCTX = BATCH * CTX_LEN
NTOK = NLAT + NCTX
TM = 256
LAT_TILES = NLAT // TM
ALL_TILES = NTOK // TM
TILES_PER_BATCH = SEQ // TM
FGROUPS = 4
FGDIM = D // FGROUPS
HEADS = 8
HDIM = D // HEADS
CHUNK = 128
N_EXPERTS = 32
TOP_K = 4
D_FF = 1024
SWIGLU_ALPHA = 1.702
SWIGLU_LIMIT = 7.0
BM = 512
LANES = 128
RT = D // LANES
NORM_EPS = 1e-6
VMEM_LIMIT = 56 * 1024 * 1024

_NT = (((1,), (1,)), ((), ()))
_TN = (((0,), (0,)), ((), ()))


def _cparams(sem, vmem=None, **kw):
    return pltpu.CompilerParams(dimension_semantics=sem, vmem_limit_bytes=vmem, **kw)


def _sigmoid(x):
    return 1.0 / (1.0 + jnp.exp(-x))


def _rms(x, g):
    return x * lax.rsqrt(jnp.mean(x * x, axis=-1, keepdims=True) + NORM_EPS) * g


def _tile_cond(i):
    return jnp.where(i < LAT_TILES, i // TILES_PER_BATCH, 2)


def _adaln_kernel(cond_ref, w_ref, b_ref, o_ref):
    c = cond_ref[...]
    s = c * _sigmoid(c)
    o_ref[...] = jnp.dot(s, w_ref[...], precision=HIGHEST,
                         preferred_element_type=F32) + b_ref[...]


def _adaln(cond8, mod_w, mod_b):
    depth = mod_w.shape[0]
    nb = 1536
    out = pl.pallas_call(
        _adaln_kernel,
        grid=(depth, 6 * D // nb),
        in_specs=[pl.BlockSpec((8, D), lambda l, j: (0, 0)),
                  pl.BlockSpec((None, D, nb), lambda l, j: (l, 0, j)),
                  pl.BlockSpec((None, 1, nb), lambda l, j: (l, 0, j))],
        out_specs=pl.BlockSpec((None, 8, nb), lambda l, j: (l, 0, j)),
        out_shape=jax.ShapeDtypeStruct((depth, 8, 6 * D), F32),
        compiler_params=_cparams(("arbitrary", "arbitrary"), VMEM_LIMIT),
        name="adaln",
    )(cond8, mod_w, mod_b.reshape(depth, 1, 6 * D))
    return out.reshape(depth * 8, 6, D)


def _dft_cs(n):
    k = np.arange(n)
    ang = 2.0 * np.pi * np.outer(k, k) / n
    s = 1.0 / np.sqrt(n)
    return np.cos(ang) * s, np.sin(ang) * s


def _fourier_constants():
    cd, sd = _dft_cs(FGDIM)
    dch = np.concatenate([cd, sd], axis=1)
    cc, sc = _dft_cs(GRID_W)
    eye = np.eye(TM // GRID_W)
    kc, ks = np.kron(eye, cc), np.kron(eye, sc)
    m_lat = np.block([[kc, -ks], [ks, kc]])
    cp, sp = _dft_cs(CTX_LEN)
    m_ctx = np.block([[cp, -sp], [sp, cp]])
    mpos = np.stack([m_lat, m_ctx])
    cr, sr = _dft_cs(GRID_H)
    mrow = np.concatenate([cr, -sr], axis=1)
    return (jnp.asarray(dch, BF16), jnp.asarray(mpos, BF16), jnp.asarray(mrow, BF16))


def _fourier_in_kernel(x_ref, mod_ref, g_ref, win_ref, dch_ref, mpos_ref, vr_ref, vi_ref):
    x = x_ref[...]
    h = _rms(x, g_ref[...]) * (1.0 + mod_ref[1:2, :]) + mod_ref[0:1, :]
    u = jnp.dot(h.astype(BF16), win_ref[...], preferred_element_type=F32).astype(BF16)
    parts = [jnp.dot(u[:, g * FGDIM:(g + 1) * FGDIM], dch_ref[...],
                     preferred_element_type=F32) for g in range(FGROUPS)]
    uc = jnp.concatenate([p[:, :FGDIM] for p in parts], axis=1)
    us = jnp.concatenate([p[:, FGDIM:] for p in parts], axis=1)
    st = jnp.concatenate([uc, us], axis=0).astype(BF16)
    v = jnp.dot(mpos_ref[...], st, preferred_element_type=F32)
    vr_ref[...] = v[:TM]
    vi_ref[...] = v[TM:]


def _fourier_in(x_all, mods, layer, norm_g, w_in, dch, mpos):
    tile = pl.BlockSpec((TM, D), lambda i: (i, 0))
    return pl.pallas_call(
        _fourier_in_kernel,
        grid=(ALL_TILES,),
        in_specs=[tile,
                  pl.BlockSpec((None, 6, D), lambda i: (layer * 8 + _tile_cond(i), 0, 0)),
                  pl.BlockSpec((1, D), lambda i: (0, 0)),
                  pl.BlockSpec((D, D), lambda i: (0, 0)),
                  pl.BlockSpec((FGDIM, 2 * FGDIM), lambda i: (0, 0)),
                  pl.BlockSpec((None, 2 * TM, 2 * TM), lambda i: (jnp.where(i < LAT_TILES, 0, 1), 0, 0))],
        out_specs=[tile, tile],
        out_shape=[jax.ShapeDtypeStruct((NTOK, D), F32)] * 2,
        compiler_params=_cparams(("arbitrary",), VMEM_LIMIT),
        name="fourier_in",
    )(x_all, mods, norm_g.reshape(1, D), w_in.astype(BF16), dch, mpos)


CB = 8


def _fourier_out_lat_kernel(vr_ref, vi_ref, x_ref, mrow_ref, wout_ref, mod_ref, o_ref):
    g1 = mod_ref[2:3, :]
    for c in range(CB):
        st = jnp.concatenate([vr_ref[:, c, :], vi_ref[:, c, :]], axis=0).astype(BF16)
        yf = jnp.dot(mrow_ref[...], st, preferred_element_type=F32)
        y = jnp.dot(yf.astype(BF16), wout_ref[...], preferred_element_type=F32)
        o_ref[:, c, :] = x_ref[:, c, :] + g1 * y


def _fourier_out_ctx_kernel(yr_ref, x_ref, wout_ref, mod_ref, o_ref):
    y = jnp.dot(yr_ref[...].astype(BF16), wout_ref[...], preferred_element_type=F32)
    o_ref[...] = x_ref[...] + mod_ref[2:3, :] * y


def _fourier_out(vr, vi, x_all, mods, layer, w_out, mrow):
    wout = w_out.astype(BF16)
    rows = NTOK // GRID_W
    v3 = lambda a: a.reshape(rows, GRID_W, D)
    blk = pl.BlockSpec((GRID_H, CB, D), lambda b, c: (b, c, 0))
    x_new = pl.pallas_call(
        _fourier_out_lat_kernel,
        grid=(BATCH, GRID_W // CB),
        in_specs=[blk, blk, blk,
                  pl.BlockSpec((GRID_H, 2 * GRID_H), lambda b, c: (0, 0)),
                  pl.BlockSpec((D, D), lambda b, c: (0, 0)),
                  pl.BlockSpec((None, 6, D), lambda b, c: (layer * 8 + b, 0, 0))],
        out_specs=blk,
        out_shape=jax.ShapeDtypeStruct((rows, GRID_W, D), F32),
        input_output_aliases={2: 0},
        compiler_params=_cparams(("arbitrary", "arbitrary"), VMEM_LIMIT),
        name="fourier_out_lat",
    )(v3(vr), v3(vi), v3(x_all), mrow, wout, mods).reshape(NTOK, D)
    ctile = pl.BlockSpec((TM, D), lambda i: (LAT_TILES + i, 0))
    return pl.pallas_call(
        _fourier_out_ctx_kernel,
        grid=(NCTX // TM,),
        in_specs=[ctile, ctile,
                  pl.BlockSpec((D, D), lambda i: (0, 0)),
                  pl.BlockSpec((None, 6, D), lambda i: (layer * 8 + 2, 0, 0))],
        out_specs=ctile,
        out_shape=jax.ShapeDtypeStruct((NTOK, D), F32),
        input_output_aliases={1: 0},
        compiler_params=_cparams(("arbitrary",), VMEM_LIMIT),
        name="fourier_out_ctx",
    )(vr, x_new, wout, mods)


def _hgrn_in_kernel(x_ref, mod_ref, g_ref, win_ref, hlb_ref, q_ref, ff_ref, fb_ref, v_ref, gs_ref,
                    *, layer):
    x = x_ref[...]
    h = (_rms(x, g_ref[...]) * (1.0 + mod_ref[1:2, :]) + mod_ref[0:1, :]).astype(BF16)
    raw = [hlb_ref[l] for l in range(hlb_ref.shape[0])]
    mx = functools.reduce(jnp.maximum, raw)
    ex = [jnp.exp(r - mx) for r in raw]
    den = functools.reduce(lambda a, b: a + b, ex)
    soft = [e / den for e in ex]
    lb = functools.reduce(lambda a, b: a + b, soft[:layer + 1]) - soft[0]

    def proj(j):
        return jnp.dot(h, win_ref[:, j * D:(j + 1) * D], preferred_element_type=F32)

    q = proj(0)
    q_ref[...] = q * _sigmoid(q)
    ff_ref[...] = lb[0:1, :] + (1.0 - lb[0:1, :]) * _sigmoid(proj(1))
    fb_ref[...] = lb[1:2, :] + (1.0 - lb[1:2, :]) * _sigmoid(proj(2))
    v_ref[...] = proj(3)
    g = proj(4)
    gs_ref[...] = g * _sigmoid(g)


def _hgrn_in(x_all, mods, layer, norm_g, w_in, hlb):
    tile = pl.BlockSpec((TM, D), lambda i: (i, 0))
    depth = hlb.shape[0]
    return pl.pallas_call(
        functools.partial(_hgrn_in_kernel, layer=layer),
        grid=(ALL_TILES,),
        in_specs=[tile,
                  pl.BlockSpec((None, 6, D), lambda i: (layer * 8 + _tile_cond(i), 0, 0)),
                  pl.BlockSpec((1, D), lambda i: (0, 0)),
                  pl.BlockSpec((D, 5 * D), lambda i: (0, 0)),
                  pl.BlockSpec((depth, 2, D), lambda i: (0, 0, 0))],
        out_specs=[tile] * 5,
        out_shape=[jax.ShapeDtypeStruct((NTOK, D), F32)] * 5,
        compiler_params=_cparams(("arbitrary",), VMEM_LIMIT),
        name="hgrn_in",
    )(x_all, mods, norm_g.reshape(1, D), w_in.astype(BF16), hlb)


N_LEVELS = 7
SUB = 8
NGRP = CHUNK // SUB
FINE_LEVELS = 3


def _scan_pair_kernel(qf_ref, ff_ref, vf_ref, qb_ref, fb_ref, vb_ref, of_ref, ob_ref, sf_ref, sb_ref):
    @pl.when(pl.program_id(1) == 0)
    def _():
        sf_ref[...] = jnp.zeros_like(sf_ref)
        sb_ref[...] = jnp.zeros_like(sb_ref)

    _scan_chunk(qf_ref, ff_ref, vf_ref, of_ref, sf_ref, rev=False)
    _scan_chunk(qb_ref, fb_ref, vb_ref, ob_ref, sb_ref, rev=True)


def _scan_chunk(q_ref, fg_ref, v_ref, o_ref, s_ref, *, rev):
    q = q_ref[...]
    fg = fg_ref[...]
    k = 1.0 - fg
    vb = v_ref[...].astype(BF16)
    t = lax.broadcasted_iota(jnp.int32, (CHUNK, D), 0)
    ti = lax.broadcasted_iota(jnp.int32, (CHUNK, CHUNK), 0)
    si = lax.broadcasted_iota(jnp.int32, (CHUNK, CHUNK), 1)
    scores = [None] * HEADS

    def add_level(qs, ks, mask):
        qb, kb = qs.astype(BF16), ks.astype(BF16)
        for h in range(HEADS):
            sl = slice(h * HDIM, (h + 1) * HDIM)
            sc = lax.dot_general(qb[:, sl], kb[:, sl], _NT, preferred_element_type=F32)
            sc = jnp.where(mask, sc, 0.0)
            scores[h] = sc if scores[h] is None else scores[h] + sc

    add_level(q, k, ti == si)
    run = fg
    rest = jnp.ones_like(fg)
    tot = fg
    for l in range(FINE_LEVELS):
        hbit = 1 << l
        odd = (t & hbit) != 0
        far = (t & hbit) == 0 if rev else odd
        t_far = (ti & hbit) == 0 if rev else (ti & hbit) != 0
        s_near = (si & hbit) != 0 if rev else (si & hbit) == 0
        add_level(run * q, rest * k,
                  ((ti >> (l + 1)) == (si >> (l + 1))) & t_far & s_near)
        sib = jnp.where(odd, pltpu.roll(tot, hbit, 0), pltpu.roll(tot, CHUNK - hbit, 0))
        run = jnp.where(far, run * sib, run)
        rest = jnp.where(far, rest, rest * sib)
        tot = tot * sib

    def groups(a):
        return [a[b * SUB:(b + 1) * SUB, :] for b in range(NGRP)]

    q_g, k_g, run_g, rest_g, tot_g = (groups(a) for a in (q, k, run, rest, tot))
    sc_g = [groups(scores[h]) for h in range(HEADS)]
    lane = lax.broadcasted_iota(jnp.int32, (SUB, CHUNK), 1)
    zero_g = jnp.zeros((SUB, D), F32)
    for l in range(FINE_LEVELS, N_LEVELS):
        bit = 1 << (l - FINE_LEVELS)
        is_far = [((b & bit) == 0) == rev for b in range(NGRP)]
        far_groups = [b for b in range(NGRP) if is_far[b]]
        qb = jnp.concatenate([run_g[b] * q_g[b] for b in far_groups], axis=0).astype(BF16)
        kb = jnp.concatenate([zero_g if is_far[b] else rest_g[b] * k_g[b] for b in range(NGRP)],
                             axis=0).astype(BF16)
        span = 2 << l
        keep = [None if span == CHUNK else
                (lane >= b * SUB // span * span) & (lane < b * SUB // span * span + span)
                for b in far_groups]
        for h in range(HEADS):
            sl = slice(h * HDIM, (h + 1) * HDIM)
            sc = lax.dot_general(qb[:, sl], kb[:, sl], _NT, preferred_element_type=F32)
            for i, b in enumerate(far_groups):
                piece = sc[i * SUB:(i + 1) * SUB, :]
                if keep[i] is not None:
                    piece = jnp.where(keep[i], piece, 0.0)
                sc_g[h][b] = sc_g[h][b] + piece
        sib_g = [tot_g[b ^ bit] for b in range(NGRP)]
        run_g = [run_g[b] * sib_g[b] if is_far[b] else run_g[b] for b in range(NGRP)]
        rest_g = [rest_g[b] if is_far[b] else rest_g[b] * sib_g[b] for b in range(NGRP)]
        tot_g = [tot_g[b] * sib_g[b] for b in range(NGRP)]
    qin = (q * jnp.concatenate(run_g, axis=0)).astype(BF16)
    kst = (k * jnp.concatenate(rest_g, axis=0)).astype(BF16)
    tot_row = tot_g[0][0:1, :]
    for h in range(HEADS):
        sl = slice(h * HDIM, (h + 1) * HDIM)
        st = s_ref[h]
        sc = jnp.concatenate(sc_g[h], axis=0).astype(BF16)
        o_ref[:, sl] = (jnp.dot(sc, vb[:, sl], preferred_element_type=F32)
                        + lax.dot_general(qin[:, sl], st.astype(BF16), _NT,
                                          preferred_element_type=F32))
        s_ref[h] = st * tot_row[:, sl] + lax.dot_general(vb[:, sl], kst[:, sl], _TN,
                                                         preferred_element_type=F32)


LAT_CHUNKS = SEQ // CHUNK
CTX_CHUNKS = CTX_LEN // CHUNK
SCAN_STEPS = CTX_CHUNKS + LAT_CHUNKS


def _scan(q, f_fw, f_bw, v):
    def idx_fw(b, s):
        return (jnp.where(s < CTX_CHUNKS, NLAT // CHUNK + CTX_CHUNKS * b + s,
                          LAT_CHUNKS * b + (s - CTX_CHUNKS)), 0)

    def idx_bw(b, s):
        return (jnp.where(s < CTX_CHUNKS, NLAT // CHUNK + CTX_CHUNKS * b + (CTX_CHUNKS - 1 - s),
                          LAT_CHUNKS * b + (SCAN_STEPS - 1 - s)), 0)
    fw = pl.BlockSpec((CHUNK, D), idx_fw)
    bw = pl.BlockSpec((CHUNK, D), idx_bw)
    state = pltpu.VMEM((HEADS, HDIM, HDIM), F32)
    return pl.pallas_call(
        _scan_pair_kernel,
        grid=(BATCH, SCAN_STEPS),
        in_specs=[fw, fw, fw, bw, bw, bw],
        out_specs=[fw, bw],
        out_shape=[jax.ShapeDtypeStruct((NTOK, D), F32)] * 2,
        scratch_shapes=[state, state],
        compiler_params=_cparams(("arbitrary", "arbitrary"), VMEM_LIMIT),
        name="scan",
    )(q, f_fw, v, q, f_bw, v)


def _hgrn_out_kernel(of_ref, ob_ref, gs_ref, x_ref, ng_ref, wout_ref, mod_ref, o_ref):
    o = of_ref[...] + ob_ref[...]
    parts = []
    for h in range(HEADS):
        oh = o[:, h * HDIM:(h + 1) * HDIM]
        parts.append(oh * lax.rsqrt(jnp.mean(oh * oh, axis=-1, keepdims=True) + NORM_EPS))
    on = jnp.concatenate(parts, axis=1) * ng_ref[...]
    y = jnp.dot((on * gs_ref[...]).astype(BF16), wout_ref[...], preferred_element_type=F32)
    o_ref[...] = x_ref[...] + mod_ref[2:3, :] * y


def _hgrn_out(o_fw, o_bw, gs, x_all, mods, layer, norm_g, w_out, n_tiles):
    tile = pl.BlockSpec((TM, D), lambda i: (i, 0))
    return pl.pallas_call(
        _hgrn_out_kernel,
        grid=(n_tiles,),
        in_specs=[tile, tile, tile, tile,
                  pl.BlockSpec((1, D), lambda i: (0, 0)),
                  pl.BlockSpec((D, D), lambda i: (0, 0)),
                  pl.BlockSpec((None, 6, D), lambda i: (layer * 8 + _tile_cond(i), 0, 0))],
        out_specs=tile,
        out_shape=jax.ShapeDtypeStruct((n_tiles * TM, D), F32),
        compiler_params=_cparams(("arbitrary",), VMEM_LIMIT),
        name="hgrn_out",
    )(o_fw, o_bw, gs, x_all, norm_g.reshape(1, D), w_out.astype(BF16), mods)


def _to_row_tiled(ref, val):
    for j in range(RT):
        ref[pl.ds(j, val.shape[0], stride=RT), :] = val[:, j * LANES:(j + 1) * LANES]


def _from_row_tiled(ref, n, base=0):
    return [ref[pl.ds(base + j, n, stride=RT), :] for j in range(RT)]


def _route_kernel(x_ref, mod_ref, g_ref, wrt_ref, brt_ref, h_ref, e_ref, rank_ref, gate_ref, cnt_ref):
    @pl.when(pl.program_id(0) == 0)
    def _():
        cnt_ref[...] = jnp.zeros_like(cnt_ref)

    h = _rms(x_ref[...], g_ref[...]) * (1.0 + mod_ref[4:5, :]) + mod_ref[3:4, :]
    _to_row_tiled(h_ref, h)
    w = wrt_ref[...]
    w_hi = w.astype(BF16)
    w_lo = (w - w_hi.astype(F32)).astype(BF16)
    h_hi = h.astype(BF16)
    h_lo = (h - h_hi.astype(F32)).astype(BF16)
    part = lax.dot_general(jnp.concatenate([w_hi, w_lo], axis=0), h_hi, _NT,
                           preferred_element_type=F32)
    logits = (part[:N_EXPERTS] + part[N_EXPERTS:]
              + lax.dot_general(w_hi, h_lo, _NT, preferred_element_type=F32)
              + brt_ref[:, 0:1])
    row = lax.broadcasted_iota(jnp.int32, (N_EXPERTS, TM), 0).astype(F32)
    vals = logits
    sel = jnp.zeros((N_EXPERTS, TM), F32)
    tops, idxs = [], []
    for _ in range(TOP_K):
        m = jnp.max(vals, axis=0, keepdims=True)
        idx = jnp.min(jnp.where(vals == m, row, float(N_EXPERTS)), axis=0, keepdims=True)
        hit = row == idx
        vals = jnp.where(hit, -jnp.inf, vals)
        sel = jnp.where(hit, 1.0, sel)
        tops.append(m)
        idxs.append(idx)
    ex = [jnp.exp(m - tops[0]) for m in tops]
    den = ex[0] + ex[1] + ex[2] + ex[3]
    r = lax.broadcasted_iota(jnp.int32, (TM, TM), 0)
    c = lax.broadcasted_iota(jnp.int32, (TM, TM), 1)
    before = jnp.where(r < c, 1.0, 0.0).astype(BF16)
    pref = jnp.dot(sel.astype(BF16), before, preferred_element_type=F32) + cnt_ref[:, 0:1]
    slot = lax.broadcasted_iota(jnp.int32, (8, TM), 0)
    e_out = jnp.zeros((8, TM), F32)
    rank_out = jnp.zeros((8, TM), F32)
    gate_out = jnp.zeros((8, TM), F32)
    for kk in range(TOP_K):
        rank = jnp.sum(jnp.where(row == idxs[kk], pref, 0.0), axis=0, keepdims=True)
        e_out = jnp.where(slot == kk, idxs[kk], e_out)
        rank_out = jnp.where(slot == kk, rank, rank_out)
        gate_out = jnp.where(slot == kk, ex[kk] / den, gate_out)
    e_ref[...] = e_out.astype(jnp.int32)
    rank_ref[...] = rank_out.astype(jnp.int32)
    gate_ref[...] = gate_out
    cnt_ref[...] += jnp.sum(sel, axis=1, keepdims=True)


def _route(x_all, mods, layer, norm_g, w_r, b_r, n_tiles):
    tile = pl.BlockSpec((TM, D), lambda i: (i, 0))
    small = pl.BlockSpec((None, 8, TM), lambda i: (i, 0, 0))
    n = n_tiles * TM
    return pl.pallas_call(
        _route_kernel,
        grid=(n_tiles,),
        in_specs=[tile,
                  pl.BlockSpec((None, 6, D), lambda i: (layer * 8 + _tile_cond(i), 0, 0)),
                  pl.BlockSpec((1, D), lambda i: (0, 0)),
                  pl.BlockSpec((N_EXPERTS, D), lambda i: (0, 0)),
                  pl.BlockSpec((N_EXPERTS, LANES), lambda i: (0, 0))],
        out_specs=[pl.BlockSpec((TM * RT, LANES), lambda i: (i, 0)), small, small, small,
                   pl.BlockSpec((N_EXPERTS, LANES), lambda i: (0, 0))],
        out_shape=[jax.ShapeDtypeStruct((n * RT, LANES), F32),
                   jax.ShapeDtypeStruct((n_tiles, 8, TM), jnp.int32),
                   jax.ShapeDtypeStruct((n_tiles, 8, TM), jnp.int32),
                   jax.ShapeDtypeStruct((n_tiles, 8, TM), F32),
                   jax.ShapeDtypeStruct((N_EXPERTS, LANES), F32)],
        compiler_params=_cparams(("arbitrary",), VMEM_LIMIT),
        name="route",
    )(x_all, mods, norm_g.reshape(1, D), w_r.T, jnp.broadcast_to(b_r[:, None], (N_EXPERTS, LANES)))


def _dest_kernel(ps_ref, e_ref, rank_ref, d_ref):
    e = e_ref[...]
    acc = rank_ref[...]
    for j in range(N_EXPERTS):
        acc = acc + jnp.where(e == j, ps_ref[j], 0)
    d_ref[...] = acc


def _dest(pad_start, e, rank):
    full = pl.BlockSpec(e.shape, lambda i, ps: (0, 0, 0))
    return pl.pallas_call(
        _dest_kernel,
        grid_spec=pltpu.PrefetchScalarGridSpec(num_scalar_prefetch=1, grid=(1,),
                                               in_specs=[full, full], out_specs=full),
        out_shape=jax.ShapeDtypeStruct(e.shape, jnp.int32),
        name="dest",
    )(pad_start, e, rank)


PAD_PIECES = tuple(1 << p for p in reversed(range(BM.bit_length() - 1)))


def _zero_fill(lo_ref, hi_ref, xb_hbm, zeros, sem, n_blocks, wait):
    def go(rows, row0):
        cp = pltpu.make_async_copy(zeros.at[pl.ds(0, rows * RT), :],
                                   xb_hbm.at[pl.ds(row0 * RT, rows * RT), :], sem.at[1])
        cp.wait() if wait else cp.start()

    def per_expert(e, carry):
        row = lo_ref[e]
        n = hi_ref[e] - row
        for piece in PAD_PIECES:
            @pl.when((n & piece) != 0)
            def _():
                go(piece, row)
            row = row + (n & piece)
        return carry
    lax.fori_loop(0, N_EXPERTS, per_expert, 0)

    def per_block(b, carry):
        go(BM, b * BM)
        return carry
    lax.fori_loop(hi_ref[N_EXPERTS - 1] // BM, n_blocks, per_block, 0)


def _dispatch_kernel(lo_ref, hi_ref, dest_ref, h_ref, xb_hbm, zeros, sem, *, n_blocks):
    i = pl.program_id(0)

    @pl.when(i == 0)
    def _():
        zeros[...] = jnp.zeros_like(zeros)
        _zero_fill(lo_ref, hi_ref, xb_hbm, zeros, sem, n_blocks, wait=False)

    def body(t, carry):
        src = h_ref.at[pl.ds(t * RT, RT), :]
        for kk in range(TOP_K):
            d = dest_ref[0, 0, kk * TM + t]
            pltpu.make_async_copy(src, xb_hbm.at[pl.ds(d * RT, RT), :],
                                  sem.at[0]).start(priority=kk % 2)
        return carry
    lax.fori_loop(0, TM, body, 0, unroll=4)
    for kk in range(TOP_K):
        pltpu.make_async_copy(h_ref, xb_hbm.at[pl.ds(0, TM * RT), :], sem.at[0]).wait()

    @pl.when(i == pl.num_programs(0) - 1)
    def _():
        _zero_fill(lo_ref, hi_ref, xb_hbm, zeros, sem, n_blocks, wait=True)


def _dispatch(h_rt, dest, pad_lo, pad_hi, n_blocks):
    n_tiles = dest.shape[0]
    dest3 = dest[:, :TOP_K, :].reshape(n_tiles, 1, TOP_K * TM)
    grid_spec = pltpu.PrefetchScalarGridSpec(
        num_scalar_prefetch=2,
        grid=(n_tiles,),
        in_specs=[pl.BlockSpec((1, 1, TOP_K * TM), lambda i, lo, hi: (i, 0, 0),
                               memory_space=pltpu.SMEM),
                  pl.BlockSpec((TM * RT, LANES), lambda i, lo, hi: (i, 0))],
        out_specs=pl.BlockSpec(memory_space=pl.ANY),
        scratch_shapes=[pltpu.VMEM((BM * RT, LANES), F32),
                        pltpu.SemaphoreType.DMA((2,))],
    )
    xb = pl.pallas_call(
        functools.partial(_dispatch_kernel, n_blocks=n_blocks),
        grid_spec=grid_spec,
        out_shape=jax.ShapeDtypeStruct((n_blocks * BM * RT, LANES), F32),
        compiler_params=_cparams(("arbitrary",), VMEM_LIMIT, disable_bounds_checks=True),
        name="dispatch",
    )(pad_lo, pad_hi, dest3, h_rt)
    return xb, dest3


def _expert_kernel(bexp_ref, nused_ref, first_ref, wslot_ref, next_ref, xb_ref, w1_hbm, b1_ref,
                   w2_hbm, b2_ref, y_ref, w1s, w2s, w1b, w2b, sem, *, layer):
    i = pl.program_id(0)
    n_used = nused_ref[0]

    def weight_copies(e, slot):
        return (pltpu.make_async_copy(w1_hbm.at[layer, e], w1s.at[slot], sem.at[0, slot]),
                pltpu.make_async_copy(w2_hbm.at[layer, e], w2s.at[slot], sem.at[1, slot]))

    @pl.when(i == 0)
    def _():
        for cp in weight_copies(bexp_ref[0], 0):
            cp.start()

    @pl.when(first_ref[i] == 1)
    def _():
        slot = wslot_ref[i]
        for cp in weight_copies(bexp_ref[i], slot):
            cp.wait()
        w1b[...] = w1s[slot].astype(BF16)
        w2b[...] = w2s[slot].astype(BF16)

        @pl.when(next_ref[i] >= 0)
        def _():
            for cp in weight_copies(next_ref[i], 1 - slot):
                cp.start()

    @pl.when(i < n_used)
    def _():
        x = jnp.concatenate(_from_row_tiled(xb_ref, BM), axis=1).astype(BF16)
        u = jnp.dot(x, w1b[...], preferred_element_type=F32) + b1_ref[...]
        glu = jnp.minimum(u[:, :D_FF], SWIGLU_LIMIT)
        lin = jnp.clip(u[:, D_FF:], -SWIGLU_LIMIT, SWIGLU_LIMIT)
        act = glu * _sigmoid(SWIGLU_ALPHA * glu) * (lin + 1.0)
        y = jnp.dot(act.astype(BF16), w2b[...], preferred_element_type=F32) + b2_ref[...]
        _to_row_tiled(y_ref, y)

    @pl.when(i >= n_used)
    def _():
        y_ref[...] = jnp.zeros_like(y_ref)


def _experts(xb, block_exp, n_used, counts, layer, w1, b1, w2, b2):
    n_blocks = xb.shape[0] // (BM * RT)
    used = jnp.arange(n_blocks, dtype=jnp.int32) < n_used[0]
    changed = jnp.concatenate([jnp.ones((1,), bool), block_exp[1:] != block_exp[:-1]])
    first = jnp.logical_and(used, changed).astype(jnp.int32)
    wslot = (jnp.cumsum(first) - 1) % 2
    ids = jnp.where(counts > 0, jnp.arange(N_EXPERTS, dtype=jnp.int32), N_EXPERTS)
    later = jnp.concatenate([lax.cummin(ids, reverse=True)[1:],
                             jnp.full((1,), N_EXPERTS, jnp.int32)])
    next_exp = jnp.where(later == N_EXPERTS, -1, later)[block_exp]
    smap = lambda f: (lambda i, be, nu, fi, ws, nx: f(i, be, nu))
    grid_spec = pltpu.PrefetchScalarGridSpec(
        num_scalar_prefetch=5,
        grid=(n_blocks,),
        in_specs=[
            pl.BlockSpec((BM * RT, LANES), smap(lambda i, be, nu: (jnp.minimum(i, nu[0] - 1), 0))),
            pl.BlockSpec(memory_space=pl.ANY),
            pl.BlockSpec((None, 1, 2 * D_FF), smap(lambda i, be, nu: (layer * N_EXPERTS + be[i], 0, 0))),
            pl.BlockSpec(memory_space=pl.ANY),
            pl.BlockSpec((None, 1, D), smap(lambda i, be, nu: (layer * N_EXPERTS + be[i], 0, 0))),
        ],
        out_specs=pl.BlockSpec((BM * RT, LANES), smap(lambda i, be, nu: (i, 0))),
        scratch_shapes=[pltpu.VMEM((2, D, 2 * D_FF), F32),
                        pltpu.VMEM((2, D_FF, D), F32),
                        pltpu.VMEM((D, 2 * D_FF), BF16),
                        pltpu.VMEM((D_FF, D), BF16),
                        pltpu.SemaphoreType.DMA((2, 2))],
    )
    return pl.pallas_call(
        functools.partial(_expert_kernel, layer=layer),
        grid_spec=grid_spec,
        out_shape=jax.ShapeDtypeStruct(xb.shape, F32),
        compiler_params=_cparams(("arbitrary",), VMEM_LIMIT),
        name="experts",
    )(block_exp, n_used, first, wslot.astype(jnp.int32), next_exp.astype(jnp.int32), xb, w1,
      b1.reshape(-1, 1, 2 * D_FF), w2, b2.reshape(-1, 1, D))


CG = 8
N_CG = TM // CG
CG_COPIES = TOP_K * TM // N_CG


def _combine_kernel(dest_ref, dest_next_ref, y_hbm, gate_ref, x_ref, mod_ref, fg_ref, o_ref, buf, gcol,
                    sem, *, final_norm):
    i = pl.program_id(0)
    n = pl.num_programs(0)
    slot = i % 2

    def start_rows(dest, to_slot, g):
        for u in range(CG_COPIES):
            r = g * CG_COPIES + u
            pltpu.make_async_copy(y_hbm.at[pl.ds(dest[0, 0, r] * RT, RT), :],
                                  buf.at[to_slot, pl.ds(r * RT, RT), :],
                                  sem.at[to_slot]).start(priority=u % 2)

    def wait_rows(s):
        pltpu.make_async_copy(y_hbm.at[pl.ds(0, TOP_K * TM * RT), :], buf.at[s], sem.at[s]).wait()

    @pl.when(i == 0)
    def _():
        def first(g, carry):
            start_rows(dest_ref, 0, g)
            return carry
        lax.fori_loop(0, N_CG, first, 0)

    wait_rows(slot)
    eye = (lax.broadcasted_iota(jnp.int32, (8, 8), 0)
           == lax.broadcasted_iota(jnp.int32, (8, 8), 1)).astype(F32)
    gcol[...] = lax.dot_general(gate_ref[...], eye, _TN, precision=HIGHEST,
                                preferred_element_type=F32)
    cur = buf.at[slot]
    g2 = mod_ref[5:6, :]

    def body(g, carry):
        start_rows(dest_next_ref, 1 - slot, g)
        tok = pl.multiple_of(g * CG, CG)
        gate = gcol[pl.ds(tok, CG), :]
        outs = []
        for j in range(RT):
            acc = None
            for kk in range(TOP_K):
                piece = gate[:, kk:kk + 1] * cur[pl.ds((kk * TM + g * CG) * RT + j, CG, stride=RT), :]
                acc = piece if acc is None else acc + piece
            outs.append(acc)
        out = x_ref[pl.ds(tok, CG), :] + g2 * jnp.concatenate(outs, axis=1)
        if final_norm:
            out = _rms(out, fg_ref[...])
        o_ref[pl.ds(tok, CG), :] = out
        return carry
    lax.fori_loop(0, N_CG, body, 0)

    @pl.when(i == n - 1)
    def _():
        wait_rows(1 - slot)


def _combine(yb, dest3, gate, x_all, mods, layer, final_g, n_tiles, final_norm):
    last = n_tiles - 1
    tile = pl.BlockSpec((TM, D), lambda i: (i, 0))
    return pl.pallas_call(
        functools.partial(_combine_kernel, final_norm=final_norm),
        grid=(n_tiles,),
        in_specs=[pl.BlockSpec((1, 1, TOP_K * TM), lambda i: (i, 0, 0), memory_space=pltpu.SMEM),
                  pl.BlockSpec((1, 1, TOP_K * TM), lambda i: (jnp.minimum(i + 1, last), 0, 0),
                               memory_space=pltpu.SMEM),
                  pl.BlockSpec(memory_space=pl.ANY),
                  pl.BlockSpec((None, 8, TM), lambda i: (i, 0, 0)),
                  tile,
                  pl.BlockSpec((None, 6, D), lambda i: (layer * 8 + _tile_cond(i), 0, 0)),
                  pl.BlockSpec((1, D), lambda i: (0, 0))],
        out_specs=tile,
        out_shape=jax.ShapeDtypeStruct((n_tiles * TM, D), F32),
        scratch_shapes=[pltpu.VMEM((2, TOP_K * TM * RT, LANES), F32),
                        pltpu.VMEM((TM, 8), F32),
                        pltpu.SemaphoreType.DMA((2,))],
        compiler_params=_cparams(("arbitrary",), VMEM_LIMIT, disable_bounds_checks=True),
        name="combine",
    )(dest3, dest3, yb, gate, x_all, mods, final_g.reshape(1, D))


def _moe(x_all, mods, layer, norm_g, w_r, b_r, w1, b1, w2, b2, final_g, n_tiles, final_norm):
    n = n_tiles * TM
    h_rt, e, rank, gate, cnt = _route(x_all, mods, layer, norm_g, w_r, b_r, n_tiles)
    counts = cnt[:, 0].astype(jnp.int32)
    padded = (counts + BM - 1) // BM * BM
    pad_end = jnp.cumsum(padded)
    pad_start = pad_end - padded
    n_blocks = -(-(n * TOP_K) // BM) + N_EXPERTS
    starts = jnp.arange(n_blocks, dtype=jnp.int32) * BM
    block_exp = jnp.minimum(jnp.sum((pad_end[None, :] <= starts[:, None]).astype(jnp.int32), axis=1),
                            N_EXPERTS - 1)
    n_used = pad_end[-1:] // BM
    dest = _dest(pad_start, e, rank)
    xb, dest3 = _dispatch(h_rt, dest, pad_start + counts, pad_end, n_blocks)
    yb = _experts(xb, block_exp, n_used, counts, layer, w1, b1, w2, b2)
    return _combine(yb, dest3, gate, x_all, mods, layer, final_g, n_tiles, final_norm)


def kernel(x, c, ctx, c_ctx, mod_w, mod_b, norm1_g, norm2_g, fourier_w_in, fourier_w_out,
           hgrn_w_in, hgrn_lower_bounds, hgrn_norm_g, hgrn_w_out, router_w, router_b,
           expert_w1, expert_b1, expert_w2, expert_b2, final_norm_g):
    assert x.shape == (BATCH, SEQ, D) and ctx.shape == (BATCH, CTX_LEN, D)
    cond8 = jnp.zeros((8, D), F32).at[:BATCH].set(c).at[BATCH].set(c_ctx)
    mods = _adaln(cond8, mod_w, mod_b)
    x_all = jnp.concatenate([x.reshape(NLAT, D), ctx.reshape(NCTX, D)], axis=0)
    dch, mpos, mrow = _fourier_constants()
    experts = (expert_w1, expert_b1, expert_w2, expert_b2)

    vr, vi = _fourier_in(x_all, mods, 0, norm1_g[0], fourier_w_in[0], dch, mpos)
    x_all = _fourier_out(vr, vi, x_all, mods, 0, fourier_w_out[0], mrow)
    x_all = _moe(x_all, mods, 0, norm2_g[0], router_w[0], router_b[0], *experts,
                 final_norm_g, ALL_TILES, False)

    q, ff, fb, v, gs = _hgrn_in(x_all, mods, 1, norm1_g[1], hgrn_w_in[0], hgrn_lower_bounds)
    o_fw, o_bw = _scan(q, ff, fb, v)
    x_lat = _hgrn_out(o_fw, o_bw, gs, x_all, mods, 1, hgrn_norm_g[0], hgrn_w_out[0], LAT_TILES)
    out = _moe(x_lat, mods, 1, norm2_g[1], router_w[1], router_b[1], *experts,
               final_norm_g, LAT_TILES, True)
    return out.reshape(BATCH, SEQ, D)
```

```python
import functools

import numpy as np
import jax
import jax.numpy as jnp
from jax import lax
from jax.experimental import pallas as pl
from jax.experimental.pallas import tpu as pltpu

F32 = jnp.float32
BF16 = jnp.bfloat16
HIGHEST = lax.Precision.HIGHEST

D = 1024
BATCH = 2
SEQ = 8192
CTX_LEN = 256
GRID_W = 64
GRID_H = SEQ // GRID_W
NLAT = BATCH * SEQ
NCTX = BATCH * CTX_LEN
NTOK = NLAT + NCTX
TM = 256
LAT_TILES = NLAT // TM
ALL_TILES = NTOK // TM
TILES_PER_BATCH = SEQ // TM
FGROUPS = 4
FGDIM = D // FGROUPS
HEADS = 8
HDIM = D // HEADS
CHUNK = 128
N_EXPERTS = 32
TOP_K = 4
D_FF = 1024
SWIGLU_ALPHA = 1.702
SWIGLU_LIMIT = 7.0
BM = 512
LANES = 128
RT = D // LANES
NORM_EPS = 1e-6
VMEM_LIMIT = 56 * 1024 * 1024

_NT = (((1,), (1,)), ((), ()))
_TN = (((0,), (0,)), ((), ()))


def _cparams(sem, vmem=None, **kw):
    return pltpu.CompilerParams(dimension_semantics=sem, vmem_limit_bytes=vmem, **kw)


def _sigmoid(x):
    return 1.0 / (1.0 + jnp.exp(-x))


def _rms(x, g):
    return x * lax.rsqrt(jnp.mean(x * x, axis=-1, keepdims=True) + NORM_EPS) * g


def _tile_cond(i):
    return jnp.where(i < LAT_TILES, i // TILES_PER_BATCH, 2)


def _adaln_kernel(cond_ref, w_ref, b_ref, o_ref):
    c = cond_ref[...]
    s = c * _sigmoid(c)
    o_ref[...] = jnp.dot(s, w_ref[...], precision=HIGHEST,
                         preferred_element_type=F32) + b_ref[...]


def _adaln(cond8, mod_w, mod_b):
    depth = mod_w.shape[0]
    nb = 1536
    out = pl.pallas_call(
        _adaln_kernel,
        grid=(depth, 6 * D // nb),
        in_specs=[pl.BlockSpec((8, D), lambda l, j: (0, 0)),
                  pl.BlockSpec((None, D, nb), lambda l, j: (l, 0, j)),
                  pl.BlockSpec((None, 1, nb), lambda l, j: (l, 0, j))],
        out_specs=pl.BlockSpec((None, 8, nb), lambda l, j: (l, 0, j)),
        out_shape=jax.ShapeDtypeStruct((depth, 8, 6 * D), F32),
        compiler_params=_cparams(("arbitrary", "arbitrary"), VMEM_LIMIT),
        name="adaln",
    )(cond8, mod_w, mod_b.reshape(depth, 1, 6 * D))
    return out.reshape(depth * 8, 6, D)


def _dft_cs(n):
    k = np.arange(n)
    ang = 2.0 * np.pi * np.outer(k, k) / n
    s = 1.0 / np.sqrt(n)
    return np.cos(ang) * s, np.sin(ang) * s


def _fourier_constants():
    cd, sd = _dft_cs(FGDIM)
    dch = np.concatenate([cd, sd], axis=1)
    cc, sc = _dft_cs(GRID_W)
    eye = np.eye(TM // GRID_W)
    kc, ks = np.kron(eye, cc), np.kron(eye, sc)
    m_lat = np.block([[kc, -ks], [ks, kc]])
    cp, sp = _dft_cs(CTX_LEN)
    m_ctx = np.block([[cp, -sp], [sp, cp]])
    mpos = np.stack([m_lat, m_ctx])
    cr, sr = _dft_cs(GRID_H)
    mrow = np.concatenate([cr, -sr], axis=1)
    return (jnp.asarray(dch, BF16), jnp.asarray(mpos, BF16), jnp.asarray(mrow, BF16))


def _fourier_in_kernel(x_ref, mod_ref, g_ref, win_ref, dch_ref, mpos_ref, vr_ref, vi_ref):
    x = x_ref[...]
    h = _rms(x, g_ref[...]) * (1.0 + mod_ref[1:2, :]) + mod_ref[0:1, :]
    u = jnp.dot(h.astype(BF16), win_ref[...], preferred_element_type=F32).astype(BF16)
    parts = [jnp.dot(u[:, g * FGDIM:(g + 1) * FGDIM], dch_ref[...],
                     preferred_element_type=F32) for g in range(FGROUPS)]
    uc = jnp.concatenate([p[:, :FGDIM] for p in parts], axis=1)
    us = jnp.concatenate([p[:, FGDIM:] for p in parts], axis=1)
    st = jnp.concatenate([uc, us], axis=0).astype(BF16)
    v = jnp.dot(mpos_ref[...], st, preferred_element_type=F32)
    vr_ref[...] = v[:TM]
    vi_ref[...] = v[TM:]


def _fourier_in(x_all, mods, layer, norm_g, w_in, dch, mpos):
    tile = pl.BlockSpec((TM, D), lambda i: (i, 0))
    return pl.pallas_call(
        _fourier_in_kernel,
        grid=(ALL_TILES,),
        in_specs=[tile,
                  pl.BlockSpec((None, 6, D), lambda i: (layer * 8 + _tile_cond(i), 0, 0)),
                  pl.BlockSpec((1, D), lambda i: (0, 0)),
                  pl.BlockSpec((D, D), lambda i: (0, 0)),
                  pl.BlockSpec((FGDIM, 2 * FGDIM), lambda i: (0, 0)),
                  pl.BlockSpec((None, 2 * TM, 2 * TM), lambda i: (jnp.where(i < LAT_TILES, 0, 1), 0, 0))],
        out_specs=[tile, tile],
        out_shape=[jax.ShapeDtypeStruct((NTOK, D), F32)] * 2,
        compiler_params=_cparams(("arbitrary",), VMEM_LIMIT),
        name="fourier_in",
    )(x_all, mods, norm_g.reshape(1, D), w_in.astype(BF16), dch, mpos)


CB = 8


def _fourier_out_lat_kernel(vr_ref, vi_ref, x_ref, mrow_ref, wout_ref, mod_ref, o_ref):
    g1 = mod_ref[2:3, :]
    yf = []
    for c in range(CB):
        st = jnp.concatenate([vr_ref[:, c, :], vi_ref[:, c, :]], axis=0).astype(BF16)
        yf.append(jnp.dot(mrow_ref[...], st, preferred_element_type=F32).astype(BF16))
    y = jnp.dot(jnp.concatenate(yf, axis=0), wout_ref[...], preferred_element_type=F32)
    for c in range(CB):
        o_ref[:, c, :] = x_ref[:, c, :] + g1 * y[c * GRID_H:(c + 1) * GRID_H, :]


def _fourier_out_ctx_kernel(yr_ref, x_ref, wout_ref, mod_ref, o_ref):
    y = jnp.dot(yr_ref[...].astype(BF16), wout_ref[...], preferred_element_type=F32)
    o_ref[...] = x_ref[...] + mod_ref[2:3, :] * y


def _fourier_out(vr, vi, x_all, mods, layer, w_out, mrow):
    wout = w_out.astype(BF16)
    rows = NTOK // GRID_W
    v3 = lambda a: a.reshape(rows, GRID_W, D)
    blk = pl.BlockSpec((GRID_H, CB, D), lambda b, c: (b, c, 0))
    x_new = pl.pallas_call(
        _fourier_out_lat_kernel,
        grid=(BATCH, GRID_W // CB),
        in_specs=[blk, blk, blk,
                  pl.BlockSpec((GRID_H, 2 * GRID_H), lambda b, c: (0, 0)),
                  pl.BlockSpec((D, D), lambda b, c: (0, 0)),
                  pl.BlockSpec((None, 6, D), lambda b, c: (layer * 8 + b, 0, 0))],
        out_specs=blk,
        out_shape=jax.ShapeDtypeStruct((rows, GRID_W, D), F32),
        input_output_aliases={2: 0},
        compiler_params=_cparams(("arbitrary", "arbitrary"), VMEM_LIMIT),
        name="fourier_out_lat",
    )(v3(vr), v3(vi), v3(x_all), mrow, wout, mods).reshape(NTOK, D)
    ctile = pl.BlockSpec((TM, D), lambda i: (LAT_TILES + i, 0))
    return pl.pallas_call(
        _fourier_out_ctx_kernel,
        grid=(NCTX // TM,),
        in_specs=[ctile, ctile,
                  pl.BlockSpec((D, D), lambda i: (0, 0)),
                  pl.BlockSpec((None, 6, D), lambda i: (layer * 8 + 2, 0, 0))],
        out_specs=ctile,
        out_shape=jax.ShapeDtypeStruct((NTOK, D), F32),
        input_output_aliases={1: 0},
        compiler_params=_cparams(("arbitrary",), VMEM_LIMIT),
        name="fourier_out_ctx",
    )(vr, x_new, wout, mods)


def _hgrn_in_kernel(x_ref, mod_ref, g_ref, win_ref, hlb_ref, q_ref, ff_ref, fb_ref, v_ref, gs_ref,
                    *, layer):
    x = x_ref[...]
    h = (_rms(x, g_ref[...]) * (1.0 + mod_ref[1:2, :]) + mod_ref[0:1, :]).astype(BF16)
    raw = [hlb_ref[l] for l in range(hlb_ref.shape[0])]
    mx = functools.reduce(jnp.maximum, raw)
    ex = [jnp.exp(r - mx) for r in raw]
    den = functools.reduce(lambda a, b: a + b, ex)
    soft = [e / den for e in ex]
    lb = functools.reduce(lambda a, b: a + b, soft[:layer + 1]) - soft[0]

    def proj(j):
        return jnp.dot(h, win_ref[:, j * D:(j + 1) * D], preferred_element_type=F32)

    q = proj(0)
    q_ref[...] = q * _sigmoid(q)
    ff_ref[...] = lb[0:1, :] + (1.0 - lb[0:1, :]) * _sigmoid(proj(1))
    fb_ref[...] = lb[1:2, :] + (1.0 - lb[1:2, :]) * _sigmoid(proj(2))
    v_ref[...] = proj(3)
    g = proj(4)
    gs_ref[...] = g * _sigmoid(g)


def _hgrn_in(x_all, mods, layer, norm_g, w_in, hlb):
    tile = pl.BlockSpec((TM, D), lambda i: (i, 0))
    depth = hlb.shape[0]
    return pl.pallas_call(
        functools.partial(_hgrn_in_kernel, layer=layer),
        grid=(ALL_TILES,),
        in_specs=[tile,
                  pl.BlockSpec((None, 6, D), lambda i: (layer * 8 + _tile_cond(i), 0, 0)),
                  pl.BlockSpec((1, D), lambda i: (0, 0)),
                  pl.BlockSpec((D, 5 * D), lambda i: (0, 0)),
                  pl.BlockSpec((depth, 2, D), lambda i: (0, 0, 0))],
        out_specs=[tile] * 5,
        out_shape=[jax.ShapeDtypeStruct((NTOK, D), F32)] * 5,
        compiler_params=_cparams(("arbitrary",), VMEM_LIMIT),
        name="hgrn_in",
    )(x_all, mods, norm_g.reshape(1, D), w_in.astype(BF16), hlb)


N_LEVELS = 7
SUB = 8
NGRP = CHUNK // SUB
FINE_LEVELS = 3


def _scan_pair_kernel(qf_ref, ff_ref, vf_ref, qb_ref, fb_ref, vb_ref, of_ref, ob_ref, sf_ref, sb_ref):
    @pl.when(pl.program_id(1) == 0)
    def _():
        sf_ref[...] = jnp.zeros_like(sf_ref)
        sb_ref[...] = jnp.zeros_like(sb_ref)

    _scan_chunk(qf_ref, ff_ref, vf_ref, of_ref, sf_ref, rev=False)
    _scan_chunk(qb_ref, fb_ref, vb_ref, ob_ref, sb_ref, rev=True)


def _scan_chunk(q_ref, fg_ref, v_ref, o_ref, s_ref, *, rev):
    q = q_ref[...]
    fg = fg_ref[...]
    k = 1.0 - fg
    vb = v_ref[...].astype(BF16)
    t = lax.broadcasted_iota(jnp.int32, (CHUNK, D), 0)
    ti = lax.broadcasted_iota(jnp.int32, (CHUNK, CHUNK), 0)
    si = lax.broadcasted_iota(jnp.int32, (CHUNK, CHUNK), 1)
    scores = [None] * HEADS

    def add_level(qs, ks, mask):
        qb, kb = qs.astype(BF16), ks.astype(BF16)
        for h in range(HEADS):
            sl = slice(h * HDIM, (h + 1) * HDIM)
            sc = lax.dot_general(qb[:, sl], kb[:, sl], _NT, preferred_element_type=F32)
            sc = jnp.where(mask, sc, 0.0)
            scores[h] = sc if scores[h] is None else scores[h] + sc

    add_level(q, k, ti == si)
    run = fg
    rest = jnp.ones_like(fg)
    tot = fg
    for l in range(FINE_LEVELS):
        hbit = 1 << l
        odd = (t & hbit) != 0
        far = (t & hbit) == 0 if rev else odd
        t_far = (ti & hbit) == 0 if rev else (ti & hbit) != 0
        s_near = (si & hbit) != 0 if rev else (si & hbit) == 0
        add_level(run * q, rest * k,
                  ((ti >> (l + 1)) == (si >> (l + 1))) & t_far & s_near)
        sib = jnp.where(odd, pltpu.roll(tot, hbit, 0), pltpu.roll(tot, CHUNK - hbit, 0))
        run = jnp.where(far, run * sib, run)
        rest = jnp.where(far, rest, rest * sib)
        tot = tot * sib

    def groups(a):
        return [a[b * SUB:(b + 1) * SUB, :] for b in range(NGRP)]

    q_g, k_g, run_g, rest_g, tot_g = (groups(a) for a in (q, k, run, rest, tot))
    sc_g = [groups(scores[h]) for h in range(HEADS)]
    lane = lax.broadcasted_iota(jnp.int32, (SUB, CHUNK), 1)
    zero_g = jnp.zeros((SUB, D), F32)
    for l in range(FINE_LEVELS, N_LEVELS):
        bit = 1 << (l - FINE_LEVELS)
        is_far = [((b & bit) == 0) == rev for b in range(NGRP)]
        far_groups = [b for b in range(NGRP) if is_far[b]]
        qb = jnp.concatenate([run_g[b] * q_g[b] for b in far_groups], axis=0).astype(BF16)
        kb = jnp.concatenate([zero_g if is_far[b] else rest_g[b] * k_g[b] for b in range(NGRP)],
                             axis=0).astype(BF16)
        span = 2 << l
        keep = [None if span == CHUNK else
                (lane >= b * SUB // span * span) & (lane < b * SUB // span * span + span)
                for b in far_groups]
        for h in range(HEADS):
            sl = slice(h * HDIM, (h + 1) * HDIM)
            sc = lax.dot_general(qb[:, sl], kb[:, sl], _NT, preferred_element_type=F32)
            for i, b in enumerate(far_groups):
                piece = sc[i * SUB:(i + 1) * SUB, :]
                if keep[i] is not None:
                    piece = jnp.where(keep[i], piece, 0.0)
                sc_g[h][b] = sc_g[h][b] + piece
        sib_g = [tot_g[b ^ bit] for b in range(NGRP)]
        run_g = [run_g[b] * sib_g[b] if is_far[b] else run_g[b] for b in range(NGRP)]
        rest_g = [rest_g[b] if is_far[b] else rest_g[b] * sib_g[b] for b in range(NGRP)]
        tot_g = [tot_g[b] * sib_g[b] for b in range(NGRP)]
    qin = (q * jnp.concatenate(run_g, axis=0)).astype(BF16)
    kst = (k * jnp.concatenate(rest_g, axis=0)).astype(BF16)
    tot_row = tot_g[0][0:1, :]
    for h in range(HEADS):
        sl = slice(h * HDIM, (h + 1) * HDIM)
        st = s_ref[h]
        sc = jnp.concatenate(sc_g[h], axis=0).astype(BF16)
        o_ref[:, sl] = (jnp.dot(sc, vb[:, sl], preferred_element_type=F32)
                        + lax.dot_general(qin[:, sl], st.astype(BF16), _NT,
                                          preferred_element_type=F32))
        s_ref[h] = st * tot_row[:, sl] + lax.dot_general(vb[:, sl], kst[:, sl], _TN,
                                                         preferred_element_type=F32)


LAT_CHUNKS = SEQ // CHUNK
CTX_CHUNKS = CTX_LEN // CHUNK
SCAN_STEPS = CTX_CHUNKS + LAT_CHUNKS


def _scan(q, f_fw, f_bw, v):
    def idx_fw(b, s):
        return (jnp.where(s < CTX_CHUNKS, NLAT // CHUNK + CTX_CHUNKS * b + s,
                          LAT_CHUNKS * b + (s - CTX_CHUNKS)), 0)

    def idx_bw(b, s):
        return (jnp.where(s < CTX_CHUNKS, NLAT // CHUNK + CTX_CHUNKS * b + (CTX_CHUNKS - 1 - s),
                          LAT_CHUNKS * b + (SCAN_STEPS - 1 - s)), 0)
    fw = pl.BlockSpec((CHUNK, D), idx_fw)
    bw = pl.BlockSpec((CHUNK, D), idx_bw)
    state = pltpu.VMEM((HEADS, HDIM, HDIM), F32)
    return pl.pallas_call(
        _scan_pair_kernel,
        grid=(BATCH, SCAN_STEPS),
        in_specs=[fw, fw, fw, bw, bw, bw],
        out_specs=[fw, bw],
        out_shape=[jax.ShapeDtypeStruct((NTOK, D), F32)] * 2,
        scratch_shapes=[state, state],
        compiler_params=_cparams(("arbitrary", "arbitrary"), VMEM_LIMIT),
        name="scan",
    )(q, f_fw, v, q, f_bw, v)


def _hgrn_out_kernel(of_ref, ob_ref, gs_ref, x_ref, ng_ref, wout_ref, mod_ref, o_ref):
    o = of_ref[...] + ob_ref[...]
    parts = []
    for h in range(HEADS):
        oh = o[:, h * HDIM:(h + 1) * HDIM]
        parts.append(oh * lax.rsqrt(jnp.mean(oh * oh, axis=-1, keepdims=True) + NORM_EPS))
    on = jnp.concatenate(parts, axis=1) * ng_ref[...]
    y = jnp.dot((on * gs_ref[...]).astype(BF16), wout_ref[...], preferred_element_type=F32)
    o_ref[...] = x_ref[...] + mod_ref[2:3, :] * y


def _hgrn_out(o_fw, o_bw, gs, x_all, mods, layer, norm_g, w_out, n_tiles):
    tile = pl.BlockSpec((TM, D), lambda i: (i, 0))
    return pl.pallas_call(
        _hgrn_out_kernel,
        grid=(n_tiles,),
        in_specs=[tile, tile, tile, tile,
                  pl.BlockSpec((1, D), lambda i: (0, 0)),
                  pl.BlockSpec((D, D), lambda i: (0, 0)),
                  pl.BlockSpec((None, 6, D), lambda i: (layer * 8 + _tile_cond(i), 0, 0))],
        out_specs=tile,
        out_shape=jax.ShapeDtypeStruct((n_tiles * TM, D), F32),
        compiler_params=_cparams(("arbitrary",), VMEM_LIMIT),
        name="hgrn_out",
    )(o_fw, o_bw, gs, x_all, norm_g.reshape(1, D), w_out.astype(BF16), mods)


def _to_row_tiled(ref, val):
    for j in range(RT):
        ref[pl.ds(j, val.shape[0], stride=RT), :] = val[:, j * LANES:(j + 1) * LANES]


def _from_row_tiled(ref, n, base=0):
    return [ref[pl.ds(base + j, n, stride=RT), :] for j in range(RT)]


def _route_kernel(x_ref, mod_ref, g_ref, wrt_ref, brt_ref, h_ref, e_ref, rank_ref, gate_ref, cnt_ref):
    @pl.when(pl.program_id(0) == 0)
    def _():
        cnt_ref[...] = jnp.zeros_like(cnt_ref)

    h = _rms(x_ref[...], g_ref[...]) * (1.0 + mod_ref[4:5, :]) + mod_ref[3:4, :]
    _to_row_tiled(h_ref, h)
    w = wrt_ref[...]
    w_hi = w.astype(BF16)
    w_lo = (w - w_hi.astype(F32)).astype(BF16)
    h_hi = h.astype(BF16)
    h_lo = (h - h_hi.astype(F32)).astype(BF16)
    part = lax.dot_general(jnp.concatenate([w_hi, w_lo], axis=0), h_hi, _NT,
                           preferred_element_type=F32)
    logits = (part[:N_EXPERTS] + part[N_EXPERTS:]
              + lax.dot_general(w_hi, h_lo, _NT, preferred_element_type=F32)
              + brt_ref[:, 0:1])
    row = lax.broadcasted_iota(jnp.int32, (N_EXPERTS, TM), 0).astype(F32)
    vals = logits
    sel = jnp.zeros((N_EXPERTS, TM), F32)
    tops, idxs = [], []
    for _ in range(TOP_K):
        m = jnp.max(vals, axis=0, keepdims=True)
        idx = jnp.min(jnp.where(vals == m, row, float(N_EXPERTS)), axis=0, keepdims=True)
        hit = row == idx
        vals = jnp.where(hit, -jnp.inf, vals)
        sel = jnp.where(hit, 1.0, sel)
        tops.append(m)
        idxs.append(idx)
    ex = [jnp.exp(m - tops[0]) for m in tops]
    den = ex[0] + ex[1] + ex[2] + ex[3]
    r = lax.broadcasted_iota(jnp.int32, (TM, TM), 0)
    c = lax.broadcasted_iota(jnp.int32, (TM, TM), 1)
    before = jnp.where(r < c, 1.0, 0.0).astype(BF16)
    pref = jnp.dot(sel.astype(BF16), before, preferred_element_type=F32) + cnt_ref[:, 0:1]
    slot = lax.broadcasted_iota(jnp.int32, (8, TM), 0)
    e_out = jnp.zeros((8, TM), F32)
    rank_out = jnp.zeros((8, TM), F32)
    gate_out = jnp.zeros((8, TM), F32)
    for kk in range(TOP_K):
        rank = jnp.sum(jnp.where(row == idxs[kk], pref, 0.0), axis=0, keepdims=True)
        e_out = jnp.where(slot == kk, idxs[kk], e_out)
        rank_out = jnp.where(slot == kk, rank, rank_out)
        gate_out = jnp.where(slot == kk, ex[kk] / den, gate_out)
    e_ref[...] = e_out.astype(jnp.int32)
    rank_ref[...] = rank_out.astype(jnp.int32)
    gate_ref[...] = gate_out
    cnt_ref[...] += jnp.sum(sel, axis=1, keepdims=True)


def _route(x_all, mods, layer, norm_g, w_r, b_r, n_tiles):
    tile = pl.BlockSpec((TM, D), lambda i: (i, 0))
    small = pl.BlockSpec((None, 8, TM), lambda i: (i, 0, 0))
    n = n_tiles * TM
    return pl.pallas_call(
        _route_kernel,
        grid=(n_tiles,),
        in_specs=[tile,
                  pl.BlockSpec((None, 6, D), lambda i: (layer * 8 + _tile_cond(i), 0, 0)),
                  pl.BlockSpec((1, D), lambda i: (0, 0)),
                  pl.BlockSpec((N_EXPERTS, D), lambda i: (0, 0)),
                  pl.BlockSpec((N_EXPERTS, LANES), lambda i: (0, 0))],
        out_specs=[pl.BlockSpec((TM * RT, LANES), lambda i: (i, 0)), small, small, small,
                   pl.BlockSpec((N_EXPERTS, LANES), lambda i: (0, 0))],
        out_shape=[jax.ShapeDtypeStruct((n * RT, LANES), F32),
                   jax.ShapeDtypeStruct((n_tiles, 8, TM), jnp.int32),
                   jax.ShapeDtypeStruct((n_tiles, 8, TM), jnp.int32),
                   jax.ShapeDtypeStruct((n_tiles, 8, TM), F32),
                   jax.ShapeDtypeStruct((N_EXPERTS, LANES), F32)],
        compiler_params=_cparams(("arbitrary",), VMEM_LIMIT),
        name="route",
    )(x_all, mods, norm_g.reshape(1, D), w_r.T, jnp.broadcast_to(b_r[:, None], (N_EXPERTS, LANES)))


def _dest_kernel(ps_ref, e_ref, rank_ref, d_ref):
    e = e_ref[...]
    acc = rank_ref[...]
    for j in range(N_EXPERTS):
        acc = acc + jnp.where(e == j, ps_ref[j], 0)
    d_ref[...] = acc


def _dest(pad_start, e, rank):
    full = pl.BlockSpec(e.shape, lambda i, ps: (0, 0, 0))
    return pl.pallas_call(
        _dest_kernel,
        grid_spec=pltpu.PrefetchScalarGridSpec(num_scalar_prefetch=1, grid=(1,),
                                               in_specs=[full, full], out_specs=full),
        out_shape=jax.ShapeDtypeStruct(e.shape, jnp.int32),
        name="dest",
    )(pad_start, e, rank)


PAD_PIECES = tuple(1 << p for p in reversed(range(BM.bit_length() - 1)))


def _zero_fill(lo_ref, hi_ref, xb_hbm, zeros, sem, n_blocks, wait):
    def go(rows, row0):
        cp = pltpu.make_async_copy(zeros.at[pl.ds(0, rows * RT), :],
                                   xb_hbm.at[pl.ds(row0 * RT, rows * RT), :], sem.at[1])
        cp.wait() if wait else cp.start()

    def per_expert(e, carry):
        row = lo_ref[e]
        n = hi_ref[e] - row
        for piece in PAD_PIECES:
            @pl.when((n & piece) != 0)
            def _():
                go(piece, row)
            row = row + (n & piece)
        return carry
    lax.fori_loop(0, N_EXPERTS, per_expert, 0)

    def per_block(b, carry):
        go(BM, b * BM)
        return carry
    lax.fori_loop(hi_ref[N_EXPERTS - 1] // BM, n_blocks, per_block, 0)


def _dispatch_kernel(lo_ref, hi_ref, dest_ref, h_ref, xb_hbm, zeros, sem, *, n_blocks):
    i = pl.program_id(0)

    @pl.when(i == 0)
    def _():
        zeros[...] = jnp.zeros_like(zeros)
        _zero_fill(lo_ref, hi_ref, xb_hbm, zeros, sem, n_blocks, wait=False)

    def body(t, carry):
        src = h_ref.at[pl.ds(t * RT, RT), :]
        for kk in range(TOP_K):
            d = dest_ref[0, 0, kk * TM + t]
            pltpu.make_async_copy(src, xb_hbm.at[pl.ds(d * RT, RT), :],
                                  sem.at[0]).start(priority=kk % 2)
        return carry
    lax.fori_loop(0, TM, body, 0, unroll=4)
    for kk in range(TOP_K):
        pltpu.make_async_copy(h_ref, xb_hbm.at[pl.ds(0, TM * RT), :], sem.at[0]).wait()

    @pl.when(i == pl.num_programs(0) - 1)
    def _():
        _zero_fill(lo_ref, hi_ref, xb_hbm, zeros, sem, n_blocks, wait=True)


def _dispatch(h_rt, dest, pad_lo, pad_hi, n_blocks):
    n_tiles = dest.shape[0]
    dest3 = dest[:, :TOP_K, :].reshape(n_tiles, 1, TOP_K * TM)
    grid_spec = pltpu.PrefetchScalarGridSpec(
        num_scalar_prefetch=2,
        grid=(n_tiles,),
        in_specs=[pl.BlockSpec((1, 1, TOP_K * TM), lambda i, lo, hi: (i, 0, 0),
                               memory_space=pltpu.SMEM),
                  pl.BlockSpec((TM * RT, LANES), lambda i, lo, hi: (i, 0))],
        out_specs=pl.BlockSpec(memory_space=pl.ANY),
        scratch_shapes=[pltpu.VMEM((BM * RT, LANES), F32),
                        pltpu.SemaphoreType.DMA((2,))],
    )
    xb = pl.pallas_call(
        functools.partial(_dispatch_kernel, n_blocks=n_blocks),
        grid_spec=grid_spec,
        out_shape=jax.ShapeDtypeStruct((n_blocks * BM * RT, LANES), F32),
        compiler_params=_cparams(("arbitrary",), VMEM_LIMIT, disable_bounds_checks=True),
        name="dispatch",
    )(pad_lo, pad_hi, dest3, h_rt)
    return xb, dest3


def _expert_kernel(bexp_ref, nused_ref, first_ref, wslot_ref, next_ref, xb_ref, w1_hbm, b1_ref,
                   w2_hbm, b2_ref, y_ref, w1s, w2s, w1b, w2b, sem, *, layer):
    i = pl.program_id(0)
    n_used = nused_ref[0]

    def weight_copies(e, slot):
        return (pltpu.make_async_copy(w1_hbm.at[layer, e], w1s.at[slot], sem.at[0, slot]),
                pltpu.make_async_copy(w2_hbm.at[layer, e], w2s.at[slot], sem.at[1, slot]))

    @pl.when(i == 0)
    def _():
        for cp in weight_copies(bexp_ref[0], 0):
            cp.start()

    @pl.when(first_ref[i] == 1)
    def _():
        slot = wslot_ref[i]
        for cp in weight_copies(bexp_ref[i], slot):
            cp.wait()
        w1b[...] = w1s[slot].astype(BF16)
        w2b[...] = w2s[slot].astype(BF16)

        @pl.when(next_ref[i] >= 0)
        def _():
            for cp in weight_copies(next_ref[i], 1 - slot):
                cp.start()

    @pl.when(i < n_used)
    def _():
        x = jnp.concatenate(_from_row_tiled(xb_ref, BM), axis=1).astype(BF16)
        u = jnp.dot(x, w1b[...], preferred_element_type=F32) + b1_ref[...]
        glu = jnp.minimum(u[:, :D_FF], SWIGLU_LIMIT)
        lin = jnp.clip(u[:, D_FF:], -SWIGLU_LIMIT, SWIGLU_LIMIT)
        act = glu * _sigmoid(SWIGLU_ALPHA * glu) * (lin + 1.0)
        y = jnp.dot(act.astype(BF16), w2b[...], preferred_element_type=F32) + b2_ref[...]
        _to_row_tiled(y_ref, y)

    @pl.when(i >= n_used)
    def _():
        y_ref[...] = jnp.zeros_like(y_ref)


def _experts(xb, block_exp, n_used, counts, layer, w1, b1, w2, b2):
    n_blocks = xb.shape[0] // (BM * RT)
    used = jnp.arange(n_blocks, dtype=jnp.int32) < n_used[0]
    changed = jnp.concatenate([jnp.ones((1,), bool), block_exp[1:] != block_exp[:-1]])
    first = jnp.logical_and(used, changed).astype(jnp.int32)
    wslot = (jnp.cumsum(first) - 1) % 2
    ids = jnp.where(counts > 0, jnp.arange(N_EXPERTS, dtype=jnp.int32), N_EXPERTS)
    later = jnp.concatenate([lax.cummin(ids, reverse=True)[1:],
                             jnp.full((1,), N_EXPERTS, jnp.int32)])
    next_exp = jnp.where(later == N_EXPERTS, -1, later)[block_exp]
    smap = lambda f: (lambda i, be, nu, fi, ws, nx: f(i, be, nu))
    grid_spec = pltpu.PrefetchScalarGridSpec(
        num_scalar_prefetch=5,
        grid=(n_blocks,),
        in_specs=[
            pl.BlockSpec((BM * RT, LANES), smap(lambda i, be, nu: (jnp.minimum(i, nu[0] - 1), 0))),
            pl.BlockSpec(memory_space=pl.ANY),
            pl.BlockSpec((None, 1, 2 * D_FF), smap(lambda i, be, nu: (layer * N_EXPERTS + be[i], 0, 0))),
            pl.BlockSpec(memory_space=pl.ANY),
            pl.BlockSpec((None, 1, D), smap(lambda i, be, nu: (layer * N_EXPERTS + be[i], 0, 0))),
        ],
        out_specs=pl.BlockSpec((BM * RT, LANES), smap(lambda i, be, nu: (i, 0))),
        scratch_shapes=[pltpu.VMEM((2, D, 2 * D_FF), F32),
                        pltpu.VMEM((2, D_FF, D), F32),
                        pltpu.VMEM((D, 2 * D_FF), BF16),
                        pltpu.VMEM((D_FF, D), BF16),
                        pltpu.SemaphoreType.DMA((2, 2))],
    )
    return pl.pallas_call(
        functools.partial(_expert_kernel, layer=layer),
        grid_spec=grid_spec,
        out_shape=jax.ShapeDtypeStruct(xb.shape, F32),
        compiler_params=_cparams(("arbitrary",), VMEM_LIMIT),
        name="experts",
    )(block_exp, n_used, first, wslot.astype(jnp.int32), next_exp.astype(jnp.int32), xb, w1,
      b1.reshape(-1, 1, 2 * D_FF), w2, b2.reshape(-1, 1, D))


def _combine_kernel(dest_ref, dest_next_ref, y_hbm, gate_ref, x_ref, mod_ref, fg_ref, o_ref, buf, sem,
                    *, final_norm):
    i = pl.program_id(0)
    n = pl.num_programs(0)
    slot = i % 2
    rows = TOP_K * TM

    def gather(dest, to_slot):
        def body(p, carry):
            for half in range(2):
                r = 2 * p + half
                pltpu.make_async_copy(y_hbm.at[pl.ds(dest[0, 0, r] * RT, RT), :],
                                      buf.at[to_slot, pl.ds(r * RT, RT), :],
                                      sem.at[to_slot]).start(priority=half)
            return carry
        lax.fori_loop(0, rows // 2, body, 0, unroll=4)

    @pl.when(i == 0)
    def _():
        gather(dest_ref, 0)

    @pl.when(i + 1 < n)
    def _():
        gather(dest_next_ref, 1 - slot)

    pltpu.make_async_copy(y_hbm.at[pl.ds(0, rows * RT), :], buf.at[slot], sem.at[slot]).wait()
    eye = (lax.broadcasted_iota(jnp.int32, (8, 8), 0)
           == lax.broadcasted_iota(jnp.int32, (8, 8), 1)).astype(F32)
    gate = lax.dot_general(gate_ref[...], eye, _TN, precision=HIGHEST,
                           preferred_element_type=F32)
    cur = buf.at[slot]
    outs = []
    for j in range(RT):
        acc = None
        for kk in range(TOP_K):
            piece = gate[:, kk:kk + 1] * cur[pl.ds(kk * TM * RT + j, TM, stride=RT), :]
            acc = piece if acc is None else acc + piece
        outs.append(acc)
    out = x_ref[...] + mod_ref[5:6, :] * jnp.concatenate(outs, axis=1)
    if final_norm:
        out = _rms(out, fg_ref[...])
    o_ref[...] = out


def _combine(yb, dest3, gate, x_all, mods, layer, final_g, n_tiles, final_norm):
    last = n_tiles - 1
    tile = pl.BlockSpec((TM, D), lambda i: (i, 0))
    return pl.pallas_call(
        functools.partial(_combine_kernel, final_norm=final_norm),
        grid=(n_tiles,),
        in_specs=[pl.BlockSpec((1, 1, TOP_K * TM), lambda i: (i, 0, 0), memory_space=pltpu.SMEM),
                  pl.BlockSpec((1, 1, TOP_K * TM), lambda i: (jnp.minimum(i + 1, last), 0, 0),
                               memory_space=pltpu.SMEM),
                  pl.BlockSpec(memory_space=pl.ANY),
                  pl.BlockSpec((None, 8, TM), lambda i: (i, 0, 0)),
                  tile,
                  pl.BlockSpec((None, 6, D), lambda i: (layer * 8 + _tile_cond(i), 0, 0)),
                  pl.BlockSpec((1, D), lambda i: (0, 0))],
        out_specs=tile,
        out_shape=jax.ShapeDtypeStruct((n_tiles * TM, D), F32),
        scratch_shapes=[pltpu.VMEM((2, TOP_K * TM * RT, LANES), F32),
                        pltpu.SemaphoreType.DMA((2,))],
        compiler_params=_cparams(("arbitrary",), VMEM_LIMIT, disable_bounds_checks=True),
        name="combine",
    )(dest3, dest3, yb, gate, x_all, mods, final_g.reshape(1, D))


def _moe(x_all, mods, layer, norm_g, w_r, b_r, w1, b1, w2, b2, final_g, n_tiles, final_norm):
    n = n_tiles * TM
    h_rt, e, rank, gate, cnt = _route(x_all, mods, layer, norm_g, w_r, b_r, n_tiles)
    counts = cnt[:, 0].astype(jnp.int32)
    padded = (counts + BM - 1) // BM * BM
    pad_end = jnp.cumsum(padded)
    pad_start = pad_end - padded
    n_blocks = -(-(n * TOP_K) // BM) + N_EXPERTS
    starts = jnp.arange(n_blocks, dtype=jnp.int32) * BM
    block_exp = jnp.minimum(jnp.sum((pad_end[None, :] <= starts[:, None]).astype(jnp.int32), axis=1),
                            N_EXPERTS - 1)
    n_used = pad_end[-1:] // BM
    dest = _dest(pad_start, e, rank)
    xb, dest3 = _dispatch(h_rt, dest, pad_start + counts, pad_end, n_blocks)
    yb = _experts(xb, block_exp, n_used, counts, layer, w1, b1, w2, b2)
    return _combine(yb, dest3, gate, x_all, mods, layer, final_g, n_tiles, final_norm)


def kernel(x, c, ctx, c_ctx, mod_w, mod_b, norm1_g, norm2_g, fourier_w_in, fourier_w_out,
           hgrn_w_in, hgrn_lower_bounds, hgrn_norm_g, hgrn_w_out, router_w, router_b,
           expert_w1, expert_b1, expert_w2, expert_b2, final_norm_g):
    assert x.shape == (BATCH, SEQ, D) and ctx.shape == (BATCH, CTX_LEN, D)
    cond8 = jnp.zeros((8, D), F32).at[:BATCH].set(c).at[BATCH].set(c_ctx)
    mods = _adaln(cond8, mod_w, mod_b)
    x_all = jnp.concatenate([x.reshape(NLAT, D), ctx.reshape(NCTX, D)], axis=0)
    dch, mpos, mrow = _fourier_constants()
    experts = (expert_w1, expert_b1, expert_w2, expert_b2)

    vr, vi = _fourier_in(x_all, mods, 0, norm1_g[0], fourier_w_in[0], dch, mpos)
    x_all = _fourier_out(vr, vi, x_all, mods, 0, fourier_w_out[0], mrow)
    x_all = _moe(x_all, mods, 0, norm2_g[0], router_w[0], router_b[0], *experts,
                 final_norm_g, ALL_TILES, False)

    q, ff, fb, v, gs = _hgrn_in(x_all, mods, 1, norm1_g[1], hgrn_w_in[0], hgrn_lower_bounds)
    o_fw, o_bw = _scan(q, ff, fb, v)
    x_lat = _hgrn_out(o_fw, o_bw, gs, x_all, mods, 1, hgrn_norm_g[0], hgrn_w_out[0], LAT_TILES)
    out = _moe(x_lat, mods, 1, norm2_g[1], router_w[1], router_b[1], *experts,
               final_norm_g, LAT_TILES, True)
    return out.reshape(BATCH, SEQ, D)
```

```python
import functools

import numpy as np
import jax
import jax.numpy as jnp
from jax import lax
from jax.experimental import pallas as pl
from jax.experimental.pallas import tpu as pltpu

F32 = jnp.float32
BF16 = jnp.bfloat16
HIGHEST = lax.Precision.HIGHEST

D = 1024
BATCH = 2
SEQ = 8192
CTX_LEN = 256
GRID_W = 64
GRID_H = SEQ // GRID_W
NLAT = BATCH * SEQ
NCTX = BATCH * CTX_LEN
NTOK = NLAT + NCTX
TM = 256
LAT_TILES = NLAT // TM
ALL_TILES = NTOK // TM
TILES_PER_BATCH = SEQ // TM
FGROUPS = 4
FGDIM = D // FGROUPS
HEADS = 8
HDIM = D // HEADS
CHUNK = 128
N_EXPERTS = 32
TOP_K = 4
D_FF = 1024
SWIGLU_ALPHA = 1.702
SWIGLU_LIMIT = 7.0
BM = 512
LANES = 128
RT = D // LANES
NORM_EPS = 1e-6
VMEM_LIMIT = 56 * 1024 * 1024

_NT = (((1,), (1,)), ((), ()))
_TN = (((0,), (0,)), ((), ()))


def _cparams(sem, vmem=None, **kw):
    return pltpu.CompilerParams(dimension_semantics=sem, vmem_limit_bytes=vmem, **kw)


def _sigmoid(x):
    return 1.0 / (1.0 + jnp.exp(-x))


def _rms(x, g):
    return x * lax.rsqrt(jnp.mean(x * x, axis=-1, keepdims=True) + NORM_EPS) * g


def _tile_cond(i):
    return jnp.where(i < LAT_TILES, i // TILES_PER_BATCH, 2)


def _adaln_kernel(cond_ref, w_ref, b_ref, o_ref):
    c = cond_ref[...]
    s = c * _sigmoid(c)
    o_ref[...] = jnp.dot(s, w_ref[...], precision=HIGHEST,
                         preferred_element_type=F32) + b_ref[...]


def _adaln(cond8, mod_w, mod_b):
    depth = mod_w.shape[0]
    nb = 1536
    out = pl.pallas_call(
        _adaln_kernel,
        grid=(depth, 6 * D // nb),
        in_specs=[pl.BlockSpec((8, D), lambda l, j: (0, 0)),
                  pl.BlockSpec((None, D, nb), lambda l, j: (l, 0, j)),
                  pl.BlockSpec((None, 1, nb), lambda l, j: (l, 0, j))],
        out_specs=pl.BlockSpec((None, 8, nb), lambda l, j: (l, 0, j)),
        out_shape=jax.ShapeDtypeStruct((depth, 8, 6 * D), F32),
        compiler_params=_cparams(("arbitrary", "arbitrary"), VMEM_LIMIT),
        name="adaln",
    )(cond8, mod_w, mod_b.reshape(depth, 1, 6 * D))
    return out.reshape(depth * 8, 6, D)


def _dft_cs(n):
    k = np.arange(n)
    ang = 2.0 * np.pi * np.outer(k, k) / n
    s = 1.0 / np.sqrt(n)
    return np.cos(ang) * s, np.sin(ang) * s


def _fourier_constants():
    cd, sd = _dft_cs(FGDIM)
    dch = np.concatenate([cd, sd], axis=1)
    cc, sc = _dft_cs(GRID_W)
    eye = np.eye(TM // GRID_W)
    kc, ks = np.kron(eye, cc), np.kron(eye, sc)
    m_lat = np.block([[kc, -ks], [ks, kc]])
    cp, sp = _dft_cs(CTX_LEN)
    m_ctx = np.block([[cp, -sp], [sp, cp]])
    mpos = np.stack([m_lat, m_ctx])
    cr, sr = _dft_cs(GRID_H)
    mrow = np.concatenate([cr, -sr], axis=1)
    return (jnp.asarray(dch, BF16), jnp.asarray(mpos, BF16), jnp.asarray(mrow, BF16))


def _fourier_in_kernel(x_ref, mod_ref, g_ref, win_ref, dch_ref, mpos_ref, vr_ref, vi_ref):
    x = x_ref[...]
    h = _rms(x, g_ref[...]) * (1.0 + mod_ref[1:2, :]) + mod_ref[0:1, :]
    u = jnp.dot(h.astype(BF16), win_ref[...], preferred_element_type=F32).astype(BF16)
    parts = [jnp.dot(u[:, g * FGDIM:(g + 1) * FGDIM], dch_ref[...],
                     preferred_element_type=F32) for g in range(FGROUPS)]
    uc = jnp.concatenate([p[:, :FGDIM] for p in parts], axis=1)
    us = jnp.concatenate([p[:, FGDIM:] for p in parts], axis=1)
    st = jnp.concatenate([uc, us], axis=0).astype(BF16)
    v = jnp.dot(mpos_ref[...], st, preferred_element_type=F32)
    vr_ref[...] = v[:TM]
    vi_ref[...] = v[TM:]


def _fourier_in(x_all, mods, layer, norm_g, w_in, dch, mpos):
    tile = pl.BlockSpec((TM, D), lambda i: (i, 0))
    return pl.pallas_call(
        _fourier_in_kernel,
        grid=(ALL_TILES,),
        in_specs=[tile,
                  pl.BlockSpec((None, 6, D), lambda i: (layer * 8 + _tile_cond(i), 0, 0)),
                  pl.BlockSpec((1, D), lambda i: (0, 0)),
                  pl.BlockSpec((D, D), lambda i: (0, 0)),
                  pl.BlockSpec((FGDIM, 2 * FGDIM), lambda i: (0, 0)),
                  pl.BlockSpec((None, 2 * TM, 2 * TM), lambda i: (jnp.where(i < LAT_TILES, 0, 1), 0, 0))],
        out_specs=[tile, tile],
        out_shape=[jax.ShapeDtypeStruct((NTOK, D), F32)] * 2,
        compiler_params=_cparams(("arbitrary",), VMEM_LIMIT),
        name="fourier_in",
    )(x_all, mods, norm_g.reshape(1, D), w_in.astype(BF16), dch, mpos)


CB = 8


def _fourier_out_lat_kernel(vr_ref, vi_ref, x_ref, mrow_ref, wout_ref, mod_ref, o_ref):
    g1 = mod_ref[2:3, :]
    yf = []
    for c in range(CB):
        st = jnp.concatenate([vr_ref[:, c, :], vi_ref[:, c, :]], axis=0).astype(BF16)
        yf.append(jnp.dot(mrow_ref[...], st, preferred_element_type=F32).astype(BF16))
    y = jnp.dot(jnp.concatenate(yf, axis=0), wout_ref[...], preferred_element_type=F32)
    for c in range(CB):
        o_ref[:, c, :] = x_ref[:, c, :] + g1 * y[c * GRID_H:(c + 1) * GRID_H, :]


def _fourier_out_ctx_kernel(yr_ref, x_ref, wout_ref, mod_ref, o_ref):
    y = jnp.dot(yr_ref[...].astype(BF16), wout_ref[...], preferred_element_type=F32)
    o_ref[...] = x_ref[...] + mod_ref[2:3, :] * y


def _fourier_out(vr, vi, x_all, mods, layer, w_out, mrow):
    wout = w_out.astype(BF16)
    rows = NTOK // GRID_W
    v3 = lambda a: a.reshape(rows, GRID_W, D)
    blk = pl.BlockSpec((GRID_H, CB, D), lambda b, c: (b, c, 0))
    x_new = pl.pallas_call(
        _fourier_out_lat_kernel,
        grid=(BATCH, GRID_W // CB),
        in_specs=[blk, blk, blk,
                  pl.BlockSpec((GRID_H, 2 * GRID_H), lambda b, c: (0, 0)),
                  pl.BlockSpec((D, D), lambda b, c: (0, 0)),
                  pl.BlockSpec((None, 6, D), lambda b, c: (layer * 8 + b, 0, 0))],
        out_specs=blk,
        out_shape=jax.ShapeDtypeStruct((rows, GRID_W, D), F32),
        input_output_aliases={2: 0},
        compiler_params=_cparams(("arbitrary", "arbitrary"), VMEM_LIMIT),
        name="fourier_out_lat",
    )(v3(vr), v3(vi), v3(x_all), mrow, wout, mods).reshape(NTOK, D)
    ctile = pl.BlockSpec((TM, D), lambda i: (LAT_TILES + i, 0))
    return pl.pallas_call(
        _fourier_out_ctx_kernel,
        grid=(NCTX // TM,),
        in_specs=[ctile, ctile,
                  pl.BlockSpec((D, D), lambda i: (0, 0)),
                  pl.BlockSpec((None, 6, D), lambda i: (layer * 8 + 2, 0, 0))],
        out_specs=ctile,
        out_shape=jax.ShapeDtypeStruct((NTOK, D), F32),
        input_output_aliases={1: 0},
        compiler_params=_cparams(("arbitrary",), VMEM_LIMIT),
        name="fourier_out_ctx",
    )(vr, x_new, wout, mods)


def _hgrn_in_kernel(x_ref, mod_ref, g_ref, win_ref, hlb_ref, q_ref, ff_ref, fb_ref, v_ref, gs_ref,
                    *, layer):
    x = x_ref[...]
    h = (_rms(x, g_ref[...]) * (1.0 + mod_ref[1:2, :]) + mod_ref[0:1, :]).astype(BF16)
    raw = [hlb_ref[l] for l in range(hlb_ref.shape[0])]
    mx = functools.reduce(jnp.maximum, raw)
    ex = [jnp.exp(r - mx) for r in raw]
    den = functools.reduce(lambda a, b: a + b, ex)
    soft = [e / den for e in ex]
    lb = functools.reduce(lambda a, b: a + b, soft[:layer + 1]) - soft[0]

    def proj(j):
        return jnp.dot(h, win_ref[:, j * D:(j + 1) * D], preferred_element_type=F32)

    q = proj(0)
    q_ref[...] = q * _sigmoid(q)
    ff_ref[...] = lb[0:1, :] + (1.0 - lb[0:1, :]) * _sigmoid(proj(1))
    fb_ref[...] = lb[1:2, :] + (1.0 - lb[1:2, :]) * _sigmoid(proj(2))
    v_ref[...] = proj(3)
    g = proj(4)
    gs_ref[...] = g * _sigmoid(g)


def _hgrn_in(x_all, mods, layer, norm_g, w_in, hlb):
    tile = pl.BlockSpec((TM, D), lambda i: (i, 0))
    depth = hlb.shape[0]
    return pl.pallas_call(
        functools.partial(_hgrn_in_kernel, layer=layer),
        grid=(ALL_TILES,),
        in_specs=[tile,
                  pl.BlockSpec((None, 6, D), lambda i: (layer * 8 + _tile_cond(i), 0, 0)),
                  pl.BlockSpec((1, D), lambda i: (0, 0)),
                  pl.BlockSpec((D, 5 * D), lambda i: (0, 0)),
                  pl.BlockSpec((depth, 2, D), lambda i: (0, 0, 0))],
        out_specs=[tile] * 5,
        out_shape=[jax.ShapeDtypeStruct((NTOK, D), F32)] * 5,
        compiler_params=_cparams(("arbitrary",), VMEM_LIMIT),
        name="hgrn_in",
    )(x_all, mods, norm_g.reshape(1, D), w_in.astype(BF16), hlb)


N_LEVELS = 7
SUB = 8
NGRP = CHUNK // SUB
FINE_LEVELS = 3


def _scan_pair_kernel(qf_ref, ff_ref, vf_ref, qb_ref, fb_ref, vb_ref, of_ref, ob_ref, sf_ref, sb_ref):
    @pl.when(pl.program_id(1) == 0)
    def _():
        sf_ref[...] = jnp.zeros_like(sf_ref)
        sb_ref[...] = jnp.zeros_like(sb_ref)

    _scan_chunk(qf_ref, ff_ref, vf_ref, of_ref, sf_ref, rev=False)
    _scan_chunk(qb_ref, fb_ref, vb_ref, ob_ref, sb_ref, rev=True)


def _scan_chunk(q_ref, fg_ref, v_ref, o_ref, s_ref, *, rev):
    t = lax.broadcasted_iota(jnp.int32, (CHUNK, HDIM), 0)
    ti = lax.broadcasted_iota(jnp.int32, (CHUNK, CHUNK), 0)
    si = lax.broadcasted_iota(jnp.int32, (CHUNK, CHUNK), 1)
    lane = lax.broadcasted_iota(jnp.int32, (SUB, CHUNK), 1)
    diag = ti == si
    fine = []
    for l in range(FINE_LEVELS):
        hbit = 1 << l
        odd = (t & hbit) != 0
        far = (t & hbit) == 0 if rev else odd
        t_far = (ti & hbit) == 0 if rev else (ti & hbit) != 0
        s_near = (si & hbit) != 0 if rev else (si & hbit) == 0
        fine.append((odd, far, ((ti >> (l + 1)) == (si >> (l + 1))) & t_far & s_near))
    coarse = []
    for l in range(FINE_LEVELS, N_LEVELS):
        bit = 1 << (l - FINE_LEVELS)
        is_far = [((b & bit) == 0) == rev for b in range(NGRP)]
        span = 2 << l
        keep = {b: None if span == CHUNK else
                (lane >= b * SUB // span * span) & (lane < b * SUB // span * span + span)
                for b in range(NGRP) if is_far[b]}
        coarse.append((bit, is_far, keep))

    def score(qs, ks):
        return lax.dot_general(qs.astype(BF16), ks.astype(BF16), _NT, preferred_element_type=F32)

    def groups(a):
        return [a[b * SUB:(b + 1) * SUB, :] for b in range(NGRP)]

    zero_g = jnp.zeros((SUB, HDIM), F32)
    for h in range(HEADS):
        sl = slice(h * HDIM, (h + 1) * HDIM)
        q = q_ref[:, sl]
        fg = fg_ref[:, sl]
        k = 1.0 - fg
        vb = v_ref[:, sl].astype(BF16)
        sc = jnp.where(diag, score(q, k), 0.0)
        run = fg
        rest = jnp.ones_like(fg)
        tot = fg
        for l, (odd, far, mask) in enumerate(fine):
            hbit = 1 << l
            sc = sc + jnp.where(mask, score(run * q, rest * k), 0.0)
            sib = jnp.where(odd, pltpu.roll(tot, hbit, 0), pltpu.roll(tot, CHUNK - hbit, 0))
            run = jnp.where(far, run * sib, run)
            rest = jnp.where(far, rest, rest * sib)
            tot = tot * sib
        q_g, k_g, run_g, rest_g, tot_g, sc_g = (groups(a) for a in (q, k, run, rest, tot, sc))
        for bit, is_far, keep in coarse:
            far_groups = [b for b in range(NGRP) if is_far[b]]
            part = score(jnp.concatenate([run_g[b] * q_g[b] for b in far_groups], axis=0),
                         jnp.concatenate([zero_g if is_far[b] else rest_g[b] * k_g[b]
                                          for b in range(NGRP)], axis=0))
            for i, b in enumerate(far_groups):
                piece = part[i * SUB:(i + 1) * SUB, :]
                if keep[b] is not None:
                    piece = jnp.where(keep[b], piece, 0.0)
                sc_g[b] = sc_g[b] + piece
            sib_g = [tot_g[b ^ bit] for b in range(NGRP)]
            run_g = [run_g[b] * sib_g[b] if is_far[b] else run_g[b] for b in range(NGRP)]
            rest_g = [rest_g[b] if is_far[b] else rest_g[b] * sib_g[b] for b in range(NGRP)]
            tot_g = [tot_g[b] * sib_g[b] for b in range(NGRP)]
        qin = (q * jnp.concatenate(run_g, axis=0)).astype(BF16)
        kst = (k * jnp.concatenate(rest_g, axis=0)).astype(BF16)
        st = s_ref[h]
        o_ref[:, sl] = (jnp.dot(jnp.concatenate(sc_g, axis=0).astype(BF16), vb,
                                preferred_element_type=F32)
                        + lax.dot_general(qin, st.astype(BF16), _NT, preferred_element_type=F32))
        s_ref[h] = st * tot_g[0][0:1, :] + lax.dot_general(vb, kst, _TN,
                                                            preferred_element_type=F32)


LAT_CHUNKS = SEQ // CHUNK
CTX_CHUNKS = CTX_LEN // CHUNK
SCAN_STEPS = CTX_CHUNKS + LAT_CHUNKS


def _scan(q, f_fw, f_bw, v):
    def idx_fw(b, s):
        return (jnp.where(s < CTX_CHUNKS, NLAT // CHUNK + CTX_CHUNKS * b + s,
                          LAT_CHUNKS * b + (s - CTX_CHUNKS)), 0)

    def idx_bw(b, s):
        return (jnp.where(s < CTX_CHUNKS, NLAT // CHUNK + CTX_CHUNKS * b + (CTX_CHUNKS - 1 - s),
                          LAT_CHUNKS * b + (SCAN_STEPS - 1 - s)), 0)
    fw = pl.BlockSpec((CHUNK, D), idx_fw)
    bw = pl.BlockSpec((CHUNK, D), idx_bw)
    state = pltpu.VMEM((HEADS, HDIM, HDIM), F32)
    return pl.pallas_call(
        _scan_pair_kernel,
        grid=(BATCH, SCAN_STEPS),
        in_specs=[fw, fw, fw, bw, bw, bw],
        out_specs=[fw, bw],
        out_shape=[jax.ShapeDtypeStruct((NTOK, D), F32)] * 2,
        scratch_shapes=[state, state],
        compiler_params=_cparams(("arbitrary", "arbitrary"), VMEM_LIMIT),
        name="scan",
    )(q, f_fw, v, q, f_bw, v)


def _hgrn_out_kernel(of_ref, ob_ref, gs_ref, x_ref, ng_ref, wout_ref, mod_ref, o_ref):
    o = of_ref[...] + ob_ref[...]
    parts = []
    for h in range(HEADS):
        oh = o[:, h * HDIM:(h + 1) * HDIM]
        parts.append(oh * lax.rsqrt(jnp.mean(oh * oh, axis=-1, keepdims=True) + NORM_EPS))
    on = jnp.concatenate(parts, axis=1) * ng_ref[...]
    y = jnp.dot((on * gs_ref[...]).astype(BF16), wout_ref[...], preferred_element_type=F32)
    o_ref[...] = x_ref[...] + mod_ref[2:3, :] * y


def _hgrn_out(o_fw, o_bw, gs, x_all, mods, layer, norm_g, w_out, n_tiles):
    tile = pl.BlockSpec((TM, D), lambda i: (i, 0))
    return pl.pallas_call(
        _hgrn_out_kernel,
        grid=(n_tiles,),
        in_specs=[tile, tile, tile, tile,
                  pl.BlockSpec((1, D), lambda i: (0, 0)),
                  pl.BlockSpec((D, D), lambda i: (0, 0)),
                  pl.BlockSpec((None, 6, D), lambda i: (layer * 8 + _tile_cond(i), 0, 0))],
        out_specs=tile,
        out_shape=jax.ShapeDtypeStruct((n_tiles * TM, D), F32),
        compiler_params=_cparams(("arbitrary",), VMEM_LIMIT),
        name="hgrn_out",
    )(o_fw, o_bw, gs, x_all, norm_g.reshape(1, D), w_out.astype(BF16), mods)


def _to_row_tiled(ref, val):
    for j in range(RT):
        ref[pl.ds(j, val.shape[0], stride=RT), :] = val[:, j * LANES:(j + 1) * LANES]


def _from_row_tiled(ref, n, base=0):
    return [ref[pl.ds(base + j, n, stride=RT), :] for j in range(RT)]


def _route_kernel(x_ref, mod_ref, g_ref, wrt_ref, brt_ref, h_ref, e_ref, rank_ref, gate_ref, cnt_ref):
    @pl.when(pl.program_id(0) == 0)
    def _():
        cnt_ref[...] = jnp.zeros_like(cnt_ref)

    h = _rms(x_ref[...], g_ref[...]) * (1.0 + mod_ref[4:5, :]) + mod_ref[3:4, :]
    _to_row_tiled(h_ref, h)
    w = wrt_ref[...]
    w_hi = w.astype(BF16)
    w_lo = (w - w_hi.astype(F32)).astype(BF16)
    h_hi = h.astype(BF16)
    h_lo = (h - h_hi.astype(F32)).astype(BF16)
    part = lax.dot_general(jnp.concatenate([w_hi, w_lo], axis=0), h_hi, _NT,
                           preferred_element_type=F32)
    logits = (part[:N_EXPERTS] + part[N_EXPERTS:]
              + lax.dot_general(w_hi, h_lo, _NT, preferred_element_type=F32)
              + brt_ref[:, 0:1])
    row = lax.broadcasted_iota(jnp.int32, (N_EXPERTS, TM), 0).astype(F32)
    vals = logits
    sel = jnp.zeros((N_EXPERTS, TM), F32)
    tops, idxs = [], []
    for _ in range(TOP_K):
        m = jnp.max(vals, axis=0, keepdims=True)
        idx = jnp.min(jnp.where(vals == m, row, float(N_EXPERTS)), axis=0, keepdims=True)
        hit = row == idx
        vals = jnp.where(hit, -jnp.inf, vals)
        sel = jnp.where(hit, 1.0, sel)
        tops.append(m)
        idxs.append(idx)
    ex = [jnp.exp(m - tops[0]) for m in tops]
    den = ex[0] + ex[1] + ex[2] + ex[3]
    r = lax.broadcasted_iota(jnp.int32, (TM, TM), 0)
    c = lax.broadcasted_iota(jnp.int32, (TM, TM), 1)
    before = jnp.where(r < c, 1.0, 0.0).astype(BF16)
    pref = jnp.dot(sel.astype(BF16), before, preferred_element_type=F32) + cnt_ref[:, 0:1]
    slot = lax.broadcasted_iota(jnp.int32, (8, TM), 0)
    e_out = jnp.zeros((8, TM), F32)
    rank_out = jnp.zeros((8, TM), F32)
    gate_out = jnp.zeros((8, TM), F32)
    for kk in range(TOP_K):
        rank = jnp.sum(jnp.where(row == idxs[kk], pref, 0.0), axis=0, keepdims=True)
        e_out = jnp.where(slot == kk, idxs[kk], e_out)
        rank_out = jnp.where(slot == kk, rank, rank_out)
        gate_out = jnp.where(slot == kk, ex[kk] / den, gate_out)
    e_ref[...] = e_out.astype(jnp.int32)
    rank_ref[...] = rank_out.astype(jnp.int32)
    gate_ref[...] = gate_out
    cnt_ref[...] += jnp.sum(sel, axis=1, keepdims=True)


def _route(x_all, mods, layer, norm_g, w_r, b_r, n_tiles):
    tile = pl.BlockSpec((TM, D), lambda i: (i, 0))
    small = pl.BlockSpec((None, 8, TM), lambda i: (i, 0, 0))
    n = n_tiles * TM
    return pl.pallas_call(
        _route_kernel,
        grid=(n_tiles,),
        in_specs=[tile,
                  pl.BlockSpec((None, 6, D), lambda i: (layer * 8 + _tile_cond(i), 0, 0)),
                  pl.BlockSpec((1, D), lambda i: (0, 0)),
                  pl.BlockSpec((N_EXPERTS, D), lambda i: (0, 0)),
                  pl.BlockSpec((N_EXPERTS, LANES), lambda i: (0, 0))],
        out_specs=[pl.BlockSpec((TM * RT, LANES), lambda i: (i, 0)), small, small, small,
                   pl.BlockSpec((N_EXPERTS, LANES), lambda i: (0, 0))],
        out_shape=[jax.ShapeDtypeStruct((n * RT, LANES), F32),
                   jax.ShapeDtypeStruct((n_tiles, 8, TM), jnp.int32),
                   jax.ShapeDtypeStruct((n_tiles, 8, TM), jnp.int32),
                   jax.ShapeDtypeStruct((n_tiles, 8, TM), F32),
                   jax.ShapeDtypeStruct((N_EXPERTS, LANES), F32)],
        compiler_params=_cparams(("arbitrary",), VMEM_LIMIT),
        name="route",
    )(x_all, mods, norm_g.reshape(1, D), w_r.T, jnp.broadcast_to(b_r[:, None], (N_EXPERTS, LANES)))


def _dest_kernel(ps_ref, e_ref, rank_ref, d_ref):
    e = e_ref[...]
    acc = rank_ref[...]
    for j in range(N_EXPERTS):
        acc = acc + jnp.where(e == j, ps_ref[j], 0)
    d_ref[...] = acc


def _dest(pad_start, e, rank):
    full = pl.BlockSpec(e.shape, lambda i, ps: (0, 0, 0))
    return pl.pallas_call(
        _dest_kernel,
        grid_spec=pltpu.PrefetchScalarGridSpec(num_scalar_prefetch=1, grid=(1,),
                                               in_specs=[full, full], out_specs=full),
        out_shape=jax.ShapeDtypeStruct(e.shape, jnp.int32),
        name="dest",
    )(pad_start, e, rank)


PAD_PIECES = tuple(1 << p for p in reversed(range(BM.bit_length() - 1)))


def _zero_fill(lo_ref, hi_ref, xb_hbm, zeros, sem, n_blocks, wait):
    def go(rows, row0):
        cp = pltpu.make_async_copy(zeros.at[pl.ds(0, rows * RT), :],
                                   xb_hbm.at[pl.ds(row0 * RT, rows * RT), :], sem.at[1])
        cp.wait() if wait else cp.start()

    def per_expert(e, carry):
        row = lo_ref[e]
        n = hi_ref[e] - row
        for piece in PAD_PIECES:
            @pl.when((n & piece) != 0)
            def _():
                go(piece, row)
            row = row + (n & piece)
        return carry
    lax.fori_loop(0, N_EXPERTS, per_expert, 0)

    def per_block(b, carry):
        go(BM, b * BM)
        return carry
    lax.fori_loop(hi_ref[N_EXPERTS - 1] // BM, n_blocks, per_block, 0)


def _dispatch_kernel(lo_ref, hi_ref, dest_ref, h_ref, xb_hbm, zeros, sem, *, n_blocks):
    i = pl.program_id(0)

    @pl.when(i == 0)
    def _():
        zeros[...] = jnp.zeros_like(zeros)
        _zero_fill(lo_ref, hi_ref, xb_hbm, zeros, sem, n_blocks, wait=False)

    def body(t, carry):
        src = h_ref.at[pl.ds(t * RT, RT), :]
        for kk in range(TOP_K):
            d = dest_ref[0, 0, kk * TM + t]
            pltpu.make_async_copy(src, xb_hbm.at[pl.ds(d * RT, RT), :],
                                  sem.at[0]).start(priority=kk % 2)
        return carry
    lax.fori_loop(0, TM, body, 0, unroll=4)
    for kk in range(TOP_K):
        pltpu.make_async_copy(h_ref, xb_hbm.at[pl.ds(0, TM * RT), :], sem.at[0]).wait()

    @pl.when(i == pl.num_programs(0) - 1)
    def _():
        _zero_fill(lo_ref, hi_ref, xb_hbm, zeros, sem, n_blocks, wait=True)


def _dispatch(h_rt, dest, pad_lo, pad_hi, n_blocks):
    n_tiles = dest.shape[0]
    dest3 = dest[:, :TOP_K, :].reshape(n_tiles, 1, TOP_K * TM)
    grid_spec = pltpu.PrefetchScalarGridSpec(
        num_scalar_prefetch=2,
        grid=(n_tiles,),
        in_specs=[pl.BlockSpec((1, 1, TOP_K * TM), lambda i, lo, hi: (i, 0, 0),
                               memory_space=pltpu.SMEM),
                  pl.BlockSpec((TM * RT, LANES), lambda i, lo, hi: (i, 0))],
        out_specs=pl.BlockSpec(memory_space=pl.ANY),
        scratch_shapes=[pltpu.VMEM((BM * RT, LANES), F32),
                        pltpu.SemaphoreType.DMA((2,))],
    )
    xb = pl.pallas_call(
        functools.partial(_dispatch_kernel, n_blocks=n_blocks),
        grid_spec=grid_spec,
        out_shape=jax.ShapeDtypeStruct((n_blocks * BM * RT, LANES), F32),
        compiler_params=_cparams(("arbitrary",), VMEM_LIMIT, disable_bounds_checks=True),
        name="dispatch",
    )(pad_lo, pad_hi, dest3, h_rt)
    return xb, dest3


def _expert_kernel(bexp_ref, nused_ref, first_ref, wslot_ref, next_ref, xb_ref, w1_hbm, b1_ref,
                   w2_hbm, b2_ref, y_ref, w1s, w2s, w1b, w2b, sem, *, layer):
    i = pl.program_id(0)
    n_used = nused_ref[0]

    def weight_copies(e, slot):
        return (pltpu.make_async_copy(w1_hbm.at[layer, e], w1s.at[slot], sem.at[0, slot]),
                pltpu.make_async_copy(w2_hbm.at[layer, e], w2s.at[slot], sem.at[1, slot]))

    @pl.when(i == 0)
    def _():
        for cp in weight_copies(bexp_ref[0], 0):
            cp.start()

    @pl.when(first_ref[i] == 1)
    def _():
        slot = wslot_ref[i]
        for cp in weight_copies(bexp_ref[i], slot):
            cp.wait()
        w1b[...] = w1s[slot].astype(BF16)
        w2b[...] = w2s[slot].astype(BF16)

        @pl.when(next_ref[i] >= 0)
        def _():
            for cp in weight_copies(next_ref[i], 1 - slot):
                cp.start()

    @pl.when(i < n_used)
    def _():
        x = jnp.concatenate(_from_row_tiled(xb_ref, BM), axis=1).astype(BF16)
        u = jnp.dot(x, w1b[...], preferred_element_type=F32) + b1_ref[...]
        glu = jnp.minimum(u[:, :D_FF], SWIGLU_LIMIT)
        lin = jnp.clip(u[:, D_FF:], -SWIGLU_LIMIT, SWIGLU_LIMIT)
        act = glu * _sigmoid(SWIGLU_ALPHA * glu) * (lin + 1.0)
        y = jnp.dot(act.astype(BF16), w2b[...], preferred_element_type=F32) + b2_ref[...]
        _to_row_tiled(y_ref, y)

    @pl.when(i >= n_used)
    def _():
        y_ref[...] = jnp.zeros_like(y_ref)


def _experts(xb, block_exp, n_used, counts, layer, w1, b1, w2, b2):
    n_blocks = xb.shape[0] // (BM * RT)
    used = jnp.arange(n_blocks, dtype=jnp.int32) < n_used[0]
    changed = jnp.concatenate([jnp.ones((1,), bool), block_exp[1:] != block_exp[:-1]])
    first = jnp.logical_and(used, changed).astype(jnp.int32)
    wslot = (jnp.cumsum(first) - 1) % 2
    ids = jnp.where(counts > 0, jnp.arange(N_EXPERTS, dtype=jnp.int32), N_EXPERTS)
    later = jnp.concatenate([lax.cummin(ids, reverse=True)[1:],
                             jnp.full((1,), N_EXPERTS, jnp.int32)])
    next_exp = jnp.where(later == N_EXPERTS, -1, later)[block_exp]
    smap = lambda f: (lambda i, be, nu, fi, ws, nx: f(i, be, nu))
    grid_spec = pltpu.PrefetchScalarGridSpec(
        num_scalar_prefetch=5,
        grid=(n_blocks,),
        in_specs=[
            pl.BlockSpec((BM * RT, LANES), smap(lambda i, be, nu: (jnp.minimum(i, nu[0] - 1), 0))),
            pl.BlockSpec(memory_space=pl.ANY),
            pl.BlockSpec((None, 1, 2 * D_FF), smap(lambda i, be, nu: (layer * N_EXPERTS + be[i], 0, 0))),
            pl.BlockSpec(memory_space=pl.ANY),
            pl.BlockSpec((None, 1, D), smap(lambda i, be, nu: (layer * N_EXPERTS + be[i], 0, 0))),
        ],
        out_specs=pl.BlockSpec((BM * RT, LANES), smap(lambda i, be, nu: (i, 0))),
        scratch_shapes=[pltpu.VMEM((2, D, 2 * D_FF), F32),
                        pltpu.VMEM((2, D_FF, D), F32),
                        pltpu.VMEM((D, 2 * D_FF), BF16),
                        pltpu.VMEM((D_FF, D), BF16),
                        pltpu.SemaphoreType.DMA((2, 2))],
    )
    return pl.pallas_call(
        functools.partial(_expert_kernel, layer=layer),
        grid_spec=grid_spec,
        out_shape=jax.ShapeDtypeStruct(xb.shape, F32),
        compiler_params=_cparams(("arbitrary",), VMEM_LIMIT),
        name="experts",
    )(block_exp, n_used, first, wslot.astype(jnp.int32), next_exp.astype(jnp.int32), xb, w1,
      b1.reshape(-1, 1, 2 * D_FF), w2, b2.reshape(-1, 1, D))


def _combine_kernel(dest_ref, dest_next_ref, y_hbm, gate_ref, x_ref, mod_ref, fg_ref, o_ref, buf, sem,
                    *, final_norm):
    i = pl.program_id(0)
    n = pl.num_programs(0)
    slot = i % 2
    rows = TOP_K * TM

    def gather(dest, to_slot):
        def body(p, carry):
            for half in range(2):
                r = 2 * p + half
                pltpu.make_async_copy(y_hbm.at[pl.ds(dest[0, 0, r] * RT, RT), :],
                                      buf.at[to_slot, pl.ds(r * RT, RT), :],
                                      sem.at[to_slot]).start(priority=half)
            return carry
        lax.fori_loop(0, rows // 2, body, 0, unroll=4)

    @pl.when(i == 0)
    def _():
        gather(dest_ref, 0)

    @pl.when(i + 1 < n)
    def _():
        gather(dest_next_ref, 1 - slot)

    pltpu.make_async_copy(y_hbm.at[pl.ds(0, rows * RT), :], buf.at[slot], sem.at[slot]).wait()
    eye = (lax.broadcasted_iota(jnp.int32, (8, 8), 0)
           == lax.broadcasted_iota(jnp.int32, (8, 8), 1)).astype(F32)
    gate = lax.dot_general(gate_ref[...], eye, _TN, precision=HIGHEST,
                           preferred_element_type=F32)
    cur = buf.at[slot]
    outs = []
    for j in range(RT):
        acc = None
        for kk in range(TOP_K):
            piece = gate[:, kk:kk + 1] * cur[pl.ds(kk * TM * RT + j, TM, stride=RT), :]
            acc = piece if acc is None else acc + piece
        outs.append(acc)
    out = x_ref[...] + mod_ref[5:6, :] * jnp.concatenate(outs, axis=1)
    if final_norm:
        out = _rms(out, fg_ref[...])
    o_ref[...] = out


def _combine(yb, dest3, gate, x_all, mods, layer, final_g, n_tiles, final_norm):
    last = n_tiles - 1
    tile = pl.BlockSpec((TM, D), lambda i: (i, 0))
    return pl.pallas_call(
        functools.partial(_combine_kernel, final_norm=final_norm),
        grid=(n_tiles,),
        in_specs=[pl.BlockSpec((1, 1, TOP_K * TM), lambda i: (i, 0, 0), memory_space=pltpu.SMEM),
                  pl.BlockSpec((1, 1, TOP_K * TM), lambda i: (jnp.minimum(i + 1, last), 0, 0),
                               memory_space=pltpu.SMEM),
                  pl.BlockSpec(memory_space=pl.ANY),
                  pl.BlockSpec((None, 8, TM), lambda i: (i, 0, 0)),
                  tile,
                  pl.BlockSpec((None, 6, D), lambda i: (layer * 8 + _tile_cond(i), 0, 0)),
                  pl.BlockSpec((1, D), lambda i: (0, 0))],
        out_specs=tile,
        out_shape=jax.ShapeDtypeStruct((n_tiles * TM, D), F32),
        scratch_shapes=[pltpu.VMEM((2, TOP_K * TM * RT, LANES), F32),
                        pltpu.SemaphoreType.DMA((2,))],
        compiler_params=_cparams(("arbitrary",), VMEM_LIMIT, disable_bounds_checks=True),
        name="combine",
    )(dest3, dest3, yb, gate, x_all, mods, final_g.reshape(1, D))


def _moe(x_all, mods, layer, norm_g, w_r, b_r, w1, b1, w2, b2, final_g, n_tiles, final_norm):
    n = n_tiles * TM
    h_rt, e, rank, gate, cnt = _route(x_all, mods, layer, norm_g, w_r, b_r, n_tiles)
    counts = cnt[:, 0].astype(jnp.int32)
    padded = (counts + BM - 1) // BM * BM
    pad_end = jnp.cumsum(padded)
    pad_start = pad_end - padded
    n_blocks = -(-(n * TOP_K) // BM) + N_EXPERTS
    starts = jnp.arange(n_blocks, dtype=jnp.int32) * BM
    block_exp = jnp.minimum(jnp.sum((pad_end[None, :] <= starts[:, None]).astype(jnp.int32), axis=1),
                            N_EXPERTS - 1)
    n_used = pad_end[-1:] // BM
    dest = _dest(pad_start, e, rank)
    xb, dest3 = _dispatch(h_rt, dest, pad_start + counts, pad_end, n_blocks)
    yb = _experts(xb, block_exp, n_used, counts, layer, w1, b1, w2, b2)
    return _combine(yb, dest3, gate, x_all, mods, layer, final_g, n_tiles, final_norm)


def kernel(x, c, ctx, c_ctx, mod_w, mod_b, norm1_g, norm2_g, fourier_w_in, fourier_w_out,
           hgrn_w_in, hgrn_lower_bounds, hgrn_norm_g, hgrn_w_out, router_w, router_b,
           expert_w1, expert_b1, expert_w2, expert_b2, final_norm_g):
    assert x.shape == (BATCH, SEQ, D) and ctx.shape == (BATCH, CTX_LEN, D)
    cond8 = jnp.zeros((8, D), F32).at[:BATCH].set(c).at[BATCH].set(c_ctx)
    mods = _adaln(cond8, mod_w, mod_b)
    x_all = jnp.concatenate([x.reshape(NLAT, D), ctx.reshape(NCTX, D)], axis=0)
    dch, mpos, mrow = _fourier_constants()
    experts = (expert_w1, expert_b1, expert_w2, expert_b2)

    vr, vi = _fourier_in(x_all, mods, 0, norm1_g[0], fourier_w_in[0], dch, mpos)
    x_all = _fourier_out(vr, vi, x_all, mods, 0, fourier_w_out[0], mrow)
    x_all = _moe(x_all, mods, 0, norm2_g[0], router_w[0], router_b[0], *experts,
                 final_norm_g, ALL_TILES, False)

    q, ff, fb, v, gs = _hgrn_in(x_all, mods, 1, norm1_g[1], hgrn_w_in[0], hgrn_lower_bounds)
    o_fw, o_bw = _scan(q, ff, fb, v)
    x_lat = _hgrn_out(o_fw, o_bw, gs, x_all, mods, 1, hgrn_norm_g[0], hgrn_w_out[0], LAT_TILES)
    out = _moe(x_lat, mods, 1, norm2_g[1], router_w[1], router_b[1], *experts,
               final_norm_g, LAT_TILES, True)
    return out.reshape(BATCH, SEQ, D)
```

```python
import functools

import numpy as np
import jax
import jax.numpy as jnp
from jax import lax
from jax.experimental import pallas as pl
from jax.experimental.pallas import tpu as pltpu

F32 = jnp.float32
BF16 = jnp.bfloat16
HIGHEST = lax.Precision.HIGHEST

D = 1024
BATCH = 2
SEQ = 8192
CTX_LEN = 256
GRID_W = 64
GRID_H = SEQ // GRID_W
NLAT = BATCH * SEQ
NCTX = BATCH * CTX_LEN
NTOK = NLAT + NCTX
TM = 256
LAT_TILES = NLAT // TM
ALL_TILES = NTOK // TM
TILES_PER_BATCH = SEQ // TM
FGROUPS = 4
FGDIM = D // FGROUPS
HEADS = 8
HDIM = D // HEADS
CHUNK = 128
N_EXPERTS = 32
TOP_K = 4
D_FF = 1024
SWIGLU_ALPHA = 1.702
SWIGLU_LIMIT = 7.0
BM = 512
LANES = 128
RT = D // LANES
NORM_EPS = 1e-6
VMEM_LIMIT = 56 * 1024 * 1024

_NT = (((1,), (1,)), ((), ()))
_TN = (((0,), (0,)), ((), ()))


def _cparams(sem, vmem=None, **kw):
    return pltpu.CompilerParams(dimension_semantics=sem, vmem_limit_bytes=vmem, **kw)


def _sigmoid(x):
    return 1.0 / (1.0 + jnp.exp(-x))


def _rms(x, g):
    return x * lax.rsqrt(jnp.mean(x * x, axis=-1, keepdims=True) + NORM_EPS) * g


def _tile_cond(i):
    return jnp.where(i < LAT_TILES, i // TILES_PER_BATCH, 2)


def _adaln_kernel(cond_ref, w_ref, b_ref, o_ref):
    c = cond_ref[...]
    s = c * _sigmoid(c)
    o_ref[...] = jnp.dot(s, w_ref[...], precision=HIGHEST,
                         preferred_element_type=F32) + b_ref[...]


def _adaln(cond8, mod_w, mod_b):
    depth = mod_w.shape[0]
    nb = 1536
    out = pl.pallas_call(
        _adaln_kernel,
        grid=(depth, 6 * D // nb),
        in_specs=[pl.BlockSpec((8, D), lambda l, j: (0, 0)),
                  pl.BlockSpec((None, D, nb), lambda l, j: (l, 0, j)),
                  pl.BlockSpec((None, 1, nb), lambda l, j: (l, 0, j))],
        out_specs=pl.BlockSpec((None, 8, nb), lambda l, j: (l, 0, j)),
        out_shape=jax.ShapeDtypeStruct((depth, 8, 6 * D), F32),
        compiler_params=_cparams(("arbitrary", "arbitrary"), VMEM_LIMIT),
        name="adaln",
    )(cond8, mod_w, mod_b.reshape(depth, 1, 6 * D))
    return out.reshape(depth * 8, 6, D)


def _dft_cs(n):
    k = np.arange(n)
    ang = 2.0 * np.pi * np.outer(k, k) / n
    s = 1.0 / np.sqrt(n)
    return np.cos(ang) * s, np.sin(ang) * s


def _fourier_constants():
    cd, sd = _dft_cs(FGDIM)
    dch = np.concatenate([cd, sd], axis=1)
    cc, sc = _dft_cs(GRID_W)
    eye = np.eye(TM // GRID_W)
    kc, ks = np.kron(eye, cc), np.kron(eye, sc)
    m_lat = np.block([[kc, -ks], [ks, kc]])
    cp, sp = _dft_cs(CTX_LEN)
    m_ctx = np.block([[cp, -sp], [sp, cp]])
    mpos = np.stack([m_lat, m_ctx])
    cr, sr = _dft_cs(GRID_H)
    mrow = np.concatenate([cr, -sr], axis=1)
    return (jnp.asarray(dch, BF16), jnp.asarray(mpos, BF16), jnp.asarray(mrow, BF16))


def _fourier_in_kernel(x_ref, mod_ref, g_ref, win_ref, dch_ref, mpos_ref, vr_ref, vi_ref):
    x = x_ref[...]
    h = _rms(x, g_ref[...]) * (1.0 + mod_ref[1:2, :]) + mod_ref[0:1, :]
    u = jnp.dot(h.astype(BF16), win_ref[...], preferred_element_type=F32).astype(BF16)
    parts = [jnp.dot(u[:, g * FGDIM:(g + 1) * FGDIM], dch_ref[...],
                     preferred_element_type=F32) for g in range(FGROUPS)]
    uc = jnp.concatenate([p[:, :FGDIM] for p in parts], axis=1)
    us = jnp.concatenate([p[:, FGDIM:] for p in parts], axis=1)
    st = jnp.concatenate([uc, us], axis=0).astype(BF16)
    v = jnp.dot(mpos_ref[...], st, preferred_element_type=F32)
    vr_ref[...] = v[:TM]
    vi_ref[...] = v[TM:]


def _fourier_in(x_all, mods, layer, norm_g, w_in, dch, mpos):
    tile = pl.BlockSpec((TM, D), lambda i: (i, 0))
    return pl.pallas_call(
        _fourier_in_kernel,
        grid=(ALL_TILES,),
        in_specs=[tile,
                  pl.BlockSpec((None, 6, D), lambda i: (layer * 8 + _tile_cond(i), 0, 0)),
                  pl.BlockSpec((1, D), lambda i: (0, 0)),
                  pl.BlockSpec((D, D), lambda i: (0, 0)),
                  pl.BlockSpec((FGDIM, 2 * FGDIM), lambda i: (0, 0)),
                  pl.BlockSpec((None, 2 * TM, 2 * TM), lambda i: (jnp.where(i < LAT_TILES, 0, 1), 0, 0))],
        out_specs=[tile, tile],
        out_shape=[jax.ShapeDtypeStruct((NTOK, D), F32)] * 2,
        compiler_params=_cparams(("arbitrary",), VMEM_LIMIT),
        name="fourier_in",
    )(x_all, mods, norm_g.reshape(1, D), w_in.astype(BF16), dch, mpos)


CB = 8


def _fourier_out_lat_kernel(vr_ref, vi_ref, x_ref, mrow_ref, wout_ref, mod_ref, o_ref):
    g1 = mod_ref[2:3, :]
    yf = []
    for c in range(CB):
        st = jnp.concatenate([vr_ref[:, c, :], vi_ref[:, c, :]], axis=0).astype(BF16)
        yf.append(jnp.dot(mrow_ref[...], st, preferred_element_type=F32).astype(BF16))
    y = jnp.dot(jnp.concatenate(yf, axis=0), wout_ref[...], preferred_element_type=F32)
    for c in range(CB):
        o_ref[:, c, :] = x_ref[:, c, :] + g1 * y[c * GRID_H:(c + 1) * GRID_H, :]


def _fourier_out_ctx_kernel(yr_ref, x_ref, wout_ref, mod_ref, o_ref):
    y = jnp.dot(yr_ref[...].astype(BF16), wout_ref[...], preferred_element_type=F32)
    o_ref[...] = x_ref[...] + mod_ref[2:3, :] * y


def _fourier_out(vr, vi, x_all, mods, layer, w_out, mrow):
    wout = w_out.astype(BF16)
    rows = NTOK // GRID_W
    v3 = lambda a: a.reshape(rows, GRID_W, D)
    blk = pl.BlockSpec((GRID_H, CB, D), lambda b, c: (b, c, 0))
    x_new = pl.pallas_call(
        _fourier_out_lat_kernel,
        grid=(BATCH, GRID_W // CB),
        in_specs=[blk, blk, blk,
                  pl.BlockSpec((GRID_H, 2 * GRID_H), lambda b, c: (0, 0)),
                  pl.BlockSpec((D, D), lambda b, c: (0, 0)),
                  pl.BlockSpec((None, 6, D), lambda b, c: (layer * 8 + b, 0, 0))],
        out_specs=blk,
        out_shape=jax.ShapeDtypeStruct((rows, GRID_W, D), F32),
        input_output_aliases={2: 0},
        compiler_params=_cparams(("arbitrary", "arbitrary"), VMEM_LIMIT),
        name="fourier_out_lat",
    )(v3(vr), v3(vi), v3(x_all), mrow, wout, mods).reshape(NTOK, D)
    ctile = pl.BlockSpec((TM, D), lambda i: (LAT_TILES + i, 0))
    return pl.pallas_call(
        _fourier_out_ctx_kernel,
        grid=(NCTX // TM,),
        in_specs=[ctile, ctile,
                  pl.BlockSpec((D, D), lambda i: (0, 0)),
                  pl.BlockSpec((None, 6, D), lambda i: (layer * 8 + 2, 0, 0))],
        out_specs=ctile,
        out_shape=jax.ShapeDtypeStruct((NTOK, D), F32),
        input_output_aliases={1: 0},
        compiler_params=_cparams(("arbitrary",), VMEM_LIMIT),
        name="fourier_out_ctx",
    )(vr, x_new, wout, mods)


def _hgrn_in_kernel(x_ref, mod_ref, g_ref, win_ref, hlb_ref, q_ref, ff_ref, fb_ref, v_ref, gs_ref,
                    *, layer):
    x = x_ref[...]
    h = (_rms(x, g_ref[...]) * (1.0 + mod_ref[1:2, :]) + mod_ref[0:1, :]).astype(BF16)
    raw = [hlb_ref[l] for l in range(hlb_ref.shape[0])]
    mx = functools.reduce(jnp.maximum, raw)
    ex = [jnp.exp(r - mx) for r in raw]
    den = functools.reduce(lambda a, b: a + b, ex)
    soft = [e / den for e in ex]
    lb = functools.reduce(lambda a, b: a + b, soft[:layer + 1]) - soft[0]

    def proj(j):
        return jnp.dot(h, win_ref[:, j * D:(j + 1) * D], preferred_element_type=F32)

    q = proj(0)
    q_ref[...] = q * _sigmoid(q)
    ff_ref[...] = lb[0:1, :] + (1.0 - lb[0:1, :]) * _sigmoid(proj(1))
    fb_ref[...] = lb[1:2, :] + (1.0 - lb[1:2, :]) * _sigmoid(proj(2))
    v_ref[...] = proj(3)
    g = proj(4)
    gs_ref[...] = g * _sigmoid(g)


def _hgrn_in(x_all, mods, layer, norm_g, w_in, hlb):
    tile = pl.BlockSpec((TM, D), lambda i: (i, 0))
    depth = hlb.shape[0]
    return pl.pallas_call(
        functools.partial(_hgrn_in_kernel, layer=layer),
        grid=(ALL_TILES,),
        in_specs=[tile,
                  pl.BlockSpec((None, 6, D), lambda i: (layer * 8 + _tile_cond(i), 0, 0)),
                  pl.BlockSpec((1, D), lambda i: (0, 0)),
                  pl.BlockSpec((D, 5 * D), lambda i: (0, 0)),
                  pl.BlockSpec((depth, 2, D), lambda i: (0, 0, 0))],
        out_specs=[tile] * 5,
        out_shape=[jax.ShapeDtypeStruct((NTOK, D), F32)] * 5,
        compiler_params=_cparams(("arbitrary",), VMEM_LIMIT),
        name="hgrn_in",
    )(x_all, mods, norm_g.reshape(1, D), w_in.astype(BF16), hlb)


N_LEVELS = 7
SUB = 8
NGRP = CHUNK // SUB
FINE_LEVELS = 3


def _scan_pair_kernel(qf_ref, ff_ref, vf_ref, qb_ref, fb_ref, vb_ref, of_ref, ob_ref, sf_ref, sb_ref):
    @pl.when(pl.program_id(1) == 0)
    def _():
        sf_ref[...] = jnp.zeros_like(sf_ref)
        sb_ref[...] = jnp.zeros_like(sb_ref)

    _scan_chunk(qf_ref, ff_ref, vf_ref, of_ref, sf_ref, rev=False)
    _scan_chunk(qb_ref, fb_ref, vb_ref, ob_ref, sb_ref, rev=True)


def _scan_chunk(q_ref, fg_ref, v_ref, o_ref, s_ref, *, rev):
    q = q_ref[...]
    fg = fg_ref[...]
    k = 1.0 - fg
    vb = v_ref[...].astype(BF16)
    t = lax.broadcasted_iota(jnp.int32, (CHUNK, D), 0)
    ti = lax.broadcasted_iota(jnp.int32, (CHUNK, CHUNK), 0)
    si = lax.broadcasted_iota(jnp.int32, (CHUNK, CHUNK), 1)
    scores = [None] * HEADS

    def add_level(qs, ks, mask):
        qb, kb = qs.astype(BF16), ks.astype(BF16)
        for h in range(HEADS):
            sl = slice(h * HDIM, (h + 1) * HDIM)
            sc = lax.dot_general(qb[:, sl], kb[:, sl], _NT, preferred_element_type=F32)
            sc = jnp.where(mask, sc, 0.0)
            scores[h] = sc if scores[h] is None else scores[h] + sc

    add_level(q, k, ti == si)
    run = fg
    rest = jnp.ones_like(fg)
    tot = fg
    for l in range(FINE_LEVELS):
        hbit = 1 << l
        odd = (t & hbit) != 0
        far = (t & hbit) == 0 if rev else odd
        t_far = (ti & hbit) == 0 if rev else (ti & hbit) != 0
        s_near = (si & hbit) != 0 if rev else (si & hbit) == 0
        add_level(run * q, rest * k,
                  ((ti >> (l + 1)) == (si >> (l + 1))) & t_far & s_near)
        sib = jnp.where(odd, pltpu.roll(tot, hbit, 0), pltpu.roll(tot, CHUNK - hbit, 0))
        run = jnp.where(far, run * sib, run)
        rest = jnp.where(far, rest, rest * sib)
        tot = tot * sib

    def groups(a):
        return [a[b * SUB:(b + 1) * SUB, :] for b in range(NGRP)]

    q_g, k_g, run_g, rest_g, tot_g = (groups(a) for a in (q, k, run, rest, tot))
    sc_g = [groups(scores[h]) for h in range(HEADS)]
    lane = lax.broadcasted_iota(jnp.int32, (SUB, CHUNK), 1)
    zero_g = jnp.zeros((SUB, D), F32)
    for l in range(FINE_LEVELS, N_LEVELS):
        bit = 1 << (l - FINE_LEVELS)
        is_far = [((b & bit) == 0) == rev for b in range(NGRP)]
        far_groups = [b for b in range(NGRP) if is_far[b]]
        qb = jnp.concatenate([run_g[b] * q_g[b] for b in far_groups], axis=0).astype(BF16)
        kb = jnp.concatenate([zero_g if is_far[b] else rest_g[b] * k_g[b] for b in range(NGRP)],
                             axis=0).astype(BF16)
        span = 2 << l
        keep = [None if span == CHUNK else
                (lane >= b * SUB // span * span) & (lane < b * SUB // span * span + span)
                for b in far_groups]
        for h in range(HEADS):
            sl = slice(h * HDIM, (h + 1) * HDIM)
            sc = lax.dot_general(qb[:, sl], kb[:, sl], _NT, preferred_element_type=F32)
            for i, b in enumerate(far_groups):
                piece = sc[i * SUB:(i + 1) * SUB, :]
                if keep[i] is not None:
                    piece = jnp.where(keep[i], piece, 0.0)
                sc_g[h][b] = sc_g[h][b] + piece
        sib_g = [tot_g[b ^ bit] for b in range(NGRP)]
        run_g = [run_g[b] * sib_g[b] if is_far[b] else run_g[b] for b in range(NGRP)]
        rest_g = [rest_g[b] if is_far[b] else rest_g[b] * sib_g[b] for b in range(NGRP)]
        tot_g = [tot_g[b] * sib_g[b] for b in range(NGRP)]
    qin = (q * jnp.concatenate(run_g, axis=0)).astype(BF16)
    kst = (k * jnp.concatenate(rest_g, axis=0)).astype(BF16)
    tot_row = tot_g[0][0:1, :]
    for h in range(HEADS):
        sl = slice(h * HDIM, (h + 1) * HDIM)
        st = s_ref[h]
        sc = jnp.concatenate(sc_g[h], axis=0).astype(BF16)
        o_ref[:, sl] = (jnp.dot(sc, vb[:, sl], preferred_element_type=F32)
                        + lax.dot_general(qin[:, sl], st.astype(BF16), _NT,
                                          preferred_element_type=F32))
        s_ref[h] = st * tot_row[:, sl] + lax.dot_general(vb[:, sl], kst[:, sl], _TN,
                                                         preferred_element_type=F32)


LAT_CHUNKS = SEQ // CHUNK
CTX_CHUNKS = CTX_LEN // CHUNK
SCAN_STEPS = CTX_CHUNKS + LAT_CHUNKS


def _scan(q, f_fw, f_bw, v):
    def idx_fw(b, s):
        return (jnp.where(s < CTX_CHUNKS, NLAT // CHUNK + CTX_CHUNKS * b + s,
                          LAT_CHUNKS * b + (s - CTX_CHUNKS)), 0)

    def idx_bw(b, s):
        return (jnp.where(s < CTX_CHUNKS, NLAT // CHUNK + CTX_CHUNKS * b + (CTX_CHUNKS - 1 - s),
                          LAT_CHUNKS * b + (SCAN_STEPS - 1 - s)), 0)
    fw = pl.BlockSpec((CHUNK, D), idx_fw)
    bw = pl.BlockSpec((CHUNK, D), idx_bw)
    state = pltpu.VMEM((HEADS, HDIM, HDIM), F32)
    return pl.pallas_call(
        _scan_pair_kernel,
        grid=(BATCH, SCAN_STEPS),
        in_specs=[fw, fw, fw, bw, bw, bw],
        out_specs=[fw, bw],
        out_shape=[jax.ShapeDtypeStruct((NTOK, D), F32)] * 2,
        scratch_shapes=[state, state],
        compiler_params=_cparams(("arbitrary", "arbitrary"), VMEM_LIMIT),
        name="scan",
    )(q, f_fw, v, q, f_bw, v)


def _hgrn_out_kernel(of_ref, ob_ref, gs_ref, x_ref, ng_ref, wout_ref, mod_ref, o_ref):
    o = of_ref[...] + ob_ref[...]
    parts = []
    for h in range(HEADS):
        oh = o[:, h * HDIM:(h + 1) * HDIM]
        parts.append(oh * lax.rsqrt(jnp.mean(oh * oh, axis=-1, keepdims=True) + NORM_EPS))
    on = jnp.concatenate(parts, axis=1) * ng_ref[...]
    y = jnp.dot((on * gs_ref[...]).astype(BF16), wout_ref[...], preferred_element_type=F32)
    o_ref[...] = x_ref[...] + mod_ref[2:3, :] * y


def _hgrn_out(o_fw, o_bw, gs, x_all, mods, layer, norm_g, w_out, n_tiles):
    tile = pl.BlockSpec((TM, D), lambda i: (i, 0))
    return pl.pallas_call(
        _hgrn_out_kernel,
        grid=(n_tiles,),
        in_specs=[tile, tile, tile, tile,
                  pl.BlockSpec((1, D), lambda i: (0, 0)),
                  pl.BlockSpec((D, D), lambda i: (0, 0)),
                  pl.BlockSpec((None, 6, D), lambda i: (layer * 8 + _tile_cond(i), 0, 0))],
        out_specs=tile,
        out_shape=jax.ShapeDtypeStruct((n_tiles * TM, D), F32),
        compiler_params=_cparams(("arbitrary",), VMEM_LIMIT),
        name="hgrn_out",
    )(o_fw, o_bw, gs, x_all, norm_g.reshape(1, D), w_out.astype(BF16), mods)


def _to_row_tiled(ref, val):
    for j in range(RT):
        ref[pl.ds(j, val.shape[0], stride=RT), :] = val[:, j * LANES:(j + 1) * LANES]


def _from_row_tiled(ref, n, base=0):
    return [ref[pl.ds(base + j, n, stride=RT), :] for j in range(RT)]


def _route_kernel(x_ref, mod_ref, g_ref, wrt_ref, brt_ref, h_ref, e_ref, rank_ref, gate_ref, cnt_ref):
    @pl.when(pl.program_id(0) == 0)
    def _():
        cnt_ref[...] = jnp.zeros_like(cnt_ref)

    h = _rms(x_ref[...], g_ref[...]) * (1.0 + mod_ref[4:5, :]) + mod_ref[3:4, :]
    _to_row_tiled(h_ref, h)
    w = wrt_ref[...]
    w_hi = w.astype(BF16)
    w_lo = (w - w_hi.astype(F32)).astype(BF16)
    h_hi = h.astype(BF16)
    h_lo = (h - h_hi.astype(F32)).astype(BF16)
    part = lax.dot_general(jnp.concatenate([w_hi, w_lo], axis=0), h_hi, _NT,
                           preferred_element_type=F32)
    logits = (part[:N_EXPERTS] + part[N_EXPERTS:]
              + lax.dot_general(w_hi, h_lo, _NT, preferred_element_type=F32)
              + brt_ref[:, 0:1])
    row = lax.broadcasted_iota(jnp.int32, (N_EXPERTS, TM), 0).astype(F32)
    vals = logits
    sel = jnp.zeros((N_EXPERTS, TM), F32)
    tops, idxs = [], []
    for _ in range(TOP_K):
        m = jnp.max(vals, axis=0, keepdims=True)
        idx = jnp.min(jnp.where(vals == m, row, float(N_EXPERTS)), axis=0, keepdims=True)
        hit = row == idx
        vals = jnp.where(hit, -jnp.inf, vals)
        sel = jnp.where(hit, 1.0, sel)
        tops.append(m)
        idxs.append(idx)
    ex = [jnp.exp(m - tops[0]) for m in tops]
    den = ex[0] + ex[1] + ex[2] + ex[3]
    r = lax.broadcasted_iota(jnp.int32, (TM, TM), 0)
    c = lax.broadcasted_iota(jnp.int32, (TM, TM), 1)
    before = jnp.where(r < c, 1.0, 0.0).astype(BF16)
    pref = jnp.dot(sel.astype(BF16), before, preferred_element_type=F32) + cnt_ref[:, 0:1]
    slot = lax.broadcasted_iota(jnp.int32, (8, TM), 0)
    e_out = jnp.zeros((8, TM), F32)
    rank_out = jnp.zeros((8, TM), F32)
    gate_out = jnp.zeros((8, TM), F32)
    for kk in range(TOP_K):
        rank = jnp.sum(jnp.where(row == idxs[kk], pref, 0.0), axis=0, keepdims=True)
        e_out = jnp.where(slot == kk, idxs[kk], e_out)
        rank_out = jnp.where(slot == kk, rank, rank_out)
        gate_out = jnp.where(slot == kk, ex[kk] / den, gate_out)
    e_ref[...] = e_out.astype(jnp.int32)
    rank_ref[...] = rank_out.astype(jnp.int32)
    gate_ref[...] = gate_out
    cnt_ref[...] += jnp.sum(sel, axis=1, keepdims=True)


def _route(x_all, mods, layer, norm_g, w_r, b_r, n_tiles):
    tile = pl.BlockSpec((TM, D), lambda i: (i, 0))
    small = pl.BlockSpec((None, 8, TM), lambda i: (i, 0, 0))
    n = n_tiles * TM
    return pl.pallas_call(
        _route_kernel,
        grid=(n_tiles,),
        in_specs=[tile,
                  pl.BlockSpec((None, 6, D), lambda i: (layer * 8 + _tile_cond(i), 0, 0)),
                  pl.BlockSpec((1, D), lambda i: (0, 0)),
                  pl.BlockSpec((N_EXPERTS, D), lambda i: (0, 0)),
                  pl.BlockSpec((N_EXPERTS, LANES), lambda i: (0, 0))],
        out_specs=[pl.BlockSpec((TM * RT, LANES), lambda i: (i, 0)), small, small, small,
                   pl.BlockSpec((N_EXPERTS, LANES), lambda i: (0, 0))],
        out_shape=[jax.ShapeDtypeStruct((n * RT, LANES), F32),
                   jax.ShapeDtypeStruct((n_tiles, 8, TM), jnp.int32),
                   jax.ShapeDtypeStruct((n_tiles, 8, TM), jnp.int32),
                   jax.ShapeDtypeStruct((n_tiles, 8, TM), F32),
                   jax.ShapeDtypeStruct((N_EXPERTS, LANES), F32)],
        compiler_params=_cparams(("arbitrary",), VMEM_LIMIT),
        name="route",
    )(x_all, mods, norm_g.reshape(1, D), w_r.T, jnp.broadcast_to(b_r[:, None], (N_EXPERTS, LANES)))


def _dest_kernel(ps_ref, e_ref, rank_ref, d_ref):
    e = e_ref[...]
    acc = rank_ref[...]
    for j in range(N_EXPERTS):
        acc = acc + jnp.where(e == j, ps_ref[j], 0)
    d_ref[...] = acc


def _dest(pad_start, e, rank):
    full = pl.BlockSpec(e.shape, lambda i, ps: (0, 0, 0))
    return pl.pallas_call(
        _dest_kernel,
        grid_spec=pltpu.PrefetchScalarGridSpec(num_scalar_prefetch=1, grid=(1,),
                                               in_specs=[full, full], out_specs=full),
        out_shape=jax.ShapeDtypeStruct(e.shape, jnp.int32),
        name="dest",
    )(pad_start, e, rank)


PAD_PIECES = tuple(1 << p for p in reversed(range(BM.bit_length() - 1)))


def _zero_fill(lo_ref, hi_ref, xb_hbm, zeros, sem, n_blocks, wait):
    def go(rows, row0):
        cp = pltpu.make_async_copy(zeros.at[pl.ds(0, rows * RT), :],
                                   xb_hbm.at[pl.ds(row0 * RT, rows * RT), :], sem.at[1])
        cp.wait() if wait else cp.start()

    def per_expert(e, carry):
        row = lo_ref[e]
        n = hi_ref[e] - row
        for piece in PAD_PIECES:
            @pl.when((n & piece) != 0)
            def _():
                go(piece, row)
            row = row + (n & piece)
        return carry
    lax.fori_loop(0, N_EXPERTS, per_expert, 0)

    def per_block(b, carry):
        go(BM, b * BM)
        return carry
    lax.fori_loop(hi_ref[N_EXPERTS - 1] // BM, n_blocks, per_block, 0)


def _dispatch_kernel(lo_ref, hi_ref, dest_ref, h_ref, xb_hbm, zeros, sem, *, n_blocks):
    i = pl.program_id(0)

    @pl.when(i == 0)
    def _():
        zeros[...] = jnp.zeros_like(zeros)
        _zero_fill(lo_ref, hi_ref, xb_hbm, zeros, sem, n_blocks, wait=False)

    def body(t, carry):
        src = h_ref.at[pl.ds(t * RT, RT), :]
        for kk in range(TOP_K):
            d = dest_ref[0, 0, kk * TM + t]
            pltpu.make_async_copy(src, xb_hbm.at[pl.ds(d * RT, RT), :],
                                  sem.at[0]).start(priority=kk % 2)
        return carry
    lax.fori_loop(0, TM, body, 0, unroll=4)
    for kk in range(TOP_K):
        pltpu.make_async_copy(h_ref, xb_hbm.at[pl.ds(0, TM * RT), :], sem.at[0]).wait()

    @pl.when(i == pl.num_programs(0) - 1)
    def _():
        _zero_fill(lo_ref, hi_ref, xb_hbm, zeros, sem, n_blocks, wait=True)


def _dispatch(h_rt, dest, pad_lo, pad_hi, n_blocks):
    n_tiles = dest.shape[0]
    dest3 = dest[:, :TOP_K, :].reshape(n_tiles, 1, TOP_K * TM)
    grid_spec = pltpu.PrefetchScalarGridSpec(
        num_scalar_prefetch=2,
        grid=(n_tiles,),
        in_specs=[pl.BlockSpec((1, 1, TOP_K * TM), lambda i, lo, hi: (i, 0, 0),
                               memory_space=pltpu.SMEM),
                  pl.BlockSpec((TM * RT, LANES), lambda i, lo, hi: (i, 0))],
        out_specs=pl.BlockSpec(memory_space=pl.ANY),
        scratch_shapes=[pltpu.VMEM((BM * RT, LANES), F32),
                        pltpu.SemaphoreType.DMA((2,))],
    )
    xb = pl.pallas_call(
        functools.partial(_dispatch_kernel, n_blocks=n_blocks),
        grid_spec=grid_spec,
        out_shape=jax.ShapeDtypeStruct((n_blocks * BM * RT, LANES), F32),
        compiler_params=_cparams(("arbitrary",), VMEM_LIMIT, disable_bounds_checks=True),
        name="dispatch",
    )(pad_lo, pad_hi, dest3, h_rt)
    return xb, dest3


ROW_STEPS = tuple(range(BM // 4, BM + 1, BM // 4))


def _expert_kernel(bexp_ref, nused_ref, first_ref, wslot_ref, next_ref, valid_ref, xb_ref, w1_hbm, b1_ref,
                   w2_hbm, b2_ref, y_ref, w1s, w2s, w1b, w2b, sem, *, layer):
    i = pl.program_id(0)
    del nused_ref

    def weight_copies(e, slot):
        return (pltpu.make_async_copy(w1_hbm.at[layer, e], w1s.at[slot], sem.at[0, slot]),
                pltpu.make_async_copy(w2_hbm.at[layer, e], w2s.at[slot], sem.at[1, slot]))

    @pl.when(i == 0)
    def _():
        for cp in weight_copies(bexp_ref[0], 0):
            cp.start()

    @pl.when(first_ref[i] == 1)
    def _():
        slot = wslot_ref[i]
        for cp in weight_copies(bexp_ref[i], slot):
            cp.wait()
        w1b[...] = w1s[slot].astype(BF16)
        w2b[...] = w2s[slot].astype(BF16)

        @pl.when(next_ref[i] >= 0)
        def _():
            for cp in weight_copies(next_ref[i], 1 - slot):
                cp.start()

    valid = valid_ref[i]
    for lo, rows in zip((0,) + ROW_STEPS[:-1], ROW_STEPS):
        @pl.when(jnp.logical_and(valid > lo, valid <= rows))
        def _():
            x = jnp.concatenate(_from_row_tiled(xb_ref, rows), axis=1).astype(BF16)
            u = jnp.dot(x, w1b[...], preferred_element_type=F32) + b1_ref[...]
            glu = jnp.minimum(u[:, :D_FF], SWIGLU_LIMIT)
            lin = jnp.clip(u[:, D_FF:], -SWIGLU_LIMIT, SWIGLU_LIMIT)
            act = glu * _sigmoid(SWIGLU_ALPHA * glu) * (lin + 1.0)
            y = jnp.dot(act.astype(BF16), w2b[...], preferred_element_type=F32) + b2_ref[...]
            _to_row_tiled(y_ref.at[pl.ds(0, rows * RT), :], y)
            if rows < BM:
                y_ref[pl.ds(rows * RT, (BM - rows) * RT), :] = jnp.zeros(((BM - rows) * RT, LANES), F32)

    @pl.when(valid == 0)
    def _():
        y_ref[...] = jnp.zeros_like(y_ref)


def _experts(xb, block_exp, n_used, counts, pad_lo, layer, w1, b1, w2, b2):
    n_blocks = xb.shape[0] // (BM * RT)
    used = jnp.arange(n_blocks, dtype=jnp.int32) < n_used[0]
    changed = jnp.concatenate([jnp.ones((1,), bool), block_exp[1:] != block_exp[:-1]])
    first = jnp.logical_and(used, changed).astype(jnp.int32)
    wslot = (jnp.cumsum(first) - 1) % 2
    ar = jnp.arange(N_EXPERTS, dtype=jnp.int32)
    later = jnp.min(jnp.where((ar[None, :] > ar[:, None]) & (counts > 0)[None, :], ar[None, :],
                              N_EXPERTS), axis=1)
    of_block = block_exp[:, None] == ar[None, :]
    next_exp = jnp.sum(jnp.where(of_block, jnp.where(later == N_EXPERTS, -1, later)[None, :], 0), axis=1)
    starts = jnp.arange(n_blocks, dtype=jnp.int32) * BM
    valid = jnp.clip(jnp.sum(jnp.where(of_block, pad_lo[None, :], 0), axis=1) - starts, 0, BM)
    smap = lambda f: (lambda i, be, nu, fi, ws, nx, va: f(i, be, nu))
    grid_spec = pltpu.PrefetchScalarGridSpec(
        num_scalar_prefetch=6,
        grid=(n_blocks,),
        in_specs=[
            pl.BlockSpec((BM * RT, LANES), smap(lambda i, be, nu: (jnp.minimum(i, nu[0] - 1), 0))),
            pl.BlockSpec(memory_space=pl.ANY),
            pl.BlockSpec((None, 1, 2 * D_FF), smap(lambda i, be, nu: (layer * N_EXPERTS + be[i], 0, 0))),
            pl.BlockSpec(memory_space=pl.ANY),
            pl.BlockSpec((None, 1, D), smap(lambda i, be, nu: (layer * N_EXPERTS + be[i], 0, 0))),
        ],
        out_specs=pl.BlockSpec((BM * RT, LANES), smap(lambda i, be, nu: (i, 0))),
        scratch_shapes=[pltpu.VMEM((2, D, 2 * D_FF), F32),
                        pltpu.VMEM((2, D_FF, D), F32),
                        pltpu.VMEM((D, 2 * D_FF), BF16),
                        pltpu.VMEM((D_FF, D), BF16),
                        pltpu.SemaphoreType.DMA((2, 2))],
    )
    return pl.pallas_call(
        functools.partial(_expert_kernel, layer=layer),
        grid_spec=grid_spec,
        out_shape=jax.ShapeDtypeStruct(xb.shape, F32),
        compiler_params=_cparams(("arbitrary",), VMEM_LIMIT),
        name="experts",
    )(block_exp, n_used, first, wslot.astype(jnp.int32), next_exp.astype(jnp.int32),
      valid.astype(jnp.int32), xb, w1,
      b1.reshape(-1, 1, 2 * D_FF), w2, b2.reshape(-1, 1, D))


def _combine_kernel(dest_ref, dest_next_ref, y_hbm, gate_ref, x_ref, mod_ref, fg_ref, o_ref, buf, sem,
                    *, final_norm):
    i = pl.program_id(0)
    n = pl.num_programs(0)
    slot = i % 2
    rows = TOP_K * TM

    def gather(dest, to_slot):
        def body(p, carry):
            for half in range(2):
                r = 2 * p + half
                pltpu.make_async_copy(y_hbm.at[pl.ds(dest[0, 0, r] * RT, RT), :],
                                      buf.at[to_slot, pl.ds(r * RT, RT), :],
                                      sem.at[to_slot]).start(priority=half)
            return carry
        lax.fori_loop(0, rows // 2, body, 0, unroll=4)

    @pl.when(i == 0)
    def _():
        gather(dest_ref, 0)

    @pl.when(i + 1 < n)
    def _():
        gather(dest_next_ref, 1 - slot)

    pltpu.make_async_copy(y_hbm.at[pl.ds(0, rows * RT), :], buf.at[slot], sem.at[slot]).wait()
    eye = (lax.broadcasted_iota(jnp.int32, (8, 8), 0)
           == lax.broadcasted_iota(jnp.int32, (8, 8), 1)).astype(F32)
    gate = lax.dot_general(gate_ref[...], eye, _TN, precision=HIGHEST,
                           preferred_element_type=F32)
    cur = buf.at[slot]
    outs = []
    for j in range(RT):
        acc = None
        for kk in range(TOP_K):
            piece = gate[:, kk:kk + 1] * cur[pl.ds(kk * TM * RT + j, TM, stride=RT), :]
            acc = piece if acc is None else acc + piece
        outs.append(acc)
    out = x_ref[...] + mod_ref[5:6, :] * jnp.concatenate(outs, axis=1)
    if final_norm:
        out = _rms(out, fg_ref[...])
    o_ref[...] = out


def _combine(yb, dest3, gate, x_all, mods, layer, final_g, n_tiles, final_norm):
    last = n_tiles - 1
    tile = pl.BlockSpec((TM, D), lambda i: (i, 0))
    return pl.pallas_call(
        functools.partial(_combine_kernel, final_norm=final_norm),
        grid=(n_tiles,),
        in_specs=[pl.BlockSpec((1, 1, TOP_K * TM), lambda i: (i, 0, 0), memory_space=pltpu.SMEM),
                  pl.BlockSpec((1, 1, TOP_K * TM), lambda i: (jnp.minimum(i + 1, last), 0, 0),
                               memory_space=pltpu.SMEM),
                  pl.BlockSpec(memory_space=pl.ANY),
                  pl.BlockSpec((None, 8, TM), lambda i: (i, 0, 0)),
                  tile,
                  pl.BlockSpec((None, 6, D), lambda i: (layer * 8 + _tile_cond(i), 0, 0)),
                  pl.BlockSpec((1, D), lambda i: (0, 0))],
        out_specs=tile,
        out_shape=jax.ShapeDtypeStruct((n_tiles * TM, D), F32),
        scratch_shapes=[pltpu.VMEM((2, TOP_K * TM * RT, LANES), F32),
                        pltpu.SemaphoreType.DMA((2,))],
        compiler_params=_cparams(("arbitrary",), VMEM_LIMIT, disable_bounds_checks=True),
        name="combine",
    )(dest3, dest3, yb, gate, x_all, mods, final_g.reshape(1, D))


def _moe(x_all, mods, layer, norm_g, w_r, b_r, w1, b1, w2, b2, final_g, n_tiles, final_norm):
    n = n_tiles * TM
    h_rt, e, rank, gate, cnt = _route(x_all, mods, layer, norm_g, w_r, b_r, n_tiles)
    counts = cnt[:, 0].astype(jnp.int32)
    padded = (counts + BM - 1) // BM * BM
    pad_end = jnp.cumsum(padded)
    pad_start = pad_end - padded
    n_blocks = -(-(n * TOP_K) // BM) + N_EXPERTS
    starts = jnp.arange(n_blocks, dtype=jnp.int32) * BM
    block_exp = jnp.minimum(jnp.sum((pad_end[None, :] <= starts[:, None]).astype(jnp.int32), axis=1),
                            N_EXPERTS - 1)
    n_used = pad_end[-1:] // BM
    dest = _dest(pad_start, e, rank)
    xb, dest3 = _dispatch(h_rt, dest, pad_start + counts, pad_end, n_blocks)
    yb = _experts(xb, block_exp, n_used, counts, pad_start + counts, layer, w1, b1, w2, b2)
    return _combine(yb, dest3, gate, x_all, mods, layer, final_g, n_tiles, final_norm)


def kernel(x, c, ctx, c_ctx, mod_w, mod_b, norm1_g, norm2_g, fourier_w_in, fourier_w_out,
           hgrn_w_in, hgrn_lower_bounds, hgrn_norm_g, hgrn_w_out, router_w, router_b,
           expert_w1, expert_b1, expert_w2, expert_b2, final_norm_g):
    assert x.shape == (BATCH, SEQ, D) and ctx.shape == (BATCH, CTX_LEN, D)
    cond8 = jnp.zeros((8, D), F32).at[:BATCH].set(c).at[BATCH].set(c_ctx)
    mods = _adaln(cond8, mod_w, mod_b)
    x_all = jnp.concatenate([x.reshape(NLAT, D), ctx.reshape(NCTX, D)], axis=0)
    dch, mpos, mrow = _fourier_constants()
    experts = (expert_w1, expert_b1, expert_w2, expert_b2)

    vr, vi = _fourier_in(x_all, mods, 0, norm1_g[0], fourier_w_in[0], dch, mpos)
    x_all = _fourier_out(vr, vi, x_all, mods, 0, fourier_w_out[0], mrow)
    x_all = _moe(x_all, mods, 0, norm2_g[0], router_w[0], router_b[0], *experts,
                 final_norm_g, ALL_TILES, False)

    q, ff, fb, v, gs = _hgrn_in(x_all, mods, 1, norm1_g[1], hgrn_w_in[0], hgrn_lower_bounds)
    o_fw, o_bw = _scan(q, ff, fb, v)
    x_lat = _hgrn_out(o_fw, o_bw, gs, x_all, mods, 1, hgrn_norm_g[0], hgrn_w_out[0], LAT_TILES)
    out = _moe(x_lat, mods, 1, norm2_g[1], router_w[1], router_b[1], *experts,
               final_norm_g, LAT_TILES, True)
    return out.reshape(BATCH, SEQ, D)
```

```python
import functools

import numpy as np
import jax
import jax.numpy as jnp
from jax import lax
from jax.experimental import pallas as pl
from jax.experimental.pallas import tpu as pltpu

F32 = jnp.float32
BF16 = jnp.bfloat16
HIGHEST = lax.Precision.HIGHEST

D = 1024
BATCH = 2
SEQ = 8192
CTX_LEN = 256
GRID_W = 64
GRID_H = SEQ // GRID_W
NLAT = BATCH * SEQ
NCTX = BATCH * CTX_LEN
NTOK = NLAT + NCTX
TM = 256
LAT_TILES = NLAT // TM
ALL_TILES = NTOK // TM
TILES_PER_BATCH = SEQ // TM
FGROUPS = 4
FGDIM = D // FGROUPS
HEADS = 8
HDIM = D // HEADS
CHUNK = 128
N_EXPERTS = 32
TOP_K = 4
D_FF = 1024
SWIGLU_ALPHA = 1.702
SWIGLU_LIMIT = 7.0
BM = 512
LANES = 128
RT = D // LANES
NORM_EPS = 1e-6
VMEM_LIMIT = 56 * 1024 * 1024

_NT = (((1,), (1,)), ((), ()))
_TN = (((0,), (0,)), ((), ()))


def _cparams(sem, vmem=None, **kw):
    return pltpu.CompilerParams(dimension_semantics=sem, vmem_limit_bytes=vmem, **kw)


def _sigmoid(x):
    return 1.0 / (1.0 + jnp.exp(-x))


def _rms(x, g):
    return x * lax.rsqrt(jnp.mean(x * x, axis=-1, keepdims=True) + NORM_EPS) * g


def _tile_cond(i):
    return jnp.where(i < LAT_TILES, i // TILES_PER_BATCH, 2)


def _adaln_kernel(cond_ref, w_ref, b_ref, o_ref):
    c = cond_ref[...]
    s = c * _sigmoid(c)
    o_ref[...] = jnp.dot(s, w_ref[...], precision=HIGHEST,
                         preferred_element_type=F32) + b_ref[...]


def _adaln(cond8, mod_w, mod_b):
    depth = mod_w.shape[0]
    nb = 1536
    out = pl.pallas_call(
        _adaln_kernel,
        grid=(depth, 6 * D // nb),
        in_specs=[pl.BlockSpec((8, D), lambda l, j: (0, 0)),
                  pl.BlockSpec((None, D, nb), lambda l, j: (l, 0, j)),
                  pl.BlockSpec((None, 1, nb), lambda l, j: (l, 0, j))],
        out_specs=pl.BlockSpec((None, 8, nb), lambda l, j: (l, 0, j)),
        out_shape=jax.ShapeDtypeStruct((depth, 8, 6 * D), F32),
        compiler_params=_cparams(("arbitrary", "arbitrary"), VMEM_LIMIT),
        name="adaln",
    )(cond8, mod_w, mod_b.reshape(depth, 1, 6 * D))
    return out.reshape(depth * 8, 6, D)


def _dft_cs(n):
    k = np.arange(n)
    ang = 2.0 * np.pi * np.outer(k, k) / n
    s = 1.0 / np.sqrt(n)
    return np.cos(ang) * s, np.sin(ang) * s


def _fourier_constants():
    cd, sd = _dft_cs(FGDIM)
    dch = np.concatenate([cd, sd], axis=1)
    cc, sc = _dft_cs(GRID_W)
    eye = np.eye(TM // GRID_W)
    kc, ks = np.kron(eye, cc), np.kron(eye, sc)
    m_lat = np.block([[kc, -ks], [ks, kc]])
    cp, sp = _dft_cs(CTX_LEN)
    m_ctx = np.block([[cp, -sp], [sp, cp]])
    mpos = np.stack([m_lat, m_ctx])
    cr, sr = _dft_cs(GRID_H)
    mrow = np.concatenate([cr, -sr], axis=1)
    return (jnp.asarray(dch, BF16), jnp.asarray(mpos, BF16), jnp.asarray(mrow, BF16))


def _fourier_in_kernel(x_ref, mod_ref, g_ref, win_ref, dch_ref, mpos_ref, vr_ref, vi_ref):
    x = x_ref[...]
    h = _rms(x, g_ref[...]) * (1.0 + mod_ref[1:2, :]) + mod_ref[0:1, :]
    u = jnp.dot(h.astype(BF16), win_ref[...], preferred_element_type=F32).astype(BF16)
    parts = [jnp.dot(u[:, g * FGDIM:(g + 1) * FGDIM], dch_ref[...],
                     preferred_element_type=F32) for g in range(FGROUPS)]
    uc = jnp.concatenate([p[:, :FGDIM] for p in parts], axis=1)
    us = jnp.concatenate([p[:, FGDIM:] for p in parts], axis=1)
    st = jnp.concatenate([uc, us], axis=0).astype(BF16)
    v = jnp.dot(mpos_ref[...], st, preferred_element_type=F32)
    vr_ref[...] = v[:TM]
    vi_ref[...] = v[TM:]


def _fourier_in(x_all, mods, layer, norm_g, w_in, dch, mpos):
    tile = pl.BlockSpec((TM, D), lambda i: (i, 0))
    return pl.pallas_call(
        _fourier_in_kernel,
        grid=(ALL_TILES,),
        in_specs=[tile,
                  pl.BlockSpec((None, 6, D), lambda i: (layer * 8 + _tile_cond(i), 0, 0)),
                  pl.BlockSpec((1, D), lambda i: (0, 0)),
                  pl.BlockSpec((D, D), lambda i: (0, 0)),
                  pl.BlockSpec((FGDIM, 2 * FGDIM), lambda i: (0, 0)),
                  pl.BlockSpec((None, 2 * TM, 2 * TM), lambda i: (jnp.where(i < LAT_TILES, 0, 1), 0, 0))],
        out_specs=[tile, tile],
        out_shape=[jax.ShapeDtypeStruct((NTOK, D), F32)] * 2,
        compiler_params=_cparams(("arbitrary",), VMEM_LIMIT),
        name="fourier_in",
    )(x_all, mods, norm_g.reshape(1, D), w_in.astype(BF16), dch, mpos)


CB = 8


def _fourier_out_lat_kernel(vr_ref, vi_ref, x_ref, mrow_ref, wout_ref, mod_ref, o_ref):
    g1 = mod_ref[2:3, :]
    yf = []
    for c in range(CB):
        st = jnp.concatenate([vr_ref[:, c, :], vi_ref[:, c, :]], axis=0).astype(BF16)
        yf.append(jnp.dot(mrow_ref[...], st, preferred_element_type=F32).astype(BF16))
    y = jnp.dot(jnp.concatenate(yf, axis=0), wout_ref[...], preferred_element_type=F32)
    for c in range(CB):
        o_ref[:, c, :] = x_ref[:, c, :] + g1 * y[c * GRID_H:(c + 1) * GRID_H, :]


def _fourier_out_ctx_kernel(yr_ref, x_ref, wout_ref, mod_ref, o_ref):
    y = jnp.dot(yr_ref[...].astype(BF16), wout_ref[...], preferred_element_type=F32)
    o_ref[...] = x_ref[...] + mod_ref[2:3, :] * y


def _fourier_out(vr, vi, x_all, mods, layer, w_out, mrow):
    wout = w_out.astype(BF16)
    rows = NTOK // GRID_W
    v3 = lambda a: a.reshape(rows, GRID_W, D)
    blk = pl.BlockSpec((GRID_H, CB, D), lambda b, c: (b, c, 0))
    x_new = pl.pallas_call(
        _fourier_out_lat_kernel,
        grid=(BATCH, GRID_W // CB),
        in_specs=[blk, blk, blk,
                  pl.BlockSpec((GRID_H, 2 * GRID_H), lambda b, c: (0, 0)),
                  pl.BlockSpec((D, D), lambda b, c: (0, 0)),
                  pl.BlockSpec((None, 6, D), lambda b, c: (layer * 8 + b, 0, 0))],
        out_specs=blk,
        out_shape=jax.ShapeDtypeStruct((rows, GRID_W, D), F32),
        input_output_aliases={2: 0},
        compiler_params=_cparams(("arbitrary", "arbitrary"), VMEM_LIMIT),
        name="fourier_out_lat",
    )(v3(vr), v3(vi), v3(x_all), mrow, wout, mods).reshape(NTOK, D)
    ctile = pl.BlockSpec((TM, D), lambda i: (LAT_TILES + i, 0))
    return pl.pallas_call(
        _fourier_out_ctx_kernel,
        grid=(NCTX // TM,),
        in_specs=[ctile, ctile,
                  pl.BlockSpec((D, D), lambda i: (0, 0)),
                  pl.BlockSpec((None, 6, D), lambda i: (layer * 8 + 2, 0, 0))],
        out_specs=ctile,
        out_shape=jax.ShapeDtypeStruct((NTOK, D), F32),
        input_output_aliases={1: 0},
        compiler_params=_cparams(("arbitrary",), VMEM_LIMIT),
        name="fourier_out_ctx",
    )(vr, x_new, wout, mods)


def _hgrn_in_kernel(x_ref, mod_ref, g_ref, win_ref, hlb_ref, q_ref, ff_ref, fb_ref, v_ref, gs_ref,
                    *, layer):
    x = x_ref[...]
    h = (_rms(x, g_ref[...]) * (1.0 + mod_ref[1:2, :]) + mod_ref[0:1, :]).astype(BF16)
    raw = [hlb_ref[l] for l in range(hlb_ref.shape[0])]
    mx = functools.reduce(jnp.maximum, raw)
    ex = [jnp.exp(r - mx) for r in raw]
    den = functools.reduce(lambda a, b: a + b, ex)
    soft = [e / den for e in ex]
    lb = functools.reduce(lambda a, b: a + b, soft[:layer + 1]) - soft[0]

    def proj(j):
        return jnp.dot(h, win_ref[:, j * D:(j + 1) * D], preferred_element_type=F32)

    q = proj(0)
    q_ref[...] = q * _sigmoid(q)
    ff_ref[...] = lb[0:1, :] + (1.0 - lb[0:1, :]) * _sigmoid(proj(1))
    fb_ref[...] = lb[1:2, :] + (1.0 - lb[1:2, :]) * _sigmoid(proj(2))
    v_ref[...] = proj(3)
    g = proj(4)
    gs_ref[...] = (g * _sigmoid(g)).astype(BF16)


def _hgrn_in(x_all, mods, layer, norm_g, w_in, hlb):
    tile = pl.BlockSpec((TM, D), lambda i: (i, 0))
    depth = hlb.shape[0]
    return pl.pallas_call(
        functools.partial(_hgrn_in_kernel, layer=layer),
        grid=(ALL_TILES,),
        in_specs=[tile,
                  pl.BlockSpec((None, 6, D), lambda i: (layer * 8 + _tile_cond(i), 0, 0)),
                  pl.BlockSpec((1, D), lambda i: (0, 0)),
                  pl.BlockSpec((D, 5 * D), lambda i: (0, 0)),
                  pl.BlockSpec((depth, 2, D), lambda i: (0, 0, 0))],
        out_specs=[tile] * 5,
        out_shape=[jax.ShapeDtypeStruct((NTOK, D), F32)] * 4 + [jax.ShapeDtypeStruct((NTOK, D), BF16)],
        compiler_params=_cparams(("arbitrary",), VMEM_LIMIT),
        name="hgrn_in",
    )(x_all, mods, norm_g.reshape(1, D), w_in.astype(BF16), hlb)


N_LEVELS = 7
SUB = 8
NGRP = CHUNK // SUB
FINE_LEVELS = 3


def _scan_pair_kernel(qf_ref, ff_ref, vf_ref, qb_ref, fb_ref, vb_ref, of_ref, ob_ref, sf_ref, sb_ref):
    @pl.when(pl.program_id(1) == 0)
    def _():
        sf_ref[...] = jnp.zeros_like(sf_ref)
        sb_ref[...] = jnp.zeros_like(sb_ref)

    _scan_chunk(qf_ref, ff_ref, vf_ref, of_ref, sf_ref, rev=False)
    _scan_chunk(qb_ref, fb_ref, vb_ref, ob_ref, sb_ref, rev=True)


def _scan_chunk(q_ref, fg_ref, v_ref, o_ref, s_ref, *, rev):
    q = q_ref[...]
    fg = fg_ref[...]
    k = 1.0 - fg
    vb = v_ref[...].astype(BF16)
    t = lax.broadcasted_iota(jnp.int32, (CHUNK, D), 0)
    ti = lax.broadcasted_iota(jnp.int32, (CHUNK, CHUNK), 0)
    si = lax.broadcasted_iota(jnp.int32, (CHUNK, CHUNK), 1)
    scores = [None] * HEADS

    def add_level(qs, ks, mask):
        qb, kb = qs.astype(BF16), ks.astype(BF16)
        for h in range(HEADS):
            sl = slice(h * HDIM, (h + 1) * HDIM)
            sc = lax.dot_general(qb[:, sl], kb[:, sl], _NT, preferred_element_type=F32)
            sc = jnp.where(mask, sc, 0.0)
            scores[h] = sc if scores[h] is None else scores[h] + sc

    add_level(q, k, ti == si)
    run = fg
    rest = jnp.ones_like(fg)
    tot = fg
    for l in range(FINE_LEVELS):
        hbit = 1 << l
        odd = (t & hbit) != 0
        far = (t & hbit) == 0 if rev else odd
        t_far = (ti & hbit) == 0 if rev else (ti & hbit) != 0
        s_near = (si & hbit) != 0 if rev else (si & hbit) == 0
        add_level(run * q, rest * k,
                  ((ti >> (l + 1)) == (si >> (l + 1))) & t_far & s_near)
        tot3 = tot.reshape(NGRP, SUB, D)
        sib = jnp.where(odd, pltpu.roll(tot3, hbit, 1).reshape(CHUNK, D),
                        pltpu.roll(tot3, SUB - hbit, 1).reshape(CHUNK, D))
        run = jnp.where(far, run * sib, run)
        rest = jnp.where(far, rest, rest * sib)
        tot = tot * sib

    def groups(a):
        return [a[b * SUB:(b + 1) * SUB, :] for b in range(NGRP)]

    q_g, k_g, run_g, rest_g, tot_g = (groups(a) for a in (q, k, run, rest, tot))
    sc_g = [groups(scores[h]) for h in range(HEADS)]
    lane = lax.broadcasted_iota(jnp.int32, (SUB, CHUNK), 1)
    zero_g = jnp.zeros((SUB, D), F32)
    for l in range(FINE_LEVELS, N_LEVELS):
        bit = 1 << (l - FINE_LEVELS)
        is_far = [((b & bit) == 0) == rev for b in range(NGRP)]
        far_groups = [b for b in range(NGRP) if is_far[b]]
        qb = jnp.concatenate([run_g[b] * q_g[b] for b in far_groups], axis=0).astype(BF16)
        kb = jnp.concatenate([zero_g if is_far[b] else rest_g[b] * k_g[b] for b in range(NGRP)],
                             axis=0).astype(BF16)
        span = 2 << l
        keep = [None if span == CHUNK else
                (lane >= b * SUB // span * span) & (lane < b * SUB // span * span + span)
                for b in far_groups]
        for h in range(HEADS):
            sl = slice(h * HDIM, (h + 1) * HDIM)
            sc = lax.dot_general(qb[:, sl], kb[:, sl], _NT, preferred_element_type=F32)
            for i, b in enumerate(far_groups):
                piece = sc[i * SUB:(i + 1) * SUB, :]
                if keep[i] is not None:
                    piece = jnp.where(keep[i], piece, 0.0)
                sc_g[h][b] = sc_g[h][b] + piece
        sib_g = [tot_g[b ^ bit] for b in range(NGRP)]
        run_g = [run_g[b] * sib_g[b] if is_far[b] else run_g[b] for b in range(NGRP)]
        rest_g = [rest_g[b] if is_far[b] else rest_g[b] * sib_g[b] for b in range(NGRP)]
        tot_g = [tot_g[b] * sib_g[b] for b in range(NGRP)]
    qin = (q * jnp.concatenate(run_g, axis=0)).astype(BF16)
    kst = (k * jnp.concatenate(rest_g, axis=0)).astype(BF16)
    tot_row = tot_g[0][0:1, :]
    for h in range(HEADS):
        sl = slice(h * HDIM, (h + 1) * HDIM)
        st = s_ref[h]
        sc = jnp.concatenate(sc_g[h], axis=0).astype(BF16)
        o_ref[:, sl] = (jnp.dot(sc, vb[:, sl], preferred_element_type=F32)
                        + lax.dot_general(qin[:, sl], st.astype(BF16), _NT,
                                          preferred_element_type=F32)).astype(BF16)
        s_ref[h] = st * tot_row[:, sl] + lax.dot_general(vb[:, sl], kst[:, sl], _TN,
                                                         preferred_element_type=F32)


LAT_CHUNKS = SEQ // CHUNK
CTX_CHUNKS = CTX_LEN // CHUNK
SCAN_STEPS = CTX_CHUNKS + LAT_CHUNKS


def _scan(q, f_fw, f_bw, v):
    def idx_fw(b, s):
        return (jnp.where(s < CTX_CHUNKS, NLAT // CHUNK + CTX_CHUNKS * b + s,
                          LAT_CHUNKS * b + (s - CTX_CHUNKS)), 0)

    def idx_bw(b, s):
        return (jnp.where(s < CTX_CHUNKS, NLAT // CHUNK + CTX_CHUNKS * b + (CTX_CHUNKS - 1 - s),
                          LAT_CHUNKS * b + (SCAN_STEPS - 1 - s)), 0)
    fw = pl.BlockSpec((CHUNK, D), idx_fw)
    bw = pl.BlockSpec((CHUNK, D), idx_bw)
    state = pltpu.VMEM((HEADS, HDIM, HDIM), F32)
    return pl.pallas_call(
        _scan_pair_kernel,
        grid=(BATCH, SCAN_STEPS),
        in_specs=[fw, fw, fw, bw, bw, bw],
        out_specs=[fw, bw],
        out_shape=[jax.ShapeDtypeStruct((NTOK, D), BF16)] * 2,
        scratch_shapes=[state, state],
        compiler_params=_cparams(("arbitrary", "arbitrary"), VMEM_LIMIT),
        name="scan",
    )(q, f_fw, v, q, f_bw, v)


def _hgrn_out_kernel(of_ref, ob_ref, gs_ref, x_ref, ng_ref, wout_ref, mod_ref, o_ref):
    o = of_ref[...].astype(F32) + ob_ref[...].astype(F32)
    parts = []
    for h in range(HEADS):
        oh = o[:, h * HDIM:(h + 1) * HDIM]
        parts.append(oh * lax.rsqrt(jnp.mean(oh * oh, axis=-1, keepdims=True) + NORM_EPS))
    on = jnp.concatenate(parts, axis=1) * ng_ref[...]
    y = jnp.dot((on * gs_ref[...].astype(F32)).astype(BF16), wout_ref[...],
                preferred_element_type=F32)
    o_ref[...] = x_ref[...] + mod_ref[2:3, :] * y


def _hgrn_out(o_fw, o_bw, gs, x_all, mods, layer, norm_g, w_out, n_tiles):
    tile = pl.BlockSpec((TM, D), lambda i: (i, 0))
    return pl.pallas_call(
        _hgrn_out_kernel,
        grid=(n_tiles,),
        in_specs=[tile, tile, tile, tile,
                  pl.BlockSpec((1, D), lambda i: (0, 0)),
                  pl.BlockSpec((D, D), lambda i: (0, 0)),
                  pl.BlockSpec((None, 6, D), lambda i: (layer * 8 + _tile_cond(i), 0, 0))],
        out_specs=tile,
        out_shape=jax.ShapeDtypeStruct((n_tiles * TM, D), F32),
        compiler_params=_cparams(("arbitrary",), VMEM_LIMIT),
        name="hgrn_out",
    )(o_fw, o_bw, gs, x_all, norm_g.reshape(1, D), w_out.astype(BF16), mods)


def _to_row_tiled(ref, val):
    for j in range(RT):
        ref[pl.ds(j, val.shape[0], stride=RT), :] = val[:, j * LANES:(j + 1) * LANES]


def _from_row_tiled(ref, n, base=0):
    return [ref[pl.ds(base + j, n, stride=RT), :] for j in range(RT)]


def _route_kernel(x_ref, mod_ref, g_ref, wrt_ref, brt_ref, h_ref, e_ref, rank_ref, gate_ref, cnt_ref):
    @pl.when(pl.program_id(0) == 0)
    def _():
        cnt_ref[...] = jnp.zeros_like(cnt_ref)

    h = _rms(x_ref[...], g_ref[...]) * (1.0 + mod_ref[4:5, :]) + mod_ref[3:4, :]
    _to_row_tiled(h_ref, h)
    w = wrt_ref[...]
    w_hi = w.astype(BF16)
    w_lo = (w - w_hi.astype(F32)).astype(BF16)
    h_hi = h.astype(BF16)
    h_lo = (h - h_hi.astype(F32)).astype(BF16)
    part = lax.dot_general(jnp.concatenate([w_hi, w_lo], axis=0), h_hi, _NT,
                           preferred_element_type=F32)
    logits = (part[:N_EXPERTS] + part[N_EXPERTS:]
              + lax.dot_general(w_hi, h_lo, _NT, preferred_element_type=F32)
              + brt_ref[:, 0:1])
    row = lax.broadcasted_iota(jnp.int32, (N_EXPERTS, TM), 0).astype(F32)
    vals = logits
    sel = jnp.zeros((N_EXPERTS, TM), F32)
    tops, idxs = [], []
    for _ in range(TOP_K):
        m = jnp.max(vals, axis=0, keepdims=True)
        idx = jnp.min(jnp.where(vals == m, row, float(N_EXPERTS)), axis=0, keepdims=True)
        hit = row == idx
        vals = jnp.where(hit, -jnp.inf, vals)
        sel = jnp.where(hit, 1.0, sel)
        tops.append(m)
        idxs.append(idx)
    ex = [jnp.exp(m - tops[0]) for m in tops]
    den = ex[0] + ex[1] + ex[2] + ex[3]
    r = lax.broadcasted_iota(jnp.int32, (TM, TM), 0)
    c = lax.broadcasted_iota(jnp.int32, (TM, TM), 1)
    before = jnp.where(r < c, 1.0, 0.0).astype(BF16)
    pref = jnp.dot(sel.astype(BF16), before, preferred_element_type=F32) + cnt_ref[:, 0:1]
    slot = lax.broadcasted_iota(jnp.int32, (8, TM), 0)
    e_out = jnp.zeros((8, TM), F32)
    rank_out = jnp.zeros((8, TM), F32)
    gate_out = jnp.zeros((8, TM), F32)
    for kk in range(TOP_K):
        rank = jnp.sum(jnp.where(row == idxs[kk], pref, 0.0), axis=0, keepdims=True)
        e_out = jnp.where(slot == kk, idxs[kk], e_out)
        rank_out = jnp.where(slot == kk, rank, rank_out)
        gate_out = jnp.where(slot == kk, ex[kk] / den, gate_out)
    e_ref[...] = e_out.astype(jnp.int32)
    rank_ref[...] = rank_out.astype(jnp.int32)
    gate_ref[...] = gate_out
    cnt_ref[...] += jnp.sum(sel, axis=1, keepdims=True)


def _route(x_all, mods, layer, norm_g, w_r, b_r, n_tiles):
    tile = pl.BlockSpec((TM, D), lambda i: (i, 0))
    small = pl.BlockSpec((None, 8, TM), lambda i: (i, 0, 0))
    n = n_tiles * TM
    return pl.pallas_call(
        _route_kernel,
        grid=(n_tiles,),
        in_specs=[tile,
                  pl.BlockSpec((None, 6, D), lambda i: (layer * 8 + _tile_cond(i), 0, 0)),
                  pl.BlockSpec((1, D), lambda i: (0, 0)),
                  pl.BlockSpec((N_EXPERTS, D), lambda i: (0, 0)),
                  pl.BlockSpec((N_EXPERTS, LANES), lambda i: (0, 0))],
        out_specs=[pl.BlockSpec((TM * RT, LANES), lambda i: (i, 0)), small, small, small,
                   pl.BlockSpec((N_EXPERTS, LANES), lambda i: (0, 0))],
        out_shape=[jax.ShapeDtypeStruct((n * RT, LANES), F32),
                   jax.ShapeDtypeStruct((n_tiles, 8, TM), jnp.int32),
                   jax.ShapeDtypeStruct((n_tiles, 8, TM), jnp.int32),
                   jax.ShapeDtypeStruct((n_tiles, 8, TM), F32),
                   jax.ShapeDtypeStruct((N_EXPERTS, LANES), F32)],
        compiler_params=_cparams(("arbitrary",), VMEM_LIMIT),
        name="route",
    )(x_all, mods, norm_g.reshape(1, D), w_r.T, jnp.broadcast_to(b_r[:, None], (N_EXPERTS, LANES)))


def _dest_kernel(ps_ref, e_ref, rank_ref, d_ref):
    e = e_ref[...]
    acc = rank_ref[...]
    for j in range(N_EXPERTS):
        acc = acc + jnp.where(e == j, ps_ref[j], 0)
    d_ref[...] = acc


def _dest(pad_start, e, rank):
    full = pl.BlockSpec(e.shape, lambda i, ps: (0, 0, 0))
    return pl.pallas_call(
        _dest_kernel,
        grid_spec=pltpu.PrefetchScalarGridSpec(num_scalar_prefetch=1, grid=(1,),
                                               in_specs=[full, full], out_specs=full),
        out_shape=jax.ShapeDtypeStruct(e.shape, jnp.int32),
        name="dest",
    )(pad_start, e, rank)


PAD_PIECES = tuple(1 << p for p in reversed(range(BM.bit_length() - 1)))


def _zero_fill(lo_ref, hi_ref, xb_hbm, zeros, sem, n_blocks, wait):
    def go(rows, row0):
        cp = pltpu.make_async_copy(zeros.at[pl.ds(0, rows * RT), :],
                                   xb_hbm.at[pl.ds(row0 * RT, rows * RT), :], sem.at[1])
        cp.wait() if wait else cp.start()

    def per_expert(e, carry):
        row = lo_ref[e]
        n = hi_ref[e] - row
        for piece in PAD_PIECES:
            @pl.when((n & piece) != 0)
            def _():
                go(piece, row)
            row = row + (n & piece)
        return carry
    lax.fori_loop(0, N_EXPERTS, per_expert, 0)

    def per_block(b, carry):
        go(BM, b * BM)
        return carry
    lax.fori_loop(hi_ref[N_EXPERTS - 1] // BM, n_blocks, per_block, 0)


def _dispatch_kernel(lo_ref, hi_ref, dest_ref, h_ref, xb_hbm, zeros, sem, *, n_blocks):
    i = pl.program_id(0)

    @pl.when(i == 0)
    def _():
        zeros[...] = jnp.zeros_like(zeros)
        _zero_fill(lo_ref, hi_ref, xb_hbm, zeros, sem, n_blocks, wait=False)

    def body(t, carry):
        src = h_ref.at[pl.ds(t * RT, RT), :]
        for kk in range(TOP_K):
            d = dest_ref[0, 0, kk * TM + t]
            pltpu.make_async_copy(src, xb_hbm.at[pl.ds(d * RT, RT), :],
                                  sem.at[0]).start(priority=kk % 2)
        return carry
    lax.fori_loop(0, TM, body, 0, unroll=4)
    for kk in range(TOP_K):
        pltpu.make_async_copy(h_ref, xb_hbm.at[pl.ds(0, TM * RT), :], sem.at[0]).wait()

    @pl.when(i == pl.num_programs(0) - 1)
    def _():
        _zero_fill(lo_ref, hi_ref, xb_hbm, zeros, sem, n_blocks, wait=True)


def _dispatch(h_rt, dest, pad_lo, pad_hi, n_blocks):
    n_tiles = dest.shape[0]
    dest3 = dest[:, :TOP_K, :].reshape(n_tiles, 1, TOP_K * TM)
    grid_spec = pltpu.PrefetchScalarGridSpec(
        num_scalar_prefetch=2,
        grid=(n_tiles,),
        in_specs=[pl.BlockSpec((1, 1, TOP_K * TM), lambda i, lo, hi: (i, 0, 0),
                               memory_space=pltpu.SMEM),
                  pl.BlockSpec((TM * RT, LANES), lambda i, lo, hi: (i, 0))],
        out_specs=pl.BlockSpec(memory_space=pl.ANY),
        scratch_shapes=[pltpu.VMEM((BM * RT, LANES), F32),
                        pltpu.SemaphoreType.DMA((2,))],
    )
    xb = pl.pallas_call(
        functools.partial(_dispatch_kernel, n_blocks=n_blocks),
        grid_spec=grid_spec,
        out_shape=jax.ShapeDtypeStruct((n_blocks * BM * RT, LANES), F32),
        compiler_params=_cparams(("arbitrary",), VMEM_LIMIT, disable_bounds_checks=True),
        name="dispatch",
    )(pad_lo, pad_hi, dest3, h_rt)
    return xb, dest3


ROW_STEPS = tuple(range(BM // 4, BM + 1, BM // 4))


def _expert_kernel(bexp_ref, nused_ref, first_ref, wslot_ref, next_ref, valid_ref, xb_ref, w1_hbm, b1_ref,
                   w2_hbm, b2_ref, y_ref, w1s, w2s, w1b, w2b, sem, *, layer):
    i = pl.program_id(0)
    del nused_ref

    def weight_copies(e, slot):
        return (pltpu.make_async_copy(w1_hbm.at[layer, e], w1s.at[slot], sem.at[0, slot]),
                pltpu.make_async_copy(w2_hbm.at[layer, e], w2s.at[slot], sem.at[1, slot]))

    @pl.when(i == 0)
    def _():
        for cp in weight_copies(bexp_ref[0], 0):
            cp.start()

    @pl.when(first_ref[i] == 1)
    def _():
        slot = wslot_ref[i]
        for cp in weight_copies(bexp_ref[i], slot):
            cp.wait()
        w1b[...] = w1s[slot].astype(BF16)
        w2b[...] = w2s[slot].astype(BF16)

        @pl.when(next_ref[i] >= 0)
        def _():
            for cp in weight_copies(next_ref[i], 1 - slot):
                cp.start()

    valid = valid_ref[i]
    for lo, rows in zip((0,) + ROW_STEPS[:-1], ROW_STEPS):
        @pl.when(jnp.logical_and(valid > lo, valid <= rows))
        def _():
            x = jnp.concatenate(_from_row_tiled(xb_ref, rows), axis=1).astype(BF16)
            u = jnp.dot(x, w1b[...], preferred_element_type=F32) + b1_ref[...]
            glu = jnp.minimum(u[:, :D_FF], SWIGLU_LIMIT)
            lin = jnp.clip(u[:, D_FF:], -SWIGLU_LIMIT, SWIGLU_LIMIT)
            act = glu * _sigmoid(SWIGLU_ALPHA * glu) * (lin + 1.0)
            y = jnp.dot(act.astype(BF16), w2b[...], preferred_element_type=F32) + b2_ref[...]
            _to_row_tiled(y_ref.at[pl.ds(0, rows * RT), :], y)
            if rows < BM:
                y_ref[pl.ds(rows * RT, (BM - rows) * RT), :] = jnp.zeros(((BM - rows) * RT, LANES), F32)

    @pl.when(valid == 0)
    def _():
        y_ref[...] = jnp.zeros_like(y_ref)


def _experts(xb, block_exp, n_used, counts, pad_lo, layer, w1, b1, w2, b2):
    n_blocks = xb.shape[0] // (BM * RT)
    used = jnp.arange(n_blocks, dtype=jnp.int32) < n_used[0]
    changed = jnp.concatenate([jnp.ones((1,), bool), block_exp[1:] != block_exp[:-1]])
    first = jnp.logical_and(used, changed).astype(jnp.int32)
    wslot = (jnp.cumsum(first) - 1) % 2
    ar = jnp.arange(N_EXPERTS, dtype=jnp.int32)
    later = jnp.min(jnp.where((ar[None, :] > ar[:, None]) & (counts > 0)[None, :], ar[None, :],
                              N_EXPERTS), axis=1)
    of_block = block_exp[:, None] == ar[None, :]
    next_exp = jnp.sum(jnp.where(of_block, jnp.where(later == N_EXPERTS, -1, later)[None, :], 0), axis=1)
    starts = jnp.arange(n_blocks, dtype=jnp.int32) * BM
    valid = jnp.clip(jnp.sum(jnp.where(of_block, pad_lo[None, :], 0), axis=1) - starts, 0, BM)
    smap = lambda f: (lambda i, be, nu, fi, ws, nx, va: f(i, be, nu))
    grid_spec = pltpu.PrefetchScalarGridSpec(
        num_scalar_prefetch=6,
        grid=(n_blocks,),
        in_specs=[
            pl.BlockSpec((BM * RT, LANES), smap(lambda i, be, nu: (jnp.minimum(i, nu[0] - 1), 0))),
            pl.BlockSpec(memory_space=pl.ANY),
            pl.BlockSpec((None, 1, 2 * D_FF), smap(lambda i, be, nu: (layer * N_EXPERTS + be[i], 0, 0))),
            pl.BlockSpec(memory_space=pl.ANY),
            pl.BlockSpec((None, 1, D), smap(lambda i, be, nu: (layer * N_EXPERTS + be[i], 0, 0))),
        ],
        out_specs=pl.BlockSpec((BM * RT, LANES), smap(lambda i, be, nu: (i, 0))),
        scratch_shapes=[pltpu.VMEM((2, D, 2 * D_FF), F32),
                        pltpu.VMEM((2, D_FF, D), F32),
                        pltpu.VMEM((D, 2 * D_FF), BF16),
                        pltpu.VMEM((D_FF, D), BF16),
                        pltpu.SemaphoreType.DMA((2, 2))],
    )
    return pl.pallas_call(
        functools.partial(_expert_kernel, layer=layer),
        grid_spec=grid_spec,
        out_shape=jax.ShapeDtypeStruct(xb.shape, F32),
        compiler_params=_cparams(("arbitrary",), VMEM_LIMIT),
        name="experts",
    )(block_exp, n_used, first, wslot.astype(jnp.int32), next_exp.astype(jnp.int32),
      valid.astype(jnp.int32), xb, w1,
      b1.reshape(-1, 1, 2 * D_FF), w2, b2.reshape(-1, 1, D))


def _combine_kernel(dest_ref, dest_next_ref, y_hbm, gate_ref, x_ref, mod_ref, fg_ref, o_ref, buf, sem,
                    *, final_norm):
    i = pl.program_id(0)
    n = pl.num_programs(0)
    slot = i % 2
    rows = TOP_K * TM

    def gather(dest, to_slot):
        def body(p, carry):
            for half in range(2):
                r = 2 * p + half
                pltpu.make_async_copy(y_hbm.at[pl.ds(dest[0, 0, r] * RT, RT), :],
                                      buf.at[to_slot, pl.ds(r * RT, RT), :],
                                      sem.at[to_slot]).start(priority=half)
            return carry
        lax.fori_loop(0, rows // 2, body, 0, unroll=4)

    @pl.when(i == 0)
    def _():
        gather(dest_ref, 0)

    @pl.when(i + 1 < n)
    def _():
        gather(dest_next_ref, 1 - slot)

    pltpu.make_async_copy(y_hbm.at[pl.ds(0, rows * RT), :], buf.at[slot], sem.at[slot]).wait()
    eye = (lax.broadcasted_iota(jnp.int32, (8, 8), 0)
           == lax.broadcasted_iota(jnp.int32, (8, 8), 1)).astype(F32)
    gate = lax.dot_general(gate_ref[...], eye, _TN, precision=HIGHEST,
                           preferred_element_type=F32)
    cur = buf.at[slot]
    outs = []
    for j in range(RT):
        acc = None
        for kk in range(TOP_K):
            piece = gate[:, kk:kk + 1] * cur[pl.ds(kk * TM * RT + j, TM, stride=RT), :]
            acc = piece if acc is None else acc + piece
        outs.append(acc)
    out = x_ref[...] + mod_ref[5:6, :] * jnp.concatenate(outs, axis=1)
    if final_norm:
        out = _rms(out, fg_ref[...])
    o_ref[...] = out


def _combine(yb, dest3, gate, x_all, mods, layer, final_g, n_tiles, final_norm):
    last = n_tiles - 1
    tile = pl.BlockSpec((TM, D), lambda i: (i, 0))
    return pl.pallas_call(
        functools.partial(_combine_kernel, final_norm=final_norm),
        grid=(n_tiles,),
        in_specs=[pl.BlockSpec((1, 1, TOP_K * TM), lambda i: (i, 0, 0), memory_space=pltpu.SMEM),
                  pl.BlockSpec((1, 1, TOP_K * TM), lambda i: (jnp.minimum(i + 1, last), 0, 0),
                               memory_space=pltpu.SMEM),
                  pl.BlockSpec(memory_space=pl.ANY),
                  pl.BlockSpec((None, 8, TM), lambda i: (i, 0, 0)),
                  tile,
                  pl.BlockSpec((None, 6, D), lambda i: (layer * 8 + _tile_cond(i), 0, 0)),
                  pl.BlockSpec((1, D), lambda i: (0, 0))],
        out_specs=tile,
        out_shape=jax.ShapeDtypeStruct((n_tiles * TM, D), F32),
        scratch_shapes=[pltpu.VMEM((2, TOP_K * TM * RT, LANES), F32),
                        pltpu.SemaphoreType.DMA((2,))],
        compiler_params=_cparams(("arbitrary",), VMEM_LIMIT, disable_bounds_checks=True),
        name="combine",
    )(dest3, dest3, yb, gate, x_all, mods, final_g.reshape(1, D))


def _moe(x_all, mods, layer, norm_g, w_r, b_r, w1, b1, w2, b2, final_g, n_tiles, final_norm):
    n = n_tiles * TM
    h_rt, e, rank, gate, cnt = _route(x_all, mods, layer, norm_g, w_r, b_r, n_tiles)
    counts = cnt[:, 0].astype(jnp.int32)
    padded = (counts + BM - 1) // BM * BM
    pad_end = jnp.cumsum(padded)
    pad_start = pad_end - padded
    n_blocks = -(-(n * TOP_K) // BM) + N_EXPERTS
    starts = jnp.arange(n_blocks, dtype=jnp.int32) * BM
    block_exp = jnp.minimum(jnp.sum((pad_end[None, :] <= starts[:, None]).astype(jnp.int32), axis=1),
                            N_EXPERTS - 1)
    n_used = pad_end[-1:] // BM
    dest = _dest(pad_start, e, rank)
    xb, dest3 = _dispatch(h_rt, dest, pad_start + counts, pad_end, n_blocks)
    yb = _experts(xb, block_exp, n_used, counts, pad_start + counts, layer, w1, b1, w2, b2)
    return _combine(yb, dest3, gate, x_all, mods, layer, final_g, n_tiles, final_norm)


def kernel(x, c, ctx, c_ctx, mod_w, mod_b, norm1_g, norm2_g, fourier_w_in, fourier_w_out,
           hgrn_w_in, hgrn_lower_bounds, hgrn_norm_g, hgrn_w_out, router_w, router_b,
           expert_w1, expert_b1, expert_w2, expert_b2, final_norm_g):
    assert x.shape == (BATCH, SEQ, D) and ctx.shape == (BATCH, CTX_LEN, D)
    cond8 = jnp.zeros((8, D), F32).at[:BATCH].set(c).at[BATCH].set(c_ctx)
    mods = _adaln(cond8, mod_w, mod_b)
    x_all = jnp.concatenate([x.reshape(NLAT, D), ctx.reshape(NCTX, D)], axis=0)
    dch, mpos, mrow = _fourier_constants()
    experts = (expert_w1, expert_b1, expert_w2, expert_b2)

    vr, vi = _fourier_in(x_all, mods, 0, norm1_g[0], fourier_w_in[0], dch, mpos)
    x_all = _fourier_out(vr, vi, x_all, mods, 0, fourier_w_out[0], mrow)
    x_all = _moe(x_all, mods, 0, norm2_g[0], router_w[0], router_b[0], *experts,
                 final_norm_g, ALL_TILES, False)

    q, ff, fb, v, gs = _hgrn_in(x_all, mods, 1, norm1_g[1], hgrn_w_in[0], hgrn_lower_bounds)
    o_fw, o_bw = _scan(q, ff, fb, v)
    x_lat = _hgrn_out(o_fw, o_bw, gs, x_all, mods, 1, hgrn_norm_g[0], hgrn_w_out[0], LAT_TILES)
    out = _moe(x_lat, mods, 1, norm2_g[1], router_w[1], router_b[1], *experts,
               final_norm_g, LAT_TILES, True)
    return out.reshape(BATCH, SEQ, D)
```

```python
import functools

import numpy as np
import jax
import jax.numpy as jnp
from jax import lax
from jax.experimental import pallas as pl
from jax.experimental.pallas import tpu as pltpu

F32 = jnp.float32
BF16 = jnp.bfloat16
HIGHEST = lax.Precision.HIGHEST

D = 1024
BATCH = 2
SEQ = 8192
CTX_LEN = 256
GRID_W = 64
GRID_H = SEQ // GRID_W
NLAT = BATCH * SEQ
NCTX = BATCH * CTX_LEN
NTOK = NLAT + NCTX
TM = 256
LAT_TILES = NLAT // TM
ALL_TILES = NTOK // TM
TILES_PER_BATCH = SEQ // TM
FGROUPS = 4
FGDIM = D // FGROUPS
HEADS = 8
HDIM = D // HEADS
CHUNK = 128
N_EXPERTS = 32
TOP_K = 4
D_FF = 1024
SWIGLU_ALPHA = 1.702
SWIGLU_LIMIT = 7.0
BM = 512
LANES = 128
RT = D // LANES
NORM_EPS = 1e-6
VMEM_LIMIT = 56 * 1024 * 1024

_NT = (((1,), (1,)), ((), ()))
_TN = (((0,), (0,)), ((), ()))


def _cparams(sem, vmem=None, **kw):
    return pltpu.CompilerParams(dimension_semantics=sem, vmem_limit_bytes=vmem, **kw)


def _sigmoid(x):
    return 1.0 / (1.0 + jnp.exp(-x))


def _rms(x, g):
    return x * lax.rsqrt(jnp.mean(x * x, axis=-1, keepdims=True) + NORM_EPS) * g


def _tile_cond(i):
    return jnp.where(i < LAT_TILES, i // TILES_PER_BATCH, 2)


def _adaln_kernel(cond_ref, w_ref, b_ref, o_ref):
    c = cond_ref[...]
    s = c * _sigmoid(c)
    o_ref[...] = jnp.dot(s, w_ref[...], precision=HIGHEST,
                         preferred_element_type=F32) + b_ref[...]


def _adaln(cond8, mod_w, mod_b):
    depth = mod_w.shape[0]
    nb = 1536
    out = pl.pallas_call(
        _adaln_kernel,
        grid=(depth, 6 * D // nb),
        in_specs=[pl.BlockSpec((8, D), lambda l, j: (0, 0)),
                  pl.BlockSpec((None, D, nb), lambda l, j: (l, 0, j)),
                  pl.BlockSpec((None, 1, nb), lambda l, j: (l, 0, j))],
        out_specs=pl.BlockSpec((None, 8, nb), lambda l, j: (l, 0, j)),
        out_shape=jax.ShapeDtypeStruct((depth, 8, 6 * D), F32),
        compiler_params=_cparams(("arbitrary", "arbitrary"), VMEM_LIMIT),
        name="adaln",
    )(cond8, mod_w, mod_b.reshape(depth, 1, 6 * D))
    return out.reshape(depth * 8, 6, D)


def _dft_cs(n):
    k = np.arange(n)
    ang = 2.0 * np.pi * np.outer(k, k) / n
    s = 1.0 / np.sqrt(n)
    return np.cos(ang) * s, np.sin(ang) * s


def _fourier_constants():
    cd, sd = _dft_cs(FGDIM)
    dch = np.concatenate([cd, sd], axis=1)
    cc, sc = _dft_cs(GRID_W)
    eye = np.eye(TM // GRID_W)
    kc, ks = np.kron(eye, cc), np.kron(eye, sc)
    m_lat = np.block([[kc, -ks], [ks, kc]])
    cp, sp = _dft_cs(CTX_LEN)
    m_ctx = np.block([[cp, -sp], [sp, cp]])
    mpos = np.stack([m_lat, m_ctx])
    cr, sr = _dft_cs(GRID_H)
    mrow = np.concatenate([cr, -sr], axis=1)
    return (jnp.asarray(dch, BF16), jnp.asarray(mpos, BF16), jnp.asarray(mrow, BF16))


def _fourier_in_kernel(x_ref, mod_ref, g_ref, win_ref, dch_ref, mpos_ref, vr_ref, vi_ref):
    x = x_ref[...]
    h = _rms(x, g_ref[...]) * (1.0 + mod_ref[1:2, :]) + mod_ref[0:1, :]
    u = jnp.dot(h.astype(BF16), win_ref[...], preferred_element_type=F32).astype(BF16)
    parts = [jnp.dot(u[:, g * FGDIM:(g + 1) * FGDIM], dch_ref[...],
                     preferred_element_type=F32) for g in range(FGROUPS)]
    uc = jnp.concatenate([p[:, :FGDIM] for p in parts], axis=1)
    us = jnp.concatenate([p[:, FGDIM:] for p in parts], axis=1)
    st = jnp.concatenate([uc, us], axis=0).astype(BF16)
    v = jnp.dot(mpos_ref[...], st, preferred_element_type=F32)
    vr_ref[...] = v[:TM]
    vi_ref[...] = v[TM:]


def _fourier_in(x_all, mods, layer, norm_g, w_in, dch, mpos):
    tile = pl.BlockSpec((TM, D), lambda i: (i, 0))
    return pl.pallas_call(
        _fourier_in_kernel,
        grid=(ALL_TILES,),
        in_specs=[tile,
                  pl.BlockSpec((None, 6, D), lambda i: (layer * 8 + _tile_cond(i), 0, 0)),
                  pl.BlockSpec((1, D), lambda i: (0, 0)),
                  pl.BlockSpec((D, D), lambda i: (0, 0)),
                  pl.BlockSpec((FGDIM, 2 * FGDIM), lambda i: (0, 0)),
                  pl.BlockSpec((None, 2 * TM, 2 * TM), lambda i: (jnp.where(i < LAT_TILES, 0, 1), 0, 0))],
        out_specs=[tile, tile],
        out_shape=[jax.ShapeDtypeStruct((NTOK, D), F32)] * 2,
        compiler_params=_cparams(("arbitrary",), VMEM_LIMIT),
        name="fourier_in",
    )(x_all, mods, norm_g.reshape(1, D), w_in.astype(BF16), dch, mpos)


CB = 8


def _fourier_out_lat_kernel(vr_ref, vi_ref, x_ref, mrow_ref, wout_ref, mod_ref, o_ref):
    g1 = mod_ref[2:3, :]
    yf = []
    for c in range(CB):
        st = jnp.concatenate([vr_ref[:, c, :], vi_ref[:, c, :]], axis=0).astype(BF16)
        yf.append(jnp.dot(mrow_ref[...], st, preferred_element_type=F32).astype(BF16))
    y = jnp.dot(jnp.concatenate(yf, axis=0), wout_ref[...], preferred_element_type=F32)
    for c in range(CB):
        o_ref[:, c, :] = x_ref[:, c, :] + g1 * y[c * GRID_H:(c + 1) * GRID_H, :]


def _fourier_out_ctx_kernel(yr_ref, x_ref, wout_ref, mod_ref, o_ref):
    y = jnp.dot(yr_ref[...].astype(BF16), wout_ref[...], preferred_element_type=F32)
    o_ref[...] = x_ref[...] + mod_ref[2:3, :] * y


def _fourier_out(vr, vi, x_all, mods, layer, w_out, mrow):
    wout = w_out.astype(BF16)
    rows = NTOK // GRID_W
    v3 = lambda a: a.reshape(rows, GRID_W, D)
    blk = pl.BlockSpec((GRID_H, CB, D), lambda b, c: (b, c, 0))
    x_new = pl.pallas_call(
        _fourier_out_lat_kernel,
        grid=(BATCH, GRID_W // CB),
        in_specs=[blk, blk, blk,
                  pl.BlockSpec((GRID_H, 2 * GRID_H), lambda b, c: (0, 0)),
                  pl.BlockSpec((D, D), lambda b, c: (0, 0)),
                  pl.BlockSpec((None, 6, D), lambda b, c: (layer * 8 + b, 0, 0))],
        out_specs=blk,
        out_shape=jax.ShapeDtypeStruct((rows, GRID_W, D), F32),
        input_output_aliases={2: 0},
        compiler_params=_cparams(("arbitrary", "arbitrary"), VMEM_LIMIT),
        name="fourier_out_lat",
    )(v3(vr), v3(vi), v3(x_all), mrow, wout, mods).reshape(NTOK, D)
    ctile = pl.BlockSpec((TM, D), lambda i: (LAT_TILES + i, 0))
    return pl.pallas_call(
        _fourier_out_ctx_kernel,
        grid=(NCTX // TM,),
        in_specs=[ctile, ctile,
                  pl.BlockSpec((D, D), lambda i: (0, 0)),
                  pl.BlockSpec((None, 6, D), lambda i: (layer * 8 + 2, 0, 0))],
        out_specs=ctile,
        out_shape=jax.ShapeDtypeStruct((NTOK, D), F32),
        input_output_aliases={1: 0},
        compiler_params=_cparams(("arbitrary",), VMEM_LIMIT),
        name="fourier_out_ctx",
    )(vr, x_new, wout, mods)


def _hgrn_in_kernel(x_ref, mod_ref, g_ref, win_ref, hlb_ref, q_ref, ff_ref, fb_ref, v_ref, gs_ref,
                    *, layer):
    x = x_ref[...]
    h = (_rms(x, g_ref[...]) * (1.0 + mod_ref[1:2, :]) + mod_ref[0:1, :]).astype(BF16)
    raw = [hlb_ref[l] for l in range(hlb_ref.shape[0])]
    mx = functools.reduce(jnp.maximum, raw)
    ex = [jnp.exp(r - mx) for r in raw]
    den = functools.reduce(lambda a, b: a + b, ex)
    soft = [e / den for e in ex]
    lb = functools.reduce(lambda a, b: a + b, soft[:layer + 1]) - soft[0]

    def proj(j):
        return jnp.dot(h, win_ref[:, j * D:(j + 1) * D], preferred_element_type=F32)

    q = proj(0)
    q_ref[...] = q * _sigmoid(q)
    ff_ref[...] = lb[0:1, :] + (1.0 - lb[0:1, :]) * _sigmoid(proj(1))
    fb_ref[...] = lb[1:2, :] + (1.0 - lb[1:2, :]) * _sigmoid(proj(2))
    v_ref[...] = proj(3)
    g = proj(4)
    gs_ref[...] = (g * _sigmoid(g)).astype(BF16)


def _hgrn_in(x_all, mods, layer, norm_g, w_in, hlb):
    tile = pl.BlockSpec((TM, D), lambda i: (i, 0))
    depth = hlb.shape[0]
    return pl.pallas_call(
        functools.partial(_hgrn_in_kernel, layer=layer),
        grid=(ALL_TILES,),
        in_specs=[tile,
                  pl.BlockSpec((None, 6, D), lambda i: (layer * 8 + _tile_cond(i), 0, 0)),
                  pl.BlockSpec((1, D), lambda i: (0, 0)),
                  pl.BlockSpec((D, 5 * D), lambda i: (0, 0)),
                  pl.BlockSpec((depth, 2, D), lambda i: (0, 0, 0))],
        out_specs=[tile] * 5,
        out_shape=[jax.ShapeDtypeStruct((NTOK, D), F32)] * 4 + [jax.ShapeDtypeStruct((NTOK, D), BF16)],
        compiler_params=_cparams(("arbitrary",), VMEM_LIMIT),
        name="hgrn_in",
    )(x_all, mods, norm_g.reshape(1, D), w_in.astype(BF16), hlb)


N_LEVELS = 7
SUB = 8
NGRP = CHUNK // SUB
FINE_LEVELS = 3


def _scan_pair_kernel(qf_ref, ff_ref, vf_ref, qb_ref, fb_ref, vb_ref, of_ref, ob_ref, sf_ref, sb_ref):
    @pl.when(pl.program_id(1) == 0)
    def _():
        sf_ref[...] = jnp.zeros_like(sf_ref)
        sb_ref[...] = jnp.zeros_like(sb_ref)

    _scan_chunk(qf_ref, ff_ref, vf_ref, of_ref, sf_ref, rev=False)
    _scan_chunk(qb_ref, fb_ref, vb_ref, ob_ref, sb_ref, rev=True)


def _scan_chunk(q_ref, fg_ref, v_ref, o_ref, s_ref, *, rev):
    q = q_ref[...]
    fg = fg_ref[...]
    k = 1.0 - fg
    vb = v_ref[...].astype(BF16)
    t = lax.broadcasted_iota(jnp.int32, (CHUNK, D), 0)
    ti = lax.broadcasted_iota(jnp.int32, (CHUNK, CHUNK), 0)
    si = lax.broadcasted_iota(jnp.int32, (CHUNK, CHUNK), 1)
    scores = [None] * HEADS

    def add_level(qs, ks, mask):
        qb, kb = qs.astype(BF16), ks.astype(BF16)
        for h in range(HEADS):
            sl = slice(h * HDIM, (h + 1) * HDIM)
            sc = lax.dot_general(qb[:, sl], kb[:, sl], _NT, preferred_element_type=F32)
            sc = jnp.where(mask, sc, 0.0)
            scores[h] = sc if scores[h] is None else scores[h] + sc

    add_level(q, k, ti == si)
    run = fg
    rest = None
    tot = fg
    for l in range(FINE_LEVELS):
        hbit = 1 << l
        odd = (t & hbit) != 0
        far = (t & hbit) == 0 if rev else odd
        t_far = (ti & hbit) == 0 if rev else (ti & hbit) != 0
        s_near = (si & hbit) != 0 if rev else (si & hbit) == 0
        add_level(run * q, k if rest is None else rest * k,
                  ((ti >> (l + 1)) == (si >> (l + 1))) & t_far & s_near)
        tot3 = tot.reshape(NGRP, SUB, D)
        sib = jnp.where(odd, pltpu.roll(tot3, hbit, 1).reshape(CHUNK, D),
                        pltpu.roll(tot3, SUB - hbit, 1).reshape(CHUNK, D))
        run = jnp.where(far, run * sib, run)
        rest = jnp.where(far, 1.0, sib) if rest is None else jnp.where(far, rest, rest * sib)
        tot = tot * sib

    def groups(a):
        return [a[b * SUB:(b + 1) * SUB, :] for b in range(NGRP)]

    q_g, k_g, run_g, rest_g, tot_g = (groups(a) for a in (q, k, run, rest, tot))
    sc_g = [groups(scores[h]) for h in range(HEADS)]
    lane = lax.broadcasted_iota(jnp.int32, (SUB, CHUNK), 1)
    zero_g = jnp.zeros((SUB, D), F32)
    for l in range(FINE_LEVELS, N_LEVELS):
        bit = 1 << (l - FINE_LEVELS)
        is_far = [((b & bit) == 0) == rev for b in range(NGRP)]
        far_groups = [b for b in range(NGRP) if is_far[b]]
        qb = jnp.concatenate([run_g[b] * q_g[b] for b in far_groups], axis=0).astype(BF16)
        kb = jnp.concatenate([zero_g if is_far[b] else rest_g[b] * k_g[b] for b in range(NGRP)],
                             axis=0).astype(BF16)
        span = 2 << l
        keep = [None if span == CHUNK else
                (lane >= b * SUB // span * span) & (lane < b * SUB // span * span + span)
                for b in far_groups]
        for h in range(HEADS):
            sl = slice(h * HDIM, (h + 1) * HDIM)
            sc = lax.dot_general(qb[:, sl], kb[:, sl], _NT, preferred_element_type=F32)
            for i, b in enumerate(far_groups):
                piece = sc[i * SUB:(i + 1) * SUB, :]
                if keep[i] is not None:
                    piece = jnp.where(keep[i], piece, 0.0)
                sc_g[h][b] = sc_g[h][b] + piece
        sib_g = [tot_g[b ^ bit] for b in range(NGRP)]
        run_g = [run_g[b] * sib_g[b] if is_far[b] else run_g[b] for b in range(NGRP)]
        rest_g = [rest_g[b] if is_far[b] else rest_g[b] * sib_g[b] for b in range(NGRP)]
        if l < N_LEVELS - 1:
            tot_g = [tot_g[b] * sib_g[b] for b in range(NGRP)]
        else:
            tot_row = tot_g[0][0:1, :] * sib_g[0][0:1, :]
    qin = (q * jnp.concatenate(run_g, axis=0)).astype(BF16)
    kst = (k * jnp.concatenate(rest_g, axis=0)).astype(BF16)
    for h in range(HEADS):
        sl = slice(h * HDIM, (h + 1) * HDIM)
        st = s_ref[h]
        sc = jnp.concatenate(sc_g[h], axis=0).astype(BF16)
        o_ref[:, sl] = (jnp.dot(sc, vb[:, sl], preferred_element_type=F32)
                        + lax.dot_general(qin[:, sl], st.astype(BF16), _NT,
                                          preferred_element_type=F32)).astype(BF16)
        s_ref[h] = st * tot_row[:, sl] + lax.dot_general(vb[:, sl], kst[:, sl], _TN,
                                                         preferred_element_type=F32)


LAT_CHUNKS = SEQ // CHUNK
CTX_CHUNKS = CTX_LEN // CHUNK
SCAN_STEPS = CTX_CHUNKS + LAT_CHUNKS


def _scan(q, f_fw, f_bw, v):
    def idx_fw(b, s):
        return (jnp.where(s < CTX_CHUNKS, NLAT // CHUNK + CTX_CHUNKS * b + s,
                          LAT_CHUNKS * b + (s - CTX_CHUNKS)), 0)

    def idx_bw(b, s):
        return (jnp.where(s < CTX_CHUNKS, NLAT // CHUNK + CTX_CHUNKS * b + (CTX_CHUNKS - 1 - s),
                          LAT_CHUNKS * b + (SCAN_STEPS - 1 - s)), 0)
    fw = pl.BlockSpec((CHUNK, D), idx_fw)
    bw = pl.BlockSpec((CHUNK, D), idx_bw)
    state = pltpu.VMEM((HEADS, HDIM, HDIM), F32)
    return pl.pallas_call(
        _scan_pair_kernel,
        grid=(BATCH, SCAN_STEPS),
        in_specs=[fw, fw, fw, bw, bw, bw],
        out_specs=[fw, bw],
        out_shape=[jax.ShapeDtypeStruct((NTOK, D), BF16)] * 2,
        scratch_shapes=[state, state],
        compiler_params=_cparams(("arbitrary", "arbitrary"), VMEM_LIMIT),
        name="scan",
    )(q, f_fw, v, q, f_bw, v)


def _hgrn_out_kernel(of_ref, ob_ref, gs_ref, x_ref, ng_ref, wout_ref, mod_ref, o_ref):
    o = of_ref[...].astype(F32) + ob_ref[...].astype(F32)
    parts = []
    for h in range(HEADS):
        oh = o[:, h * HDIM:(h + 1) * HDIM]
        parts.append(oh * lax.rsqrt(jnp.mean(oh * oh, axis=-1, keepdims=True) + NORM_EPS))
    on = jnp.concatenate(parts, axis=1) * ng_ref[...]
    y = jnp.dot((on * gs_ref[...].astype(F32)).astype(BF16), wout_ref[...],
                preferred_element_type=F32)
    o_ref[...] = x_ref[...] + mod_ref[2:3, :] * y


def _hgrn_out(o_fw, o_bw, gs, x_all, mods, layer, norm_g, w_out, n_tiles):
    tile = pl.BlockSpec((TM, D), lambda i: (i, 0))
    return pl.pallas_call(
        _hgrn_out_kernel,
        grid=(n_tiles,),
        in_specs=[tile, tile, tile, tile,
                  pl.BlockSpec((1, D), lambda i: (0, 0)),
                  pl.BlockSpec((D, D), lambda i: (0, 0)),
                  pl.BlockSpec((None, 6, D), lambda i: (layer * 8 + _tile_cond(i), 0, 0))],
        out_specs=tile,
        out_shape=jax.ShapeDtypeStruct((n_tiles * TM, D), F32),
        compiler_params=_cparams(("arbitrary",), VMEM_LIMIT),
        name="hgrn_out",
    )(o_fw, o_bw, gs, x_all, norm_g.reshape(1, D), w_out.astype(BF16), mods)


def _to_row_tiled(ref, val):
    for j in range(RT):
        ref[pl.ds(j, val.shape[0], stride=RT), :] = val[:, j * LANES:(j + 1) * LANES]


def _from_row_tiled(ref, n, base=0):
    return [ref[pl.ds(base + j, n, stride=RT), :] for j in range(RT)]


def _route_kernel(x_ref, mod_ref, g_ref, wrt_ref, brt_ref, h_ref, e_ref, rank_ref, gate_ref, cnt_ref):
    @pl.when(pl.program_id(0) == 0)
    def _():
        cnt_ref[...] = jnp.zeros_like(cnt_ref)

    h = _rms(x_ref[...], g_ref[...]) * (1.0 + mod_ref[4:5, :]) + mod_ref[3:4, :]
    _to_row_tiled(h_ref, h)
    w = wrt_ref[...]
    w_hi = w.astype(BF16)
    w_lo = (w - w_hi.astype(F32)).astype(BF16)
    h_hi = h.astype(BF16)
    h_lo = (h - h_hi.astype(F32)).astype(BF16)
    part = lax.dot_general(jnp.concatenate([w_hi, w_lo], axis=0), h_hi, _NT,
                           preferred_element_type=F32)
    logits = (part[:N_EXPERTS] + part[N_EXPERTS:]
              + lax.dot_general(w_hi, h_lo, _NT, preferred_element_type=F32)
              + brt_ref[:, 0:1])
    row = lax.broadcasted_iota(jnp.int32, (N_EXPERTS, TM), 0).astype(F32)
    vals = logits
    sel = jnp.zeros((N_EXPERTS, TM), F32)
    tops, idxs = [], []
    for _ in range(TOP_K):
        m = jnp.max(vals, axis=0, keepdims=True)
        idx = jnp.min(jnp.where(vals == m, row, float(N_EXPERTS)), axis=0, keepdims=True)
        hit = row == idx
        vals = jnp.where(hit, -jnp.inf, vals)
        sel = jnp.where(hit, 1.0, sel)
        tops.append(m)
        idxs.append(idx)
    ex = [jnp.exp(m - tops[0]) for m in tops]
    den = ex[0] + ex[1] + ex[2] + ex[3]
    r = lax.broadcasted_iota(jnp.int32, (TM, TM), 0)
    c = lax.broadcasted_iota(jnp.int32, (TM, TM), 1)
    before = jnp.where(r < c, 1.0, 0.0).astype(BF16)
    pref = jnp.dot(sel.astype(BF16), before, preferred_element_type=F32) + cnt_ref[:, 0:1]
    slot = lax.broadcasted_iota(jnp.int32, (8, TM), 0)
    e_out = jnp.zeros((8, TM), F32)
    rank_out = jnp.zeros((8, TM), F32)
    gate_out = jnp.zeros((8, TM), F32)
    for kk in range(TOP_K):
        rank = jnp.sum(jnp.where(row == idxs[kk], pref, 0.0), axis=0, keepdims=True)
        e_out = jnp.where(slot == kk, idxs[kk], e_out)
        rank_out = jnp.where(slot == kk, rank, rank_out)
        gate_out = jnp.where(slot == kk, ex[kk] / den, gate_out)
    e_ref[...] = e_out.astype(jnp.int32)
    rank_ref[...] = rank_out.astype(jnp.int32)
    gate_ref[...] = gate_out
    cnt_ref[...] += jnp.sum(sel, axis=1, keepdims=True)


def _route(x_all, mods, layer, norm_g, w_r, b_r, n_tiles):
    tile = pl.BlockSpec((TM, D), lambda i: (i, 0))
    small = pl.BlockSpec((None, 8, TM), lambda i: (i, 0, 0))
    n = n_tiles * TM
    return pl.pallas_call(
        _route_kernel,
        grid=(n_tiles,),
        in_specs=[tile,
                  pl.BlockSpec((None, 6, D), lambda i: (layer * 8 + _tile_cond(i), 0, 0)),
                  pl.BlockSpec((1, D), lambda i: (0, 0)),
                  pl.BlockSpec((N_EXPERTS, D), lambda i: (0, 0)),
                  pl.BlockSpec((N_EXPERTS, LANES), lambda i: (0, 0))],
        out_specs=[pl.BlockSpec((TM * RT, LANES), lambda i: (i, 0)), small, small, small,
                   pl.BlockSpec((N_EXPERTS, LANES), lambda i: (0, 0))],
        out_shape=[jax.ShapeDtypeStruct((n * RT, LANES), F32),
                   jax.ShapeDtypeStruct((n_tiles, 8, TM), jnp.int32),
                   jax.ShapeDtypeStruct((n_tiles, 8, TM), jnp.int32),
                   jax.ShapeDtypeStruct((n_tiles, 8, TM), F32),
                   jax.ShapeDtypeStruct((N_EXPERTS, LANES), F32)],
        compiler_params=_cparams(("arbitrary",), VMEM_LIMIT),
        name="route",
    )(x_all, mods, norm_g.reshape(1, D), w_r.T, jnp.broadcast_to(b_r[:, None], (N_EXPERTS, LANES)))


def _dest_kernel(ps_ref, e_ref, rank_ref, d_ref):
    e = e_ref[...]
    acc = rank_ref[...]
    for j in range(N_EXPERTS):
        acc = acc + jnp.where(e == j, ps_ref[j], 0)
    d_ref[...] = acc


def _dest(pad_start, e, rank):
    full = pl.BlockSpec(e.shape, lambda i, ps: (0, 0, 0))
    return pl.pallas_call(
        _dest_kernel,
        grid_spec=pltpu.PrefetchScalarGridSpec(num_scalar_prefetch=1, grid=(1,),
                                               in_specs=[full, full], out_specs=full),
        out_shape=jax.ShapeDtypeStruct(e.shape, jnp.int32),
        name="dest",
    )(pad_start, e, rank)


PAD_PIECES = tuple(1 << p for p in reversed(range(BM.bit_length() - 1)))


def _zero_fill(lo_ref, hi_ref, xb_hbm, zeros, sem, n_blocks, wait):
    def go(rows, row0):
        cp = pltpu.make_async_copy(zeros.at[pl.ds(0, rows * RT), :],
                                   xb_hbm.at[pl.ds(row0 * RT, rows * RT), :], sem.at[1])
        cp.wait() if wait else cp.start()

    def per_expert(e, carry):
        row = lo_ref[e]
        n = hi_ref[e] - row
        for piece in PAD_PIECES:
            @pl.when((n & piece) != 0)
            def _():
                go(piece, row)
            row = row + (n & piece)
        return carry
    lax.fori_loop(0, N_EXPERTS, per_expert, 0)

    def per_block(b, carry):
        go(BM, b * BM)
        return carry
    lax.fori_loop(hi_ref[N_EXPERTS - 1] // BM, n_blocks, per_block, 0)


def _dispatch_kernel(lo_ref, hi_ref, dest_ref, h_ref, xb_hbm, zeros, sem, *, n_blocks):
    i = pl.program_id(0)

    @pl.when(i == 0)
    def _():
        zeros[...] = jnp.zeros_like(zeros)
        _zero_fill(lo_ref, hi_ref, xb_hbm, zeros, sem, n_blocks, wait=False)

    def body(t, carry):
        src = h_ref.at[pl.ds(t * RT, RT), :]
        for kk in range(TOP_K):
            d = dest_ref[0, 0, kk * TM + t]
            pltpu.make_async_copy(src, xb_hbm.at[pl.ds(d * RT, RT), :],
                                  sem.at[0]).start(priority=kk % 2)
        return carry
    lax.fori_loop(0, TM, body, 0, unroll=4)
    for kk in range(TOP_K):
        pltpu.make_async_copy(h_ref, xb_hbm.at[pl.ds(0, TM * RT), :], sem.at[0]).wait()

    @pl.when(i == pl.num_programs(0) - 1)
    def _():
        _zero_fill(lo_ref, hi_ref, xb_hbm, zeros, sem, n_blocks, wait=True)


def _dispatch(h_rt, dest, pad_lo, pad_hi, n_blocks):
    n_tiles = dest.shape[0]
    dest3 = dest[:, :TOP_K, :].reshape(n_tiles, 1, TOP_K * TM)
    grid_spec = pltpu.PrefetchScalarGridSpec(
        num_scalar_prefetch=2,
        grid=(n_tiles,),
        in_specs=[pl.BlockSpec((1, 1, TOP_K * TM), lambda i, lo, hi: (i, 0, 0),
                               memory_space=pltpu.SMEM),
                  pl.BlockSpec((TM * RT, LANES), lambda i, lo, hi: (i, 0))],
        out_specs=pl.BlockSpec(memory_space=pl.ANY),
        scratch_shapes=[pltpu.VMEM((BM * RT, LANES), F32),
                        pltpu.SemaphoreType.DMA((2,))],
    )
    xb = pl.pallas_call(
        functools.partial(_dispatch_kernel, n_blocks=n_blocks),
        grid_spec=grid_spec,
        out_shape=jax.ShapeDtypeStruct((n_blocks * BM * RT, LANES), F32),
        compiler_params=_cparams(("arbitrary",), VMEM_LIMIT, disable_bounds_checks=True),
        name="dispatch",
    )(pad_lo, pad_hi, dest3, h_rt)
    return xb, dest3


ROW_STEPS = tuple(range(BM // 4, BM + 1, BM // 4))


def _expert_kernel(bexp_ref, nused_ref, first_ref, wslot_ref, next_ref, valid_ref, xb_ref, w1_hbm, b1_ref,
                   w2_hbm, b2_ref, y_ref, w1s, w2s, w1b, w2b, sem, *, layer):
    i = pl.program_id(0)
    del nused_ref

    def weight_copies(e, slot):
        return (pltpu.make_async_copy(w1_hbm.at[layer, e], w1s.at[slot], sem.at[0, slot]),
                pltpu.make_async_copy(w2_hbm.at[layer, e], w2s.at[slot], sem.at[1, slot]))

    @pl.when(i == 0)
    def _():
        for cp in weight_copies(bexp_ref[0], 0):
            cp.start()

    @pl.when(first_ref[i] == 1)
    def _():
        slot = wslot_ref[i]
        for cp in weight_copies(bexp_ref[i], slot):
            cp.wait()
        w1b[...] = w1s[slot].astype(BF16)
        w2b[...] = w2s[slot].astype(BF16)

        @pl.when(next_ref[i] >= 0)
        def _():
            for cp in weight_copies(next_ref[i], 1 - slot):
                cp.start()

    valid = valid_ref[i]
    for lo, rows in zip((0,) + ROW_STEPS[:-1], ROW_STEPS):
        @pl.when(jnp.logical_and(valid > lo, valid <= rows))
        def _():
            x = jnp.concatenate(_from_row_tiled(xb_ref, rows), axis=1).astype(BF16)
            u = jnp.dot(x, w1b[...], preferred_element_type=F32) + b1_ref[...]
            glu = jnp.minimum(u[:, :D_FF], SWIGLU_LIMIT)
            lin = jnp.clip(u[:, D_FF:], -SWIGLU_LIMIT, SWIGLU_LIMIT)
            act = glu * _sigmoid(SWIGLU_ALPHA * glu) * (lin + 1.0)
            y = jnp.dot(act.astype(BF16), w2b[...], preferred_element_type=F32) + b2_ref[...]
            _to_row_tiled(y_ref.at[pl.ds(0, rows * RT), :], y)
            if rows < BM:
                y_ref[pl.ds(rows * RT, (BM - rows) * RT), :] = jnp.zeros(((BM - rows) * RT, LANES), F32)

    @pl.when(valid == 0)
    def _():
        y_ref[...] = jnp.zeros_like(y_ref)


def _experts(xb, block_exp, n_used, counts, pad_lo, layer, w1, b1, w2, b2):
    n_blocks = xb.shape[0] // (BM * RT)
    used = jnp.arange(n_blocks, dtype=jnp.int32) < n_used[0]
    changed = jnp.concatenate([jnp.ones((1,), bool), block_exp[1:] != block_exp[:-1]])
    first = jnp.logical_and(used, changed).astype(jnp.int32)
    wslot = (jnp.cumsum(first) - 1) % 2
    ar = jnp.arange(N_EXPERTS, dtype=jnp.int32)
    later = jnp.min(jnp.where((ar[None, :] > ar[:, None]) & (counts > 0)[None, :], ar[None, :],
                              N_EXPERTS), axis=1)
    of_block = block_exp[:, None] == ar[None, :]
    next_exp = jnp.sum(jnp.where(of_block, jnp.where(later == N_EXPERTS, -1, later)[None, :], 0), axis=1)
    starts = jnp.arange(n_blocks, dtype=jnp.int32) * BM
    valid = jnp.clip(jnp.sum(jnp.where(of_block, pad_lo[None, :], 0), axis=1) - starts, 0, BM)
    smap = lambda f: (lambda i, be, nu, fi, ws, nx, va: f(i, be, nu))
    grid_spec = pltpu.PrefetchScalarGridSpec(
        num_scalar_prefetch=6,
        grid=(n_blocks,),
        in_specs=[
            pl.BlockSpec((BM * RT, LANES), smap(lambda i, be, nu: (jnp.minimum(i, nu[0] - 1), 0))),
            pl.BlockSpec(memory_space=pl.ANY),
            pl.BlockSpec((None, 1, 2 * D_FF), smap(lambda i, be, nu: (layer * N_EXPERTS + be[i], 0, 0))),
            pl.BlockSpec(memory_space=pl.ANY),
            pl.BlockSpec((None, 1, D), smap(lambda i, be, nu: (layer * N_EXPERTS + be[i], 0, 0))),
        ],
        out_specs=pl.BlockSpec((BM * RT, LANES), smap(lambda i, be, nu: (i, 0))),
        scratch_shapes=[pltpu.VMEM((2, D, 2 * D_FF), F32),
                        pltpu.VMEM((2, D_FF, D), F32),
                        pltpu.VMEM((D, 2 * D_FF), BF16),
                        pltpu.VMEM((D_FF, D), BF16),
                        pltpu.SemaphoreType.DMA((2, 2))],
    )
    return pl.pallas_call(
        functools.partial(_expert_kernel, layer=layer),
        grid_spec=grid_spec,
        out_shape=jax.ShapeDtypeStruct(xb.shape, F32),
        compiler_params=_cparams(("arbitrary",), VMEM_LIMIT),
        name="experts",
    )(block_exp, n_used, first, wslot.astype(jnp.int32), next_exp.astype(jnp.int32),
      valid.astype(jnp.int32), xb, w1,
      b1.reshape(-1, 1, 2 * D_FF), w2, b2.reshape(-1, 1, D))


def _combine_kernel(dest_ref, dest_next_ref, y_hbm, gate_ref, x_ref, mod_ref, fg_ref, o_ref, buf, sem,
                    *, final_norm):
    i = pl.program_id(0)
    n = pl.num_programs(0)
    slot = i % 2
    rows = TOP_K * TM

    def gather(dest, to_slot):
        def body(p, carry):
            for half in range(2):
                r = 2 * p + half
                pltpu.make_async_copy(y_hbm.at[pl.ds(dest[0, 0, r] * RT, RT), :],
                                      buf.at[to_slot, pl.ds(r * RT, RT), :],
                                      sem.at[to_slot]).start(priority=half)
            return carry
        lax.fori_loop(0, rows // 2, body, 0, unroll=4)

    @pl.when(i == 0)
    def _():
        gather(dest_ref, 0)

    @pl.when(i + 1 < n)
    def _():
        gather(dest_next_ref, 1 - slot)

    pltpu.make_async_copy(y_hbm.at[pl.ds(0, rows * RT), :], buf.at[slot], sem.at[slot]).wait()
    eye = (lax.broadcasted_iota(jnp.int32, (8, 8), 0)
           == lax.broadcasted_iota(jnp.int32, (8, 8), 1)).astype(F32)
    gate = lax.dot_general(gate_ref[...], eye, _TN, precision=HIGHEST,
                           preferred_element_type=F32)
    cur = buf.at[slot]
    outs = []
    for j in range(RT):
        acc = None
        for kk in range(TOP_K):
            piece = gate[:, kk:kk + 1] * cur[pl.ds(kk * TM * RT + j, TM, stride=RT), :]
            acc = piece if acc is None else acc + piece
        outs.append(acc)
    out = x_ref[...] + mod_ref[5:6, :] * jnp.concatenate(outs, axis=1)
    if final_norm:
        out = _rms(out, fg_ref[...])
    o_ref[...] = out


def _combine(yb, dest3, gate, x_all, mods, layer, final_g, n_tiles, final_norm):
    last = n_tiles - 1
    tile = pl.BlockSpec((TM, D), lambda i: (i, 0))
    return pl.pallas_call(
        functools.partial(_combine_kernel, final_norm=final_norm),
        grid=(n_tiles,),
        in_specs=[pl.BlockSpec((1, 1, TOP_K * TM), lambda i: (i, 0, 0), memory_space=pltpu.SMEM),
                  pl.BlockSpec((1, 1, TOP_K * TM), lambda i: (jnp.minimum(i + 1, last), 0, 0),
                               memory_space=pltpu.SMEM),
                  pl.BlockSpec(memory_space=pl.ANY),
                  pl.BlockSpec((None, 8, TM), lambda i: (i, 0, 0)),
                  tile,
                  pl.BlockSpec((None, 6, D), lambda i: (layer * 8 + _tile_cond(i), 0, 0)),
                  pl.BlockSpec((1, D), lambda i: (0, 0))],
        out_specs=tile,
        out_shape=jax.ShapeDtypeStruct((n_tiles * TM, D), F32),
        scratch_shapes=[pltpu.VMEM((2, TOP_K * TM * RT, LANES), F32),
                        pltpu.SemaphoreType.DMA((2,))],
        compiler_params=_cparams(("arbitrary",), VMEM_LIMIT, disable_bounds_checks=True),
        name="combine",
    )(dest3, dest3, yb, gate, x_all, mods, final_g.reshape(1, D))


def _moe(x_all, mods, layer, norm_g, w_r, b_r, w1, b1, w2, b2, final_g, n_tiles, final_norm):
    n = n_tiles * TM
    h_rt, e, rank, gate, cnt = _route(x_all, mods, layer, norm_g, w_r, b_r, n_tiles)
    counts = cnt[:, 0].astype(jnp.int32)
    padded = (counts + BM - 1) // BM * BM
    pad_end = jnp.cumsum(padded)
    pad_start = pad_end - padded
    n_blocks = -(-(n * TOP_K) // BM) + N_EXPERTS
    starts = jnp.arange(n_blocks, dtype=jnp.int32) * BM
    block_exp = jnp.minimum(jnp.sum((pad_end[None, :] <= starts[:, None]).astype(jnp.int32), axis=1),
                            N_EXPERTS - 1)
    n_used = pad_end[-1:] // BM
    dest = _dest(pad_start, e, rank)
    xb, dest3 = _dispatch(h_rt, dest, pad_start + counts, pad_end, n_blocks)
    yb = _experts(xb, block_exp, n_used, counts, pad_start + counts, layer, w1, b1, w2, b2)
    return _combine(yb, dest3, gate, x_all, mods, layer, final_g, n_tiles, final_norm)


def kernel(x, c, ctx, c_ctx, mod_w, mod_b, norm1_g, norm2_g, fourier_w_in, fourier_w_out,
           hgrn_w_in, hgrn_lower_bounds, hgrn_norm_g, hgrn_w_out, router_w, router_b,
           expert_w1, expert_b1, expert_w2, expert_b2, final_norm_g):
    assert x.shape == (BATCH, SEQ, D) and ctx.shape == (BATCH, CTX_LEN, D)
    cond8 = jnp.zeros((8, D), F32).at[:BATCH].set(c).at[BATCH].set(c_ctx)
    mods = _adaln(cond8, mod_w, mod_b)
    x_all = jnp.concatenate([x.reshape(NLAT, D), ctx.reshape(NCTX, D)], axis=0)
    dch, mpos, mrow = _fourier_constants()
    experts = (expert_w1, expert_b1, expert_w2, expert_b2)

    vr, vi = _fourier_in(x_all, mods, 0, norm1_g[0], fourier_w_in[0], dch, mpos)
    x_all = _fourier_out(vr, vi, x_all, mods, 0, fourier_w_out[0], mrow)
    x_all = _moe(x_all, mods, 0, norm2_g[0], router_w[0], router_b[0], *experts,
                 final_norm_g, ALL_TILES, False)

    q, ff, fb, v, gs = _hgrn_in(x_all, mods, 1, norm1_g[1], hgrn_w_in[0], hgrn_lower_bounds)
    o_fw, o_bw = _scan(q, ff, fb, v)
    x_lat = _hgrn_out(o_fw, o_bw, gs, x_all, mods, 1, hgrn_norm_g[0], hgrn_w_out[0], LAT_TILES)
    out = _moe(x_lat, mods, 1, norm2_g[1], router_w[1], router_b[1], *experts,
               final_norm_g, LAT_TILES, True)
    return out.reshape(BATCH, SEQ, D)
```

```python
import functools

import numpy as np
import jax
import jax.numpy as jnp
from jax import lax
from jax.experimental import pallas as pl
from jax.experimental.pallas import tpu as pltpu

F32 = jnp.float32
BF16 = jnp.bfloat16
HIGHEST = lax.Precision.HIGHEST

D = 1024
BATCH = 2
SEQ = 8192
CTX_LEN = 256
GRID_W = 64
GRID_H = SEQ // GRID_W
NLAT = BATCH * SEQ
NCTX = BATCH * CTX_LEN
NTOK = NLAT + NCTX
TM = 256
LAT_TILES = NLAT // TM
ALL_TILES = NTOK // TM
TILES_PER_BATCH = SEQ // TM
FGROUPS = 4
FGDIM = D // FGROUPS
HEADS = 8
HDIM = D // HEADS
CHUNK = 128
N_EXPERTS = 32
TOP_K = 4
D_FF = 1024
SWIGLU_ALPHA = 1.702
SWIGLU_LIMIT = 7.0
BM = 512
LANES = 128
RT = D // LANES
NORM_EPS = 1e-6
VMEM_LIMIT = 56 * 1024 * 1024

_NT = (((1,), (1,)), ((), ()))
_TN = (((0,), (0,)), ((), ()))


def _cparams(sem, vmem=None, **kw):
    return pltpu.CompilerParams(dimension_semantics=sem, vmem_limit_bytes=vmem, **kw)


def _sigmoid(x):
    return 1.0 / (1.0 + jnp.exp(-x))


def _rms(x, g):
    return x * lax.rsqrt(jnp.mean(x * x, axis=-1, keepdims=True) + NORM_EPS) * g


def _tile_cond(i):
    return jnp.where(i < LAT_TILES, i // TILES_PER_BATCH, 2)


def _adaln_kernel(cond_ref, w_ref, b_ref, o_ref):
    c = cond_ref[...]
    s = c * _sigmoid(c)
    o_ref[...] = jnp.dot(s, w_ref[...], precision=HIGHEST,
                         preferred_element_type=F32) + b_ref[...]


def _adaln(cond8, mod_w, mod_b):
    depth = mod_w.shape[0]
    nb = 1536
    out = pl.pallas_call(
        _adaln_kernel,
        grid=(depth, 6 * D // nb),
        in_specs=[pl.BlockSpec((8, D), lambda l, j: (0, 0)),
                  pl.BlockSpec((None, D, nb), lambda l, j: (l, 0, j)),
                  pl.BlockSpec((None, 1, nb), lambda l, j: (l, 0, j))],
        out_specs=pl.BlockSpec((None, 8, nb), lambda l, j: (l, 0, j)),
        out_shape=jax.ShapeDtypeStruct((depth, 8, 6 * D), F32),
        compiler_params=_cparams(("arbitrary", "arbitrary"), VMEM_LIMIT),
        name="adaln",
    )(cond8, mod_w, mod_b.reshape(depth, 1, 6 * D))
    return out.reshape(depth * 8, 6, D)


def _dft_cs(n):
    k = np.arange(n)
    ang = 2.0 * np.pi * np.outer(k, k) / n
    s = 1.0 / np.sqrt(n)
    return np.cos(ang) * s, np.sin(ang) * s


def _fourier_constants():
    cd, sd = _dft_cs(FGDIM)
    dch = np.concatenate([cd, sd], axis=1)
    cc, sc = _dft_cs(GRID_W)
    eye = np.eye(TM // GRID_W)
    kc, ks = np.kron(eye, cc), np.kron(eye, sc)
    m_lat = np.block([[kc, -ks], [ks, kc]])
    cp, sp = _dft_cs(CTX_LEN)
    m_ctx = np.block([[cp, -sp], [sp, cp]])
    mpos = np.stack([m_lat, m_ctx])
    cr, sr = _dft_cs(GRID_H)
    mrow = np.concatenate([cr, -sr], axis=1)
    return (jnp.asarray(dch, BF16), jnp.asarray(mpos, BF16), jnp.asarray(mrow, BF16))


def _fourier_in_kernel(x_ref, mod_ref, g_ref, win_ref, dch_ref, mpos_ref, vr_ref, vi_ref):
    x = x_ref[...]
    h = _rms(x, g_ref[...]) * (1.0 + mod_ref[1:2, :]) + mod_ref[0:1, :]
    u = jnp.dot(h.astype(BF16), win_ref[...], preferred_element_type=F32).astype(BF16)
    parts = [jnp.dot(u[:, g * FGDIM:(g + 1) * FGDIM], dch_ref[...],
                     preferred_element_type=F32) for g in range(FGROUPS)]
    uc = jnp.concatenate([p[:, :FGDIM] for p in parts], axis=1)
    us = jnp.concatenate([p[:, FGDIM:] for p in parts], axis=1)
    st = jnp.concatenate([uc, us], axis=0).astype(BF16)
    v = jnp.dot(mpos_ref[...], st, preferred_element_type=F32)
    vr_ref[...] = v[:TM]
    vi_ref[...] = v[TM:]


def _fourier_in(x_all, mods, layer, norm_g, w_in, dch, mpos):
    tile = pl.BlockSpec((TM, D), lambda i: (i, 0))
    return pl.pallas_call(
        _fourier_in_kernel,
        grid=(ALL_TILES,),
        in_specs=[tile,
                  pl.BlockSpec((None, 6, D), lambda i: (layer * 8 + _tile_cond(i), 0, 0)),
                  pl.BlockSpec((1, D), lambda i: (0, 0)),
                  pl.BlockSpec((D, D), lambda i: (0, 0)),
                  pl.BlockSpec((FGDIM, 2 * FGDIM), lambda i: (0, 0)),
                  pl.BlockSpec((None, 2 * TM, 2 * TM), lambda i: (jnp.where(i < LAT_TILES, 0, 1), 0, 0))],
        out_specs=[tile, tile],
        out_shape=[jax.ShapeDtypeStruct((NTOK, D), F32)] * 2,
        compiler_params=_cparams(("arbitrary",), VMEM_LIMIT),
        name="fourier_in",
    )(x_all, mods, norm_g.reshape(1, D), w_in.astype(BF16), dch, mpos)


CB = 8


def _fourier_out_lat_kernel(vr_ref, vi_ref, x_ref, mrow_ref, wout_ref, mod_ref, o_ref):
    g1 = mod_ref[2:3, :]
    yf = []
    for c in range(CB):
        st = jnp.concatenate([vr_ref[:, c, :], vi_ref[:, c, :]], axis=0).astype(BF16)
        yf.append(jnp.dot(mrow_ref[...], st, preferred_element_type=F32).astype(BF16))
    y = jnp.dot(jnp.concatenate(yf, axis=0), wout_ref[...], preferred_element_type=F32)
    for c in range(CB):
        o_ref[:, c, :] = x_ref[:, c, :] + g1 * y[c * GRID_H:(c + 1) * GRID_H, :]


def _fourier_out_ctx_kernel(yr_ref, x_ref, wout_ref, mod_ref, o_ref):
    y = jnp.dot(yr_ref[...].astype(BF16), wout_ref[...], preferred_element_type=F32)
    o_ref[...] = x_ref[...] + mod_ref[2:3, :] * y


def _fourier_out(vr, vi, x_all, mods, layer, w_out, mrow):
    wout = w_out.astype(BF16)
    rows = NTOK // GRID_W
    v3 = lambda a: a.reshape(rows, GRID_W, D)
    blk = pl.BlockSpec((GRID_H, CB, D), lambda b, c: (b, c, 0))
    x_new = pl.pallas_call(
        _fourier_out_lat_kernel,
        grid=(BATCH, GRID_W // CB),
        in_specs=[blk, blk, blk,
                  pl.BlockSpec((GRID_H, 2 * GRID_H), lambda b, c: (0, 0)),
                  pl.BlockSpec((D, D), lambda b, c: (0, 0)),
                  pl.BlockSpec((None, 6, D), lambda b, c: (layer * 8 + b, 0, 0))],
        out_specs=blk,
        out_shape=jax.ShapeDtypeStruct((rows, GRID_W, D), F32),
        input_output_aliases={2: 0},
        compiler_params=_cparams(("arbitrary", "arbitrary"), VMEM_LIMIT),
        name="fourier_out_lat",
    )(v3(vr), v3(vi), v3(x_all), mrow, wout, mods).reshape(NTOK, D)
    ctile = pl.BlockSpec((TM, D), lambda i: (LAT_TILES + i, 0))
    return pl.pallas_call(
        _fourier_out_ctx_kernel,
        grid=(NCTX // TM,),
        in_specs=[ctile, ctile,
                  pl.BlockSpec((D, D), lambda i: (0, 0)),
                  pl.BlockSpec((None, 6, D), lambda i: (layer * 8 + 2, 0, 0))],
        out_specs=ctile,
        out_shape=jax.ShapeDtypeStruct((NTOK, D), F32),
        input_output_aliases={1: 0},
        compiler_params=_cparams(("arbitrary",), VMEM_LIMIT),
        name="fourier_out_ctx",
    )(vr, x_new, wout, mods)


def _hgrn_in_kernel(x_ref, mod_ref, g_ref, win_ref, hlb_ref, q_ref, ff_ref, fb_ref, v_ref, gs_ref,
                    *, layer):
    x = x_ref[...]
    h = (_rms(x, g_ref[...]) * (1.0 + mod_ref[1:2, :]) + mod_ref[0:1, :]).astype(BF16)
    raw = [hlb_ref[l] for l in range(hlb_ref.shape[0])]
    mx = functools.reduce(jnp.maximum, raw)
    ex = [jnp.exp(r - mx) for r in raw]
    den = functools.reduce(lambda a, b: a + b, ex)
    soft = [e / den for e in ex]
    lb = functools.reduce(lambda a, b: a + b, soft[:layer + 1]) - soft[0]

    def proj(j):
        return jnp.dot(h, win_ref[:, j * D:(j + 1) * D], preferred_element_type=F32)

    q = proj(0)
    q_ref[...] = q * _sigmoid(q)
    ff_ref[...] = lb[0:1, :] + (1.0 - lb[0:1, :]) * _sigmoid(proj(1))
    fb_ref[...] = lb[1:2, :] + (1.0 - lb[1:2, :]) * _sigmoid(proj(2))
    v_ref[...] = proj(3)
    g = proj(4)
    gs_ref[...] = (g * _sigmoid(g)).astype(BF16)


def _hgrn_in(x_all, mods, layer, norm_g, w_in, hlb):
    tile = pl.BlockSpec((TM, D), lambda i: (i, 0))
    depth = hlb.shape[0]
    return pl.pallas_call(
        functools.partial(_hgrn_in_kernel, layer=layer),
        grid=(ALL_TILES,),
        in_specs=[tile,
                  pl.BlockSpec((None, 6, D), lambda i: (layer * 8 + _tile_cond(i), 0, 0)),
                  pl.BlockSpec((1, D), lambda i: (0, 0)),
                  pl.BlockSpec((D, 5 * D), lambda i: (0, 0)),
                  pl.BlockSpec((depth, 2, D), lambda i: (0, 0, 0))],
        out_specs=[tile] * 5,
        out_shape=[jax.ShapeDtypeStruct((NTOK, D), F32)] * 4 + [jax.ShapeDtypeStruct((NTOK, D), BF16)],
        compiler_params=_cparams(("arbitrary",), VMEM_LIMIT),
        name="hgrn_in",
    )(x_all, mods, norm_g.reshape(1, D), w_in.astype(BF16), hlb)


N_LEVELS = 7
SUB = 8
NGRP = CHUNK // SUB
FINE_LEVELS = 3


def _scan_pair_kernel(qf_ref, ff_ref, vf_ref, qb_ref, fb_ref, vb_ref, of_ref, ob_ref, sf_ref, sb_ref):
    @pl.when(pl.program_id(1) == 0)
    def _():
        sf_ref[...] = jnp.zeros_like(sf_ref)
        sb_ref[...] = jnp.zeros_like(sb_ref)

    _scan_chunk(qf_ref, ff_ref, vf_ref, of_ref, sf_ref, rev=False)
    _scan_chunk(qb_ref, fb_ref, vb_ref, ob_ref, sb_ref, rev=True)


def _scan_chunk(q_ref, fg_ref, v_ref, o_ref, s_ref, *, rev):
    q = q_ref[...]
    fg = fg_ref[...]
    k = 1.0 - fg
    vb = v_ref[...].astype(BF16)
    t = lax.broadcasted_iota(jnp.int32, (CHUNK, D), 0)
    ti = lax.broadcasted_iota(jnp.int32, (CHUNK, CHUNK), 0)
    si = lax.broadcasted_iota(jnp.int32, (CHUNK, CHUNK), 1)
    scores = [None] * HEADS

    def add_level(qs, ks, mask):
        qb, kb = qs.astype(BF16), ks.astype(BF16)
        for h in range(HEADS):
            sl = slice(h * HDIM, (h + 1) * HDIM)
            sc = lax.dot_general(qb[:, sl], kb[:, sl], _NT, preferred_element_type=F32)
            sc = jnp.where(mask, sc, 0.0)
            scores[h] = sc if scores[h] is None else scores[h] + sc

    add_level(q, k, ti == si)
    qr = fg * q
    kr = k
    tot = fg
    for l in range(FINE_LEVELS):
        hbit = 1 << l
        odd = (t & hbit) != 0
        far = (t & hbit) == 0 if rev else odd
        t_far = (ti & hbit) == 0 if rev else (ti & hbit) != 0
        s_near = (si & hbit) != 0 if rev else (si & hbit) == 0
        add_level(qr, kr, ((ti >> (l + 1)) == (si >> (l + 1))) & t_far & s_near)
        tot3 = tot.reshape(NGRP, SUB, D)
        sib = jnp.where(odd, pltpu.roll(tot3, hbit, 1).reshape(CHUNK, D),
                        pltpu.roll(tot3, SUB - hbit, 1).reshape(CHUNK, D))
        qr = jnp.where(far, qr * sib, qr)
        kr = jnp.where(far, kr, kr * sib)
        tot = tot * sib

    def groups(a):
        return [a[b * SUB:(b + 1) * SUB, :] for b in range(NGRP)]

    qr_g, kr_g, tot_g = (groups(a) for a in (qr, kr, tot))
    sc_g = [groups(scores[h]) for h in range(HEADS)]
    lane = lax.broadcasted_iota(jnp.int32, (SUB, CHUNK), 1)
    zero_g = jnp.zeros((SUB, D), F32)
    for l in range(FINE_LEVELS, N_LEVELS):
        bit = 1 << (l - FINE_LEVELS)
        is_far = [((b & bit) == 0) == rev for b in range(NGRP)]
        far_groups = [b for b in range(NGRP) if is_far[b]]
        qb = jnp.concatenate([qr_g[b] for b in far_groups], axis=0).astype(BF16)
        kb = jnp.concatenate([zero_g if is_far[b] else kr_g[b] for b in range(NGRP)],
                             axis=0).astype(BF16)
        span = 2 << l
        keep = [None if span == CHUNK else
                (lane >= b * SUB // span * span) & (lane < b * SUB // span * span + span)
                for b in far_groups]
        for h in range(HEADS):
            sl = slice(h * HDIM, (h + 1) * HDIM)
            sc = lax.dot_general(qb[:, sl], kb[:, sl], _NT, preferred_element_type=F32)
            for i, b in enumerate(far_groups):
                piece = sc[i * SUB:(i + 1) * SUB, :]
                if keep[i] is not None:
                    piece = jnp.where(keep[i], piece, 0.0)
                sc_g[h][b] = sc_g[h][b] + piece
        sib_g = [tot_g[b ^ bit] for b in range(NGRP)]
        qr_g = [qr_g[b] * sib_g[b] if is_far[b] else qr_g[b] for b in range(NGRP)]
        kr_g = [kr_g[b] if is_far[b] else kr_g[b] * sib_g[b] for b in range(NGRP)]
        if l < N_LEVELS - 1:
            tot_g = [tot_g[b] * sib_g[b] for b in range(NGRP)]
        else:
            tot_row = tot_g[0][0:1, :] * sib_g[0][0:1, :]
    qin = jnp.concatenate(qr_g, axis=0).astype(BF16)
    kst = jnp.concatenate(kr_g, axis=0).astype(BF16)
    for h in range(HEADS):
        sl = slice(h * HDIM, (h + 1) * HDIM)
        st = s_ref[h]
        sc = jnp.concatenate(sc_g[h], axis=0).astype(BF16)
        o_ref[:, sl] = (jnp.dot(sc, vb[:, sl], preferred_element_type=F32)
                        + lax.dot_general(qin[:, sl], st.astype(BF16), _NT,
                                          preferred_element_type=F32)).astype(BF16)
        s_ref[h] = st * tot_row[:, sl] + lax.dot_general(vb[:, sl], kst[:, sl], _TN,
                                                         preferred_element_type=F32)


LAT_CHUNKS = SEQ // CHUNK
CTX_CHUNKS = CTX_LEN // CHUNK
SCAN_STEPS = CTX_CHUNKS + LAT_CHUNKS


def _scan(q, f_fw, f_bw, v):
    def idx_fw(b, s):
        return (jnp.where(s < CTX_CHUNKS, NLAT // CHUNK + CTX_CHUNKS * b + s,
                          LAT_CHUNKS * b + (s - CTX_CHUNKS)), 0)

    def idx_bw(b, s):
        return (jnp.where(s < CTX_CHUNKS, NLAT // CHUNK + CTX_CHUNKS * b + (CTX_CHUNKS - 1 - s),
                          LAT_CHUNKS * b + (SCAN_STEPS - 1 - s)), 0)
    fw = pl.BlockSpec((CHUNK, D), idx_fw)
    bw = pl.BlockSpec((CHUNK, D), idx_bw)
    state = pltpu.VMEM((HEADS, HDIM, HDIM), F32)
    return pl.pallas_call(
        _scan_pair_kernel,
        grid=(BATCH, SCAN_STEPS),
        in_specs=[fw, fw, fw, bw, bw, bw],
        out_specs=[fw, bw],
        out_shape=[jax.ShapeDtypeStruct((NTOK, D), BF16)] * 2,
        scratch_shapes=[state, state],
        compiler_params=_cparams(("arbitrary", "arbitrary"), VMEM_LIMIT),
        name="scan",
    )(q, f_fw, v, q, f_bw, v)


def _hgrn_out_kernel(of_ref, ob_ref, gs_ref, x_ref, ng_ref, wout_ref, mod_ref, o_ref):
    o = of_ref[...].astype(F32) + ob_ref[...].astype(F32)
    parts = []
    for h in range(HEADS):
        oh = o[:, h * HDIM:(h + 1) * HDIM]
        parts.append(oh * lax.rsqrt(jnp.mean(oh * oh, axis=-1, keepdims=True) + NORM_EPS))
    on = jnp.concatenate(parts, axis=1) * ng_ref[...]
    y = jnp.dot((on * gs_ref[...].astype(F32)).astype(BF16), wout_ref[...],
                preferred_element_type=F32)
    o_ref[...] = x_ref[...] + mod_ref[2:3, :] * y


def _hgrn_out(o_fw, o_bw, gs, x_all, mods, layer, norm_g, w_out, n_tiles):
    tile = pl.BlockSpec((TM, D), lambda i: (i, 0))
    return pl.pallas_call(
        _hgrn_out_kernel,
        grid=(n_tiles,),
        in_specs=[tile, tile, tile, tile,
                  pl.BlockSpec((1, D), lambda i: (0, 0)),
                  pl.BlockSpec((D, D), lambda i: (0, 0)),
                  pl.BlockSpec((None, 6, D), lambda i: (layer * 8 + _tile_cond(i), 0, 0))],
        out_specs=tile,
        out_shape=jax.ShapeDtypeStruct((n_tiles * TM, D), F32),
        compiler_params=_cparams(("arbitrary",), VMEM_LIMIT),
        name="hgrn_out",
    )(o_fw, o_bw, gs, x_all, norm_g.reshape(1, D), w_out.astype(BF16), mods)


def _to_row_tiled(ref, val):
    for j in range(RT):
        ref[pl.ds(j, val.shape[0], stride=RT), :] = val[:, j * LANES:(j + 1) * LANES]


def _from_row_tiled(ref, n, base=0):
    return [ref[pl.ds(base + j, n, stride=RT), :] for j in range(RT)]


def _route_kernel(x_ref, mod_ref, g_ref, wrt_ref, brt_ref, h_ref, e_ref, rank_ref, gate_ref, cnt_ref):
    @pl.when(pl.program_id(0) == 0)
    def _():
        cnt_ref[...] = jnp.zeros_like(cnt_ref)

    h = _rms(x_ref[...], g_ref[...]) * (1.0 + mod_ref[4:5, :]) + mod_ref[3:4, :]
    _to_row_tiled(h_ref, h)
    w = wrt_ref[...]
    w_hi = w.astype(BF16)
    w_lo = (w - w_hi.astype(F32)).astype(BF16)
    h_hi = h.astype(BF16)
    h_lo = (h - h_hi.astype(F32)).astype(BF16)
    part = lax.dot_general(jnp.concatenate([w_hi, w_lo], axis=0), h_hi, _NT,
                           preferred_element_type=F32)
    logits = (part[:N_EXPERTS] + part[N_EXPERTS:]
              + lax.dot_general(w_hi, h_lo, _NT, preferred_element_type=F32)
              + brt_ref[:, 0:1])
    row = lax.broadcasted_iota(jnp.int32, (N_EXPERTS, TM), 0).astype(F32)
    vals = logits
    sel = jnp.zeros((N_EXPERTS, TM), F32)
    tops, idxs = [], []
    for _ in range(TOP_K):
        m = jnp.max(vals, axis=0, keepdims=True)
        idx = jnp.min(jnp.where(vals == m, row, float(N_EXPERTS)), axis=0, keepdims=True)
        hit = row == idx
        vals = jnp.where(hit, -jnp.inf, vals)
        sel = jnp.where(hit, 1.0, sel)
        tops.append(m)
        idxs.append(idx)
    ex = [jnp.exp(m - tops[0]) for m in tops]
    den = ex[0] + ex[1] + ex[2] + ex[3]
    r = lax.broadcasted_iota(jnp.int32, (TM, TM), 0)
    c = lax.broadcasted_iota(jnp.int32, (TM, TM), 1)
    before = jnp.where(r < c, 1.0, 0.0).astype(BF16)
    pref = jnp.dot(sel.astype(BF16), before, preferred_element_type=F32) + cnt_ref[:, 0:1]
    slot = lax.broadcasted_iota(jnp.int32, (8, TM), 0)
    e_out = jnp.zeros((8, TM), F32)
    rank_out = jnp.zeros((8, TM), F32)
    gate_out = jnp.zeros((8, TM), F32)
    for kk in range(TOP_K):
        rank = jnp.sum(jnp.where(row == idxs[kk], pref, 0.0), axis=0, keepdims=True)
        e_out = jnp.where(slot == kk, idxs[kk], e_out)
        rank_out = jnp.where(slot == kk, rank, rank_out)
        gate_out = jnp.where(slot == kk, ex[kk] / den, gate_out)
    e_ref[...] = e_out.astype(jnp.int32)
    rank_ref[...] = rank_out.astype(jnp.int32)
    gate_ref[...] = gate_out
    cnt_ref[...] += jnp.sum(sel, axis=1, keepdims=True)


def _route(x_all, mods, layer, norm_g, w_r, b_r, n_tiles):
    tile = pl.BlockSpec((TM, D), lambda i: (i, 0))
    small = pl.BlockSpec((None, 8, TM), lambda i: (i, 0, 0))
    n = n_tiles * TM
    return pl.pallas_call(
        _route_kernel,
        grid=(n_tiles,),
        in_specs=[tile,
                  pl.BlockSpec((None, 6, D), lambda i: (layer * 8 + _tile_cond(i), 0, 0)),
                  pl.BlockSpec((1, D), lambda i: (0, 0)),
                  pl.BlockSpec((N_EXPERTS, D), lambda i: (0, 0)),
                  pl.BlockSpec((N_EXPERTS, LANES), lambda i: (0, 0))],
        out_specs=[pl.BlockSpec((TM * RT, LANES), lambda i: (i, 0)), small, small, small,
                   pl.BlockSpec((N_EXPERTS, LANES), lambda i: (0, 0))],
        out_shape=[jax.ShapeDtypeStruct((n * RT, LANES), F32),
                   jax.ShapeDtypeStruct((n_tiles, 8, TM), jnp.int32),
                   jax.ShapeDtypeStruct((n_tiles, 8, TM), jnp.int32),
                   jax.ShapeDtypeStruct((n_tiles, 8, TM), F32),
                   jax.ShapeDtypeStruct((N_EXPERTS, LANES), F32)],
        compiler_params=_cparams(("arbitrary",), VMEM_LIMIT),
        name="route",
    )(x_all, mods, norm_g.reshape(1, D), w_r.T, jnp.broadcast_to(b_r[:, None], (N_EXPERTS, LANES)))


def _dest_kernel(ps_ref, e_ref, rank_ref, d_ref):
    e = e_ref[...]
    acc = rank_ref[...]
    for j in range(N_EXPERTS):
        acc = acc + jnp.where(e == j, ps_ref[j], 0)
    d_ref[...] = acc


def _dest(pad_start, e, rank):
    full = pl.BlockSpec(e.shape, lambda i, ps: (0, 0, 0))
    return pl.pallas_call(
        _dest_kernel,
        grid_spec=pltpu.PrefetchScalarGridSpec(num_scalar_prefetch=1, grid=(1,),
                                               in_specs=[full, full], out_specs=full),
        out_shape=jax.ShapeDtypeStruct(e.shape, jnp.int32),
        name="dest",
    )(pad_start, e, rank)


PAD_PIECES = tuple(1 << p for p in reversed(range(BM.bit_length() - 1)))


def _zero_fill(lo_ref, hi_ref, xb_hbm, zeros, sem, n_blocks, wait):
    def go(rows, row0):
        cp = pltpu.make_async_copy(zeros.at[pl.ds(0, rows * RT), :],
                                   xb_hbm.at[pl.ds(row0 * RT, rows * RT), :], sem.at[1])
        cp.wait() if wait else cp.start()

    def per_expert(e, carry):
        row = lo_ref[e]
        n = hi_ref[e] - row
        for piece in PAD_PIECES:
            @pl.when((n & piece) != 0)
            def _():
                go(piece, row)
            row = row + (n & piece)
        return carry
    lax.fori_loop(0, N_EXPERTS, per_expert, 0)

    def per_block(b, carry):
        go(BM, b * BM)
        return carry
    lax.fori_loop(hi_ref[N_EXPERTS - 1] // BM, n_blocks, per_block, 0)


def _dispatch_kernel(lo_ref, hi_ref, dest_ref, h_ref, xb_hbm, zeros, sem, *, n_blocks):
    i = pl.program_id(0)

    @pl.when(i == 0)
    def _():
        zeros[...] = jnp.zeros_like(zeros)
        _zero_fill(lo_ref, hi_ref, xb_hbm, zeros, sem, n_blocks, wait=False)

    def body(t, carry):
        src = h_ref.at[pl.ds(t * RT, RT), :]
        for kk in range(TOP_K):
            d = dest_ref[0, 0, kk * TM + t]
            pltpu.make_async_copy(src, xb_hbm.at[pl.ds(d * RT, RT), :],
                                  sem.at[0]).start(priority=kk % 2)
        return carry
    lax.fori_loop(0, TM, body, 0, unroll=4)
    for kk in range(TOP_K):
        pltpu.make_async_copy(h_ref, xb_hbm.at[pl.ds(0, TM * RT), :], sem.at[0]).wait()

    @pl.when(i == pl.num_programs(0) - 1)
    def _():
        _zero_fill(lo_ref, hi_ref, xb_hbm, zeros, sem, n_blocks, wait=True)


def _dispatch(h_rt, dest, pad_lo, pad_hi, n_blocks):
    n_tiles = dest.shape[0]
    dest3 = dest[:, :TOP_K, :].reshape(n_tiles, 1, TOP_K * TM)
    grid_spec = pltpu.PrefetchScalarGridSpec(
        num_scalar_prefetch=2,
        grid=(n_tiles,),
        in_specs=[pl.BlockSpec((1, 1, TOP_K * TM), lambda i, lo, hi: (i, 0, 0),
                               memory_space=pltpu.SMEM),
                  pl.BlockSpec((TM * RT, LANES), lambda i, lo, hi: (i, 0))],
        out_specs=pl.BlockSpec(memory_space=pl.ANY),
        scratch_shapes=[pltpu.VMEM((BM * RT, LANES), F32),
                        pltpu.SemaphoreType.DMA((2,))],
    )
    xb = pl.pallas_call(
        functools.partial(_dispatch_kernel, n_blocks=n_blocks),
        grid_spec=grid_spec,
        out_shape=jax.ShapeDtypeStruct((n_blocks * BM * RT, LANES), F32),
        compiler_params=_cparams(("arbitrary",), VMEM_LIMIT, disable_bounds_checks=True),
        name="dispatch",
    )(pad_lo, pad_hi, dest3, h_rt)
    return xb, dest3


ROW_STEPS = tuple(range(BM // 4, BM + 1, BM // 4))


def _expert_kernel(bexp_ref, nused_ref, first_ref, wslot_ref, next_ref, valid_ref, xb_ref, w1_hbm, b1_ref,
                   w2_hbm, b2_ref, y_ref, w1s, w2s, w1b, w2b, sem, *, layer):
    i = pl.program_id(0)
    del nused_ref

    def weight_copies(e, slot):
        return (pltpu.make_async_copy(w1_hbm.at[layer, e], w1s.at[slot], sem.at[0, slot]),
                pltpu.make_async_copy(w2_hbm.at[layer, e], w2s.at[slot], sem.at[1, slot]))

    @pl.when(i == 0)
    def _():
        for cp in weight_copies(bexp_ref[0], 0):
            cp.start()

    @pl.when(first_ref[i] == 1)
    def _():
        slot = wslot_ref[i]
        for cp in weight_copies(bexp_ref[i], slot):
            cp.wait()
        w1b[...] = w1s[slot].astype(BF16)
        w2b[...] = w2s[slot].astype(BF16)

        @pl.when(next_ref[i] >= 0)
        def _():
            for cp in weight_copies(next_ref[i], 1 - slot):
                cp.start()

    valid = valid_ref[i]
    for lo, rows in zip((0,) + ROW_STEPS[:-1], ROW_STEPS):
        @pl.when(jnp.logical_and(valid > lo, valid <= rows))
        def _():
            x = jnp.concatenate(_from_row_tiled(xb_ref, rows), axis=1).astype(BF16)
            u = jnp.dot(x, w1b[...], preferred_element_type=F32) + b1_ref[...]
            glu = jnp.minimum(u[:, :D_FF], SWIGLU_LIMIT)
            lin = jnp.clip(u[:, D_FF:], -SWIGLU_LIMIT, SWIGLU_LIMIT)
            act = glu * _sigmoid(SWIGLU_ALPHA * glu) * (lin + 1.0)
            y = jnp.dot(act.astype(BF16), w2b[...], preferred_element_type=F32) + b2_ref[...]
            _to_row_tiled(y_ref.at[pl.ds(0, rows * RT), :], y)
            if rows < BM:
                y_ref[pl.ds(rows * RT, (BM - rows) * RT), :] = jnp.zeros(((BM - rows) * RT, LANES), F32)

    @pl.when(valid == 0)
    def _():
        y_ref[...] = jnp.zeros_like(y_ref)


def _experts(xb, block_exp, n_used, counts, pad_lo, layer, w1, b1, w2, b2):
    n_blocks = xb.shape[0] // (BM * RT)
    used = jnp.arange(n_blocks, dtype=jnp.int32) < n_used[0]
    changed = jnp.concatenate([jnp.ones((1,), bool), block_exp[1:] != block_exp[:-1]])
    first = jnp.logical_and(used, changed).astype(jnp.int32)
    wslot = (jnp.cumsum(first) - 1) % 2
    ar = jnp.arange(N_EXPERTS, dtype=jnp.int32)
    later = jnp.min(jnp.where((ar[None, :] > ar[:, None]) & (counts > 0)[None, :], ar[None, :],
                              N_EXPERTS), axis=1)
    of_block = block_exp[:, None] == ar[None, :]
    next_exp = jnp.sum(jnp.where(of_block, jnp.where(later == N_EXPERTS, -1, later)[None, :], 0), axis=1)
    starts = jnp.arange(n_blocks, dtype=jnp.int32) * BM
    valid = jnp.clip(jnp.sum(jnp.where(of_block, pad_lo[None, :], 0), axis=1) - starts, 0, BM)
    smap = lambda f: (lambda i, be, nu, fi, ws, nx, va: f(i, be, nu))
    grid_spec = pltpu.PrefetchScalarGridSpec(
        num_scalar_prefetch=6,
        grid=(n_blocks,),
        in_specs=[
            pl.BlockSpec((BM * RT, LANES), smap(lambda i, be, nu: (jnp.minimum(i, nu[0] - 1), 0))),
            pl.BlockSpec(memory_space=pl.ANY),
            pl.BlockSpec((None, 1, 2 * D_FF), smap(lambda i, be, nu: (layer * N_EXPERTS + be[i], 0, 0))),
            pl.BlockSpec(memory_space=pl.ANY),
            pl.BlockSpec((None, 1, D), smap(lambda i, be, nu: (layer * N_EXPERTS + be[i], 0, 0))),
        ],
        out_specs=pl.BlockSpec((BM * RT, LANES), smap(lambda i, be, nu: (i, 0))),
        scratch_shapes=[pltpu.VMEM((2, D, 2 * D_FF), F32),
                        pltpu.VMEM((2, D_FF, D), F32),
                        pltpu.VMEM((D, 2 * D_FF), BF16),
                        pltpu.VMEM((D_FF, D), BF16),
                        pltpu.SemaphoreType.DMA((2, 2))],
    )
    return pl.pallas_call(
        functools.partial(_expert_kernel, layer=layer),
        grid_spec=grid_spec,
        out_shape=jax.ShapeDtypeStruct(xb.shape, F32),
        compiler_params=_cparams(("arbitrary",), VMEM_LIMIT),
        name="experts",
    )(block_exp, n_used, first, wslot.astype(jnp.int32), next_exp.astype(jnp.int32),
      valid.astype(jnp.int32), xb, w1,
      b1.reshape(-1, 1, 2 * D_FF), w2, b2.reshape(-1, 1, D))


def _combine_kernel(dest_ref, dest_next_ref, y_hbm, gate_ref, x_ref, mod_ref, fg_ref, o_ref, buf, sem,
                    *, final_norm):
    i = pl.program_id(0)
    n = pl.num_programs(0)
    slot = i % 2
    rows = TOP_K * TM

    def gather(dest, to_slot):
        def body(p, carry):
            for half in range(2):
                r = 2 * p + half
                pltpu.make_async_copy(y_hbm.at[pl.ds(dest[0, 0, r] * RT, RT), :],
                                      buf.at[to_slot, pl.ds(r * RT, RT), :],
                                      sem.at[to_slot]).start(priority=half)
            return carry
        lax.fori_loop(0, rows // 2, body, 0, unroll=4)

    @pl.when(i == 0)
    def _():
        gather(dest_ref, 0)

    @pl.when(i + 1 < n)
    def _():
        gather(dest_next_ref, 1 - slot)

    pltpu.make_async_copy(y_hbm.at[pl.ds(0, rows * RT), :], buf.at[slot], sem.at[slot]).wait()
    eye = (lax.broadcasted_iota(jnp.int32, (8, 8), 0)
           == lax.broadcasted_iota(jnp.int32, (8, 8), 1)).astype(F32)
    gate = lax.dot_general(gate_ref[...], eye, _TN, precision=HIGHEST,
                           preferred_element_type=F32)
    cur = buf.at[slot]
    outs = []
    for j in range(RT):
        acc = None
        for kk in range(TOP_K):
            piece = gate[:, kk:kk + 1] * cur[pl.ds(kk * TM * RT + j, TM, stride=RT), :]
            acc = piece if acc is None else acc + piece
        outs.append(acc)
    out = x_ref[...] + mod_ref[5:6, :] * jnp.concatenate(outs, axis=1)
    if final_norm:
        out = _rms(out, fg_ref[...])
    o_ref[...] = out


def _combine(yb, dest3, gate, x_all, mods, layer, final_g, n_tiles, final_norm):
    last = n_tiles - 1
    tile = pl.BlockSpec((TM, D), lambda i: (i, 0))
    return pl.pallas_call(
        functools.partial(_combine_kernel, final_norm=final_norm),
        grid=(n_tiles,),
        in_specs=[pl.BlockSpec((1, 1, TOP_K * TM), lambda i: (i, 0, 0), memory_space=pltpu.SMEM),
                  pl.BlockSpec((1, 1, TOP_K * TM), lambda i: (jnp.minimum(i + 1, last), 0, 0),
                               memory_space=pltpu.SMEM),
                  pl.BlockSpec(memory_space=pl.ANY),
                  pl.BlockSpec((None, 8, TM), lambda i: (i, 0, 0)),
                  tile,
                  pl.BlockSpec((None, 6, D), lambda i: (layer * 8 + _tile_cond(i), 0, 0)),
                  pl.BlockSpec((1, D), lambda i: (0, 0))],
        out_specs=tile,
        out_shape=jax.ShapeDtypeStruct((n_tiles * TM, D), F32),
        scratch_shapes=[pltpu.VMEM((2, TOP_K * TM * RT, LANES), F32),
                        pltpu.SemaphoreType.DMA((2,))],
        compiler_params=_cparams(("arbitrary",), VMEM_LIMIT, disable_bounds_checks=True),
        name="combine",
    )(dest3, dest3, yb, gate, x_all, mods, final_g.reshape(1, D))


def _moe(x_all, mods, layer, norm_g, w_r, b_r, w1, b1, w2, b2, final_g, n_tiles, final_norm):
    n = n_tiles * TM
    h_rt, e, rank, gate, cnt = _route(x_all, mods, layer, norm_g, w_r, b_r, n_tiles)
    counts = cnt[:, 0].astype(jnp.int32)
    padded = (counts + BM - 1) // BM * BM
    pad_end = jnp.cumsum(padded)
    pad_start = pad_end - padded
    n_blocks = -(-(n * TOP_K) // BM) + N_EXPERTS
    starts = jnp.arange(n_blocks, dtype=jnp.int32) * BM
    block_exp = jnp.minimum(jnp.sum((pad_end[None, :] <= starts[:, None]).astype(jnp.int32), axis=1),
                            N_EXPERTS - 1)
    n_used = pad_end[-1:] // BM
    dest = _dest(pad_start, e, rank)
    xb, dest3 = _dispatch(h_rt, dest, pad_start + counts, pad_end, n_blocks)
    yb = _experts(xb, block_exp, n_used, counts, pad_start + counts, layer, w1, b1, w2, b2)
    return _combine(yb, dest3, gate, x_all, mods, layer, final_g, n_tiles, final_norm)


def kernel(x, c, ctx, c_ctx, mod_w, mod_b, norm1_g, norm2_g, fourier_w_in, fourier_w_out,
           hgrn_w_in, hgrn_lower_bounds, hgrn_norm_g, hgrn_w_out, router_w, router_b,
           expert_w1, expert_b1, expert_w2, expert_b2, final_norm_g):
    assert x.shape == (BATCH, SEQ, D) and ctx.shape == (BATCH, CTX_LEN, D)
    cond8 = jnp.zeros((8, D), F32).at[:BATCH].set(c).at[BATCH].set(c_ctx)
    mods = _adaln(cond8, mod_w, mod_b)
    x_all = jnp.concatenate([x.reshape(NLAT, D), ctx.reshape(NCTX, D)], axis=0)
    dch, mpos, mrow = _fourier_constants()
    experts = (expert_w1, expert_b1, expert_w2, expert_b2)

    vr, vi = _fourier_in(x_all, mods, 0, norm1_g[0], fourier_w_in[0], dch, mpos)
    x_all = _fourier_out(vr, vi, x_all, mods, 0, fourier_w_out[0], mrow)
    x_all = _moe(x_all, mods, 0, norm2_g[0], router_w[0], router_b[0], *experts,
                 final_norm_g, ALL_TILES, False)

    q, ff, fb, v, gs = _hgrn_in(x_all, mods, 1, norm1_g[1], hgrn_w_in[0], hgrn_lower_bounds)
    o_fw, o_bw = _scan(q, ff, fb, v)
    x_lat = _hgrn_out(o_fw, o_bw, gs, x_all, mods, 1, hgrn_norm_g[0], hgrn_w_out[0], LAT_TILES)
    out = _moe(x_lat, mods, 1, norm2_g[1], router_w[1], router_b[1], *experts,
               final_norm_g, LAT_TILES, True)
    return out.reshape(BATCH, SEQ, D)
```

```python
import functools

import numpy as np
import jax
import jax.numpy as jnp
from jax import lax
from jax.experimental import pallas as pl
from jax.experimental.pallas import tpu as pltpu

F32 = jnp.float32
BF16 = jnp.bfloat16
HIGHEST = lax.Precision.HIGHEST

D = 1024
BATCH = 2
SEQ = 8192
CTX_LEN = 256
GRID_W = 64
GRID_H = SEQ // GRID_W
NLAT = BATCH * SEQ
NCTX = BATCH * CTX_LEN
NTOK = NLAT + NCTX
TM = 256
LAT_TILES = NLAT // TM
ALL_TILES = NTOK // TM
TILES_PER_BATCH = SEQ // TM
FGROUPS = 4
FGDIM = D // FGROUPS
HEADS = 8
HDIM = D // HEADS
CHUNK = 128
N_EXPERTS = 32
TOP_K = 4
D_FF = 1024
SWIGLU_ALPHA = 1.702
SWIGLU_LIMIT = 7.0
BM = 512
LANES = 128
RT = D // LANES
NORM_EPS = 1e-6
VMEM_LIMIT = 56 * 1024 * 1024

_NT = (((1,), (1,)), ((), ()))
_TN = (((0,), (0,)), ((), ()))


def _cparams(sem, vmem=None, **kw):
    return pltpu.CompilerParams(dimension_semantics=sem, vmem_limit_bytes=vmem, **kw)


def _sigmoid(x):
    return 1.0 / (1.0 + jnp.exp(-x))


def _rms(x, g):
    return x * lax.rsqrt(jnp.mean(x * x, axis=-1, keepdims=True) + NORM_EPS) * g


def _tile_cond(i):
    return jnp.where(i < LAT_TILES, i // TILES_PER_BATCH, 2)


def _adaln_kernel(cond_ref, w_ref, b_ref, o_ref):
    c = cond_ref[...]
    s = c * _sigmoid(c)
    o_ref[...] = jnp.dot(s, w_ref[...], precision=HIGHEST,
                         preferred_element_type=F32) + b_ref[...]


def _adaln(cond8, mod_w, mod_b):
    depth = mod_w.shape[0]
    nb = 1536
    out = pl.pallas_call(
        _adaln_kernel,
        grid=(depth, 6 * D // nb),
        in_specs=[pl.BlockSpec((8, D), lambda l, j: (0, 0)),
                  pl.BlockSpec((None, D, nb), lambda l, j: (l, 0, j)),
                  pl.BlockSpec((None, 1, nb), lambda l, j: (l, 0, j))],
        out_specs=pl.BlockSpec((None, 8, nb), lambda l, j: (l, 0, j)),
        out_shape=jax.ShapeDtypeStruct((depth, 8, 6 * D), F32),
        compiler_params=_cparams(("arbitrary", "arbitrary"), VMEM_LIMIT),
        name="adaln",
    )(cond8, mod_w, mod_b.reshape(depth, 1, 6 * D))
    return out.reshape(depth * 8, 6, D)


def _dft_cs(n):
    k = np.arange(n)
    ang = 2.0 * np.pi * np.outer(k, k) / n
    s = 1.0 / np.sqrt(n)
    return np.cos(ang) * s, np.sin(ang) * s


def _fourier_constants():
    cd, sd = _dft_cs(FGDIM)
    dch = np.concatenate([cd, sd], axis=1)
    cc, sc = _dft_cs(GRID_W)
    eye = np.eye(COL_ROWS)
    kc, ks = np.kron(eye, cc), np.kron(eye, sc)
    m_lat = np.block([[kc, -ks], [ks, kc]])
    cp, sp = _dft_cs(CTX_LEN)
    m_ctx = np.block([[cp, -sp], [sp, cp]])
    cr, sr = _dft_cs(GRID_H)
    mrow = np.concatenate([cr, -sr], axis=1)
    return (jnp.asarray(dch, BF16), (jnp.asarray(m_lat, BF16), jnp.asarray(m_ctx, BF16)),
            jnp.asarray(mrow, BF16))


COL_ROWS = 2


def _fourier_in_kernel(x_ref, mod_ref, g_ref, win_ref, dch_ref, mlat_ref, mctx_ref, vr_ref, vi_ref):
    x = x_ref[...]
    h = _rms(x, g_ref[...]) * (1.0 + mod_ref[1:2, :]) + mod_ref[0:1, :]
    u = jnp.dot(h.astype(BF16), win_ref[...], preferred_element_type=F32).astype(BF16)
    parts = [jnp.dot(u[:, g * FGDIM:(g + 1) * FGDIM], dch_ref[...],
                     preferred_element_type=F32) for g in range(FGROUPS)]
    uc = jnp.concatenate([p[:, :FGDIM] for p in parts], axis=1).astype(BF16)
    us = jnp.concatenate([p[:, FGDIM:] for p in parts], axis=1).astype(BF16)

    @pl.when(pl.program_id(0) < LAT_TILES)
    def _():
        n = COL_ROWS * GRID_W
        for a in range(TM // n):
            rows = slice(a * n, (a + 1) * n)
            v = jnp.dot(mlat_ref[...], jnp.concatenate([uc[rows], us[rows]], axis=0),
                        preferred_element_type=F32)
            vr_ref[rows, :] = v[:n]
            vi_ref[rows, :] = v[n:]

    @pl.when(pl.program_id(0) >= LAT_TILES)
    def _():
        v = jnp.dot(mctx_ref[...], jnp.concatenate([uc, us], axis=0), preferred_element_type=F32)
        vr_ref[...] = v[:TM]
        vi_ref[...] = v[TM:]


def _fourier_in(x_all, mods, layer, norm_g, w_in, dch, mpos):
    tile = pl.BlockSpec((TM, D), lambda i: (i, 0))
    return pl.pallas_call(
        _fourier_in_kernel,
        grid=(ALL_TILES,),
        in_specs=[tile,
                  pl.BlockSpec((None, 6, D), lambda i: (layer * 8 + _tile_cond(i), 0, 0)),
                  pl.BlockSpec((1, D), lambda i: (0, 0)),
                  pl.BlockSpec((D, D), lambda i: (0, 0)),
                  pl.BlockSpec((FGDIM, 2 * FGDIM), lambda i: (0, 0)),
                  pl.BlockSpec((2 * COL_ROWS * GRID_W, 2 * COL_ROWS * GRID_W), lambda i: (0, 0)),
                  pl.BlockSpec((2 * CTX_LEN, 2 * CTX_LEN), lambda i: (0, 0))],
        out_specs=[tile, tile],
        out_shape=[jax.ShapeDtypeStruct((NTOK, D), F32)] * 2,
        compiler_params=_cparams(("arbitrary",), VMEM_LIMIT),
        name="fourier_in",
    )(x_all, mods, norm_g.reshape(1, D), w_in.astype(BF16), dch, *mpos)


CB = 8


def _fourier_out_lat_kernel(vr_ref, vi_ref, x_ref, mrow_ref, wout_ref, mod_ref, o_ref):
    g1 = mod_ref[2:3, :]
    yf = []
    for c in range(CB):
        st = jnp.concatenate([vr_ref[:, c, :], vi_ref[:, c, :]], axis=0).astype(BF16)
        yf.append(jnp.dot(mrow_ref[...], st, preferred_element_type=F32).astype(BF16))
    y = jnp.dot(jnp.concatenate(yf, axis=0), wout_ref[...], preferred_element_type=F32)
    for c in range(CB):
        o_ref[:, c, :] = x_ref[:, c, :] + g1 * y[c * GRID_H:(c + 1) * GRID_H, :]


def _fourier_out_ctx_kernel(yr_ref, x_ref, wout_ref, mod_ref, o_ref):
    y = jnp.dot(yr_ref[...].astype(BF16), wout_ref[...], preferred_element_type=F32)
    o_ref[...] = x_ref[...] + mod_ref[2:3, :] * y


def _fourier_out(vr, vi, x_all, mods, layer, w_out, mrow):
    wout = w_out.astype(BF16)
    rows = NTOK // GRID_W
    v3 = lambda a: a.reshape(rows, GRID_W, D)
    blk = pl.BlockSpec((GRID_H, CB, D), lambda b, c: (b, c, 0))
    x_new = pl.pallas_call(
        _fourier_out_lat_kernel,
        grid=(BATCH, GRID_W // CB),
        in_specs=[blk, blk, blk,
                  pl.BlockSpec((GRID_H, 2 * GRID_H), lambda b, c: (0, 0)),
                  pl.BlockSpec((D, D), lambda b, c: (0, 0)),
                  pl.BlockSpec((None, 6, D), lambda b, c: (layer * 8 + b, 0, 0))],
        out_specs=blk,
        out_shape=jax.ShapeDtypeStruct((rows, GRID_W, D), F32),
        input_output_aliases={2: 0},
        compiler_params=_cparams(("arbitrary", "arbitrary"), VMEM_LIMIT),
        name="fourier_out_lat",
    )(v3(vr), v3(vi), v3(x_all), mrow, wout, mods).reshape(NTOK, D)
    ctile = pl.BlockSpec((TM, D), lambda i: (LAT_TILES + i, 0))
    return pl.pallas_call(
        _fourier_out_ctx_kernel,
        grid=(NCTX // TM,),
        in_specs=[ctile, ctile,
                  pl.BlockSpec((D, D), lambda i: (0, 0)),
                  pl.BlockSpec((None, 6, D), lambda i: (layer * 8 + 2, 0, 0))],
        out_specs=ctile,
        out_shape=jax.ShapeDtypeStruct((NTOK, D), F32),
        input_output_aliases={1: 0},
        compiler_params=_cparams(("arbitrary",), VMEM_LIMIT),
        name="fourier_out_ctx",
    )(vr, x_new, wout, mods)


def _hgrn_in_kernel(x_ref, mod_ref, g_ref, win_ref, hlb_ref, q_ref, ff_ref, fb_ref, v_ref, gs_ref,
                    *, layer):
    x = x_ref[...]
    h = (_rms(x, g_ref[...]) * (1.0 + mod_ref[1:2, :]) + mod_ref[0:1, :]).astype(BF16)
    raw = [hlb_ref[l] for l in range(hlb_ref.shape[0])]
    mx = functools.reduce(jnp.maximum, raw)
    ex = [jnp.exp(r - mx) for r in raw]
    den = functools.reduce(lambda a, b: a + b, ex)
    soft = [e / den for e in ex]
    lb = functools.reduce(lambda a, b: a + b, soft[:layer + 1]) - soft[0]

    def proj(j):
        return jnp.dot(h, win_ref[:, j * D:(j + 1) * D], preferred_element_type=F32)

    q = proj(0)
    q_ref[...] = q * _sigmoid(q)
    ff_ref[...] = lb[0:1, :] + (1.0 - lb[0:1, :]) * _sigmoid(proj(1))
    fb_ref[...] = lb[1:2, :] + (1.0 - lb[1:2, :]) * _sigmoid(proj(2))
    v_ref[...] = proj(3)
    g = proj(4)
    gs_ref[...] = (g * _sigmoid(g)).astype(BF16)


def _hgrn_in(x_all, mods, layer, norm_g, w_in, hlb):
    tile = pl.BlockSpec((2 * TM, D), lambda i: (i, 0))
    depth = hlb.shape[0]
    return pl.pallas_call(
        functools.partial(_hgrn_in_kernel, layer=layer),
        grid=(ALL_TILES // 2,),
        in_specs=[tile,
                  pl.BlockSpec((None, 6, D), lambda i: (layer * 8 + _tile_cond(2 * i), 0, 0)),
                  pl.BlockSpec((1, D), lambda i: (0, 0)),
                  pl.BlockSpec((D, 5 * D), lambda i: (0, 0)),
                  pl.BlockSpec((depth, 2, D), lambda i: (0, 0, 0))],
        out_specs=[tile] * 5,
        out_shape=[jax.ShapeDtypeStruct((NTOK, D), F32)] * 4 + [jax.ShapeDtypeStruct((NTOK, D), BF16)],
        compiler_params=_cparams(("arbitrary",), VMEM_LIMIT),
        name="hgrn_in",
    )(x_all, mods, norm_g.reshape(1, D), w_in.astype(BF16), hlb)


N_LEVELS = 7
SUB = 8
NGRP = CHUNK // SUB
FINE_LEVELS = 3


def _scan_pair_kernel(qf_ref, ff_ref, vf_ref, qb_ref, fb_ref, vb_ref, of_ref, ob_ref, sf_ref, sb_ref):
    @pl.when(pl.program_id(1) == 0)
    def _():
        sf_ref[...] = jnp.zeros_like(sf_ref)
        sb_ref[...] = jnp.zeros_like(sb_ref)

    _scan_chunk(qf_ref, ff_ref, vf_ref, of_ref, sf_ref, rev=False)
    _scan_chunk(qb_ref, fb_ref, vb_ref, ob_ref, sb_ref, rev=True)


def _scan_chunk(q_ref, fg_ref, v_ref, o_ref, s_ref, *, rev):
    q = q_ref[...]
    fg = fg_ref[...]
    k = 1.0 - fg
    vb = v_ref[...].astype(BF16)
    t = lax.broadcasted_iota(jnp.int32, (CHUNK, D), 0)
    ti = lax.broadcasted_iota(jnp.int32, (CHUNK, CHUNK), 0)
    si = lax.broadcasted_iota(jnp.int32, (CHUNK, CHUNK), 1)
    scores = [None] * HEADS

    def add_level(qs, ks, mask):
        qb, kb = qs.astype(BF16), ks.astype(BF16)
        for h in range(HEADS):
            sl = slice(h * HDIM, (h + 1) * HDIM)
            sc = lax.dot_general(qb[:, sl], kb[:, sl], _NT, preferred_element_type=F32)
            sc = jnp.where(mask, sc, 0.0)
            scores[h] = sc if scores[h] is None else scores[h] + sc

    add_level(q, k, ti == si)
    qr = fg * q
    kr = k
    tot = fg
    for l in range(FINE_LEVELS):
        hbit = 1 << l
        odd = (t & hbit) != 0
        far = (t & hbit) == 0 if rev else odd
        t_far = (ti & hbit) == 0 if rev else (ti & hbit) != 0
        s_near = (si & hbit) != 0 if rev else (si & hbit) == 0
        add_level(qr, kr, ((ti >> (l + 1)) == (si >> (l + 1))) & t_far & s_near)
        tot3 = tot.reshape(NGRP, SUB, D)
        sib = jnp.where(odd, pltpu.roll(tot3, hbit, 1).reshape(CHUNK, D),
                        pltpu.roll(tot3, SUB - hbit, 1).reshape(CHUNK, D))
        qr = jnp.where(far, qr * sib, qr)
        kr = jnp.where(far, kr, kr * sib)
        tot = tot * sib

    def groups(a):
        return [a[b * SUB:(b + 1) * SUB, :] for b in range(NGRP)]

    qr_g, kr_g, tot_g = (groups(a) for a in (qr, kr, tot))
    sc_g = [groups(scores[h]) for h in range(HEADS)]
    lane = lax.broadcasted_iota(jnp.int32, (SUB, CHUNK), 1)
    zero_g = jnp.zeros((SUB, D), F32)
    for l in range(FINE_LEVELS, N_LEVELS):
        bit = 1 << (l - FINE_LEVELS)
        is_far = [((b & bit) == 0) == rev for b in range(NGRP)]
        far_groups = [b for b in range(NGRP) if is_far[b]]
        qb = jnp.concatenate([qr_g[b] for b in far_groups], axis=0).astype(BF16)
        kb = jnp.concatenate([zero_g if is_far[b] else kr_g[b] for b in range(NGRP)],
                             axis=0).astype(BF16)
        span = 2 << l
        keep = [None if span == CHUNK else
                (lane >= b * SUB // span * span) & (lane < b * SUB // span * span + span)
                for b in far_groups]
        for h in range(HEADS):
            sl = slice(h * HDIM, (h + 1) * HDIM)
            sc = lax.dot_general(qb[:, sl], kb[:, sl], _NT, preferred_element_type=F32)
            for i, b in enumerate(far_groups):
                piece = sc[i * SUB:(i + 1) * SUB, :]
                if keep[i] is not None:
                    piece = jnp.where(keep[i], piece, 0.0)
                sc_g[h][b] = sc_g[h][b] + piece
        sib_g = [tot_g[b ^ bit] for b in range(NGRP)]
        qr_g = [qr_g[b] * sib_g[b] if is_far[b] else qr_g[b] for b in range(NGRP)]
        kr_g = [kr_g[b] if is_far[b] else kr_g[b] * sib_g[b] for b in range(NGRP)]
        if l < N_LEVELS - 1:
            tot_g = [tot_g[b] * sib_g[b] for b in range(NGRP)]
        else:
            tot_row = tot_g[0][0:1, :] * sib_g[0][0:1, :]
    qin = jnp.concatenate(qr_g, axis=0).astype(BF16)
    kst = jnp.concatenate(kr_g, axis=0).astype(BF16)
    for h in range(HEADS):
        sl = slice(h * HDIM, (h + 1) * HDIM)
        st = s_ref[h]
        sc = jnp.concatenate(sc_g[h], axis=0).astype(BF16)
        o_ref[:, sl] = (jnp.dot(sc, vb[:, sl], preferred_element_type=F32)
                        + lax.dot_general(qin[:, sl], st.astype(BF16), _NT,
                                          preferred_element_type=F32)).astype(BF16)
        s_ref[h] = st * tot_row[:, sl] + lax.dot_general(vb[:, sl], kst[:, sl], _TN,
                                                         preferred_element_type=F32)


LAT_CHUNKS = SEQ // CHUNK
CTX_CHUNKS = CTX_LEN // CHUNK
SCAN_STEPS = CTX_CHUNKS + LAT_CHUNKS


def _scan(q, f_fw, f_bw, v):
    def idx_fw(b, s):
        return (jnp.where(s < CTX_CHUNKS, NLAT // CHUNK + CTX_CHUNKS * b + s,
                          LAT_CHUNKS * b + (s - CTX_CHUNKS)), 0)

    def idx_bw(b, s):
        return (jnp.where(s < CTX_CHUNKS, NLAT // CHUNK + CTX_CHUNKS * b + (CTX_CHUNKS - 1 - s),
                          LAT_CHUNKS * b + (SCAN_STEPS - 1 - s)), 0)
    fw = pl.BlockSpec((CHUNK, D), idx_fw)
    bw = pl.BlockSpec((CHUNK, D), idx_bw)
    state = pltpu.VMEM((HEADS, HDIM, HDIM), F32)
    return pl.pallas_call(
        _scan_pair_kernel,
        grid=(BATCH, SCAN_STEPS),
        in_specs=[fw, fw, fw, bw, bw, bw],
        out_specs=[fw, bw],
        out_shape=[jax.ShapeDtypeStruct((NTOK, D), BF16)] * 2,
        scratch_shapes=[state, state],
        compiler_params=_cparams(("arbitrary", "arbitrary"), VMEM_LIMIT),
        name="scan",
    )(q, f_fw, v, q, f_bw, v)


def _hgrn_out_kernel(of_ref, ob_ref, gs_ref, x_ref, ng_ref, wout_ref, mod_ref, o_ref):
    o = of_ref[...].astype(F32) + ob_ref[...].astype(F32)
    parts = []
    for h in range(HEADS):
        oh = o[:, h * HDIM:(h + 1) * HDIM]
        parts.append(oh * lax.rsqrt(jnp.mean(oh * oh, axis=-1, keepdims=True) + NORM_EPS))
    on = jnp.concatenate(parts, axis=1) * ng_ref[...]
    y = jnp.dot((on * gs_ref[...].astype(F32)).astype(BF16), wout_ref[...],
                preferred_element_type=F32)
    o_ref[...] = x_ref[...] + mod_ref[2:3, :] * y


def _hgrn_out(o_fw, o_bw, gs, x_all, mods, layer, norm_g, w_out, n_tiles):
    tile = pl.BlockSpec((TM, D), lambda i: (i, 0))
    return pl.pallas_call(
        _hgrn_out_kernel,
        grid=(n_tiles,),
        in_specs=[tile, tile, tile, tile,
                  pl.BlockSpec((1, D), lambda i: (0, 0)),
                  pl.BlockSpec((D, D), lambda i: (0, 0)),
                  pl.BlockSpec((None, 6, D), lambda i: (layer * 8 + _tile_cond(i), 0, 0))],
        out_specs=tile,
        out_shape=jax.ShapeDtypeStruct((n_tiles * TM, D), F32),
        compiler_params=_cparams(("arbitrary",), VMEM_LIMIT),
        name="hgrn_out",
    )(o_fw, o_bw, gs, x_all, norm_g.reshape(1, D), w_out.astype(BF16), mods)


def _to_row_tiled(ref, val):
    for j in range(RT):
        ref[pl.ds(j, val.shape[0], stride=RT), :] = val[:, j * LANES:(j + 1) * LANES]


def _from_row_tiled(ref, n, base=0):
    return [ref[pl.ds(base + j, n, stride=RT), :] for j in range(RT)]


def _route_kernel(x_ref, mod_ref, g_ref, wrt_ref, brt_ref, h_ref, e_ref, rank_ref, gate_ref, cnt_ref):
    @pl.when(pl.program_id(0) == 0)
    def _():
        cnt_ref[...] = jnp.zeros_like(cnt_ref)

    h = _rms(x_ref[...], g_ref[...]) * (1.0 + mod_ref[4:5, :]) + mod_ref[3:4, :]
    _to_row_tiled(h_ref, h)
    w = wrt_ref[...]
    w_hi = w.astype(BF16)
    w_lo = (w - w_hi.astype(F32)).astype(BF16)
    h_hi = h.astype(BF16)
    h_lo = (h - h_hi.astype(F32)).astype(BF16)
    part = lax.dot_general(jnp.concatenate([w_hi, w_lo], axis=0), h_hi, _NT,
                           preferred_element_type=F32)
    logits = (part[:N_EXPERTS] + part[N_EXPERTS:]
              + lax.dot_general(w_hi, h_lo, _NT, preferred_element_type=F32)
              + brt_ref[:, 0:1])
    row = lax.broadcasted_iota(jnp.int32, (N_EXPERTS, TM), 0).astype(F32)
    vals = logits
    sel = jnp.zeros((N_EXPERTS, TM), F32)
    tops, idxs = [], []
    for _ in range(TOP_K):
        m = jnp.max(vals, axis=0, keepdims=True)
        idx = jnp.min(jnp.where(vals == m, row, float(N_EXPERTS)), axis=0, keepdims=True)
        hit = row == idx
        vals = jnp.where(hit, -jnp.inf, vals)
        sel = jnp.where(hit, 1.0, sel)
        tops.append(m)
        idxs.append(idx)
    ex = [jnp.exp(m - tops[0]) for m in tops]
    den = ex[0] + ex[1] + ex[2] + ex[3]
    r = lax.broadcasted_iota(jnp.int32, (TM, TM), 0)
    c = lax.broadcasted_iota(jnp.int32, (TM, TM), 1)
    before = jnp.where(r < c, 1.0, 0.0).astype(BF16)
    pref = jnp.dot(sel.astype(BF16), before, preferred_element_type=F32) + cnt_ref[:, 0:1]
    slot = lax.broadcasted_iota(jnp.int32, (8, TM), 0)
    e_out = jnp.zeros((8, TM), F32)
    rank_out = jnp.zeros((8, TM), F32)
    gate_out = jnp.zeros((8, TM), F32)
    for kk in range(TOP_K):
        rank = jnp.sum(jnp.where(row == idxs[kk], pref, 0.0), axis=0, keepdims=True)
        e_out = jnp.where(slot == kk, idxs[kk], e_out)
        rank_out = jnp.where(slot == kk, rank, rank_out)
        gate_out = jnp.where(slot == kk, ex[kk] / den, gate_out)
    e_ref[...] = e_out.astype(jnp.int32)
    rank_ref[...] = rank_out.astype(jnp.int32)
    gate_ref[...] = gate_out
    cnt_ref[...] += jnp.sum(sel, axis=1, keepdims=True)


def _route(x_all, mods, layer, norm_g, w_r, b_r, n_tiles):
    tile = pl.BlockSpec((TM, D), lambda i: (i, 0))
    small = pl.BlockSpec((None, 8, TM), lambda i: (i, 0, 0))
    n = n_tiles * TM
    return pl.pallas_call(
        _route_kernel,
        grid=(n_tiles,),
        in_specs=[tile,
                  pl.BlockSpec((None, 6, D), lambda i: (layer * 8 + _tile_cond(i), 0, 0)),
                  pl.BlockSpec((1, D), lambda i: (0, 0)),
                  pl.BlockSpec((N_EXPERTS, D), lambda i: (0, 0)),
                  pl.BlockSpec((N_EXPERTS, LANES), lambda i: (0, 0))],
        out_specs=[pl.BlockSpec((TM * RT, LANES), lambda i: (i, 0)), small, small, small,
                   pl.BlockSpec((N_EXPERTS, LANES), lambda i: (0, 0))],
        out_shape=[jax.ShapeDtypeStruct((n * RT, LANES), F32),
                   jax.ShapeDtypeStruct((n_tiles, 8, TM), jnp.int32),
                   jax.ShapeDtypeStruct((n_tiles, 8, TM), jnp.int32),
                   jax.ShapeDtypeStruct((n_tiles, 8, TM), F32),
                   jax.ShapeDtypeStruct((N_EXPERTS, LANES), F32)],
        compiler_params=_cparams(("arbitrary",), VMEM_LIMIT),
        name="route",
    )(x_all, mods, norm_g.reshape(1, D), w_r.T, jnp.broadcast_to(b_r[:, None], (N_EXPERTS, LANES)))


def _dest_kernel(ps_ref, e_ref, rank_ref, d_ref):
    e = e_ref[...]
    acc = rank_ref[...]
    for j in range(N_EXPERTS):
        acc = acc + jnp.where(e == j, ps_ref[j], 0)
    d_ref[...] = acc


def _dest(pad_start, e, rank):
    full = pl.BlockSpec(e.shape, lambda i, ps: (0, 0, 0))
    return pl.pallas_call(
        _dest_kernel,
        grid_spec=pltpu.PrefetchScalarGridSpec(num_scalar_prefetch=1, grid=(1,),
                                               in_specs=[full, full], out_specs=full),
        out_shape=jax.ShapeDtypeStruct(e.shape, jnp.int32),
        name="dest",
    )(pad_start, e, rank)


PAD_PIECES = tuple(1 << p for p in reversed(range(BM.bit_length() - 1)))


def _zero_fill(lo_ref, hi_ref, xb_hbm, zeros, sem, n_blocks, wait):
    def go(rows, row0):
        cp = pltpu.make_async_copy(zeros.at[pl.ds(0, rows * RT), :],
                                   xb_hbm.at[pl.ds(row0 * RT, rows * RT), :], sem.at[1])
        cp.wait() if wait else cp.start()

    def per_expert(e, carry):
        row = lo_ref[e]
        n = hi_ref[e] - row
        for piece in PAD_PIECES:
            @pl.when((n & piece) != 0)
            def _():
                go(piece, row)
            row = row + (n & piece)
        return carry
    lax.fori_loop(0, N_EXPERTS, per_expert, 0)

    def per_block(b, carry):
        go(BM, b * BM)
        return carry
    lax.fori_loop(hi_ref[N_EXPERTS - 1] // BM, n_blocks, per_block, 0)


def _dispatch_kernel(lo_ref, hi_ref, dest_ref, h_ref, xb_hbm, zeros, sem, *, n_blocks):
    i = pl.program_id(0)

    @pl.when(i == 0)
    def _():
        zeros[...] = jnp.zeros_like(zeros)
        _zero_fill(lo_ref, hi_ref, xb_hbm, zeros, sem, n_blocks, wait=False)

    def body(t, carry):
        src = h_ref.at[pl.ds(t * RT, RT), :]
        for kk in range(TOP_K):
            d = dest_ref[0, 0, kk * TM + t]
            pltpu.make_async_copy(src, xb_hbm.at[pl.ds(d * RT, RT), :],
                                  sem.at[0]).start(priority=kk % 2)
        return carry
    lax.fori_loop(0, TM, body, 0, unroll=4)
    for kk in range(TOP_K):
        pltpu.make_async_copy(h_ref, xb_hbm.at[pl.ds(0, TM * RT), :], sem.at[0]).wait()

    @pl.when(i == pl.num_programs(0) - 1)
    def _():
        _zero_fill(lo_ref, hi_ref, xb_hbm, zeros, sem, n_blocks, wait=True)


def _dispatch(h_rt, dest, pad_lo, pad_hi, n_blocks):
    n_tiles = dest.shape[0]
    dest3 = dest[:, :TOP_K, :].reshape(n_tiles, 1, TOP_K * TM)
    grid_spec = pltpu.PrefetchScalarGridSpec(
        num_scalar_prefetch=2,
        grid=(n_tiles,),
        in_specs=[pl.BlockSpec((1, 1, TOP_K * TM), lambda i, lo, hi: (i, 0, 0),
                               memory_space=pltpu.SMEM),
                  pl.BlockSpec((TM * RT, LANES), lambda i, lo, hi: (i, 0))],
        out_specs=pl.BlockSpec(memory_space=pl.ANY),
        scratch_shapes=[pltpu.VMEM((BM * RT, LANES), F32),
                        pltpu.SemaphoreType.DMA((2,))],
    )
    xb = pl.pallas_call(
        functools.partial(_dispatch_kernel, n_blocks=n_blocks),
        grid_spec=grid_spec,
        out_shape=jax.ShapeDtypeStruct((n_blocks * BM * RT, LANES), F32),
        compiler_params=_cparams(("arbitrary",), VMEM_LIMIT, disable_bounds_checks=True),
        name="dispatch",
    )(pad_lo, pad_hi, dest3, h_rt)
    return xb, dest3


ROW_STEPS = tuple(range(BM // 4, BM + 1, BM // 4))


def _expert_kernel(bexp_ref, nused_ref, first_ref, wslot_ref, next_ref, valid_ref, xb_ref, w1_hbm, b1_ref,
                   w2_hbm, b2_ref, y_ref, w1s, w2s, w1b, w2b, sem, *, layer):
    i = pl.program_id(0)
    del nused_ref

    def weight_copies(e, slot):
        return (pltpu.make_async_copy(w1_hbm.at[layer, e], w1s.at[slot], sem.at[0, slot]),
                pltpu.make_async_copy(w2_hbm.at[layer, e], w2s.at[slot], sem.at[1, slot]))

    @pl.when(i == 0)
    def _():
        for cp in weight_copies(bexp_ref[0], 0):
            cp.start()

    @pl.when(first_ref[i] == 1)
    def _():
        slot = wslot_ref[i]
        for cp in weight_copies(bexp_ref[i], slot):
            cp.wait()
        w1b[...] = w1s[slot].astype(BF16)
        w2b[...] = w2s[slot].astype(BF16)

        @pl.when(next_ref[i] >= 0)
        def _():
            for cp in weight_copies(next_ref[i], 1 - slot):
                cp.start()

    valid = valid_ref[i]
    for lo, rows in zip((0,) + ROW_STEPS[:-1], ROW_STEPS):
        @pl.when(jnp.logical_and(valid > lo, valid <= rows))
        def _():
            x = jnp.concatenate(_from_row_tiled(xb_ref, rows), axis=1).astype(BF16)
            u = jnp.dot(x, w1b[...], preferred_element_type=F32) + b1_ref[...]
            glu = jnp.minimum(u[:, :D_FF], SWIGLU_LIMIT)
            lin = jnp.clip(u[:, D_FF:], -SWIGLU_LIMIT, SWIGLU_LIMIT)
            act = glu * _sigmoid(SWIGLU_ALPHA * glu) * (lin + 1.0)
            y = jnp.dot(act.astype(BF16), w2b[...], preferred_element_type=F32) + b2_ref[...]
            _to_row_tiled(y_ref.at[pl.ds(0, rows * RT), :], y)
            if rows < BM:
                y_ref[pl.ds(rows * RT, (BM - rows) * RT), :] = jnp.zeros(((BM - rows) * RT, LANES), F32)

    @pl.when(valid == 0)
    def _():
        y_ref[...] = jnp.zeros_like(y_ref)


def _experts(xb, block_exp, n_used, counts, pad_lo, layer, w1, b1, w2, b2):
    n_blocks = xb.shape[0] // (BM * RT)
    used = jnp.arange(n_blocks, dtype=jnp.int32) < n_used[0]
    changed = jnp.concatenate([jnp.ones((1,), bool), block_exp[1:] != block_exp[:-1]])
    first = jnp.logical_and(used, changed).astype(jnp.int32)
    wslot = (jnp.cumsum(first) - 1) % 2
    ar = jnp.arange(N_EXPERTS, dtype=jnp.int32)
    later = jnp.min(jnp.where((ar[None, :] > ar[:, None]) & (counts > 0)[None, :], ar[None, :],
                              N_EXPERTS), axis=1)
    of_block = block_exp[:, None] == ar[None, :]
    next_exp = jnp.sum(jnp.where(of_block, jnp.where(later == N_EXPERTS, -1, later)[None, :], 0), axis=1)
    starts = jnp.arange(n_blocks, dtype=jnp.int32) * BM
    valid = jnp.clip(jnp.sum(jnp.where(of_block, pad_lo[None, :], 0), axis=1) - starts, 0, BM)
    smap = lambda f: (lambda i, be, nu, fi, ws, nx, va: f(i, be, nu))
    grid_spec = pltpu.PrefetchScalarGridSpec(
        num_scalar_prefetch=6,
        grid=(n_blocks,),
        in_specs=[
            pl.BlockSpec((BM * RT, LANES), smap(lambda i, be, nu: (jnp.minimum(i, nu[0] - 1), 0))),
            pl.BlockSpec(memory_space=pl.ANY),
            pl.BlockSpec((None, 1, 2 * D_FF), smap(lambda i, be, nu: (layer * N_EXPERTS + be[i], 0, 0))),
            pl.BlockSpec(memory_space=pl.ANY),
            pl.BlockSpec((None, 1, D), smap(lambda i, be, nu: (layer * N_EXPERTS + be[i], 0, 0))),
        ],
        out_specs=pl.BlockSpec((BM * RT, LANES), smap(lambda i, be, nu: (i, 0))),
        scratch_shapes=[pltpu.VMEM((2, D, 2 * D_FF), F32),
                        pltpu.VMEM((2, D_FF, D), F32),
                        pltpu.VMEM((D, 2 * D_FF), BF16),
                        pltpu.VMEM((D_FF, D), BF16),
                        pltpu.SemaphoreType.DMA((2, 2))],
    )
    return pl.pallas_call(
        functools.partial(_expert_kernel, layer=layer),
        grid_spec=grid_spec,
        out_shape=jax.ShapeDtypeStruct(xb.shape, F32),
        compiler_params=_cparams(("arbitrary",), VMEM_LIMIT),
        name="experts",
    )(block_exp, n_used, first, wslot.astype(jnp.int32), next_exp.astype(jnp.int32),
      valid.astype(jnp.int32), xb, w1,
      b1.reshape(-1, 1, 2 * D_FF), w2, b2.reshape(-1, 1, D))


def _combine_kernel(dest_ref, dest_next_ref, y_hbm, gate_ref, x_ref, mod_ref, fg_ref, o_ref, buf, sem,
                    *, final_norm):
    i = pl.program_id(0)
    n = pl.num_programs(0)
    slot = i % 2
    rows = TOP_K * TM

    def gather(dest, to_slot):
        def body(p, carry):
            for half in range(2):
                r = 2 * p + half
                pltpu.make_async_copy(y_hbm.at[pl.ds(dest[0, 0, r] * RT, RT), :],
                                      buf.at[to_slot, pl.ds(r * RT, RT), :],
                                      sem.at[to_slot]).start(priority=half)
            return carry
        lax.fori_loop(0, rows // 2, body, 0, unroll=4)

    @pl.when(i == 0)
    def _():
        gather(dest_ref, 0)

    @pl.when(i + 1 < n)
    def _():
        gather(dest_next_ref, 1 - slot)

    pltpu.make_async_copy(y_hbm.at[pl.ds(0, rows * RT), :], buf.at[slot], sem.at[slot]).wait()
    eye = (lax.broadcasted_iota(jnp.int32, (8, 8), 0)
           == lax.broadcasted_iota(jnp.int32, (8, 8), 1)).astype(F32)
    gate = lax.dot_general(gate_ref[...], eye, _TN, precision=HIGHEST,
                           preferred_element_type=F32)
    cur = buf.at[slot]
    outs = []
    for j in range(RT):
        acc = None
        for kk in range(TOP_K):
            piece = gate[:, kk:kk + 1] * cur[pl.ds(kk * TM * RT + j, TM, stride=RT), :]
            acc = piece if acc is None else acc + piece
        outs.append(acc)
    out = x_ref[...] + mod_ref[5:6, :] * jnp.concatenate(outs, axis=1)
    if final_norm:
        out = _rms(out, fg_ref[...])
    o_ref[...] = out


def _combine(yb, dest3, gate, x_all, mods, layer, final_g, n_tiles, final_norm):
    last = n_tiles - 1
    tile = pl.BlockSpec((TM, D), lambda i: (i, 0))
    return pl.pallas_call(
        functools.partial(_combine_kernel, final_norm=final_norm),
        grid=(n_tiles,),
        in_specs=[pl.BlockSpec((1, 1, TOP_K * TM), lambda i: (i, 0, 0), memory_space=pltpu.SMEM),
                  pl.BlockSpec((1, 1, TOP_K * TM), lambda i: (jnp.minimum(i + 1, last), 0, 0),
                               memory_space=pltpu.SMEM),
                  pl.BlockSpec(memory_space=pl.ANY),
                  pl.BlockSpec((None, 8, TM), lambda i: (i, 0, 0)),
                  tile,
                  pl.BlockSpec((None, 6, D), lambda i: (layer * 8 + _tile_cond(i), 0, 0)),
                  pl.BlockSpec((1, D), lambda i: (0, 0))],
        out_specs=tile,
        out_shape=jax.ShapeDtypeStruct((n_tiles * TM, D), F32),
        scratch_shapes=[pltpu.VMEM((2, TOP_K * TM * RT, LANES), F32),
                        pltpu.SemaphoreType.DMA((2,))],
        compiler_params=_cparams(("arbitrary",), VMEM_LIMIT, disable_bounds_checks=True),
        name="combine",
    )(dest3, dest3, yb, gate, x_all, mods, final_g.reshape(1, D))


def _moe(x_all, mods, layer, norm_g, w_r, b_r, w1, b1, w2, b2, final_g, n_tiles, final_norm):
    n = n_tiles * TM
    h_rt, e, rank, gate, cnt = _route(x_all, mods, layer, norm_g, w_r, b_r, n_tiles)
    counts = cnt[:, 0].astype(jnp.int32)
    padded = (counts + BM - 1) // BM * BM
    pad_end = jnp.cumsum(padded)
    pad_start = pad_end - padded
    n_blocks = -(-(n * TOP_K) // BM) + N_EXPERTS
    starts = jnp.arange(n_blocks, dtype=jnp.int32) * BM
    block_exp = jnp.minimum(jnp.sum((pad_end[None, :] <= starts[:, None]).astype(jnp.int32), axis=1),
                            N_EXPERTS - 1)
    n_used = pad_end[-1:] // BM
    dest = _dest(pad_start, e, rank)
    xb, dest3 = _dispatch(h_rt, dest, pad_start + counts, pad_end, n_blocks)
    yb = _experts(xb, block_exp, n_used, counts, pad_start + counts, layer, w1, b1, w2, b2)
    return _combine(yb, dest3, gate, x_all, mods, layer, final_g, n_tiles, final_norm)


def kernel(x, c, ctx, c_ctx, mod_w, mod_b, norm1_g, norm2_g, fourier_w_in, fourier_w_out,
           hgrn_w_in, hgrn_lower_bounds, hgrn_norm_g, hgrn_w_out, router_w, router_b,
           expert_w1, expert_b1, expert_w2, expert_b2, final_norm_g):
    assert x.shape == (BATCH, SEQ, D) and ctx.shape == (BATCH, CTX_LEN, D)
    cond8 = jnp.zeros((8, D), F32).at[:BATCH].set(c).at[BATCH].set(c_ctx)
    mods = _adaln(cond8, mod_w, mod_b)
    x_all = jnp.concatenate([x.reshape(NLAT, D), ctx.reshape(NCTX, D)], axis=0)
    dch, mpos, mrow = _fourier_constants()
    experts = (expert_w1, expert_b1, expert_w2, expert_b2)

    vr, vi = _fourier_in(x_all, mods, 0, norm1_g[0], fourier_w_in[0], dch, mpos)
    x_all = _fourier_out(vr, vi, x_all, mods, 0, fourier_w_out[0], mrow)
    x_all = _moe(x_all, mods, 0, norm2_g[0], router_w[0], router_b[0], *experts,
                 final_norm_g, ALL_TILES, False)

    q, ff, fb, v, gs = _hgrn_in(x_all, mods, 1, norm1_g[1], hgrn_w_in[0], hgrn_lower_bounds)
    o_fw, o_bw = _scan(q, ff, fb, v)
    x_lat = _hgrn_out(o_fw, o_bw, gs, x_all, mods, 1, hgrn_norm_g[0], hgrn_w_out[0], LAT_TILES)
    out = _moe(x_lat, mods, 1, norm2_g[1], router_w[1], router_b[1], *experts,
               final_norm_g, LAT_TILES, True)
    return out.reshape(BATCH, SEQ, D)
```

```python
import functools

import numpy as np
import jax
import jax.numpy as jnp
from jax import lax
from jax.experimental import pallas as pl
from jax.experimental.pallas import tpu as pltpu

F32 = jnp.float32
BF16 = jnp.bfloat16
HIGHEST = lax.Precision.HIGHEST

D = 1024
BATCH = 2
SEQ = 8192
CTX_LEN = 256
GRID_W = 64
GRID_H = SEQ // GRID_W
NLAT = BATCH * SEQ
NCTX = BATCH * CTX_LEN
NTOK = NLAT + NCTX
TM = 256
LAT_TILES = NLAT // TM
ALL_TILES = NTOK // TM
TILES_PER_BATCH = SEQ // TM
FGROUPS = 4
FGDIM = D // FGROUPS
HEADS = 8
HDIM = D // HEADS
CHUNK = 128
N_EXPERTS = 32
TOP_K = 4
D_FF = 1024
SWIGLU_ALPHA = 1.702
SWIGLU_LIMIT = 7.0
BM = 512
LANES = 128
RT = D // LANES
NORM_EPS = 1e-6
VMEM_LIMIT = 56 * 1024 * 1024

_NT = (((1,), (1,)), ((), ()))
_TN = (((0,), (0,)), ((), ()))


def _cparams(sem, vmem=None, **kw):
    return pltpu.CompilerParams(dimension_semantics=sem, vmem_limit_bytes=vmem, **kw)


def _sigmoid(x):
    return 1.0 / (1.0 + jnp.exp(-x))


def _rms(x, g):
    return x * lax.rsqrt(jnp.mean(x * x, axis=-1, keepdims=True) + NORM_EPS) * g


def _tile_cond(i):
    return jnp.where(i < LAT_TILES, i // TILES_PER_BATCH, 2)


def _adaln_kernel(cond_ref, w_ref, b_ref, o_ref):
    c = cond_ref[...]
    s = c * _sigmoid(c)
    o_ref[...] = jnp.dot(s, w_ref[...], precision=HIGHEST,
                         preferred_element_type=F32) + b_ref[...]


def _adaln(cond8, mod_w, mod_b):
    depth = mod_w.shape[0]
    nb = 1536
    out = pl.pallas_call(
        _adaln_kernel,
        grid=(depth, 6 * D // nb),
        in_specs=[pl.BlockSpec((8, D), lambda l, j: (0, 0)),
                  pl.BlockSpec((None, D, nb), lambda l, j: (l, 0, j)),
                  pl.BlockSpec((None, 1, nb), lambda l, j: (l, 0, j))],
        out_specs=pl.BlockSpec((None, 8, nb), lambda l, j: (l, 0, j)),
        out_shape=jax.ShapeDtypeStruct((depth, 8, 6 * D), F32),
        compiler_params=_cparams(("arbitrary", "arbitrary"), VMEM_LIMIT),
        name="adaln",
    )(cond8, mod_w, mod_b.reshape(depth, 1, 6 * D))
    return out.reshape(depth * 8, 6, D)


def _dft_cs(n):
    k = np.arange(n)
    ang = 2.0 * np.pi * np.outer(k, k) / n
    s = 1.0 / np.sqrt(n)
    return np.cos(ang) * s, np.sin(ang) * s


def _fourier_constants():
    cd, sd = _dft_cs(FGDIM)
    dch = np.concatenate([cd, sd], axis=1)
    cc, sc = _dft_cs(GRID_W)
    eye = np.eye(TM // GRID_W)
    kc, ks = np.kron(eye, cc), np.kron(eye, sc)
    m_lat = np.block([[kc, -ks], [ks, kc]])
    cp, sp = _dft_cs(CTX_LEN)
    m_ctx = np.block([[cp, -sp], [sp, cp]])
    mpos = np.stack([m_lat, m_ctx])
    cr, sr = _dft_cs(GRID_H)
    mrow = np.concatenate([cr, -sr], axis=1)
    return (jnp.asarray(dch, BF16), jnp.asarray(mpos, BF16), jnp.asarray(mrow, BF16))


def _fourier_in_kernel(x_ref, mod_ref, g_ref, win_ref, dch_ref, mpos_ref, vr_ref, vi_ref):
    x = x_ref[...]
    h = _rms(x, g_ref[...]) * (1.0 + mod_ref[1:2, :]) + mod_ref[0:1, :]
    u = jnp.dot(h.astype(BF16), win_ref[...], preferred_element_type=F32).astype(BF16)
    parts = [jnp.dot(u[:, g * FGDIM:(g + 1) * FGDIM], dch_ref[...],
                     preferred_element_type=F32) for g in range(FGROUPS)]
    uc = jnp.concatenate([p[:, :FGDIM] for p in parts], axis=1)
    us = jnp.concatenate([p[:, FGDIM:] for p in parts], axis=1)
    st = jnp.concatenate([uc, us], axis=0).astype(BF16)
    v = jnp.dot(mpos_ref[...], st, preferred_element_type=F32)
    vr_ref[...] = v[:TM]
    vi_ref[...] = v[TM:]


def _fourier_in(x_all, mods, layer, norm_g, w_in, dch, mpos):
    tile = pl.BlockSpec((TM, D), lambda i: (i, 0))
    return pl.pallas_call(
        _fourier_in_kernel,
        grid=(ALL_TILES,),
        in_specs=[tile,
                  pl.BlockSpec((None, 6, D), lambda i: (layer * 8 + _tile_cond(i), 0, 0)),
                  pl.BlockSpec((1, D), lambda i: (0, 0)),
                  pl.BlockSpec((D, D), lambda i: (0, 0)),
                  pl.BlockSpec((FGDIM, 2 * FGDIM), lambda i: (0, 0)),
                  pl.BlockSpec((None, 2 * TM, 2 * TM), lambda i: (jnp.where(i < LAT_TILES, 0, 1), 0, 0))],
        out_specs=[tile, tile],
        out_shape=[jax.ShapeDtypeStruct((NTOK, D), F32)] * 2,
        compiler_params=_cparams(("arbitrary",), VMEM_LIMIT),
        name="fourier_in",
    )(x_all, mods, norm_g.reshape(1, D), w_in.astype(BF16), dch, mpos)


CB = 8


def _fourier_out_lat_kernel(vr_ref, vi_ref, x_ref, mrow_ref, wout_ref, mod_ref, o_ref):
    g1 = mod_ref[2:3, :]
    yf = []
    for c in range(CB):
        st = jnp.concatenate([vr_ref[:, c, :], vi_ref[:, c, :]], axis=0).astype(BF16)
        yf.append(jnp.dot(mrow_ref[...], st, preferred_element_type=F32).astype(BF16))
    y = jnp.dot(jnp.concatenate(yf, axis=0), wout_ref[...], preferred_element_type=F32)
    for c in range(CB):
        o_ref[:, c, :] = x_ref[:, c, :] + g1 * y[c * GRID_H:(c + 1) * GRID_H, :]


def _fourier_out_ctx_kernel(yr_ref, x_ref, wout_ref, mod_ref, o_ref):
    y = jnp.dot(yr_ref[...].astype(BF16), wout_ref[...], preferred_element_type=F32)
    o_ref[...] = x_ref[...] + mod_ref[2:3, :] * y


def _fourier_out(vr, vi, x_all, mods, layer, w_out, mrow):
    wout = w_out.astype(BF16)
    rows = NTOK // GRID_W
    v3 = lambda a: a.reshape(rows, GRID_W, D)
    blk = pl.BlockSpec((GRID_H, CB, D), lambda b, c: (b, c, 0))
    x_new = pl.pallas_call(
        _fourier_out_lat_kernel,
        grid=(BATCH, GRID_W // CB),
        in_specs=[blk, blk, blk,
                  pl.BlockSpec((GRID_H, 2 * GRID_H), lambda b, c: (0, 0)),
                  pl.BlockSpec((D, D), lambda b, c: (0, 0)),
                  pl.BlockSpec((None, 6, D), lambda b, c: (layer * 8 + b, 0, 0))],
        out_specs=blk,
        out_shape=jax.ShapeDtypeStruct((rows, GRID_W, D), F32),
        input_output_aliases={2: 0},
        compiler_params=_cparams(("arbitrary", "arbitrary"), VMEM_LIMIT),
        name="fourier_out_lat",
    )(v3(vr), v3(vi), v3(x_all), mrow, wout, mods).reshape(NTOK, D)
    ctile = pl.BlockSpec((TM, D), lambda i: (LAT_TILES + i, 0))
    return pl.pallas_call(
        _fourier_out_ctx_kernel,
        grid=(NCTX // TM,),
        in_specs=[ctile, ctile,
                  pl.BlockSpec((D, D), lambda i: (0, 0)),
                  pl.BlockSpec((None, 6, D), lambda i: (layer * 8 + 2, 0, 0))],
        out_specs=ctile,
        out_shape=jax.ShapeDtypeStruct((NTOK, D), F32),
        input_output_aliases={1: 0},
        compiler_params=_cparams(("arbitrary",), VMEM_LIMIT),
        name="fourier_out_ctx",
    )(vr, x_new, wout, mods)


def _hgrn_in_kernel(x_ref, mod_ref, g_ref, win_ref, hlb_ref, q_ref, ff_ref, fb_ref, v_ref, gs_ref,
                    *, layer):
    x = x_ref[...]
    h = (_rms(x, g_ref[...]) * (1.0 + mod_ref[1:2, :]) + mod_ref[0:1, :]).astype(BF16)
    raw = [hlb_ref[l] for l in range(hlb_ref.shape[0])]
    mx = functools.reduce(jnp.maximum, raw)
    ex = [jnp.exp(r - mx) for r in raw]
    den = functools.reduce(lambda a, b: a + b, ex)
    soft = [e / den for e in ex]
    lb = functools.reduce(lambda a, b: a + b, soft[:layer + 1]) - soft[0]

    def proj(j):
        return jnp.dot(h, win_ref[:, j * D:(j + 1) * D], preferred_element_type=F32)

    def per_head(ref, val):
        for hd in range(HEADS):
            ref[hd] = val[:, hd * HDIM:(hd + 1) * HDIM]

    q = proj(0)
    per_head(q_ref, q * _sigmoid(q))
    per_head(ff_ref, lb[0:1, :] + (1.0 - lb[0:1, :]) * _sigmoid(proj(1)))
    per_head(fb_ref, lb[1:2, :] + (1.0 - lb[1:2, :]) * _sigmoid(proj(2)))
    per_head(v_ref, proj(3))
    g = proj(4)
    gs_ref[...] = (g * _sigmoid(g)).astype(BF16)


def _hgrn_in(x_all, mods, layer, norm_g, w_in, hlb):
    tile = pl.BlockSpec((TM, D), lambda i: (i, 0))
    depth = hlb.shape[0]
    return pl.pallas_call(
        functools.partial(_hgrn_in_kernel, layer=layer),
        grid=(ALL_TILES,),
        in_specs=[tile,
                  pl.BlockSpec((None, 6, D), lambda i: (layer * 8 + _tile_cond(i), 0, 0)),
                  pl.BlockSpec((1, D), lambda i: (0, 0)),
                  pl.BlockSpec((D, 5 * D), lambda i: (0, 0)),
                  pl.BlockSpec((depth, 2, D), lambda i: (0, 0, 0))],
        out_specs=[pl.BlockSpec((HEADS, TM, HDIM), lambda i: (0, i, 0))] * 4 + [tile],
        out_shape=[jax.ShapeDtypeStruct((HEADS, NTOK, HDIM), F32)] * 4
                  + [jax.ShapeDtypeStruct((NTOK, D), BF16)],
        compiler_params=_cparams(("arbitrary",), VMEM_LIMIT),
        name="hgrn_in",
    )(x_all, mods, norm_g.reshape(1, D), w_in.astype(BF16), hlb)


N_LEVELS = 7
SUB = 8
NGRP = CHUNK // SUB
FINE_LEVELS = 3


def _scan_pair_kernel(qf_ref, ff_ref, vf_ref, qb_ref, fb_ref, vb_ref, of_ref, ob_ref, sf_ref, sb_ref,
                      rf_ref, rb_ref):
    @pl.when(pl.program_id(1) == 0)
    def _():
        sf_ref[...] = jnp.zeros_like(sf_ref)
        sb_ref[...] = jnp.zeros_like(sb_ref)

    _scan_chunk(qf_ref, ff_ref, vf_ref, of_ref, sf_ref, rf_ref, rev=False)
    _scan_chunk(qb_ref, fb_ref, vb_ref, ob_ref, sb_ref, rb_ref, rev=True)


PG = CHUNK // SUB


def _scan_chunk(q_ref, fg_ref, v_ref, o_ref, s_ref, relay_ref, *, rev):
    heads = [slice(h * HDIM, (h + 1) * HDIM) for h in range(HEADS)]

    def score(qs, ks, h):
        return lax.dot_general(qs[:, heads[h]], ks[:, heads[h]], _NT, preferred_element_type=F32)

    def by_residue(ref):
        return jnp.concatenate(
            [jnp.concatenate([ref[h, pl.ds(r, PG, stride=SUB), :] for r in range(SUB)], axis=0)
             for h in range(HEADS)], axis=1)

    def residue_groups(a):
        return [a[r * PG:(r + 1) * PG, :] for r in range(SUB)]

    q_p = by_residue(q_ref)
    fg_p = by_residue(fg_ref)
    k_p = 1.0 - fg_p
    v_p = by_residue(v_ref).astype(BF16)
    key = lax.broadcasted_iota(jnp.int32, (PG, CHUNK), 1)
    same_a = (key & (PG - 1)) == lax.broadcasted_iota(jnp.int32, (PG, CHUNK), 0)
    key_r = key >> (PG.bit_length() - 1)
    zero_p = jnp.zeros((PG, D), F32)
    sc_p = [[None] * SUB for _ in range(HEADS)]
    qd, kd = q_p.astype(BF16), k_p.astype(BF16)
    for h in range(HEADS):
        sc = score(qd, kd, h)
        for r in range(SUB):
            sc_p[h][r] = jnp.where(same_a & (key_r == r), sc[r * PG:(r + 1) * PG, :], 0.0)
    qr_p = residue_groups(fg_p * q_p)
    kr_p = residue_groups(k_p)
    tot_p = residue_groups(fg_p)
    for l in range(FINE_LEVELS):
        bit = 1 << l
        is_far = [((r & bit) == 0) == rev for r in range(SUB)]
        far_groups = [r for r in range(SUB) if is_far[r]]
        qb = jnp.concatenate([qr_p[r] for r in far_groups], axis=0).astype(BF16)
        kb = jnp.concatenate([zero_p if is_far[r] else kr_p[r] for r in range(SUB)],
                             axis=0).astype(BF16)
        keep = [same_a & ((key_r >> (l + 1)) == (r >> (l + 1))) for r in far_groups]
        for h in range(HEADS):
            sc = score(qb, kb, h)
            for i, r in enumerate(far_groups):
                sc_p[h][r] = sc_p[h][r] + jnp.where(keep[i], sc[i * PG:(i + 1) * PG, :], 0.0)
        sib_p = [tot_p[r ^ bit] for r in range(SUB)]
        qr_p = [qr_p[r] * sib_p[r] if is_far[r] else qr_p[r] for r in range(SUB)]
        kr_p = [kr_p[r] if is_far[r] else kr_p[r] * sib_p[r] for r in range(SUB)]
        tot_p = [tot_p[r] * sib_p[r] for r in range(SUB)]
    for x, grp in enumerate((qr_p, kr_p, tot_p)):
        for h in range(HEADS):
            for r in range(SUB):
                relay_ref[x, h, pl.ds(r, PG, stride=SUB), :] = grp[r][:, heads[h]]
    qr, kr, tot = (jnp.concatenate([relay_ref[x, h] for h in range(HEADS)], axis=1)
                   for x in range(3))

    def groups(a):
        return [a[b * SUB:(b + 1) * SUB, :] for b in range(NGRP)]

    qr_g, kr_g, tot_g = (groups(a) for a in (qr, kr, tot))
    sc_g = [[None] * NGRP for _ in range(HEADS)]
    lane = lax.broadcasted_iota(jnp.int32, (SUB, CHUNK), 1)
    zero_g = jnp.zeros((SUB, D), F32)
    for l in range(FINE_LEVELS, N_LEVELS):
        bit = 1 << (l - FINE_LEVELS)
        is_far = [((b & bit) == 0) == rev for b in range(NGRP)]
        far_groups = [b for b in range(NGRP) if is_far[b]]
        qb = jnp.concatenate([qr_g[b] for b in far_groups], axis=0).astype(BF16)
        kb = jnp.concatenate([zero_g if is_far[b] else kr_g[b] for b in range(NGRP)],
                             axis=0).astype(BF16)
        span = 2 << l
        keep = [None if span == CHUNK else
                (lane >= b * SUB // span * span) & (lane < b * SUB // span * span + span)
                for b in far_groups]
        for h in range(HEADS):
            sc = score(qb, kb, h)
            for i, b in enumerate(far_groups):
                piece = sc[i * SUB:(i + 1) * SUB, :]
                if keep[i] is not None:
                    piece = jnp.where(keep[i], piece, 0.0)
                sc_g[h][b] = piece if sc_g[h][b] is None else sc_g[h][b] + piece
        sib_g = [tot_g[b ^ bit] for b in range(NGRP)]
        qr_g = [qr_g[b] * sib_g[b] if is_far[b] else qr_g[b] for b in range(NGRP)]
        kr_g = [kr_g[b] if is_far[b] else kr_g[b] * sib_g[b] for b in range(NGRP)]
        if l < N_LEVELS - 1:
            tot_g = [tot_g[b] * sib_g[b] for b in range(NGRP)]
        else:
            tot_row = tot_g[0][0:1, :] * sib_g[0][0:1, :]
    qin = jnp.concatenate(qr_g, axis=0).astype(BF16)
    kst = jnp.concatenate(kr_g, axis=0).astype(BF16)
    vb = jnp.concatenate([v_ref[h] for h in range(HEADS)], axis=1).astype(BF16)
    zero_s = jnp.zeros((SUB, CHUNK), F32)
    for h in range(HEADS):
        sl = heads[h]
        st = s_ref[h]
        sc = jnp.concatenate([zero_s if g is None else g for g in sc_g[h]], axis=0).astype(BF16)
        o_rows = (jnp.dot(sc, vb[:, sl], preferred_element_type=F32)
                  + lax.dot_general(qin[:, sl], st.astype(BF16), _NT, preferred_element_type=F32))
        o_res = jnp.dot(jnp.concatenate(sc_p[h], axis=0).astype(BF16), v_p[:, sl],
                        preferred_element_type=F32)
        for r in range(SUB):
            relay_ref[3, h, pl.ds(r, PG, stride=SUB), :] = o_res[r * PG:(r + 1) * PG, :]
        o_ref[:, sl] = (o_rows + relay_ref[3, h]).astype(BF16)
        s_ref[h] = st * tot_row[:, sl] + lax.dot_general(vb[:, sl], kst[:, sl], _TN,
                                                         preferred_element_type=F32)


LAT_CHUNKS = SEQ // CHUNK
CTX_CHUNKS = CTX_LEN // CHUNK
SCAN_STEPS = CTX_CHUNKS + LAT_CHUNKS


def _scan(q, f_fw, f_bw, v):
    def idx_fw(b, s):
        return (jnp.where(s < CTX_CHUNKS, NLAT // CHUNK + CTX_CHUNKS * b + s,
                          LAT_CHUNKS * b + (s - CTX_CHUNKS)), 0)

    def idx_bw(b, s):
        return (jnp.where(s < CTX_CHUNKS, NLAT // CHUNK + CTX_CHUNKS * b + (CTX_CHUNKS - 1 - s),
                          LAT_CHUNKS * b + (SCAN_STEPS - 1 - s)), 0)
    fw = pl.BlockSpec((CHUNK, D), idx_fw)
    bw = pl.BlockSpec((CHUNK, D), idx_bw)
    fw_in = pl.BlockSpec((HEADS, CHUNK, HDIM), lambda b, s: (0, idx_fw(b, s)[0], 0))
    bw_in = pl.BlockSpec((HEADS, CHUNK, HDIM), lambda b, s: (0, idx_bw(b, s)[0], 0))
    state = pltpu.VMEM((HEADS, HDIM, HDIM), F32)
    relay = pltpu.VMEM((4, HEADS, CHUNK, HDIM), F32)
    return pl.pallas_call(
        _scan_pair_kernel,
        grid=(BATCH, SCAN_STEPS),
        in_specs=[fw_in, fw_in, fw_in, bw_in, bw_in, bw_in],
        out_specs=[fw, bw],
        out_shape=[jax.ShapeDtypeStruct((NTOK, D), BF16)] * 2,
        scratch_shapes=[state, state, relay, relay],
        compiler_params=_cparams(("arbitrary", "arbitrary"), VMEM_LIMIT),
        name="scan",
    )(q, f_fw, v, q, f_bw, v)


def _hgrn_out_kernel(of_ref, ob_ref, gs_ref, x_ref, ng_ref, wout_ref, mod_ref, o_ref):
    o = of_ref[...].astype(F32) + ob_ref[...].astype(F32)
    parts = []
    for h in range(HEADS):
        oh = o[:, h * HDIM:(h + 1) * HDIM]
        parts.append(oh * lax.rsqrt(jnp.mean(oh * oh, axis=-1, keepdims=True) + NORM_EPS))
    on = jnp.concatenate(parts, axis=1) * ng_ref[...]
    y = jnp.dot((on * gs_ref[...].astype(F32)).astype(BF16), wout_ref[...],
                preferred_element_type=F32)
    o_ref[...] = x_ref[...] + mod_ref[2:3, :] * y


def _hgrn_out(o_fw, o_bw, gs, x_all, mods, layer, norm_g, w_out, n_tiles):
    tile = pl.BlockSpec((TM, D), lambda i: (i, 0))
    return pl.pallas_call(
        _hgrn_out_kernel,
        grid=(n_tiles,),
        in_specs=[tile, tile, tile, tile,
                  pl.BlockSpec((1, D), lambda i: (0, 0)),
                  pl.BlockSpec((D, D), lambda i: (0, 0)),
                  pl.BlockSpec((None, 6, D), lambda i: (layer * 8 + _tile_cond(i), 0, 0))],
        out_specs=tile,
        out_shape=jax.ShapeDtypeStruct((n_tiles * TM, D), F32),
        compiler_params=_cparams(("arbitrary",), VMEM_LIMIT),
        name="hgrn_out",
    )(o_fw, o_bw, gs, x_all, norm_g.reshape(1, D), w_out.astype(BF16), mods)


def _to_row_tiled(ref, val):
    for j in range(RT):
        ref[pl.ds(j, val.shape[0], stride=RT), :] = val[:, j * LANES:(j + 1) * LANES]


def _from_row_tiled(ref, n, base=0):
    return [ref[pl.ds(base + j, n, stride=RT), :] for j in range(RT)]


def _route_kernel(x_ref, mod_ref, g_ref, wrt_ref, brt_ref, h_ref, e_ref, rank_ref, gate_ref, cnt_ref):
    @pl.when(pl.program_id(0) == 0)
    def _():
        cnt_ref[...] = jnp.zeros_like(cnt_ref)

    h = _rms(x_ref[...], g_ref[...]) * (1.0 + mod_ref[4:5, :]) + mod_ref[3:4, :]
    _to_row_tiled(h_ref, h)
    w = wrt_ref[...]
    w_hi = w.astype(BF16)
    w_lo = (w - w_hi.astype(F32)).astype(BF16)
    h_hi = h.astype(BF16)
    h_lo = (h - h_hi.astype(F32)).astype(BF16)
    part = lax.dot_general(jnp.concatenate([w_hi, w_lo], axis=0), h_hi, _NT,
                           preferred_element_type=F32)
    logits = (part[:N_EXPERTS] + part[N_EXPERTS:]
              + lax.dot_general(w_hi, h_lo, _NT, preferred_element_type=F32)
              + brt_ref[:, 0:1])
    row = lax.broadcasted_iota(jnp.int32, (N_EXPERTS, TM), 0).astype(F32)
    vals = logits
    sel = jnp.zeros((N_EXPERTS, TM), F32)
    tops, idxs = [], []
    for _ in range(TOP_K):
        m = jnp.max(vals, axis=0, keepdims=True)
        idx = jnp.min(jnp.where(vals == m, row, float(N_EXPERTS)), axis=0, keepdims=True)
        hit = row == idx
        vals = jnp.where(hit, -jnp.inf, vals)
        sel = jnp.where(hit, 1.0, sel)
        tops.append(m)
        idxs.append(idx)
    ex = [jnp.exp(m - tops[0]) for m in tops]
    den = ex[0] + ex[1] + ex[2] + ex[3]
    r = lax.broadcasted_iota(jnp.int32, (TM, TM), 0)
    c = lax.broadcasted_iota(jnp.int32, (TM, TM), 1)
    before = jnp.where(r < c, 1.0, 0.0).astype(BF16)
    pref = jnp.dot(sel.astype(BF16), before, preferred_element_type=F32) + cnt_ref[:, 0:1]
    slot = lax.broadcasted_iota(jnp.int32, (8, TM), 0)
    e_out = jnp.zeros((8, TM), F32)
    rank_out = jnp.zeros((8, TM), F32)
    gate_out = jnp.zeros((8, TM), F32)
    for kk in range(TOP_K):
        rank = jnp.sum(jnp.where(row == idxs[kk], pref, 0.0), axis=0, keepdims=True)
        e_out = jnp.where(slot == kk, idxs[kk], e_out)
        rank_out = jnp.where(slot == kk, rank, rank_out)
        gate_out = jnp.where(slot == kk, ex[kk] / den, gate_out)
    e_ref[...] = e_out.astype(jnp.int32)
    rank_ref[...] = rank_out.astype(jnp.int32)
    gate_ref[...] = gate_out
    cnt_ref[...] += jnp.sum(sel, axis=1, keepdims=True)


def _route(x_all, mods, layer, norm_g, w_r, b_r, n_tiles):
    tile = pl.BlockSpec((TM, D), lambda i: (i, 0))
    small = pl.BlockSpec((None, 8, TM), lambda i: (i, 0, 0))
    n = n_tiles * TM
    return pl.pallas_call(
        _route_kernel,
        grid=(n_tiles,),
        in_specs=[tile,
                  pl.BlockSpec((None, 6, D), lambda i: (layer * 8 + _tile_cond(i), 0, 0)),
                  pl.BlockSpec((1, D), lambda i: (0, 0)),
                  pl.BlockSpec((N_EXPERTS, D), lambda i: (0, 0)),
                  pl.BlockSpec((N_EXPERTS, LANES), lambda i: (0, 0))],
        out_specs=[pl.BlockSpec((TM * RT, LANES), lambda i: (i, 0)), small, small, small,
                   pl.BlockSpec((N_EXPERTS, LANES), lambda i: (0, 0))],
        out_shape=[jax.ShapeDtypeStruct((n * RT, LANES), F32),
                   jax.ShapeDtypeStruct((n_tiles, 8, TM), jnp.int32),
                   jax.ShapeDtypeStruct((n_tiles, 8, TM), jnp.int32),
                   jax.ShapeDtypeStruct((n_tiles, 8, TM), F32),
                   jax.ShapeDtypeStruct((N_EXPERTS, LANES), F32)],
        compiler_params=_cparams(("arbitrary",), VMEM_LIMIT),
        name="route",
    )(x_all, mods, norm_g.reshape(1, D), w_r.T, jnp.broadcast_to(b_r[:, None], (N_EXPERTS, LANES)))


def _dest_kernel(ps_ref, e_ref, rank_ref, d_ref):
    e = e_ref[...]
    acc = rank_ref[...]
    for j in range(N_EXPERTS):
        acc = acc + jnp.where(e == j, ps_ref[j], 0)
    d_ref[...] = acc


def _dest(pad_start, e, rank):
    full = pl.BlockSpec(e.shape, lambda i, ps: (0, 0, 0))
    return pl.pallas_call(
        _dest_kernel,
        grid_spec=pltpu.PrefetchScalarGridSpec(num_scalar_prefetch=1, grid=(1,),
                                               in_specs=[full, full], out_specs=full),
        out_shape=jax.ShapeDtypeStruct(e.shape, jnp.int32),
        name="dest",
    )(pad_start, e, rank)


PAD_PIECES = tuple(1 << p for p in reversed(range(BM.bit_length() - 1)))


def _zero_fill(lo_ref, hi_ref, xb_hbm, zeros, sem, n_blocks, wait):
    def go(rows, row0):
        cp = pltpu.make_async_copy(zeros.at[pl.ds(0, rows * RT), :],
                                   xb_hbm.at[pl.ds(row0 * RT, rows * RT), :], sem.at[1])
        cp.wait() if wait else cp.start()

    def per_expert(e, carry):
        row = lo_ref[e]
        n = hi_ref[e] - row
        for piece in PAD_PIECES:
            @pl.when((n & piece) != 0)
            def _():
                go(piece, row)
            row = row + (n & piece)
        return carry
    lax.fori_loop(0, N_EXPERTS, per_expert, 0)

    def per_block(b, carry):
        go(BM, b * BM)
        return carry
    lax.fori_loop(hi_ref[N_EXPERTS - 1] // BM, n_blocks, per_block, 0)


def _dispatch_kernel(lo_ref, hi_ref, dest_ref, h_ref, xb_hbm, zeros, sem, *, n_blocks):
    i = pl.program_id(0)

    @pl.when(i == 0)
    def _():
        zeros[...] = jnp.zeros_like(zeros)
        _zero_fill(lo_ref, hi_ref, xb_hbm, zeros, sem, n_blocks, wait=False)

    def body(t, carry):
        src = h_ref.at[pl.ds(t * RT, RT), :]
        for kk in range(TOP_K):
            d = dest_ref[0, 0, kk * TM + t]
            pltpu.make_async_copy(src, xb_hbm.at[pl.ds(d * RT, RT), :],
                                  sem.at[0]).start(priority=kk % 2)
        return carry
    lax.fori_loop(0, TM, body, 0, unroll=4)
    for kk in range(TOP_K):
        pltpu.make_async_copy(h_ref, xb_hbm.at[pl.ds(0, TM * RT), :], sem.at[0]).wait()

    @pl.when(i == pl.num_programs(0) - 1)
    def _():
        _zero_fill(lo_ref, hi_ref, xb_hbm, zeros, sem, n_blocks, wait=True)


def _dispatch(h_rt, dest, pad_lo, pad_hi, n_blocks):
    n_tiles = dest.shape[0]
    dest3 = dest[:, :TOP_K, :].reshape(n_tiles, 1, TOP_K * TM)
    grid_spec = pltpu.PrefetchScalarGridSpec(
        num_scalar_prefetch=2,
        grid=(n_tiles,),
        in_specs=[pl.BlockSpec((1, 1, TOP_K * TM), lambda i, lo, hi: (i, 0, 0),
                               memory_space=pltpu.SMEM),
                  pl.BlockSpec((TM * RT, LANES), lambda i, lo, hi: (i, 0))],
        out_specs=pl.BlockSpec(memory_space=pl.ANY),
        scratch_shapes=[pltpu.VMEM((BM * RT, LANES), F32),
                        pltpu.SemaphoreType.DMA((2,))],
    )
    xb = pl.pallas_call(
        functools.partial(_dispatch_kernel, n_blocks=n_blocks),
        grid_spec=grid_spec,
        out_shape=jax.ShapeDtypeStruct((n_blocks * BM * RT, LANES), F32),
        compiler_params=_cparams(("arbitrary",), VMEM_LIMIT, disable_bounds_checks=True),
        name="dispatch",
    )(pad_lo, pad_hi, dest3, h_rt)
    return xb, dest3


ROW_STEPS = tuple(range(BM // 4, BM + 1, BM // 4))


def _expert_kernel(bexp_ref, nused_ref, first_ref, wslot_ref, next_ref, valid_ref, xb_ref, w1_hbm, b1_ref,
                   w2_hbm, b2_ref, y_ref, w1s, w2s, w1b, w2b, sem, *, layer):
    i = pl.program_id(0)
    del nused_ref

    def weight_copies(e, slot):
        return (pltpu.make_async_copy(w1_hbm.at[layer, e], w1s.at[slot], sem.at[0, slot]),
                pltpu.make_async_copy(w2_hbm.at[layer, e], w2s.at[slot], sem.at[1, slot]))

    @pl.when(i == 0)
    def _():
        for cp in weight_copies(bexp_ref[0], 0):
            cp.start()

    @pl.when(first_ref[i] == 1)
    def _():
        slot = wslot_ref[i]
        for cp in weight_copies(bexp_ref[i], slot):
            cp.wait()
        w1b[...] = w1s[slot].astype(BF16)
        w2b[...] = w2s[slot].astype(BF16)

        @pl.when(next_ref[i] >= 0)
        def _():
            for cp in weight_copies(next_ref[i], 1 - slot):
                cp.start()

    valid = valid_ref[i]
    for lo, rows in zip((0,) + ROW_STEPS[:-1], ROW_STEPS):
        @pl.when(jnp.logical_and(valid > lo, valid <= rows))
        def _():
            x = jnp.concatenate(_from_row_tiled(xb_ref, rows), axis=1).astype(BF16)
            u = jnp.dot(x, w1b[...], preferred_element_type=F32) + b1_ref[...]
            glu = jnp.minimum(u[:, :D_FF], SWIGLU_LIMIT)
            lin = jnp.clip(u[:, D_FF:], -SWIGLU_LIMIT, SWIGLU_LIMIT)
            act = glu * _sigmoid(SWIGLU_ALPHA * glu) * (lin + 1.0)
            y = jnp.dot(act.astype(BF16), w2b[...], preferred_element_type=F32) + b2_ref[...]
            _to_row_tiled(y_ref.at[pl.ds(0, rows * RT), :], y)
            if rows < BM:
                y_ref[pl.ds(rows * RT, (BM - rows) * RT), :] = jnp.zeros(((BM - rows) * RT, LANES), F32)

    @pl.when(valid == 0)
    def _():
        y_ref[...] = jnp.zeros_like(y_ref)


def _experts(xb, block_exp, n_used, counts, pad_lo, layer, w1, b1, w2, b2):
    n_blocks = xb.shape[0] // (BM * RT)
    used = jnp.arange(n_blocks, dtype=jnp.int32) < n_used[0]
    changed = jnp.concatenate([jnp.ones((1,), bool), block_exp[1:] != block_exp[:-1]])
    first = jnp.logical_and(used, changed).astype(jnp.int32)
    wslot = (jnp.cumsum(first) - 1) % 2
    ar = jnp.arange(N_EXPERTS, dtype=jnp.int32)
    later = jnp.min(jnp.where((ar[None, :] > ar[:, None]) & (counts > 0)[None, :], ar[None, :],
                              N_EXPERTS), axis=1)
    of_block = block_exp[:, None] == ar[None, :]
    next_exp = jnp.sum(jnp.where(of_block, jnp.where(later == N_EXPERTS, -1, later)[None, :], 0), axis=1)
    starts = jnp.arange(n_blocks, dtype=jnp.int32) * BM
    valid = jnp.clip(jnp.sum(jnp.where(of_block, pad_lo[None, :], 0), axis=1) - starts, 0, BM)
    smap = lambda f: (lambda i, be, nu, fi, ws, nx, va: f(i, be, nu))
    grid_spec = pltpu.PrefetchScalarGridSpec(
        num_scalar_prefetch=6,
        grid=(n_blocks,),
        in_specs=[
            pl.BlockSpec((BM * RT, LANES), smap(lambda i, be, nu: (jnp.minimum(i, nu[0] - 1), 0))),
            pl.BlockSpec(memory_space=pl.ANY),
            pl.BlockSpec((None, 1, 2 * D_FF), smap(lambda i, be, nu: (layer * N_EXPERTS + be[i], 0, 0))),
            pl.BlockSpec(memory_space=pl.ANY),
            pl.BlockSpec((None, 1, D), smap(lambda i, be, nu: (layer * N_EXPERTS + be[i], 0, 0))),
        ],
        out_specs=pl.BlockSpec((BM * RT, LANES), smap(lambda i, be, nu: (i, 0))),
        scratch_shapes=[pltpu.VMEM((2, D, 2 * D_FF), F32),
                        pltpu.VMEM((2, D_FF, D), F32),
                        pltpu.VMEM((D, 2 * D_FF), BF16),
                        pltpu.VMEM((D_FF, D), BF16),
                        pltpu.SemaphoreType.DMA((2, 2))],
    )
    return pl.pallas_call(
        functools.partial(_expert_kernel, layer=layer),
        grid_spec=grid_spec,
        out_shape=jax.ShapeDtypeStruct(xb.shape, F32),
        compiler_params=_cparams(("arbitrary",), VMEM_LIMIT),
        name="experts",
    )(block_exp, n_used, first, wslot.astype(jnp.int32), next_exp.astype(jnp.int32),
      valid.astype(jnp.int32), xb, w1,
      b1.reshape(-1, 1, 2 * D_FF), w2, b2.reshape(-1, 1, D))


def _combine_kernel(dest_ref, dest_next_ref, y_hbm, gate_ref, x_ref, mod_ref, fg_ref, o_ref, buf, sem,
                    *, final_norm):
    i = pl.program_id(0)
    n = pl.num_programs(0)
    slot = i % 2
    rows = TOP_K * TM

    def gather(dest, to_slot):
        def body(p, carry):
            for half in range(2):
                r = 2 * p + half
                pltpu.make_async_copy(y_hbm.at[pl.ds(dest[0, 0, r] * RT, RT), :],
                                      buf.at[to_slot, pl.ds(r * RT, RT), :],
                                      sem.at[to_slot]).start(priority=half)
            return carry
        lax.fori_loop(0, rows // 2, body, 0, unroll=4)

    @pl.when(i == 0)
    def _():
        gather(dest_ref, 0)

    @pl.when(i + 1 < n)
    def _():
        gather(dest_next_ref, 1 - slot)

    pltpu.make_async_copy(y_hbm.at[pl.ds(0, rows * RT), :], buf.at[slot], sem.at[slot]).wait()
    eye = (lax.broadcasted_iota(jnp.int32, (8, 8), 0)
           == lax.broadcasted_iota(jnp.int32, (8, 8), 1)).astype(F32)
    gate = lax.dot_general(gate_ref[...], eye, _TN, precision=HIGHEST,
                           preferred_element_type=F32)
    cur = buf.at[slot]
    outs = []
    for j in range(RT):
        acc = None
        for kk in range(TOP_K):
            piece = gate[:, kk:kk + 1] * cur[pl.ds(kk * TM * RT + j, TM, stride=RT), :]
            acc = piece if acc is None else acc + piece
        outs.append(acc)
    out = x_ref[...] + mod_ref[5:6, :] * jnp.concatenate(outs, axis=1)
    if final_norm:
        out = _rms(out, fg_ref[...])
    o_ref[...] = out


def _combine(yb, dest3, gate, x_all, mods, layer, final_g, n_tiles, final_norm):
    last = n_tiles - 1
    tile = pl.BlockSpec((TM, D), lambda i: (i, 0))
    return pl.pallas_call(
        functools.partial(_combine_kernel, final_norm=final_norm),
        grid=(n_tiles,),
        in_specs=[pl.BlockSpec((1, 1, TOP_K * TM), lambda i: (i, 0, 0), memory_space=pltpu.SMEM),
                  pl.BlockSpec((1, 1, TOP_K * TM), lambda i: (jnp.minimum(i + 1, last), 0, 0),
                               memory_space=pltpu.SMEM),
                  pl.BlockSpec(memory_space=pl.ANY),
                  pl.BlockSpec((None, 8, TM), lambda i: (i, 0, 0)),
                  tile,
                  pl.BlockSpec((None, 6, D), lambda i: (layer * 8 + _tile_cond(i), 0, 0)),
                  pl.BlockSpec((1, D), lambda i: (0, 0))],
        out_specs=tile,
        out_shape=jax.ShapeDtypeStruct((n_tiles * TM, D), F32),
        scratch_shapes=[pltpu.VMEM((2, TOP_K * TM * RT, LANES), F32),
                        pltpu.SemaphoreType.DMA((2,))],
        compiler_params=_cparams(("arbitrary",), VMEM_LIMIT, disable_bounds_checks=True),
        name="combine",
    )(dest3, dest3, yb, gate, x_all, mods, final_g.reshape(1, D))


def _moe(x_all, mods, layer, norm_g, w_r, b_r, w1, b1, w2, b2, final_g, n_tiles, final_norm):
    n = n_tiles * TM
    h_rt, e, rank, gate, cnt = _route(x_all, mods, layer, norm_g, w_r, b_r, n_tiles)
    counts = cnt[:, 0].astype(jnp.int32)
    padded = (counts + BM - 1) // BM * BM
    pad_end = jnp.cumsum(padded)
    pad_start = pad_end - padded
    n_blocks = -(-(n * TOP_K) // BM) + N_EXPERTS
    starts = jnp.arange(n_blocks, dtype=jnp.int32) * BM
    block_exp = jnp.minimum(jnp.sum((pad_end[None, :] <= starts[:, None]).astype(jnp.int32), axis=1),
                            N_EXPERTS - 1)
    n_used = pad_end[-1:] // BM
    dest = _dest(pad_start, e, rank)
    xb, dest3 = _dispatch(h_rt, dest, pad_start + counts, pad_end, n_blocks)
    yb = _experts(xb, block_exp, n_used, counts, pad_start + counts, layer, w1, b1, w2, b2)
    return _combine(yb, dest3, gate, x_all, mods, layer, final_g, n_tiles, final_norm)


def kernel(x, c, ctx, c_ctx, mod_w, mod_b, norm1_g, norm2_g, fourier_w_in, fourier_w_out,
           hgrn_w_in, hgrn_lower_bounds, hgrn_norm_g, hgrn_w_out, router_w, router_b,
           expert_w1, expert_b1, expert_w2, expert_b2, final_norm_g):
    assert x.shape == (BATCH, SEQ, D) and ctx.shape == (BATCH, CTX_LEN, D)
    cond8 = jnp.zeros((8, D), F32).at[:BATCH].set(c).at[BATCH].set(c_ctx)
    mods = _adaln(cond8, mod_w, mod_b)
    x_all = jnp.concatenate([x.reshape(NLAT, D), ctx.reshape(NCTX, D)], axis=0)
    dch, mpos, mrow = _fourier_constants()
    experts = (expert_w1, expert_b1, expert_w2, expert_b2)

    vr, vi = _fourier_in(x_all, mods, 0, norm1_g[0], fourier_w_in[0], dch, mpos)
    x_all = _fourier_out(vr, vi, x_all, mods, 0, fourier_w_out[0], mrow)
    x_all = _moe(x_all, mods, 0, norm2_g[0], router_w[0], router_b[0], *experts,
                 final_norm_g, ALL_TILES, False)

    q, ff, fb, v, gs = _hgrn_in(x_all, mods, 1, norm1_g[1], hgrn_w_in[0], hgrn_lower_bounds)
    o_fw, o_bw = _scan(q, ff, fb, v)
    x_lat = _hgrn_out(o_fw, o_bw, gs, x_all, mods, 1, hgrn_norm_g[0], hgrn_w_out[0], LAT_TILES)
    out = _moe(x_lat, mods, 1, norm2_g[1], router_w[1], router_b[1], *experts,
               final_norm_g, LAT_TILES, True)
    return out.reshape(BATCH, SEQ, D)
```

```python
import functools

import numpy as np
import jax
import jax.numpy as jnp
from jax import lax
from jax.experimental import pallas as pl
from jax.experimental.pallas import tpu as pltpu

F32 = jnp.float32
BF16 = jnp.bfloat16
HIGHEST = lax.Precision.HIGHEST

D = 1024
BATCH = 2
SEQ = 8192
CTX_LEN = 256
GRID_W = 64
GRID_H = SEQ // GRID_W
NLAT = BATCH * SEQ
NCTX = BATCH * CTX_LEN
NTOK = NLAT + NCTX
TM = 256
LAT_TILES = NLAT // TM
ALL_TILES = NTOK // TM
TILES_PER_BATCH = SEQ // TM
FGROUPS = 4
FGDIM = D // FGROUPS
HEADS = 8
HDIM = D // HEADS
CHUNK = 128
N_EXPERTS = 32
TOP_K = 4
D_FF = 1024
SWIGLU_ALPHA = 1.702
SWIGLU_LIMIT = 7.0
BM = 512
LANES = 128
RT = D // LANES
NORM_EPS = 1e-6
VMEM_LIMIT = 56 * 1024 * 1024

_NT = (((1,), (1,)), ((), ()))
_TN = (((0,), (0,)), ((), ()))


def _cparams(sem, vmem=None, **kw):
    return pltpu.CompilerParams(dimension_semantics=sem, vmem_limit_bytes=vmem, **kw)


def _sigmoid(x):
    return 1.0 / (1.0 + jnp.exp(-x))


def _rms(x, g):
    return x * lax.rsqrt(jnp.mean(x * x, axis=-1, keepdims=True) + NORM_EPS) * g


def _tile_cond(i):
    return jnp.where(i < LAT_TILES, i // TILES_PER_BATCH, 2)


def _adaln_kernel(cond_ref, w_ref, b_ref, o_ref):
    c = cond_ref[...]
    s = c * _sigmoid(c)
    o_ref[...] = jnp.dot(s, w_ref[...], precision=HIGHEST,
                         preferred_element_type=F32) + b_ref[...]


def _adaln(cond8, mod_w, mod_b):
    depth = mod_w.shape[0]
    nb = 1536
    out = pl.pallas_call(
        _adaln_kernel,
        grid=(depth, 6 * D // nb),
        in_specs=[pl.BlockSpec((8, D), lambda l, j: (0, 0)),
                  pl.BlockSpec((None, D, nb), lambda l, j: (l, 0, j)),
                  pl.BlockSpec((None, 1, nb), lambda l, j: (l, 0, j))],
        out_specs=pl.BlockSpec((None, 8, nb), lambda l, j: (l, 0, j)),
        out_shape=jax.ShapeDtypeStruct((depth, 8, 6 * D), F32),
        compiler_params=_cparams(("arbitrary", "arbitrary"), VMEM_LIMIT),
        name="adaln",
    )(cond8, mod_w, mod_b.reshape(depth, 1, 6 * D))
    return out.reshape(depth * 8, 6, D)


def _dft_cs(n):
    k = np.arange(n)
    ang = 2.0 * np.pi * np.outer(k, k) / n
    s = 1.0 / np.sqrt(n)
    return np.cos(ang) * s, np.sin(ang) * s


def _fourier_constants():
    cd, sd = _dft_cs(FGDIM)
    dch = np.concatenate([cd, sd], axis=1)
    cc, sc = _dft_cs(GRID_W)
    eye = np.eye(TM // GRID_W)
    kc, ks = np.kron(eye, cc), np.kron(eye, sc)
    m_lat = np.block([[kc, -ks], [ks, kc]])
    cp, sp = _dft_cs(CTX_LEN)
    m_ctx = np.block([[cp, -sp], [sp, cp]])
    mpos = np.stack([m_lat, m_ctx])
    cr, sr = _dft_cs(GRID_H)
    mrow = np.concatenate([cr, -sr], axis=1)
    return (jnp.asarray(dch, BF16), jnp.asarray(mpos, BF16), jnp.asarray(mrow, BF16))


def _fourier_in_kernel(x_ref, mod_ref, g_ref, win_ref, dch_ref, mpos_ref, vr_ref, vi_ref):
    x = x_ref[...]
    h = _rms(x, g_ref[...]) * (1.0 + mod_ref[1:2, :]) + mod_ref[0:1, :]
    u = jnp.dot(h.astype(BF16), win_ref[...], preferred_element_type=F32).astype(BF16)
    parts = [jnp.dot(u[:, g * FGDIM:(g + 1) * FGDIM], dch_ref[...],
                     preferred_element_type=F32) for g in range(FGROUPS)]
    uc = jnp.concatenate([p[:, :FGDIM] for p in parts], axis=1)
    us = jnp.concatenate([p[:, FGDIM:] for p in parts], axis=1)
    st = jnp.concatenate([uc, us], axis=0).astype(BF16)
    v = jnp.dot(mpos_ref[...], st, preferred_element_type=F32)
    vr_ref[...] = v[:TM]
    vi_ref[...] = v[TM:]


def _fourier_in(x_all, mods, layer, norm_g, w_in, dch, mpos):
    tile = pl.BlockSpec((TM, D), lambda i: (i, 0))
    return pl.pallas_call(
        _fourier_in_kernel,
        grid=(ALL_TILES,),
        in_specs=[tile,
                  pl.BlockSpec((None, 6, D), lambda i: (layer * 8 + _tile_cond(i), 0, 0)),
                  pl.BlockSpec((1, D), lambda i: (0, 0)),
                  pl.BlockSpec((D, D), lambda i: (0, 0)),
                  pl.BlockSpec((FGDIM, 2 * FGDIM), lambda i: (0, 0)),
                  pl.BlockSpec((None, 2 * TM, 2 * TM), lambda i: (jnp.where(i < LAT_TILES, 0, 1), 0, 0))],
        out_specs=[tile, tile],
        out_shape=[jax.ShapeDtypeStruct((NTOK, D), F32)] * 2,
        compiler_params=_cparams(("arbitrary",), VMEM_LIMIT),
        name="fourier_in",
    )(x_all, mods, norm_g.reshape(1, D), w_in.astype(BF16), dch, mpos)


CB = 8


def _fourier_out_lat_kernel(vr_ref, vi_ref, x_ref, mrow_ref, wout_ref, mod_ref, o_ref):
    g1 = mod_ref[2:3, :]
    yf = []
    for c in range(CB):
        st = jnp.concatenate([vr_ref[:, c, :], vi_ref[:, c, :]], axis=0).astype(BF16)
        yf.append(jnp.dot(mrow_ref[...], st, preferred_element_type=F32).astype(BF16))
    y = jnp.dot(jnp.concatenate(yf, axis=0), wout_ref[...], preferred_element_type=F32)
    for c in range(CB):
        o_ref[:, c, :] = x_ref[:, c, :] + g1 * y[c * GRID_H:(c + 1) * GRID_H, :]


def _fourier_out_ctx_kernel(yr_ref, x_ref, wout_ref, mod_ref, o_ref):
    y = jnp.dot(yr_ref[...].astype(BF16), wout_ref[...], preferred_element_type=F32)
    o_ref[...] = x_ref[...] + mod_ref[2:3, :] * y


def _fourier_out(vr, vi, x_all, mods, layer, w_out, mrow):
    wout = w_out.astype(BF16)
    rows = NTOK // GRID_W
    v3 = lambda a: a.reshape(rows, GRID_W, D)
    blk = pl.BlockSpec((GRID_H, CB, D), lambda b, c: (b, c, 0))
    x_new = pl.pallas_call(
        _fourier_out_lat_kernel,
        grid=(BATCH, GRID_W // CB),
        in_specs=[blk, blk, blk,
                  pl.BlockSpec((GRID_H, 2 * GRID_H), lambda b, c: (0, 0)),
                  pl.BlockSpec((D, D), lambda b, c: (0, 0)),
                  pl.BlockSpec((None, 6, D), lambda b, c: (layer * 8 + b, 0, 0))],
        out_specs=blk,
        out_shape=jax.ShapeDtypeStruct((rows, GRID_W, D), F32),
        input_output_aliases={2: 0},
        compiler_params=_cparams(("arbitrary", "arbitrary"), VMEM_LIMIT),
        name="fourier_out_lat",
    )(v3(vr), v3(vi), v3(x_all), mrow, wout, mods).reshape(NTOK, D)
    ctile = pl.BlockSpec((TM, D), lambda i: (LAT_TILES + i, 0))
    return pl.pallas_call(
        _fourier_out_ctx_kernel,
        grid=(NCTX // TM,),
        in_specs=[ctile, ctile,
                  pl.BlockSpec((D, D), lambda i: (0, 0)),
                  pl.BlockSpec((None, 6, D), lambda i: (layer * 8 + 2, 0, 0))],
        out_specs=ctile,
        out_shape=jax.ShapeDtypeStruct((NTOK, D), F32),
        input_output_aliases={1: 0},
        compiler_params=_cparams(("arbitrary",), VMEM_LIMIT),
        name="fourier_out_ctx",
    )(vr, x_new, wout, mods)


def _hgrn_in_kernel(x_ref, mod_ref, g_ref, win_ref, hlb_ref, q_ref, ff_ref, fb_ref, v_ref, gs_ref,
                    *, layer):
    x = x_ref[...]
    h = (_rms(x, g_ref[...]) * (1.0 + mod_ref[1:2, :]) + mod_ref[0:1, :]).astype(BF16)
    raw = [hlb_ref[l] for l in range(hlb_ref.shape[0])]
    mx = functools.reduce(jnp.maximum, raw)
    ex = [jnp.exp(r - mx) for r in raw]
    den = functools.reduce(lambda a, b: a + b, ex)
    soft = [e / den for e in ex]
    lb = functools.reduce(lambda a, b: a + b, soft[:layer + 1]) - soft[0]

    def proj(j):
        return jnp.dot(h, win_ref[:, j * D:(j + 1) * D], preferred_element_type=F32)

    def per_head(ref, val):
        for hd in range(HEADS):
            ref[hd] = val[:, hd * HDIM:(hd + 1) * HDIM]

    q = proj(0)
    per_head(q_ref, q * _sigmoid(q))
    per_head(ff_ref, lb[0:1, :] + (1.0 - lb[0:1, :]) * _sigmoid(proj(1)))
    per_head(fb_ref, lb[1:2, :] + (1.0 - lb[1:2, :]) * _sigmoid(proj(2)))
    per_head(v_ref, proj(3))
    g = proj(4)
    gs_ref[...] = (g * _sigmoid(g)).astype(BF16)


def _hgrn_in(x_all, mods, layer, norm_g, w_in, hlb):
    tile = pl.BlockSpec((TM, D), lambda i: (i, 0))
    depth = hlb.shape[0]
    return pl.pallas_call(
        functools.partial(_hgrn_in_kernel, layer=layer),
        grid=(ALL_TILES,),
        in_specs=[tile,
                  pl.BlockSpec((None, 6, D), lambda i: (layer * 8 + _tile_cond(i), 0, 0)),
                  pl.BlockSpec((1, D), lambda i: (0, 0)),
                  pl.BlockSpec((D, 5 * D), lambda i: (0, 0)),
                  pl.BlockSpec((depth, 2, D), lambda i: (0, 0, 0))],
        out_specs=[pl.BlockSpec((HEADS, TM, HDIM), lambda i: (0, i, 0))] * 4 + [tile],
        out_shape=[jax.ShapeDtypeStruct((HEADS, NTOK, HDIM), F32)] * 4
                  + [jax.ShapeDtypeStruct((NTOK, D), BF16)],
        compiler_params=_cparams(("arbitrary",), VMEM_LIMIT),
        name="hgrn_in",
    )(x_all, mods, norm_g.reshape(1, D), w_in.astype(BF16), hlb)


N_LEVELS = 7
SUB = 8
NGRP = CHUNK // SUB
FINE_LEVELS = 3


def _scan_pair_kernel(qf_ref, ff_ref, vf_ref, qb_ref, fb_ref, vb_ref, of_ref, ob_ref, sf_ref, sb_ref,
                      rf_ref, rb_ref):
    @pl.when(pl.program_id(1) == 0)
    def _():
        sf_ref[...] = jnp.zeros_like(sf_ref)
        sb_ref[...] = jnp.zeros_like(sb_ref)

    _scan_chunk(qf_ref, ff_ref, vf_ref, of_ref, sf_ref, rf_ref, rev=False)
    _scan_chunk(qb_ref, fb_ref, vb_ref, ob_ref, sb_ref, rb_ref, rev=True)


PG = CHUNK // SUB


def _scan_chunk(q_ref, fg_ref, v_ref, o_ref, s_ref, relay_ref, *, rev):
    heads = [slice(h * HDIM, (h + 1) * HDIM) for h in range(HEADS)]

    def score(qs, ks, h):
        return jnp.dot(qs[:, heads[h]], ks[:, heads[h]].T.astype(BF16), preferred_element_type=F32)

    def by_residue(ref):
        return jnp.concatenate(
            [jnp.concatenate([ref[h, pl.ds(r, PG, stride=SUB), :] for r in range(SUB)], axis=0)
             for h in range(HEADS)], axis=1)

    def residue_groups(a):
        return [a[r * PG:(r + 1) * PG, :] for r in range(SUB)]

    q_p = by_residue(q_ref)
    fg_p = by_residue(fg_ref)
    k_p = 1.0 - fg_p
    v_p = by_residue(v_ref).astype(BF16)
    key = lax.broadcasted_iota(jnp.int32, (PG, CHUNK), 1)
    same_a = (key & (PG - 1)) == lax.broadcasted_iota(jnp.int32, (PG, CHUNK), 0)
    key_r = key >> (PG.bit_length() - 1)
    zero_p = jnp.zeros((PG, D), F32)
    sc_p = [[None] * SUB for _ in range(HEADS)]
    qd = q_p.astype(BF16)
    for h in range(HEADS):
        sc = score(qd, k_p, h)
        for r in range(SUB):
            sc_p[h][r] = jnp.where(same_a & (key_r == r), sc[r * PG:(r + 1) * PG, :], 0.0)
    qr_p = residue_groups(fg_p * q_p)
    kr_p = residue_groups(k_p)
    tot_p = residue_groups(fg_p)
    for l in range(FINE_LEVELS):
        bit = 1 << l
        is_far = [((r & bit) == 0) == rev for r in range(SUB)]
        far_groups = [r for r in range(SUB) if is_far[r]]
        qb = jnp.concatenate([qr_p[r] for r in far_groups], axis=0).astype(BF16)
        kb = jnp.concatenate([zero_p if is_far[r] else kr_p[r] for r in range(SUB)], axis=0)
        keep = [same_a & ((key_r >> (l + 1)) == (r >> (l + 1))) for r in far_groups]
        for h in range(HEADS):
            sc = score(qb, kb, h)
            for i, r in enumerate(far_groups):
                sc_p[h][r] = sc_p[h][r] + jnp.where(keep[i], sc[i * PG:(i + 1) * PG, :], 0.0)
        sib_p = [tot_p[r ^ bit] for r in range(SUB)]
        qr_p = [qr_p[r] * sib_p[r] if is_far[r] else qr_p[r] for r in range(SUB)]
        kr_p = [kr_p[r] if is_far[r] else kr_p[r] * sib_p[r] for r in range(SUB)]
        tot_p = [tot_p[r] * sib_p[r] for r in range(SUB)]
    for x, grp in enumerate((qr_p, kr_p, tot_p)):
        for h in range(HEADS):
            for r in range(SUB):
                relay_ref[x, h, pl.ds(r, PG, stride=SUB), :] = grp[r][:, heads[h]]
    qr, kr, tot = (jnp.concatenate([relay_ref[x, h] for h in range(HEADS)], axis=1)
                   for x in range(3))

    def groups(a):
        return [a[b * SUB:(b + 1) * SUB, :] for b in range(NGRP)]

    qr_g, kr_g, tot_g = (groups(a) for a in (qr, kr, tot))
    sc_g = [[None] * NGRP for _ in range(HEADS)]
    lane = lax.broadcasted_iota(jnp.int32, (SUB, CHUNK), 1)
    zero_g = jnp.zeros((SUB, D), F32)
    for l in range(FINE_LEVELS, N_LEVELS):
        bit = 1 << (l - FINE_LEVELS)
        is_far = [((b & bit) == 0) == rev for b in range(NGRP)]
        far_groups = [b for b in range(NGRP) if is_far[b]]
        qb = jnp.concatenate([qr_g[b] for b in far_groups], axis=0).astype(BF16)
        kb = jnp.concatenate([zero_g if is_far[b] else kr_g[b] for b in range(NGRP)], axis=0)
        span = 2 << l
        keep = [None if span == CHUNK else
                (lane >= b * SUB // span * span) & (lane < b * SUB // span * span + span)
                for b in far_groups]
        for h in range(HEADS):
            sc = score(qb, kb, h)
            for i, b in enumerate(far_groups):
                piece = sc[i * SUB:(i + 1) * SUB, :]
                if keep[i] is not None:
                    piece = jnp.where(keep[i], piece, 0.0)
                sc_g[h][b] = piece if sc_g[h][b] is None else sc_g[h][b] + piece
        sib_g = [tot_g[b ^ bit] for b in range(NGRP)]
        qr_g = [qr_g[b] * sib_g[b] if is_far[b] else qr_g[b] for b in range(NGRP)]
        kr_g = [kr_g[b] if is_far[b] else kr_g[b] * sib_g[b] for b in range(NGRP)]
        if l < N_LEVELS - 1:
            tot_g = [tot_g[b] * sib_g[b] for b in range(NGRP)]
        else:
            tot_row = tot_g[0][0:1, :] * sib_g[0][0:1, :]
    qin = jnp.concatenate(qr_g, axis=0).astype(BF16)
    kst = jnp.concatenate(kr_g, axis=0).astype(BF16)
    vb = jnp.concatenate([v_ref[h] for h in range(HEADS)], axis=1).astype(BF16)
    zero_s = jnp.zeros((SUB, CHUNK), F32)
    for h in range(HEADS):
        sl = heads[h]
        st = s_ref[h]
        sc = jnp.concatenate([zero_s if g is None else g for g in sc_g[h]], axis=0).astype(BF16)
        o_rows = (jnp.dot(sc, vb[:, sl], preferred_element_type=F32)
                  + jnp.dot(qin[:, sl], st.T.astype(BF16), preferred_element_type=F32))
        o_res = jnp.dot(jnp.concatenate(sc_p[h], axis=0).astype(BF16), v_p[:, sl],
                        preferred_element_type=F32)
        for r in range(SUB):
            relay_ref[3, h, pl.ds(r, PG, stride=SUB), :] = o_res[r * PG:(r + 1) * PG, :]
        o_ref[:, sl] = (o_rows + relay_ref[3, h]).astype(BF16)
        s_ref[h] = st * tot_row[:, sl] + lax.dot_general(vb[:, sl], kst[:, sl], _TN,
                                                         preferred_element_type=F32)


LAT_CHUNKS = SEQ // CHUNK
CTX_CHUNKS = CTX_LEN // CHUNK
SCAN_STEPS = CTX_CHUNKS + LAT_CHUNKS


def _scan(q, f_fw, f_bw, v):
    def idx_fw(b, s):
        return (jnp.where(s < CTX_CHUNKS, NLAT // CHUNK + CTX_CHUNKS * b + s,
                          LAT_CHUNKS * b + (s - CTX_CHUNKS)), 0)

    def idx_bw(b, s):
        return (jnp.where(s < CTX_CHUNKS, NLAT // CHUNK + CTX_CHUNKS * b + (CTX_CHUNKS - 1 - s),
                          LAT_CHUNKS * b + (SCAN_STEPS - 1 - s)), 0)
    fw = pl.BlockSpec((CHUNK, D), idx_fw)
    bw = pl.BlockSpec((CHUNK, D), idx_bw)
    fw_in = pl.BlockSpec((HEADS, CHUNK, HDIM), lambda b, s: (0, idx_fw(b, s)[0], 0))
    bw_in = pl.BlockSpec((HEADS, CHUNK, HDIM), lambda b, s: (0, idx_bw(b, s)[0], 0))
    state = pltpu.VMEM((HEADS, HDIM, HDIM), F32)
    relay = pltpu.VMEM((4, HEADS, CHUNK, HDIM), F32)
    return pl.pallas_call(
        _scan_pair_kernel,
        grid=(BATCH, SCAN_STEPS),
        in_specs=[fw_in, fw_in, fw_in, bw_in, bw_in, bw_in],
        out_specs=[fw, bw],
        out_shape=[jax.ShapeDtypeStruct((NTOK, D), BF16)] * 2,
        scratch_shapes=[state, state, relay, relay],
        compiler_params=_cparams(("arbitrary", "arbitrary"), VMEM_LIMIT),
        name="scan",
    )(q, f_fw, v, q, f_bw, v)


def _hgrn_out_kernel(of_ref, ob_ref, gs_ref, x_ref, ng_ref, wout_ref, mod_ref, o_ref):
    o = of_ref[...].astype(F32) + ob_ref[...].astype(F32)
    parts = []
    for h in range(HEADS):
        oh = o[:, h * HDIM:(h + 1) * HDIM]
        parts.append(oh * lax.rsqrt(jnp.mean(oh * oh, axis=-1, keepdims=True) + NORM_EPS))
    on = jnp.concatenate(parts, axis=1) * ng_ref[...]
    y = jnp.dot((on * gs_ref[...].astype(F32)).astype(BF16), wout_ref[...],
                preferred_element_type=F32)
    o_ref[...] = x_ref[...] + mod_ref[2:3, :] * y


def _hgrn_out(o_fw, o_bw, gs, x_all, mods, layer, norm_g, w_out, n_tiles):
    tile = pl.BlockSpec((TM, D), lambda i: (i, 0))
    return pl.pallas_call(
        _hgrn_out_kernel,
        grid=(n_tiles,),
        in_specs=[tile, tile, tile, tile,
                  pl.BlockSpec((1, D), lambda i: (0, 0)),
                  pl.BlockSpec((D, D), lambda i: (0, 0)),
                  pl.BlockSpec((None, 6, D), lambda i: (layer * 8 + _tile_cond(i), 0, 0))],
        out_specs=tile,
        out_shape=jax.ShapeDtypeStruct((n_tiles * TM, D), F32),
        compiler_params=_cparams(("arbitrary",), VMEM_LIMIT),
        name="hgrn_out",
    )(o_fw, o_bw, gs, x_all, norm_g.reshape(1, D), w_out.astype(BF16), mods)


def _to_row_tiled(ref, val):
    for j in range(RT):
        ref[pl.ds(j, val.shape[0], stride=RT), :] = val[:, j * LANES:(j + 1) * LANES]


def _from_row_tiled(ref, n, base=0):
    return [ref[pl.ds(base + j, n, stride=RT), :] for j in range(RT)]


def _route_kernel(x_ref, mod_ref, g_ref, wrt_ref, brt_ref, h_ref, e_ref, rank_ref, gate_ref, cnt_ref):
    @pl.when(pl.program_id(0) == 0)
    def _():
        cnt_ref[...] = jnp.zeros_like(cnt_ref)

    h = _rms(x_ref[...], g_ref[...]) * (1.0 + mod_ref[4:5, :]) + mod_ref[3:4, :]
    _to_row_tiled(h_ref, h)
    w = wrt_ref[...]
    w_hi = w.astype(BF16)
    w_lo = (w - w_hi.astype(F32)).astype(BF16)
    h_hi = h.astype(BF16)
    h_lo = (h - h_hi.astype(F32)).astype(BF16)
    part = lax.dot_general(jnp.concatenate([w_hi, w_lo], axis=0), h_hi, _NT,
                           preferred_element_type=F32)
    logits = (part[:N_EXPERTS] + part[N_EXPERTS:]
              + lax.dot_general(w_hi, h_lo, _NT, preferred_element_type=F32)
              + brt_ref[:, 0:1])
    row = lax.broadcasted_iota(jnp.int32, (N_EXPERTS, TM), 0).astype(F32)
    vals = logits
    sel = jnp.zeros((N_EXPERTS, TM), F32)
    tops, idxs = [], []
    for _ in range(TOP_K):
        m = jnp.max(vals, axis=0, keepdims=True)
        idx = jnp.min(jnp.where(vals == m, row, float(N_EXPERTS)), axis=0, keepdims=True)
        hit = row == idx
        vals = jnp.where(hit, -jnp.inf, vals)
        sel = jnp.where(hit, 1.0, sel)
        tops.append(m)
        idxs.append(idx)
    ex = [jnp.exp(m - tops[0]) for m in tops]
    den = ex[0] + ex[1] + ex[2] + ex[3]
    r = lax.broadcasted_iota(jnp.int32, (TM, TM), 0)
    c = lax.broadcasted_iota(jnp.int32, (TM, TM), 1)
    before = jnp.where(r < c, 1.0, 0.0).astype(BF16)
    pref = jnp.dot(sel.astype(BF16), before, preferred_element_type=F32) + cnt_ref[:, 0:1]
    slot = lax.broadcasted_iota(jnp.int32, (8, TM), 0)
    e_out = jnp.zeros((8, TM), F32)
    rank_out = jnp.zeros((8, TM), F32)
    gate_out = jnp.zeros((8, TM), F32)
    for kk in range(TOP_K):
        rank = jnp.sum(jnp.where(row == idxs[kk], pref, 0.0), axis=0, keepdims=True)
        e_out = jnp.where(slot == kk, idxs[kk], e_out)
        rank_out = jnp.where(slot == kk, rank, rank_out)
        gate_out = jnp.where(slot == kk, ex[kk] / den, gate_out)
    e_ref[...] = e_out.astype(jnp.int32)
    rank_ref[...] = rank_out.astype(jnp.int32)
    gate_ref[...] = gate_out
    cnt_ref[...] += jnp.sum(sel, axis=1, keepdims=True)


def _route(x_all, mods, layer, norm_g, w_r, b_r, n_tiles):
    tile = pl.BlockSpec((TM, D), lambda i: (i, 0))
    small = pl.BlockSpec((None, 8, TM), lambda i: (i, 0, 0))
    n = n_tiles * TM
    return pl.pallas_call(
        _route_kernel,
        grid=(n_tiles,),
        in_specs=[tile,
                  pl.BlockSpec((None, 6, D), lambda i: (layer * 8 + _tile_cond(i), 0, 0)),
                  pl.BlockSpec((1, D), lambda i: (0, 0)),
                  pl.BlockSpec((N_EXPERTS, D), lambda i: (0, 0)),
                  pl.BlockSpec((N_EXPERTS, LANES), lambda i: (0, 0))],
        out_specs=[pl.BlockSpec((TM * RT, LANES), lambda i: (i, 0)), small, small, small,
                   pl.BlockSpec((N_EXPERTS, LANES), lambda i: (0, 0))],
        out_shape=[jax.ShapeDtypeStruct((n * RT, LANES), F32),
                   jax.ShapeDtypeStruct((n_tiles, 8, TM), jnp.int32),
                   jax.ShapeDtypeStruct((n_tiles, 8, TM), jnp.int32),
                   jax.ShapeDtypeStruct((n_tiles, 8, TM), F32),
                   jax.ShapeDtypeStruct((N_EXPERTS, LANES), F32)],
        compiler_params=_cparams(("arbitrary",), VMEM_LIMIT),
        name="route",
    )(x_all, mods, norm_g.reshape(1, D), w_r.T, jnp.broadcast_to(b_r[:, None], (N_EXPERTS, LANES)))


def _dest_kernel(ps_ref, e_ref, rank_ref, d_ref):
    e = e_ref[...]
    acc = rank_ref[...]
    for j in range(N_EXPERTS):
        acc = acc + jnp.where(e == j, ps_ref[j], 0)
    d_ref[...] = acc


def _dest(pad_start, e, rank):
    full = pl.BlockSpec(e.shape, lambda i, ps: (0, 0, 0))
    return pl.pallas_call(
        _dest_kernel,
        grid_spec=pltpu.PrefetchScalarGridSpec(num_scalar_prefetch=1, grid=(1,),
                                               in_specs=[full, full], out_specs=full),
        out_shape=jax.ShapeDtypeStruct(e.shape, jnp.int32),
        name="dest",
    )(pad_start, e, rank)


PAD_PIECES = tuple(1 << p for p in reversed(range(BM.bit_length() - 1)))


def _zero_fill(lo_ref, hi_ref, xb_hbm, zeros, sem, n_blocks, wait):
    def go(rows, row0):
        cp = pltpu.make_async_copy(zeros.at[pl.ds(0, rows * RT), :],
                                   xb_hbm.at[pl.ds(row0 * RT, rows * RT), :], sem.at[1])
        cp.wait() if wait else cp.start()

    def per_expert(e, carry):
        row = lo_ref[e]
        n = hi_ref[e] - row
        for piece in PAD_PIECES:
            @pl.when((n & piece) != 0)
            def _():
                go(piece, row)
            row = row + (n & piece)
        return carry
    lax.fori_loop(0, N_EXPERTS, per_expert, 0)

    def per_block(b, carry):
        go(BM, b * BM)
        return carry
    lax.fori_loop(hi_ref[N_EXPERTS - 1] // BM, n_blocks, per_block, 0)


def _dispatch_kernel(lo_ref, hi_ref, dest_ref, h_ref, xb_hbm, zeros, sem, *, n_blocks):
    i = pl.program_id(0)

    @pl.when(i == 0)
    def _():
        zeros[...] = jnp.zeros_like(zeros)
        _zero_fill(lo_ref, hi_ref, xb_hbm, zeros, sem, n_blocks, wait=False)

    def body(t, carry):
        src = h_ref.at[pl.ds(t * RT, RT), :]
        for kk in range(TOP_K):
            d = dest_ref[0, 0, kk * TM + t]
            pltpu.make_async_copy(src, xb_hbm.at[pl.ds(d * RT, RT), :],
                                  sem.at[0]).start(priority=kk % 2)
        return carry
    lax.fori_loop(0, TM, body, 0, unroll=4)
    for kk in range(TOP_K):
        pltpu.make_async_copy(h_ref, xb_hbm.at[pl.ds(0, TM * RT), :], sem.at[0]).wait()

    @pl.when(i == pl.num_programs(0) - 1)
    def _():
        _zero_fill(lo_ref, hi_ref, xb_hbm, zeros, sem, n_blocks, wait=True)


def _dispatch(h_rt, dest, pad_lo, pad_hi, n_blocks):
    n_tiles = dest.shape[0]
    dest3 = dest[:, :TOP_K, :].reshape(n_tiles, 1, TOP_K * TM)
    grid_spec = pltpu.PrefetchScalarGridSpec(
        num_scalar_prefetch=2,
        grid=(n_tiles,),
        in_specs=[pl.BlockSpec((1, 1, TOP_K * TM), lambda i, lo, hi: (i, 0, 0),
                               memory_space=pltpu.SMEM),
                  pl.BlockSpec((TM * RT, LANES), lambda i, lo, hi: (i, 0))],
        out_specs=pl.BlockSpec(memory_space=pl.ANY),
        scratch_shapes=[pltpu.VMEM((BM * RT, LANES), F32),
                        pltpu.SemaphoreType.DMA((2,))],
    )
    xb = pl.pallas_call(
        functools.partial(_dispatch_kernel, n_blocks=n_blocks),
        grid_spec=grid_spec,
        out_shape=jax.ShapeDtypeStruct((n_blocks * BM * RT, LANES), F32),
        compiler_params=_cparams(("arbitrary",), VMEM_LIMIT, disable_bounds_checks=True),
        name="dispatch",
    )(pad_lo, pad_hi, dest3, h_rt)
    return xb, dest3


ROW_STEPS = tuple(range(BM // 4, BM + 1, BM // 4))


def _expert_kernel(bexp_ref, nused_ref, first_ref, wslot_ref, next_ref, valid_ref, xb_ref, w1_hbm, b1_ref,
                   w2_hbm, b2_ref, y_ref, w1s, w2s, w1b, w2b, sem, *, layer):
    i = pl.program_id(0)
    del nused_ref

    def weight_copies(e, slot):
        return (pltpu.make_async_copy(w1_hbm.at[layer, e], w1s.at[slot], sem.at[0, slot]),
                pltpu.make_async_copy(w2_hbm.at[layer, e], w2s.at[slot], sem.at[1, slot]))

    @pl.when(i == 0)
    def _():
        for cp in weight_copies(bexp_ref[0], 0):
            cp.start()

    @pl.when(first_ref[i] == 1)
    def _():
        slot = wslot_ref[i]
        for cp in weight_copies(bexp_ref[i], slot):
            cp.wait()
        w1b[...] = w1s[slot].astype(BF16)
        w2b[...] = w2s[slot].astype(BF16)

        @pl.when(next_ref[i] >= 0)
        def _():
            for cp in weight_copies(next_ref[i], 1 - slot):
                cp.start()

    valid = valid_ref[i]
    for lo, rows in zip((0,) + ROW_STEPS[:-1], ROW_STEPS):
        @pl.when(jnp.logical_and(valid > lo, valid <= rows))
        def _():
            x = jnp.concatenate(_from_row_tiled(xb_ref, rows), axis=1).astype(BF16)
            u = jnp.dot(x, w1b[...], preferred_element_type=F32) + b1_ref[...]
            glu = jnp.minimum(u[:, :D_FF], SWIGLU_LIMIT)
            lin = jnp.clip(u[:, D_FF:], -SWIGLU_LIMIT, SWIGLU_LIMIT)
            act = glu * _sigmoid(SWIGLU_ALPHA * glu) * (lin + 1.0)
            y = jnp.dot(act.astype(BF16), w2b[...], preferred_element_type=F32) + b2_ref[...]
            _to_row_tiled(y_ref.at[pl.ds(0, rows * RT), :], y)
            if rows < BM:
                y_ref[pl.ds(rows * RT, (BM - rows) * RT), :] = jnp.zeros(((BM - rows) * RT, LANES), F32)

    @pl.when(valid == 0)
    def _():
        y_ref[...] = jnp.zeros_like(y_ref)


def _experts(xb, block_exp, n_used, counts, pad_lo, layer, w1, b1, w2, b2):
    n_blocks = xb.shape[0] // (BM * RT)
    used = jnp.arange(n_blocks, dtype=jnp.int32) < n_used[0]
    changed = jnp.concatenate([jnp.ones((1,), bool), block_exp[1:] != block_exp[:-1]])
    first = jnp.logical_and(used, changed).astype(jnp.int32)
    wslot = (jnp.cumsum(first) - 1) % 2
    ar = jnp.arange(N_EXPERTS, dtype=jnp.int32)
    later = jnp.min(jnp.where((ar[None, :] > ar[:, None]) & (counts > 0)[None, :], ar[None, :],
                              N_EXPERTS), axis=1)
    of_block = block_exp[:, None] == ar[None, :]
    next_exp = jnp.sum(jnp.where(of_block, jnp.where(later == N_EXPERTS, -1, later)[None, :], 0), axis=1)
    starts = jnp.arange(n_blocks, dtype=jnp.int32) * BM
    valid = jnp.clip(jnp.sum(jnp.where(of_block, pad_lo[None, :], 0), axis=1) - starts, 0, BM)
    smap = lambda f: (lambda i, be, nu, fi, ws, nx, va: f(i, be, nu))
    grid_spec = pltpu.PrefetchScalarGridSpec(
        num_scalar_prefetch=6,
        grid=(n_blocks,),
        in_specs=[
            pl.BlockSpec((BM * RT, LANES), smap(lambda i, be, nu: (jnp.minimum(i, nu[0] - 1), 0))),
            pl.BlockSpec(memory_space=pl.ANY),
            pl.BlockSpec((None, 1, 2 * D_FF), smap(lambda i, be, nu: (layer * N_EXPERTS + be[i], 0, 0))),
            pl.BlockSpec(memory_space=pl.ANY),
            pl.BlockSpec((None, 1, D), smap(lambda i, be, nu: (layer * N_EXPERTS + be[i], 0, 0))),
        ],
        out_specs=pl.BlockSpec((BM * RT, LANES), smap(lambda i, be, nu: (i, 0))),
        scratch_shapes=[pltpu.VMEM((2, D, 2 * D_FF), F32),
                        pltpu.VMEM((2, D_FF, D), F32),
                        pltpu.VMEM((D, 2 * D_FF), BF16),
                        pltpu.VMEM((D_FF, D), BF16),
                        pltpu.SemaphoreType.DMA((2, 2))],
    )
    return pl.pallas_call(
        functools.partial(_expert_kernel, layer=layer),
        grid_spec=grid_spec,
        out_shape=jax.ShapeDtypeStruct(xb.shape, F32),
        compiler_params=_cparams(("arbitrary",), VMEM_LIMIT),
        name="experts",
    )(block_exp, n_used, first, wslot.astype(jnp.int32), next_exp.astype(jnp.int32),
      valid.astype(jnp.int32), xb, w1,
      b1.reshape(-1, 1, 2 * D_FF), w2, b2.reshape(-1, 1, D))


def _combine_kernel(dest_ref, dest_next_ref, y_hbm, gate_ref, x_ref, mod_ref, fg_ref, o_ref, buf, sem,
                    *, final_norm):
    i = pl.program_id(0)
    n = pl.num_programs(0)
    slot = i % 2
    rows = TOP_K * TM

    def gather(dest, to_slot):
        def body(p, carry):
            for half in range(2):
                r = 2 * p + half
                pltpu.make_async_copy(y_hbm.at[pl.ds(dest[0, 0, r] * RT, RT), :],
                                      buf.at[to_slot, pl.ds(r * RT, RT), :],
                                      sem.at[to_slot]).start(priority=half)
            return carry
        lax.fori_loop(0, rows // 2, body, 0, unroll=4)

    @pl.when(i == 0)
    def _():
        gather(dest_ref, 0)

    @pl.when(i + 1 < n)
    def _():
        gather(dest_next_ref, 1 - slot)

    pltpu.make_async_copy(y_hbm.at[pl.ds(0, rows * RT), :], buf.at[slot], sem.at[slot]).wait()
    eye = (lax.broadcasted_iota(jnp.int32, (8, 8), 0)
           == lax.broadcasted_iota(jnp.int32, (8, 8), 1)).astype(F32)
    gate = lax.dot_general(gate_ref[...], eye, _TN, precision=HIGHEST,
                           preferred_element_type=F32)
    cur = buf.at[slot]
    outs = []
    for j in range(RT):
        acc = None
        for kk in range(TOP_K):
            piece = gate[:, kk:kk + 1] * cur[pl.ds(kk * TM * RT + j, TM, stride=RT), :]
            acc = piece if acc is None else acc + piece
        outs.append(acc)
    out = x_ref[...] + mod_ref[5:6, :] * jnp.concatenate(outs, axis=1)
    if final_norm:
        out = _rms(out, fg_ref[...])
    o_ref[...] = out


def _combine(yb, dest3, gate, x_all, mods, layer, final_g, n_tiles, final_norm):
    last = n_tiles - 1
    tile = pl.BlockSpec((TM, D), lambda i: (i, 0))
    return pl.pallas_call(
        functools.partial(_combine_kernel, final_norm=final_norm),
        grid=(n_tiles,),
        in_specs=[pl.BlockSpec((1, 1, TOP_K * TM), lambda i: (i, 0, 0), memory_space=pltpu.SMEM),
                  pl.BlockSpec((1, 1, TOP_K * TM), lambda i: (jnp.minimum(i + 1, last), 0, 0),
                               memory_space=pltpu.SMEM),
                  pl.BlockSpec(memory_space=pl.ANY),
                  pl.BlockSpec((None, 8, TM), lambda i: (i, 0, 0)),
                  tile,
                  pl.BlockSpec((None, 6, D), lambda i: (layer * 8 + _tile_cond(i), 0, 0)),
                  pl.BlockSpec((1, D), lambda i: (0, 0))],
        out_specs=tile,
        out_shape=jax.ShapeDtypeStruct((n_tiles * TM, D), F32),
        scratch_shapes=[pltpu.VMEM((2, TOP_K * TM * RT, LANES), F32),
                        pltpu.SemaphoreType.DMA((2,))],
        compiler_params=_cparams(("arbitrary",), VMEM_LIMIT, disable_bounds_checks=True),
        name="combine",
    )(dest3, dest3, yb, gate, x_all, mods, final_g.reshape(1, D))


def _moe(x_all, mods, layer, norm_g, w_r, b_r, w1, b1, w2, b2, final_g, n_tiles, final_norm):
    n = n_tiles * TM
    h_rt, e, rank, gate, cnt = _route(x_all, mods, layer, norm_g, w_r, b_r, n_tiles)
    counts = cnt[:, 0].astype(jnp.int32)
    padded = (counts + BM - 1) // BM * BM
    pad_end = jnp.cumsum(padded)
    pad_start = pad_end - padded
    n_blocks = -(-(n * TOP_K) // BM) + N_EXPERTS
    starts = jnp.arange(n_blocks, dtype=jnp.int32) * BM
    block_exp = jnp.minimum(jnp.sum((pad_end[None, :] <= starts[:, None]).astype(jnp.int32), axis=1),
                            N_EXPERTS - 1)
    n_used = pad_end[-1:] // BM
    dest = _dest(pad_start, e, rank)
    xb, dest3 = _dispatch(h_rt, dest, pad_start + counts, pad_end, n_blocks)
    yb = _experts(xb, block_exp, n_used, counts, pad_start + counts, layer, w1, b1, w2, b2)
    return _combine(yb, dest3, gate, x_all, mods, layer, final_g, n_tiles, final_norm)


def kernel(x, c, ctx, c_ctx, mod_w, mod_b, norm1_g, norm2_g, fourier_w_in, fourier_w_out,
           hgrn_w_in, hgrn_lower_bounds, hgrn_norm_g, hgrn_w_out, router_w, router_b,
           expert_w1, expert_b1, expert_w2, expert_b2, final_norm_g):
    assert x.shape == (BATCH, SEQ, D) and ctx.shape == (BATCH, CTX_LEN, D)
    cond8 = jnp.zeros((8, D), F32).at[:BATCH].set(c).at[BATCH].set(c_ctx)
    mods = _adaln(cond8, mod_w, mod_b)
    x_all = jnp.concatenate([x.reshape(NLAT, D), ctx.reshape(NCTX, D)], axis=0)
    dch, mpos, mrow = _fourier_constants()
    experts = (expert_w1, expert_b1, expert_w2, expert_b2)

    vr, vi = _fourier_in(x_all, mods, 0, norm1_g[0], fourier_w_in[0], dch, mpos)
    x_all = _fourier_out(vr, vi, x_all, mods, 0, fourier_w_out[0], mrow)
    x_all = _moe(x_all, mods, 0, norm2_g[0], router_w[0], router_b[0], *experts,
                 final_norm_g, ALL_TILES, False)

    q, ff, fb, v, gs = _hgrn_in(x_all, mods, 1, norm1_g[1], hgrn_w_in[0], hgrn_lower_bounds)
    o_fw, o_bw = _scan(q, ff, fb, v)
    x_lat = _hgrn_out(o_fw, o_bw, gs, x_all, mods, 1, hgrn_norm_g[0], hgrn_w_out[0], LAT_TILES)
    out = _moe(x_lat, mods, 1, norm2_g[1], router_w[1], router_b[1], *experts,
               final_norm_g, LAT_TILES, True)
    return out.reshape(BATCH, SEQ, D)
```

```python
import functools

import numpy as np
import jax
import jax.numpy as jnp
from jax import lax
from jax.experimental import pallas as pl
from jax.experimental.pallas import tpu as pltpu

F32 = jnp.float32
BF16 = jnp.bfloat16
HIGHEST = lax.Precision.HIGHEST

D = 1024
BATCH = 2
SEQ = 8192
CTX_LEN = 256
GRID_W = 64
GRID_H = SEQ // GRID_W
NLAT = BATCH * SEQ
NCTX = BATCH * CTX_LEN
NTOK = NLAT + NCTX
TM = 256
LAT_TILES = NLAT // TM
ALL_TILES = NTOK // TM
TILES_PER_BATCH = SEQ // TM
FGROUPS = 4
FGDIM = D // FGROUPS
HEADS = 8
HDIM = D // HEADS
CHUNK = 128
N_EXPERTS = 32
TOP_K = 4
D_FF = 1024
SWIGLU_ALPHA = 1.702
SWIGLU_LIMIT = 7.0
BM = 512
LANES = 128
RT = D // LANES
NORM_EPS = 1e-6
VMEM_LIMIT = 56 * 1024 * 1024

_NT = (((1,), (1,)), ((), ()))
_TN = (((0,), (0,)), ((), ()))


def _cparams(sem, vmem=None, **kw):
    return pltpu.CompilerParams(dimension_semantics=sem, vmem_limit_bytes=vmem, **kw)


def _sigmoid(x):
    return 1.0 / (1.0 + jnp.exp(-x))


def _rms(x, g):
    return x * lax.rsqrt(jnp.mean(x * x, axis=-1, keepdims=True) + NORM_EPS) * g


def _tile_cond(i):
    return jnp.where(i < LAT_TILES, i // TILES_PER_BATCH, 2)


def _adaln_kernel(cond_ref, w_ref, b_ref, o_ref):
    c = cond_ref[...]
    s = c * _sigmoid(c)
    o_ref[...] = jnp.dot(s, w_ref[...], precision=HIGHEST,
                         preferred_element_type=F32) + b_ref[...]


def _adaln(cond8, mod_w, mod_b):
    depth = mod_w.shape[0]
    nb = 1536
    out = pl.pallas_call(
        _adaln_kernel,
        grid=(depth, 6 * D // nb),
        in_specs=[pl.BlockSpec((8, D), lambda l, j: (0, 0)),
                  pl.BlockSpec((None, D, nb), lambda l, j: (l, 0, j)),
                  pl.BlockSpec((None, 1, nb), lambda l, j: (l, 0, j))],
        out_specs=pl.BlockSpec((None, 8, nb), lambda l, j: (l, 0, j)),
        out_shape=jax.ShapeDtypeStruct((depth, 8, 6 * D), F32),
        compiler_params=_cparams(("arbitrary", "arbitrary"), VMEM_LIMIT),
        name="adaln",
    )(cond8, mod_w, mod_b.reshape(depth, 1, 6 * D))
    return out.reshape(depth * 8, 6, D)


def _dft_cs(n):
    k = np.arange(n)
    ang = 2.0 * np.pi * np.outer(k, k) / n
    s = 1.0 / np.sqrt(n)
    return np.cos(ang) * s, np.sin(ang) * s


def _fourier_constants():
    cd, sd = _dft_cs(FGDIM)
    dch = np.concatenate([cd, sd], axis=1)
    cc, sc = _dft_cs(GRID_W)
    eye = np.eye(TM // GRID_W)
    kc, ks = np.kron(eye, cc), np.kron(eye, sc)
    m_lat = np.block([[kc, -ks], [ks, kc]])
    cp, sp = _dft_cs(CTX_LEN)
    m_ctx = np.block([[cp, -sp], [sp, cp]])
    mpos = np.stack([m_lat, m_ctx])
    cr, sr = _dft_cs(GRID_H)
    mrow = np.concatenate([cr, -sr], axis=1)
    return (jnp.asarray(dch, BF16), jnp.asarray(mpos, BF16), jnp.asarray(mrow, BF16))


def _fourier_in_kernel(x_ref, mod_ref, g_ref, win_ref, dch_ref, mpos_ref, vr_ref, vi_ref):
    x = x_ref[...]
    h = _rms(x, g_ref[...]) * (1.0 + mod_ref[1:2, :]) + mod_ref[0:1, :]
    u = jnp.dot(h.astype(BF16), win_ref[...], preferred_element_type=F32).astype(BF16)
    parts = [jnp.dot(u[:, g * FGDIM:(g + 1) * FGDIM], dch_ref[...],
                     preferred_element_type=F32) for g in range(FGROUPS)]
    uc = jnp.concatenate([p[:, :FGDIM] for p in parts], axis=1)
    us = jnp.concatenate([p[:, FGDIM:] for p in parts], axis=1)
    st = jnp.concatenate([uc, us], axis=0).astype(BF16)
    v = jnp.dot(mpos_ref[...], st, preferred_element_type=F32)
    vr_ref[...] = v[:TM]
    vi_ref[...] = v[TM:]


def _fourier_in(x_all, mods, layer, norm_g, w_in, dch, mpos):
    tile = pl.BlockSpec((TM, D), lambda i: (i, 0))
    return pl.pallas_call(
        _fourier_in_kernel,
        grid=(ALL_TILES,),
        in_specs=[tile,
                  pl.BlockSpec((None, 6, D), lambda i: (layer * 8 + _tile_cond(i), 0, 0)),
                  pl.BlockSpec((1, D), lambda i: (0, 0)),
                  pl.BlockSpec((D, D), lambda i: (0, 0)),
                  pl.BlockSpec((FGDIM, 2 * FGDIM), lambda i: (0, 0)),
                  pl.BlockSpec((None, 2 * TM, 2 * TM), lambda i: (jnp.where(i < LAT_TILES, 0, 1), 0, 0))],
        out_specs=[tile, tile],
        out_shape=[jax.ShapeDtypeStruct((NTOK, D), F32)] * 2,
        compiler_params=_cparams(("arbitrary",), VMEM_LIMIT),
        name="fourier_in",
    )(x_all, mods, norm_g.reshape(1, D), w_in.astype(BF16), dch, mpos)


CB = 8


def _fourier_out_lat_kernel(vr_ref, vi_ref, x_ref, mrow_ref, wout_ref, mod_ref, o_ref):
    g1 = mod_ref[2:3, :]
    yf = []
    for c in range(CB):
        st = jnp.concatenate([vr_ref[:, c, :], vi_ref[:, c, :]], axis=0).astype(BF16)
        yf.append(jnp.dot(mrow_ref[...], st, preferred_element_type=F32).astype(BF16))
    y = jnp.dot(jnp.concatenate(yf, axis=0), wout_ref[...], preferred_element_type=F32)
    for c in range(CB):
        o_ref[:, c, :] = x_ref[:, c, :] + g1 * y[c * GRID_H:(c + 1) * GRID_H, :]


def _fourier_out_ctx_kernel(yr_ref, x_ref, wout_ref, mod_ref, o_ref):
    y = jnp.dot(yr_ref[...].astype(BF16), wout_ref[...], preferred_element_type=F32)
    o_ref[...] = x_ref[...] + mod_ref[2:3, :] * y


def _fourier_out(vr, vi, x_all, mods, layer, w_out, mrow):
    wout = w_out.astype(BF16)
    rows = NTOK // GRID_W
    v3 = lambda a: a.reshape(rows, GRID_W, D)
    blk = pl.BlockSpec((GRID_H, CB, D), lambda b, c: (b, c, 0))
    x_new = pl.pallas_call(
        _fourier_out_lat_kernel,
        grid=(BATCH, GRID_W // CB),
        in_specs=[blk, blk, blk,
                  pl.BlockSpec((GRID_H, 2 * GRID_H), lambda b, c: (0, 0)),
                  pl.BlockSpec((D, D), lambda b, c: (0, 0)),
                  pl.BlockSpec((None, 6, D), lambda b, c: (layer * 8 + b, 0, 0))],
        out_specs=blk,
        out_shape=jax.ShapeDtypeStruct((rows, GRID_W, D), F32),
        input_output_aliases={2: 0},
        compiler_params=_cparams(("arbitrary", "arbitrary"), VMEM_LIMIT),
        name="fourier_out_lat",
    )(v3(vr), v3(vi), v3(x_all), mrow, wout, mods).reshape(NTOK, D)
    ctile = pl.BlockSpec((TM, D), lambda i: (LAT_TILES + i, 0))
    return pl.pallas_call(
        _fourier_out_ctx_kernel,
        grid=(NCTX // TM,),
        in_specs=[ctile, ctile,
                  pl.BlockSpec((D, D), lambda i: (0, 0)),
                  pl.BlockSpec((None, 6, D), lambda i: (layer * 8 + 2, 0, 0))],
        out_specs=ctile,
        out_shape=jax.ShapeDtypeStruct((NTOK, D), F32),
        input_output_aliases={1: 0},
        compiler_params=_cparams(("arbitrary",), VMEM_LIMIT),
        name="fourier_out_ctx",
    )(vr, x_new, wout, mods)


def _hgrn_in_kernel(x_ref, mod_ref, g_ref, win_ref, hlb_ref, q_ref, ff_ref, fb_ref, v_ref, gs_ref,
                    *, layer):
    x = x_ref[...]
    h = (_rms(x, g_ref[...]) * (1.0 + mod_ref[1:2, :]) + mod_ref[0:1, :]).astype(BF16)
    raw = [hlb_ref[l] for l in range(hlb_ref.shape[0])]
    mx = functools.reduce(jnp.maximum, raw)
    ex = [jnp.exp(r - mx) for r in raw]
    den = functools.reduce(lambda a, b: a + b, ex)
    soft = [e / den for e in ex]
    lb = functools.reduce(lambda a, b: a + b, soft[:layer + 1]) - soft[0]

    def proj(j):
        return jnp.dot(h, win_ref[:, j * D:(j + 1) * D], preferred_element_type=F32)

    def per_head(ref, val):
        for hd in range(HEADS):
            ref[hd] = val[:, hd * HDIM:(hd + 1) * HDIM]

    q = proj(0)
    per_head(q_ref, q * _sigmoid(q))
    per_head(ff_ref, lb[0:1, :] + (1.0 - lb[0:1, :]) * _sigmoid(proj(1)))
    per_head(fb_ref, lb[1:2, :] + (1.0 - lb[1:2, :]) * _sigmoid(proj(2)))
    per_head(v_ref, proj(3))
    g = proj(4)
    gs_ref[...] = (g * _sigmoid(g)).astype(BF16)


def _hgrn_in(x_all, mods, layer, norm_g, w_in, hlb):
    tile = pl.BlockSpec((TM, D), lambda i: (i, 0))
    depth = hlb.shape[0]
    return pl.pallas_call(
        functools.partial(_hgrn_in_kernel, layer=layer),
        grid=(ALL_TILES,),
        in_specs=[tile,
                  pl.BlockSpec((None, 6, D), lambda i: (layer * 8 + _tile_cond(i), 0, 0)),
                  pl.BlockSpec((1, D), lambda i: (0, 0)),
                  pl.BlockSpec((D, 5 * D), lambda i: (0, 0)),
                  pl.BlockSpec((depth, 2, D), lambda i: (0, 0, 0))],
        out_specs=[pl.BlockSpec((HEADS, TM, HDIM), lambda i: (0, i, 0))] * 4 + [tile],
        out_shape=[jax.ShapeDtypeStruct((HEADS, NTOK, HDIM), F32)] * 4
                  + [jax.ShapeDtypeStruct((NTOK, D), BF16)],
        compiler_params=_cparams(("arbitrary",), VMEM_LIMIT),
        name="hgrn_in",
    )(x_all, mods, norm_g.reshape(1, D), w_in.astype(BF16), hlb)


N_LEVELS = 7
SUB = 8
NGRP = CHUNK // SUB
FINE_LEVELS = 3


def _scan_pair_kernel(qf_ref, ff_ref, vf_ref, qb_ref, fb_ref, vb_ref, of_ref, ob_ref, sf_ref, sb_ref,
                      rf_ref, rb_ref):
    @pl.when(pl.program_id(1) == 0)
    def _():
        sf_ref[...] = jnp.zeros_like(sf_ref)
        sb_ref[...] = jnp.zeros_like(sb_ref)

    _scan_chunk(qf_ref, ff_ref, vf_ref, of_ref, sf_ref, rf_ref, rev=False)
    _scan_chunk(qb_ref, fb_ref, vb_ref, ob_ref, sb_ref, rb_ref, rev=True)


PG = CHUNK // SUB


def _scan_chunk(q_ref, fg_ref, v_ref, o_ref, s_ref, relay_ref, *, rev):
    heads = [slice(h * HDIM, (h + 1) * HDIM) for h in range(HEADS)]

    def score(qs, ks, h):
        return jnp.dot(qs[:, heads[h]], ks[:, heads[h]].T.astype(BF16), preferred_element_type=F32)

    def by_residue(ref):
        return jnp.concatenate(
            [jnp.concatenate([ref[h, pl.ds(r, PG, stride=SUB), :] for r in range(SUB)], axis=0)
             for h in range(HEADS)], axis=1)

    def residue_groups(a):
        return [a[r * PG:(r + 1) * PG, :] for r in range(SUB)]

    q_p = by_residue(q_ref)
    fg_p = by_residue(fg_ref)
    k_p = 1.0 - fg_p
    v_p = by_residue(v_ref).astype(BF16)
    key = lax.broadcasted_iota(jnp.int32, (PG, CHUNK), 1)
    same_a = (key & (PG - 1)) == lax.broadcasted_iota(jnp.int32, (PG, CHUNK), 0)
    key_r = key >> (PG.bit_length() - 1)
    zero_p = jnp.zeros((PG, D), F32)
    sc_p = [[None] * SUB for _ in range(HEADS)]
    qd = q_p.astype(BF16)
    for h in range(HEADS):
        sc = score(qd, k_p, h)
        for r in range(SUB):
            sc_p[h][r] = jnp.where(same_a & (key_r == r), sc[r * PG:(r + 1) * PG, :], 0.0)
    qr_p = residue_groups(fg_p * q_p)
    kr_p = residue_groups(k_p)
    tot_p = residue_groups(fg_p)
    for l in range(FINE_LEVELS):
        bit = 1 << l
        is_far = [((r & bit) == 0) == rev for r in range(SUB)]
        far_groups = [r for r in range(SUB) if is_far[r]]
        qb = jnp.concatenate([qr_p[r] for r in far_groups], axis=0).astype(BF16)
        kb = jnp.concatenate([zero_p if is_far[r] else kr_p[r] for r in range(SUB)], axis=0)
        keep = [same_a & ((key_r >> (l + 1)) == (r >> (l + 1))) for r in far_groups]
        for h in range(HEADS):
            sc = score(qb, kb, h)
            for i, r in enumerate(far_groups):
                sc_p[h][r] = sc_p[h][r] + jnp.where(keep[i], sc[i * PG:(i + 1) * PG, :], 0.0)
        sib_p = [tot_p[r ^ bit] for r in range(SUB)]
        qr_p = [qr_p[r] * sib_p[r] if is_far[r] else qr_p[r] for r in range(SUB)]
        kr_p = [kr_p[r] if is_far[r] else kr_p[r] * sib_p[r] for r in range(SUB)]
        tot_p = [tot_p[r] * sib_p[r] for r in range(SUB)]
    for x, grp in enumerate((qr_p, kr_p, tot_p)):
        for h in range(HEADS):
            for r in range(SUB):
                relay_ref[x, h, pl.ds(r, PG, stride=SUB), :] = grp[r][:, heads[h]]
    qr, kr, tot = (jnp.concatenate([relay_ref[x, h] for h in range(HEADS)], axis=1)
                   for x in range(3))

    def groups(a):
        return [a[b * SUB:(b + 1) * SUB, :] for b in range(NGRP)]

    qr_g, kr_g, tot_g = (groups(a) for a in (qr, kr, tot))
    sc_g = [[None] * NGRP for _ in range(HEADS)]
    lane = lax.broadcasted_iota(jnp.int32, (SUB, CHUNK), 1)
    zero_g = jnp.zeros((SUB, D), F32)
    for l in range(FINE_LEVELS, N_LEVELS):
        bit = 1 << (l - FINE_LEVELS)
        is_far = [((b & bit) == 0) == rev for b in range(NGRP)]
        far_groups = [b for b in range(NGRP) if is_far[b]]
        qb = jnp.concatenate([qr_g[b] for b in far_groups], axis=0).astype(BF16)
        kb = jnp.concatenate([zero_g if is_far[b] else kr_g[b] for b in range(NGRP)], axis=0)
        span = 2 << l
        keep = [None if span == CHUNK else
                (lane >= b * SUB // span * span) & (lane < b * SUB // span * span + span)
                for b in far_groups]
        for h in range(HEADS):
            sc = score(qb, kb, h)
            for i, b in enumerate(far_groups):
                piece = sc[i * SUB:(i + 1) * SUB, :]
                if keep[i] is not None:
                    piece = jnp.where(keep[i], piece, 0.0)
                sc_g[h][b] = piece if sc_g[h][b] is None else sc_g[h][b] + piece
        sib_g = [tot_g[b ^ bit] for b in range(NGRP)]
        qr_g = [qr_g[b] * sib_g[b] if is_far[b] else qr_g[b] for b in range(NGRP)]
        kr_g = [kr_g[b] if is_far[b] else kr_g[b] * sib_g[b] for b in range(NGRP)]
        if l < N_LEVELS - 1:
            tot_g = [tot_g[b] * sib_g[b] for b in range(NGRP)]
        else:
            tot_row = tot_g[0][0:1, :] * sib_g[0][0:1, :]
    qin = jnp.concatenate(qr_g, axis=0).astype(BF16)
    kst = jnp.concatenate(kr_g, axis=0).astype(BF16)
    vb = jnp.concatenate([v_ref[h] for h in range(HEADS)], axis=1).astype(BF16)
    zero_s = jnp.zeros((SUB, CHUNK), F32)
    for h in range(HEADS):
        sl = heads[h]
        st = s_ref[h]
        sc = jnp.concatenate([zero_s if g is None else g for g in sc_g[h]], axis=0).astype(BF16)
        o_rows = (jnp.dot(sc, vb[:, sl], preferred_element_type=F32)
                  + jnp.dot(qin[:, sl], st.T.astype(BF16), preferred_element_type=F32))
        o_res = jnp.dot(jnp.concatenate(sc_p[h], axis=0).astype(BF16), v_p[:, sl],
                        preferred_element_type=F32)
        for r in range(SUB):
            relay_ref[3, h, pl.ds(r, PG, stride=SUB), :] = o_res[r * PG:(r + 1) * PG, :]
        o_ref[:, sl] = (o_rows + relay_ref[3, h]).astype(BF16)
        s_ref[h] = st * tot_row[:, sl] + lax.dot_general(vb[:, sl], kst[:, sl], _TN,
                                                         preferred_element_type=F32)


LAT_CHUNKS = SEQ // CHUNK
CTX_CHUNKS = CTX_LEN // CHUNK
SCAN_STEPS = CTX_CHUNKS + LAT_CHUNKS


def _scan(q, f_fw, f_bw, v):
    def idx_fw(b, s):
        return (jnp.where(s < CTX_CHUNKS, NLAT // CHUNK + CTX_CHUNKS * b + s,
                          LAT_CHUNKS * b + (s - CTX_CHUNKS)), 0)

    def idx_bw(b, s):
        return (jnp.where(s < CTX_CHUNKS, NLAT // CHUNK + CTX_CHUNKS * b + (CTX_CHUNKS - 1 - s),
                          LAT_CHUNKS * b + (SCAN_STEPS - 1 - s)), 0)
    fw = pl.BlockSpec((CHUNK, D), idx_fw)
    bw = pl.BlockSpec((CHUNK, D), idx_bw)
    fw_in = pl.BlockSpec((HEADS, CHUNK, HDIM), lambda b, s: (0, idx_fw(b, s)[0], 0))
    bw_in = pl.BlockSpec((HEADS, CHUNK, HDIM), lambda b, s: (0, idx_bw(b, s)[0], 0))
    state = pltpu.VMEM((HEADS, HDIM, HDIM), F32)
    relay = pltpu.VMEM((4, HEADS, CHUNK, HDIM), F32)
    return pl.pallas_call(
        _scan_pair_kernel,
        grid=(BATCH, SCAN_STEPS),
        in_specs=[fw_in, fw_in, fw_in, bw_in, bw_in, bw_in],
        out_specs=[fw, bw],
        out_shape=[jax.ShapeDtypeStruct((NTOK, D), BF16)] * 2,
        scratch_shapes=[state, state, relay, relay],
        compiler_params=_cparams(("arbitrary", "arbitrary"), VMEM_LIMIT),
        name="scan",
    )(q, f_fw, v, q, f_bw, v)


def _hgrn_out_kernel(of_ref, ob_ref, gs_ref, x_ref, ng_ref, wout_ref, mod_ref, o_ref):
    o = of_ref[...].astype(F32) + ob_ref[...].astype(F32)
    parts = []
    for h in range(HEADS):
        oh = o[:, h * HDIM:(h + 1) * HDIM]
        parts.append(oh * lax.rsqrt(jnp.mean(oh * oh, axis=-1, keepdims=True) + NORM_EPS))
    on = jnp.concatenate(parts, axis=1) * ng_ref[...]
    y = jnp.dot((on * gs_ref[...].astype(F32)).astype(BF16), wout_ref[...],
                preferred_element_type=F32)
    o_ref[...] = x_ref[...] + mod_ref[2:3, :] * y


def _hgrn_out(o_fw, o_bw, gs, x_all, mods, layer, norm_g, w_out, n_tiles):
    tile = pl.BlockSpec((TM, D), lambda i: (i, 0))
    return pl.pallas_call(
        _hgrn_out_kernel,
        grid=(n_tiles,),
        in_specs=[tile, tile, tile, tile,
                  pl.BlockSpec((1, D), lambda i: (0, 0)),
                  pl.BlockSpec((D, D), lambda i: (0, 0)),
                  pl.BlockSpec((None, 6, D), lambda i: (layer * 8 + _tile_cond(i), 0, 0))],
        out_specs=tile,
        out_shape=jax.ShapeDtypeStruct((n_tiles * TM, D), F32),
        compiler_params=_cparams(("arbitrary",), VMEM_LIMIT),
        name="hgrn_out",
    )(o_fw, o_bw, gs, x_all, norm_g.reshape(1, D), w_out.astype(BF16), mods)


def _to_row_tiled(ref, val):
    for j in range(RT):
        ref[pl.ds(j, val.shape[0], stride=RT), :] = val[:, j * LANES:(j + 1) * LANES]


def _from_row_tiled(ref, n, base=0):
    return [ref[pl.ds(base + j, n, stride=RT), :] for j in range(RT)]


def _route_kernel(x_ref, mod_ref, g_ref, wsplit_ref, brt_ref, h_ref, e_ref, rank_ref, gate_ref, cnt_ref):
    @pl.when(pl.program_id(0) == 0)
    def _():
        cnt_ref[...] = jnp.zeros_like(cnt_ref)

    h = _rms(x_ref[...], g_ref[...]) * (1.0 + mod_ref[4:5, :]) + mod_ref[3:4, :]
    _to_row_tiled(h_ref, h)
    h_hi = h.astype(BF16)
    h_lo = (h - h_hi.astype(F32)).astype(BF16)
    part = (jnp.dot(h_hi, wsplit_ref[...], preferred_element_type=F32)
            + jnp.dot(h_lo, wsplit_ref[...], preferred_element_type=F32)).T
    logits = part[:N_EXPERTS] + part[N_EXPERTS:2 * N_EXPERTS] + brt_ref[:, 0:1]
    row = lax.broadcasted_iota(jnp.int32, (N_EXPERTS, TM), 0).astype(F32)
    vals = logits
    sel = jnp.zeros((N_EXPERTS, TM), F32)
    tops, idxs = [], []
    for _ in range(TOP_K):
        m = jnp.max(vals, axis=0, keepdims=True)
        idx = jnp.min(jnp.where(vals == m, row, float(N_EXPERTS)), axis=0, keepdims=True)
        hit = row == idx
        vals = jnp.where(hit, -jnp.inf, vals)
        sel = jnp.where(hit, 1.0, sel)
        tops.append(m)
        idxs.append(idx)
    ex = [jnp.exp(m - tops[0]) for m in tops]
    den = ex[0] + ex[1] + ex[2] + ex[3]
    r = lax.broadcasted_iota(jnp.int32, (TM, TM), 0)
    c = lax.broadcasted_iota(jnp.int32, (TM, TM), 1)
    before = jnp.where(r < c, 1.0, 0.0).astype(BF16)
    pref = jnp.dot(sel.astype(BF16), before, preferred_element_type=F32) + cnt_ref[:, 0:1]
    slot = lax.broadcasted_iota(jnp.int32, (8, TM), 0)
    e_out = jnp.zeros((8, TM), F32)
    rank_out = jnp.zeros((8, TM), F32)
    gate_out = jnp.zeros((8, TM), F32)
    for kk in range(TOP_K):
        rank = jnp.sum(jnp.where(row == idxs[kk], pref, 0.0), axis=0, keepdims=True)
        e_out = jnp.where(slot == kk, idxs[kk], e_out)
        rank_out = jnp.where(slot == kk, rank, rank_out)
        gate_out = jnp.where(slot == kk, ex[kk] / den, gate_out)
    e_ref[...] = e_out.astype(jnp.int32)
    rank_ref[...] = rank_out.astype(jnp.int32)
    gate_ref[...] = gate_out
    cnt_ref[...] += jnp.sum(sel, axis=1, keepdims=True)


def _route(x_all, mods, layer, norm_g, w_r, b_r, n_tiles):
    tile = pl.BlockSpec((TM, D), lambda i: (i, 0))
    small = pl.BlockSpec((None, 8, TM), lambda i: (i, 0, 0))
    n = n_tiles * TM
    w_hi = w_r.astype(BF16)
    w_lo = (w_r - w_hi.astype(F32)).astype(BF16)
    w_split = jnp.concatenate([w_hi, w_lo, jnp.zeros((D, LANES - 2 * N_EXPERTS), BF16)], axis=1)
    return pl.pallas_call(
        _route_kernel,
        grid=(n_tiles,),
        in_specs=[tile,
                  pl.BlockSpec((None, 6, D), lambda i: (layer * 8 + _tile_cond(i), 0, 0)),
                  pl.BlockSpec((1, D), lambda i: (0, 0)),
                  pl.BlockSpec((D, LANES), lambda i: (0, 0)),
                  pl.BlockSpec((N_EXPERTS, LANES), lambda i: (0, 0))],
        out_specs=[pl.BlockSpec((TM * RT, LANES), lambda i: (i, 0)), small, small, small,
                   pl.BlockSpec((N_EXPERTS, LANES), lambda i: (0, 0))],
        out_shape=[jax.ShapeDtypeStruct((n * RT, LANES), F32),
                   jax.ShapeDtypeStruct((n_tiles, 8, TM), jnp.int32),
                   jax.ShapeDtypeStruct((n_tiles, 8, TM), jnp.int32),
                   jax.ShapeDtypeStruct((n_tiles, 8, TM), F32),
                   jax.ShapeDtypeStruct((N_EXPERTS, LANES), F32)],
        compiler_params=_cparams(("arbitrary",), VMEM_LIMIT),
        name="route",
    )(x_all, mods, norm_g.reshape(1, D), w_split, jnp.broadcast_to(b_r[:, None], (N_EXPERTS, LANES)))


def _dest_kernel(ps_ref, e_ref, rank_ref, d_ref):
    e = e_ref[...]
    acc = rank_ref[...]
    for j in range(N_EXPERTS):
        acc = acc + jnp.where(e == j, ps_ref[j], 0)
    d_ref[...] = acc


def _dest(pad_start, e, rank):
    full = pl.BlockSpec(e.shape, lambda i, ps: (0, 0, 0))
    return pl.pallas_call(
        _dest_kernel,
        grid_spec=pltpu.PrefetchScalarGridSpec(num_scalar_prefetch=1, grid=(1,),
                                               in_specs=[full, full], out_specs=full),
        out_shape=jax.ShapeDtypeStruct(e.shape, jnp.int32),
        name="dest",
    )(pad_start, e, rank)


PAD_PIECES = tuple(1 << p for p in reversed(range(BM.bit_length() - 1)))


def _zero_fill(lo_ref, hi_ref, xb_hbm, zeros, sem, n_blocks, wait):
    def go(rows, row0):
        cp = pltpu.make_async_copy(zeros.at[pl.ds(0, rows * RT), :],
                                   xb_hbm.at[pl.ds(row0 * RT, rows * RT), :], sem.at[1])
        cp.wait() if wait else cp.start()

    def per_expert(e, carry):
        row = lo_ref[e]
        n = hi_ref[e] - row
        for piece in PAD_PIECES:
            @pl.when((n & piece) != 0)
            def _():
                go(piece, row)
            row = row + (n & piece)
        return carry
    lax.fori_loop(0, N_EXPERTS, per_expert, 0)

    def per_block(b, carry):
        go(BM, b * BM)
        return carry
    lax.fori_loop(hi_ref[N_EXPERTS - 1] // BM, n_blocks, per_block, 0)


def _dispatch_kernel(lo_ref, hi_ref, dest_ref, h_ref, xb_hbm, zeros, sem, *, n_blocks):
    i = pl.program_id(0)

    @pl.when(i == 0)
    def _():
        zeros[...] = jnp.zeros_like(zeros)
        _zero_fill(lo_ref, hi_ref, xb_hbm, zeros, sem, n_blocks, wait=False)

    def body(t, carry):
        src = h_ref.at[pl.ds(t * RT, RT), :]
        for kk in range(TOP_K):
            d = dest_ref[0, 0, kk * TM + t]
            pltpu.make_async_copy(src, xb_hbm.at[pl.ds(d * RT, RT), :],
                                  sem.at[0]).start(priority=kk % 2)
        return carry
    lax.fori_loop(0, TM, body, 0, unroll=4)
    for kk in range(TOP_K):
        pltpu.make_async_copy(h_ref, xb_hbm.at[pl.ds(0, TM * RT), :], sem.at[0]).wait()

    @pl.when(i == pl.num_programs(0) - 1)
    def _():
        _zero_fill(lo_ref, hi_ref, xb_hbm, zeros, sem, n_blocks, wait=True)


def _dispatch(h_rt, dest, pad_lo, pad_hi, n_blocks):
    n_tiles = dest.shape[0]
    dest3 = dest[:, :TOP_K, :].reshape(n_tiles, 1, TOP_K * TM)
    grid_spec = pltpu.PrefetchScalarGridSpec(
        num_scalar_prefetch=2,
        grid=(n_tiles,),
        in_specs=[pl.BlockSpec((1, 1, TOP_K * TM), lambda i, lo, hi: (i, 0, 0),
                               memory_space=pltpu.SMEM),
                  pl.BlockSpec((TM * RT, LANES), lambda i, lo, hi: (i, 0))],
        out_specs=pl.BlockSpec(memory_space=pl.ANY),
        scratch_shapes=[pltpu.VMEM((BM * RT, LANES), F32),
                        pltpu.SemaphoreType.DMA((2,))],
    )
    xb = pl.pallas_call(
        functools.partial(_dispatch_kernel, n_blocks=n_blocks),
        grid_spec=grid_spec,
        out_shape=jax.ShapeDtypeStruct((n_blocks * BM * RT, LANES), F32),
        compiler_params=_cparams(("arbitrary",), VMEM_LIMIT, disable_bounds_checks=True),
        name="dispatch",
    )(pad_lo, pad_hi, dest3, h_rt)
    return xb, dest3


ROW_STEPS = tuple(range(BM // 4, BM + 1, BM // 4))


def _expert_kernel(bexp_ref, nused_ref, first_ref, wslot_ref, next_ref, valid_ref, xb_ref, w1_hbm, b1_ref,
                   w2_hbm, b2_ref, y_ref, w1s, w2s, w1b, w2b, sem, *, layer):
    i = pl.program_id(0)
    del nused_ref

    def weight_copies(e, slot):
        return (pltpu.make_async_copy(w1_hbm.at[layer, e], w1s.at[slot], sem.at[0, slot]),
                pltpu.make_async_copy(w2_hbm.at[layer, e], w2s.at[slot], sem.at[1, slot]))

    @pl.when(i == 0)
    def _():
        for cp in weight_copies(bexp_ref[0], 0):
            cp.start()

    @pl.when(first_ref[i] == 1)
    def _():
        slot = wslot_ref[i]
        for cp in weight_copies(bexp_ref[i], slot):
            cp.wait()
        w1b[...] = w1s[slot].astype(BF16)
        w2b[...] = w2s[slot].astype(BF16)

        @pl.when(next_ref[i] >= 0)
        def _():
            for cp in weight_copies(next_ref[i], 1 - slot):
                cp.start()

    valid = valid_ref[i]
    for lo, rows in zip((0,) + ROW_STEPS[:-1], ROW_STEPS):
        @pl.when(jnp.logical_and(valid > lo, valid <= rows))
        def _():
            x = jnp.concatenate(_from_row_tiled(xb_ref, rows), axis=1).astype(BF16)
            u = jnp.dot(x, w1b[...], preferred_element_type=F32) + b1_ref[...]
            glu = jnp.minimum(u[:, :D_FF], SWIGLU_LIMIT)
            lin = jnp.clip(u[:, D_FF:], -SWIGLU_LIMIT, SWIGLU_LIMIT)
            act = glu * _sigmoid(SWIGLU_ALPHA * glu) * (lin + 1.0)
            y = jnp.dot(act.astype(BF16), w2b[...], preferred_element_type=F32) + b2_ref[...]
            _to_row_tiled(y_ref.at[pl.ds(0, rows * RT), :], y)
            if rows < BM:
                y_ref[pl.ds(rows * RT, (BM - rows) * RT), :] = jnp.zeros(((BM - rows) * RT, LANES), F32)

    @pl.when(valid == 0)
    def _():
        y_ref[...] = jnp.zeros_like(y_ref)


def _experts(xb, block_exp, n_used, counts, pad_lo, layer, w1, b1, w2, b2):
    n_blocks = xb.shape[0] // (BM * RT)
    used = jnp.arange(n_blocks, dtype=jnp.int32) < n_used[0]
    changed = jnp.concatenate([jnp.ones((1,), bool), block_exp[1:] != block_exp[:-1]])
    first = jnp.logical_and(used, changed).astype(jnp.int32)
    wslot = (jnp.cumsum(first) - 1) % 2
    ar = jnp.arange(N_EXPERTS, dtype=jnp.int32)
    later = jnp.min(jnp.where((ar[None, :] > ar[:, None]) & (counts > 0)[None, :], ar[None, :],
                              N_EXPERTS), axis=1)
    of_block = block_exp[:, None] == ar[None, :]
    next_exp = jnp.sum(jnp.where(of_block, jnp.where(later == N_EXPERTS, -1, later)[None, :], 0), axis=1)
    starts = jnp.arange(n_blocks, dtype=jnp.int32) * BM
    valid = jnp.clip(jnp.sum(jnp.where(of_block, pad_lo[None, :], 0), axis=1) - starts, 0, BM)
    smap = lambda f: (lambda i, be, nu, fi, ws, nx, va: f(i, be, nu))
    grid_spec = pltpu.PrefetchScalarGridSpec(
        num_scalar_prefetch=6,
        grid=(n_blocks,),
        in_specs=[
            pl.BlockSpec((BM * RT, LANES), smap(lambda i, be, nu: (jnp.minimum(i, nu[0] - 1), 0))),
            pl.BlockSpec(memory_space=pl.ANY),
            pl.BlockSpec((None, 1, 2 * D_FF), smap(lambda i, be, nu: (layer * N_EXPERTS + be[i], 0, 0))),
            pl.BlockSpec(memory_space=pl.ANY),
            pl.BlockSpec((None, 1, D), smap(lambda i, be, nu: (layer * N_EXPERTS + be[i], 0, 0))),
        ],
        out_specs=pl.BlockSpec((BM * RT, LANES), smap(lambda i, be, nu: (i, 0))),
        scratch_shapes=[pltpu.VMEM((2, D, 2 * D_FF), F32),
                        pltpu.VMEM((2, D_FF, D), F32),
                        pltpu.VMEM((D, 2 * D_FF), BF16),
                        pltpu.VMEM((D_FF, D), BF16),
                        pltpu.SemaphoreType.DMA((2, 2))],
    )
    return pl.pallas_call(
        functools.partial(_expert_kernel, layer=layer),
        grid_spec=grid_spec,
        out_shape=jax.ShapeDtypeStruct(xb.shape, F32),
        compiler_params=_cparams(("arbitrary",), VMEM_LIMIT),
        name="experts",
    )(block_exp, n_used, first, wslot.astype(jnp.int32), next_exp.astype(jnp.int32),
      valid.astype(jnp.int32), xb, w1,
      b1.reshape(-1, 1, 2 * D_FF), w2, b2.reshape(-1, 1, D))


def _combine_kernel(dest_ref, dest_next_ref, y_hbm, gate_ref, x_ref, mod_ref, fg_ref, o_ref, buf, sem,
                    *, final_norm):
    i = pl.program_id(0)
    n = pl.num_programs(0)
    slot = i % 2
    rows = TOP_K * TM

    def gather(dest, to_slot):
        def body(p, carry):
            for half in range(2):
                r = 2 * p + half
                pltpu.make_async_copy(y_hbm.at[pl.ds(dest[0, 0, r] * RT, RT), :],
                                      buf.at[to_slot, pl.ds(r * RT, RT), :],
                                      sem.at[to_slot]).start(priority=half)
            return carry
        lax.fori_loop(0, rows // 2, body, 0, unroll=4)

    @pl.when(i == 0)
    def _():
        gather(dest_ref, 0)

    @pl.when(i + 1 < n)
    def _():
        gather(dest_next_ref, 1 - slot)

    pltpu.make_async_copy(y_hbm.at[pl.ds(0, rows * RT), :], buf.at[slot], sem.at[slot]).wait()
    eye = (lax.broadcasted_iota(jnp.int32, (8, 8), 0)
           == lax.broadcasted_iota(jnp.int32, (8, 8), 1)).astype(F32)
    gate = lax.dot_general(gate_ref[...], eye, _TN, precision=HIGHEST,
                           preferred_element_type=F32)
    cur = buf.at[slot]
    outs = []
    for j in range(RT):
        acc = None
        for kk in range(TOP_K):
            piece = gate[:, kk:kk + 1] * cur[pl.ds(kk * TM * RT + j, TM, stride=RT), :]
            acc = piece if acc is None else acc + piece
        outs.append(acc)
    out = x_ref[...] + mod_ref[5:6, :] * jnp.concatenate(outs, axis=1)
    if final_norm:
        out = _rms(out, fg_ref[...])
    o_ref[...] = out


def _combine(yb, dest3, gate, x_all, mods, layer, final_g, n_tiles, final_norm):
    last = n_tiles - 1
    tile = pl.BlockSpec((TM, D), lambda i: (i, 0))
    return pl.pallas_call(
        functools.partial(_combine_kernel, final_norm=final_norm),
        grid=(n_tiles,),
        in_specs=[pl.BlockSpec((1, 1, TOP_K * TM), lambda i: (i, 0, 0), memory_space=pltpu.SMEM),
                  pl.BlockSpec((1, 1, TOP_K * TM), lambda i: (jnp.minimum(i + 1, last), 0, 0),
                               memory_space=pltpu.SMEM),
                  pl.BlockSpec(memory_space=pl.ANY),
                  pl.BlockSpec((None, 8, TM), lambda i: (i, 0, 0)),
                  tile,
                  pl.BlockSpec((None, 6, D), lambda i: (layer * 8 + _tile_cond(i), 0, 0)),
                  pl.BlockSpec((1, D), lambda i: (0, 0))],
        out_specs=tile,
        out_shape=jax.ShapeDtypeStruct((n_tiles * TM, D), F32),
        scratch_shapes=[pltpu.VMEM((2, TOP_K * TM * RT, LANES), F32),
                        pltpu.SemaphoreType.DMA((2,))],
        compiler_params=_cparams(("arbitrary",), VMEM_LIMIT, disable_bounds_checks=True),
        name="combine",
    )(dest3, dest3, yb, gate, x_all, mods, final_g.reshape(1, D))


def _moe(x_all, mods, layer, norm_g, w_r, b_r, w1, b1, w2, b2, final_g, n_tiles, final_norm):
    n = n_tiles * TM
    h_rt, e, rank, gate, cnt = _route(x_all, mods, layer, norm_g, w_r, b_r, n_tiles)
    counts = cnt[:, 0].astype(jnp.int32)
    padded = (counts + BM - 1) // BM * BM
    pad_end = jnp.cumsum(padded)
    pad_start = pad_end - padded
    n_blocks = -(-(n * TOP_K) // BM) + N_EXPERTS
    starts = jnp.arange(n_blocks, dtype=jnp.int32) * BM
    block_exp = jnp.minimum(jnp.sum((pad_end[None, :] <= starts[:, None]).astype(jnp.int32), axis=1),
                            N_EXPERTS - 1)
    n_used = pad_end[-1:] // BM
    dest = _dest(pad_start, e, rank)
    xb, dest3 = _dispatch(h_rt, dest, pad_start + counts, pad_end, n_blocks)
    yb = _experts(xb, block_exp, n_used, counts, pad_start + counts, layer, w1, b1, w2, b2)
    return _combine(yb, dest3, gate, x_all, mods, layer, final_g, n_tiles, final_norm)


def kernel(x, c, ctx, c_ctx, mod_w, mod_b, norm1_g, norm2_g, fourier_w_in, fourier_w_out,
           hgrn_w_in, hgrn_lower_bounds, hgrn_norm_g, hgrn_w_out, router_w, router_b,
           expert_w1, expert_b1, expert_w2, expert_b2, final_norm_g):
    assert x.shape == (BATCH, SEQ, D) and ctx.shape == (BATCH, CTX_LEN, D)
    cond8 = jnp.zeros((8, D), F32).at[:BATCH].set(c).at[BATCH].set(c_ctx)
    mods = _adaln(cond8, mod_w, mod_b)
    x_all = jnp.concatenate([x.reshape(NLAT, D), ctx.reshape(NCTX, D)], axis=0)
    dch, mpos, mrow = _fourier_constants()
    experts = (expert_w1, expert_b1, expert_w2, expert_b2)

    vr, vi = _fourier_in(x_all, mods, 0, norm1_g[0], fourier_w_in[0], dch, mpos)
    x_all = _fourier_out(vr, vi, x_all, mods, 0, fourier_w_out[0], mrow)
    x_all = _moe(x_all, mods, 0, norm2_g[0], router_w[0], router_b[0], *experts,
                 final_norm_g, ALL_TILES, False)

    q, ff, fb, v, gs = _hgrn_in(x_all, mods, 1, norm1_g[1], hgrn_w_in[0], hgrn_lower_bounds)
    o_fw, o_bw = _scan(q, ff, fb, v)
    x_lat = _hgrn_out(o_fw, o_bw, gs, x_all, mods, 1, hgrn_norm_g[0], hgrn_w_out[0], LAT_TILES)
    out = _moe(x_lat, mods, 1, norm2_g[1], router_w[1], router_b[1], *experts,
               final_norm_g, LAT_TILES, True)
    return out.reshape(BATCH, SEQ, D)
```

```python
import functools

import numpy as np
import jax
import jax.numpy as jnp
from jax import lax
from jax.experimental import pallas as pl
from jax.experimental.pallas import tpu as pltpu

F32 = jnp.float32
BF16 = jnp.bfloat16
HIGHEST = lax.Precision.HIGHEST

D = 1024
BATCH = 2
SEQ = 8192
CTX_LEN = 256
GRID_W = 64
GRID_H = SEQ // GRID_W
NLAT = BATCH * SEQ
NCTX = BATCH * CTX_LEN
NTOK = NLAT + NCTX
TM = 256
LAT_TILES = NLAT // TM
ALL_TILES = NTOK // TM
TILES_PER_BATCH = SEQ // TM
FGROUPS = 4
FGDIM = D // FGROUPS
HEADS = 8
HDIM = D // HEADS
CHUNK = 128
N_EXPERTS = 32
TOP_K = 4
D_FF = 1024
SWIGLU_ALPHA = 1.702
SWIGLU_LIMIT = 7.0
BM = 512
LANES = 128
RT = D // LANES
NORM_EPS = 1e-6
VMEM_LIMIT = 56 * 1024 * 1024

_NT = (((1,), (1,)), ((), ()))
_TN = (((0,), (0,)), ((), ()))


def _cparams(sem, vmem=None, **kw):
    return pltpu.CompilerParams(dimension_semantics=sem, vmem_limit_bytes=vmem, **kw)


def _sigmoid(x):
    return 1.0 / (1.0 + jnp.exp(-x))


def _rms(x, g):
    return x * lax.rsqrt(jnp.mean(x * x, axis=-1, keepdims=True) + NORM_EPS) * g


def _tile_cond(i):
    return jnp.where(i < LAT_TILES, i // TILES_PER_BATCH, 2)


def _adaln_kernel(cond_ref, w_ref, b_ref, o_ref):
    c = cond_ref[...]
    s = c * _sigmoid(c)
    o_ref[...] = jnp.dot(s, w_ref[...], precision=HIGHEST,
                         preferred_element_type=F32) + b_ref[...]


def _adaln(cond8, mod_w, mod_b):
    depth = mod_w.shape[0]
    nb = 1536
    out = pl.pallas_call(
        _adaln_kernel,
        grid=(depth, 6 * D // nb),
        in_specs=[pl.BlockSpec((8, D), lambda l, j: (0, 0)),
                  pl.BlockSpec((None, D, nb), lambda l, j: (l, 0, j)),
                  pl.BlockSpec((None, 1, nb), lambda l, j: (l, 0, j))],
        out_specs=pl.BlockSpec((None, 8, nb), lambda l, j: (l, 0, j)),
        out_shape=jax.ShapeDtypeStruct((depth, 8, 6 * D), F32),
        compiler_params=_cparams(("arbitrary", "arbitrary"), VMEM_LIMIT),
        name="adaln",
    )(cond8, mod_w, mod_b.reshape(depth, 1, 6 * D))
    return out.reshape(depth * 8, 6, D)


def _dft_cs(n):
    k = np.arange(n)
    ang = 2.0 * np.pi * np.outer(k, k) / n
    s = 1.0 / np.sqrt(n)
    return np.cos(ang) * s, np.sin(ang) * s


def _fourier_constants():
    cd, sd = _dft_cs(FGDIM)
    dch = np.concatenate([cd, sd], axis=1)
    cc, sc = _dft_cs(GRID_W)
    eye = np.eye(TM // GRID_W)
    kc, ks = np.kron(eye, cc), np.kron(eye, sc)
    m_lat = np.block([[kc, -ks], [ks, kc]])
    cp, sp = _dft_cs(CTX_LEN)
    m_ctx = np.block([[cp, -sp], [sp, cp]])
    mpos = np.stack([m_lat, m_ctx])
    cr, sr = _dft_cs(GRID_H)
    mrow = np.concatenate([cr, -sr], axis=1)
    return (jnp.asarray(dch, BF16), jnp.asarray(mpos, BF16), jnp.asarray(mrow, BF16))


def _fourier_in_kernel(x_ref, mod_ref, g_ref, win_ref, dch_ref, mpos_ref, vr_ref, vi_ref):
    x = x_ref[...]
    h = _rms(x, g_ref[...]) * (1.0 + mod_ref[1:2, :]) + mod_ref[0:1, :]
    u = jnp.dot(h.astype(BF16), win_ref[...], preferred_element_type=F32).astype(BF16)
    parts = [jnp.dot(u[:, g * FGDIM:(g + 1) * FGDIM], dch_ref[...],
                     preferred_element_type=F32) for g in range(FGROUPS)]
    uc = jnp.concatenate([p[:, :FGDIM] for p in parts], axis=1)
    us = jnp.concatenate([p[:, FGDIM:] for p in parts], axis=1)
    st = jnp.concatenate([uc, us], axis=0).astype(BF16)
    v = jnp.dot(mpos_ref[...], st, preferred_element_type=F32)
    vr_ref[...] = v[:TM]
    vi_ref[...] = v[TM:]


def _fourier_in(x_all, mods, layer, norm_g, w_in, dch, mpos):
    tile = pl.BlockSpec((TM, D), lambda i: (i, 0))
    return pl.pallas_call(
        _fourier_in_kernel,
        grid=(ALL_TILES,),
        in_specs=[tile,
                  pl.BlockSpec((None, 6, D), lambda i: (layer * 8 + _tile_cond(i), 0, 0)),
                  pl.BlockSpec((1, D), lambda i: (0, 0)),
                  pl.BlockSpec((D, D), lambda i: (0, 0)),
                  pl.BlockSpec((FGDIM, 2 * FGDIM), lambda i: (0, 0)),
                  pl.BlockSpec((None, 2 * TM, 2 * TM), lambda i: (jnp.where(i < LAT_TILES, 0, 1), 0, 0))],
        out_specs=[tile, tile],
        out_shape=[jax.ShapeDtypeStruct((NTOK, D), F32)] * 2,
        compiler_params=_cparams(("arbitrary",), VMEM_LIMIT),
        name="fourier_in",
    )(x_all, mods, norm_g.reshape(1, D), w_in.astype(BF16), dch, mpos)


CB = 8


def _fourier_out_lat_kernel(vr_ref, vi_ref, x_ref, mrow_ref, wout_ref, mod_ref, o_ref):
    g1 = mod_ref[2:3, :]
    yf = []
    for c in range(CB):
        st = jnp.concatenate([vr_ref[:, c, :], vi_ref[:, c, :]], axis=0).astype(BF16)
        yf.append(jnp.dot(mrow_ref[...], st, preferred_element_type=F32).astype(BF16))
    y = jnp.dot(jnp.concatenate(yf, axis=0), wout_ref[...], preferred_element_type=F32)
    for c in range(CB):
        o_ref[:, c, :] = x_ref[:, c, :] + g1 * y[c * GRID_H:(c + 1) * GRID_H, :]


def _fourier_out_ctx_kernel(yr_ref, x_ref, wout_ref, mod_ref, o_ref):
    y = jnp.dot(yr_ref[...].astype(BF16), wout_ref[...], preferred_element_type=F32)
    o_ref[...] = x_ref[...] + mod_ref[2:3, :] * y


def _fourier_out(vr, vi, x_all, mods, layer, w_out, mrow):
    wout = w_out.astype(BF16)
    rows = NTOK // GRID_W
    v3 = lambda a: a.reshape(rows, GRID_W, D)
    blk = pl.BlockSpec((GRID_H, CB, D), lambda b, c: (b, c, 0))
    x_new = pl.pallas_call(
        _fourier_out_lat_kernel,
        grid=(BATCH, GRID_W // CB),
        in_specs=[blk, blk, blk,
                  pl.BlockSpec((GRID_H, 2 * GRID_H), lambda b, c: (0, 0)),
                  pl.BlockSpec((D, D), lambda b, c: (0, 0)),
                  pl.BlockSpec((None, 6, D), lambda b, c: (layer * 8 + b, 0, 0))],
        out_specs=blk,
        out_shape=jax.ShapeDtypeStruct((rows, GRID_W, D), F32),
        input_output_aliases={2: 0},
        compiler_params=_cparams(("arbitrary", "arbitrary"), VMEM_LIMIT),
        name="fourier_out_lat",
    )(v3(vr), v3(vi), v3(x_all), mrow, wout, mods).reshape(NTOK, D)
    ctile = pl.BlockSpec((TM, D), lambda i: (LAT_TILES + i, 0))
    return pl.pallas_call(
        _fourier_out_ctx_kernel,
        grid=(NCTX // TM,),
        in_specs=[ctile, ctile,
                  pl.BlockSpec((D, D), lambda i: (0, 0)),
                  pl.BlockSpec((None, 6, D), lambda i: (layer * 8 + 2, 0, 0))],
        out_specs=ctile,
        out_shape=jax.ShapeDtypeStruct((NTOK, D), F32),
        input_output_aliases={1: 0},
        compiler_params=_cparams(("arbitrary",), VMEM_LIMIT),
        name="fourier_out_ctx",
    )(vr, x_new, wout, mods)


def _hgrn_in_kernel(x_ref, mod_ref, g_ref, win_ref, hlb_ref, q_ref, ff_ref, fb_ref, v_ref, gs_ref,
                    *, layer):
    x = x_ref[...]
    h = (_rms(x, g_ref[...]) * (1.0 + mod_ref[1:2, :]) + mod_ref[0:1, :]).astype(BF16)
    raw = [hlb_ref[l] for l in range(hlb_ref.shape[0])]
    mx = functools.reduce(jnp.maximum, raw)
    ex = [jnp.exp(r - mx) for r in raw]
    den = functools.reduce(lambda a, b: a + b, ex)
    soft = [e / den for e in ex]
    lb = functools.reduce(lambda a, b: a + b, soft[:layer + 1]) - soft[0]

    def proj(j):
        return jnp.dot(h, win_ref[:, j * D:(j + 1) * D], preferred_element_type=F32)

    def per_head(ref, val):
        for hd in range(HEADS):
            ref[hd] = val[:, hd * HDIM:(hd + 1) * HDIM]

    q = proj(0)
    per_head(q_ref, q * _sigmoid(q))
    per_head(ff_ref, lb[0:1, :] + (1.0 - lb[0:1, :]) * _sigmoid(proj(1)))
    per_head(fb_ref, lb[1:2, :] + (1.0 - lb[1:2, :]) * _sigmoid(proj(2)))
    per_head(v_ref, proj(3))
    g = proj(4)
    gs_ref[...] = (g * _sigmoid(g)).astype(BF16)


def _hgrn_in(x_all, mods, layer, norm_g, w_in, hlb):
    tile = pl.BlockSpec((TM, D), lambda i: (i, 0))
    depth = hlb.shape[0]
    return pl.pallas_call(
        functools.partial(_hgrn_in_kernel, layer=layer),
        grid=(ALL_TILES,),
        in_specs=[tile,
                  pl.BlockSpec((None, 6, D), lambda i: (layer * 8 + _tile_cond(i), 0, 0)),
                  pl.BlockSpec((1, D), lambda i: (0, 0)),
                  pl.BlockSpec((D, 5 * D), lambda i: (0, 0)),
                  pl.BlockSpec((depth, 2, D), lambda i: (0, 0, 0))],
        out_specs=[pl.BlockSpec((HEADS, TM, HDIM), lambda i: (0, i, 0))] * 4 + [tile],
        out_shape=[jax.ShapeDtypeStruct((HEADS, NTOK, HDIM), F32)] * 4
                  + [jax.ShapeDtypeStruct((NTOK, D), BF16)],
        compiler_params=_cparams(("arbitrary",), VMEM_LIMIT),
        name="hgrn_in",
    )(x_all, mods, norm_g.reshape(1, D), w_in.astype(BF16), hlb)


N_LEVELS = 7
SUB = 8
NGRP = CHUNK // SUB
FINE_LEVELS = 3


def _scan_pair_kernel(qf_ref, ff_ref, vf_ref, qb_ref, fb_ref, vb_ref, of_ref, ob_ref, sf_ref, sb_ref,
                      rf_ref, rb_ref):
    @pl.when(pl.program_id(1) == 0)
    def _():
        sf_ref[...] = jnp.zeros_like(sf_ref)
        sb_ref[...] = jnp.zeros_like(sb_ref)

    _scan_chunk(qf_ref, ff_ref, vf_ref, of_ref, sf_ref, rf_ref, rev=False)
    _scan_chunk(qb_ref, fb_ref, vb_ref, ob_ref, sb_ref, rb_ref, rev=True)


PG = CHUNK // SUB


def _scan_chunk(q_ref, fg_ref, v_ref, o_ref, s_ref, relay_ref, *, rev):
    heads = [slice(h * HDIM, (h + 1) * HDIM) for h in range(HEADS)]

    def score(qs, ks, h):
        return jnp.dot(qs[:, heads[h]], ks[:, heads[h]].astype(BF16).T, preferred_element_type=F32)

    def by_residue(ref):
        return jnp.concatenate(
            [jnp.concatenate([ref[h, pl.ds(r, PG, stride=SUB), :] for r in range(SUB)], axis=0)
             for h in range(HEADS)], axis=1)

    def residue_groups(a):
        return [a[r * PG:(r + 1) * PG, :] for r in range(SUB)]

    q_p = by_residue(q_ref)
    fg_p = by_residue(fg_ref)
    k_p = 1.0 - fg_p
    v_p = by_residue(v_ref).astype(BF16)
    key = lax.broadcasted_iota(jnp.int32, (PG, CHUNK), 1)
    same_a = (key & (PG - 1)) == lax.broadcasted_iota(jnp.int32, (PG, CHUNK), 0)
    key_r = key >> (PG.bit_length() - 1)
    zero_p = jnp.zeros((PG, D), F32)
    sc_p = [[None] * SUB for _ in range(HEADS)]
    qd = q_p.astype(BF16)
    for h in range(HEADS):
        sc = score(qd, k_p, h)
        for r in range(SUB):
            sc_p[h][r] = jnp.where(same_a & (key_r == r), sc[r * PG:(r + 1) * PG, :], 0.0)
    qr_p = residue_groups(fg_p * q_p)
    kr_p = residue_groups(k_p)
    tot_p = residue_groups(fg_p)
    for l in range(FINE_LEVELS):
        bit = 1 << l
        is_far = [((r & bit) == 0) == rev for r in range(SUB)]
        far_groups = [r for r in range(SUB) if is_far[r]]
        qb = jnp.concatenate([qr_p[r] for r in far_groups], axis=0).astype(BF16)
        kb = jnp.concatenate([zero_p if is_far[r] else kr_p[r] for r in range(SUB)], axis=0)
        keep = [same_a & ((key_r >> (l + 1)) == (r >> (l + 1))) for r in far_groups]
        for h in range(HEADS):
            sc = score(qb, kb, h)
            for i, r in enumerate(far_groups):
                sc_p[h][r] = sc_p[h][r] + jnp.where(keep[i], sc[i * PG:(i + 1) * PG, :], 0.0)
        sib_p = [tot_p[r ^ bit] for r in range(SUB)]
        qr_p = [qr_p[r] * sib_p[r] if is_far[r] else qr_p[r] for r in range(SUB)]
        kr_p = [kr_p[r] if is_far[r] else kr_p[r] * sib_p[r] for r in range(SUB)]
        tot_p = [tot_p[r] * sib_p[r] for r in range(SUB)]
    for x, grp in enumerate((qr_p, kr_p, tot_p)):
        for h in range(HEADS):
            for r in range(SUB):
                relay_ref[x, h, pl.ds(r, PG, stride=SUB), :] = grp[r][:, heads[h]]
    qr, kr, tot = (jnp.concatenate([relay_ref[x, h] for h in range(HEADS)], axis=1)
                   for x in range(3))

    def groups(a):
        return [a[b * SUB:(b + 1) * SUB, :] for b in range(NGRP)]

    qr_g, kr_g, tot_g = (groups(a) for a in (qr, kr, tot))
    sc_g = [[None] * NGRP for _ in range(HEADS)]
    lane = lax.broadcasted_iota(jnp.int32, (SUB, CHUNK), 1)
    zero_g = jnp.zeros((SUB, D), F32)
    for l in range(FINE_LEVELS, N_LEVELS):
        bit = 1 << (l - FINE_LEVELS)
        is_far = [((b & bit) == 0) == rev for b in range(NGRP)]
        far_groups = [b for b in range(NGRP) if is_far[b]]
        qb = jnp.concatenate([qr_g[b] for b in far_groups], axis=0).astype(BF16)
        kb = jnp.concatenate([zero_g if is_far[b] else kr_g[b] for b in range(NGRP)], axis=0)
        span = 2 << l
        keep = [None if span == CHUNK else
                (lane >= b * SUB // span * span) & (lane < b * SUB // span * span + span)
                for b in far_groups]
        for h in range(HEADS):
            sc = score(qb, kb, h)
            for i, b in enumerate(far_groups):
                piece = sc[i * SUB:(i + 1) * SUB, :]
                if keep[i] is not None:
                    piece = jnp.where(keep[i], piece, 0.0)
                sc_g[h][b] = piece if sc_g[h][b] is None else sc_g[h][b] + piece
        sib_g = [tot_g[b ^ bit] for b in range(NGRP)]
        qr_g = [qr_g[b] * sib_g[b] if is_far[b] else qr_g[b] for b in range(NGRP)]
        kr_g = [kr_g[b] if is_far[b] else kr_g[b] * sib_g[b] for b in range(NGRP)]
        if l < N_LEVELS - 1:
            tot_g = [tot_g[b] * sib_g[b] for b in range(NGRP)]
        else:
            tot_row = tot_g[0][0:1, :] * sib_g[0][0:1, :]
    qin = jnp.concatenate(qr_g, axis=0).astype(BF16)
    kst = jnp.concatenate(kr_g, axis=0).astype(BF16)
    vb = jnp.concatenate([v_ref[h] for h in range(HEADS)], axis=1).astype(BF16)
    zero_s = jnp.zeros((SUB, CHUNK), F32)
    for h in range(HEADS):
        sl = heads[h]
        st = s_ref[h]
        sc = jnp.concatenate([zero_s if g is None else g for g in sc_g[h]], axis=0).astype(BF16)
        o_rows = (jnp.dot(sc, vb[:, sl], preferred_element_type=F32)
                  + jnp.dot(qin[:, sl], st.T.astype(BF16), preferred_element_type=F32))
        o_res = jnp.dot(jnp.concatenate(sc_p[h], axis=0).astype(BF16), v_p[:, sl],
                        preferred_element_type=F32)
        for r in range(SUB):
            relay_ref[3, h, pl.ds(r, PG, stride=SUB), :] = o_res[r * PG:(r + 1) * PG, :]
        o_ref[:, sl] = (o_rows + relay_ref[3, h]).astype(BF16)
        s_ref[h] = st * tot_row[:, sl] + lax.dot_general(vb[:, sl], kst[:, sl], _TN,
                                                         preferred_element_type=F32)


LAT_CHUNKS = SEQ // CHUNK
CTX_CHUNKS = CTX_LEN // CHUNK
SCAN_STEPS = CTX_CHUNKS + LAT_CHUNKS


def _scan(q, f_fw, f_bw, v):
    def idx_fw(b, s):
        return (jnp.where(s < CTX_CHUNKS, NLAT // CHUNK + CTX_CHUNKS * b + s,
                          LAT_CHUNKS * b + (s - CTX_CHUNKS)), 0)

    def idx_bw(b, s):
        return (jnp.where(s < CTX_CHUNKS, NLAT // CHUNK + CTX_CHUNKS * b + (CTX_CHUNKS - 1 - s),
                          LAT_CHUNKS * b + (SCAN_STEPS - 1 - s)), 0)
    fw = pl.BlockSpec((CHUNK, D), idx_fw)
    bw = pl.BlockSpec((CHUNK, D), idx_bw)
    fw_in = pl.BlockSpec((HEADS, CHUNK, HDIM), lambda b, s: (0, idx_fw(b, s)[0], 0))
    bw_in = pl.BlockSpec((HEADS, CHUNK, HDIM), lambda b, s: (0, idx_bw(b, s)[0], 0))
    state = pltpu.VMEM((HEADS, HDIM, HDIM), F32)
    relay = pltpu.VMEM((4, HEADS, CHUNK, HDIM), F32)
    return pl.pallas_call(
        _scan_pair_kernel,
        grid=(BATCH, SCAN_STEPS),
        in_specs=[fw_in, fw_in, fw_in, bw_in, bw_in, bw_in],
        out_specs=[fw, bw],
        out_shape=[jax.ShapeDtypeStruct((NTOK, D), BF16)] * 2,
        scratch_shapes=[state, state, relay, relay],
        compiler_params=_cparams(("arbitrary", "arbitrary"), VMEM_LIMIT),
        name="scan",
    )(q, f_fw, v, q, f_bw, v)


def _hgrn_out_kernel(of_ref, ob_ref, gs_ref, x_ref, ng_ref, wout_ref, mod_ref, o_ref):
    o = of_ref[...].astype(F32) + ob_ref[...].astype(F32)
    parts = []
    for h in range(HEADS):
        oh = o[:, h * HDIM:(h + 1) * HDIM]
        parts.append(oh * lax.rsqrt(jnp.mean(oh * oh, axis=-1, keepdims=True) + NORM_EPS))
    on = jnp.concatenate(parts, axis=1) * ng_ref[...]
    y = jnp.dot((on * gs_ref[...].astype(F32)).astype(BF16), wout_ref[...],
                preferred_element_type=F32)
    o_ref[...] = x_ref[...] + mod_ref[2:3, :] * y


def _hgrn_out(o_fw, o_bw, gs, x_all, mods, layer, norm_g, w_out, n_tiles):
    tile = pl.BlockSpec((TM, D), lambda i: (i, 0))
    return pl.pallas_call(
        _hgrn_out_kernel,
        grid=(n_tiles,),
        in_specs=[tile, tile, tile, tile,
                  pl.BlockSpec((1, D), lambda i: (0, 0)),
                  pl.BlockSpec((D, D), lambda i: (0, 0)),
                  pl.BlockSpec((None, 6, D), lambda i: (layer * 8 + _tile_cond(i), 0, 0))],
        out_specs=tile,
        out_shape=jax.ShapeDtypeStruct((n_tiles * TM, D), F32),
        compiler_params=_cparams(("arbitrary",), VMEM_LIMIT),
        name="hgrn_out",
    )(o_fw, o_bw, gs, x_all, norm_g.reshape(1, D), w_out.astype(BF16), mods)


def _to_row_tiled(ref, val):
    for j in range(RT):
        ref[pl.ds(j, val.shape[0], stride=RT), :] = val[:, j * LANES:(j + 1) * LANES]


def _from_row_tiled(ref, n, base=0):
    return [ref[pl.ds(base + j, n, stride=RT), :] for j in range(RT)]


def _route_kernel(x_ref, mod_ref, g_ref, wrt_ref, brt_ref, h_ref, e_ref, rank_ref, gate_ref, cnt_ref):
    @pl.when(pl.program_id(0) == 0)
    def _():
        cnt_ref[...] = jnp.zeros_like(cnt_ref)

    h = _rms(x_ref[...], g_ref[...]) * (1.0 + mod_ref[4:5, :]) + mod_ref[3:4, :]
    _to_row_tiled(h_ref, h)
    w = wrt_ref[...]
    w_hi = w.astype(BF16)
    w_lo = (w - w_hi.astype(F32)).astype(BF16)
    h_hi = h.astype(BF16)
    h_lo = (h - h_hi.astype(F32)).astype(BF16)
    part = lax.dot_general(jnp.concatenate([w_hi, w_lo], axis=0), h_hi, _NT,
                           preferred_element_type=F32)
    logits = (part[:N_EXPERTS] + part[N_EXPERTS:]
              + lax.dot_general(w_hi, h_lo, _NT, preferred_element_type=F32)
              + brt_ref[:, 0:1])
    row = lax.broadcasted_iota(jnp.int32, (N_EXPERTS, TM), 0).astype(F32)
    vals = logits
    sel = jnp.zeros((N_EXPERTS, TM), F32)
    tops, idxs = [], []
    for _ in range(TOP_K):
        m = jnp.max(vals, axis=0, keepdims=True)
        idx = jnp.min(jnp.where(vals == m, row, float(N_EXPERTS)), axis=0, keepdims=True)
        hit = row == idx
        vals = jnp.where(hit, -jnp.inf, vals)
        sel = jnp.where(hit, 1.0, sel)
        tops.append(m)
        idxs.append(idx)
    ex = [jnp.exp(m - tops[0]) for m in tops]
    den = ex[0] + ex[1] + ex[2] + ex[3]
    r = lax.broadcasted_iota(jnp.int32, (TM, TM), 0)
    c = lax.broadcasted_iota(jnp.int32, (TM, TM), 1)
    before = jnp.where(r < c, 1.0, 0.0).astype(BF16)
    pref = jnp.dot(sel.astype(BF16), before, preferred_element_type=F32) + cnt_ref[:, 0:1]
    slot = lax.broadcasted_iota(jnp.int32, (8, TM), 0)
    e_out = jnp.zeros((8, TM), F32)
    rank_out = jnp.zeros((8, TM), F32)
    gate_out = jnp.zeros((8, TM), F32)
    for kk in range(TOP_K):
        rank = jnp.sum(jnp.where(row == idxs[kk], pref, 0.0), axis=0, keepdims=True)
        e_out = jnp.where(slot == kk, idxs[kk], e_out)
        rank_out = jnp.where(slot == kk, rank, rank_out)
        gate_out = jnp.where(slot == kk, ex[kk] / den, gate_out)
    e_ref[...] = e_out.astype(jnp.int32)
    rank_ref[...] = rank_out.astype(jnp.int32)
    gate_ref[...] = gate_out
    cnt_ref[...] += jnp.sum(sel, axis=1, keepdims=True)


def _route(x_all, mods, layer, norm_g, w_r, b_r, n_tiles):
    tile = pl.BlockSpec((TM, D), lambda i: (i, 0))
    small = pl.BlockSpec((None, 8, TM), lambda i: (i, 0, 0))
    n = n_tiles * TM
    return pl.pallas_call(
        _route_kernel,
        grid=(n_tiles,),
        in_specs=[tile,
                  pl.BlockSpec((None, 6, D), lambda i: (layer * 8 + _tile_cond(i), 0, 0)),
                  pl.BlockSpec((1, D), lambda i: (0, 0)),
                  pl.BlockSpec((N_EXPERTS, D), lambda i: (0, 0)),
                  pl.BlockSpec((N_EXPERTS, LANES), lambda i: (0, 0))],
        out_specs=[pl.BlockSpec((TM * RT, LANES), lambda i: (i, 0)), small, small, small,
                   pl.BlockSpec((N_EXPERTS, LANES), lambda i: (0, 0))],
        out_shape=[jax.ShapeDtypeStruct((n * RT, LANES), F32),
                   jax.ShapeDtypeStruct((n_tiles, 8, TM), jnp.int32),
                   jax.ShapeDtypeStruct((n_tiles, 8, TM), jnp.int32),
                   jax.ShapeDtypeStruct((n_tiles, 8, TM), F32),
                   jax.ShapeDtypeStruct((N_EXPERTS, LANES), F32)],
        compiler_params=_cparams(("arbitrary",), VMEM_LIMIT),
        name="route",
    )(x_all, mods, norm_g.reshape(1, D), w_r.T, jnp.broadcast_to(b_r[:, None], (N_EXPERTS, LANES)))


def _dest_kernel(ps_ref, e_ref, rank_ref, d_ref):
    e = e_ref[...]
    acc = rank_ref[...]
    for j in range(N_EXPERTS):
        acc = acc + jnp.where(e == j, ps_ref[j], 0)
    d_ref[...] = acc


def _dest(pad_start, e, rank):
    full = pl.BlockSpec(e.shape, lambda i, ps: (0, 0, 0))
    return pl.pallas_call(
        _dest_kernel,
        grid_spec=pltpu.PrefetchScalarGridSpec(num_scalar_prefetch=1, grid=(1,),
                                               in_specs=[full, full], out_specs=full),
        out_shape=jax.ShapeDtypeStruct(e.shape, jnp.int32),
        name="dest",
    )(pad_start, e, rank)


PAD_PIECES = tuple(1 << p for p in reversed(range(BM.bit_length() - 1)))


def _zero_fill(lo_ref, hi_ref, xb_hbm, zeros, sem, n_blocks, wait):
    def go(rows, row0):
        cp = pltpu.make_async_copy(zeros.at[pl.ds(0, rows * RT), :],
                                   xb_hbm.at[pl.ds(row0 * RT, rows * RT), :], sem.at[1])
        cp.wait() if wait else cp.start()

    def per_expert(e, carry):
        row = lo_ref[e]
        n = hi_ref[e] - row
        for piece in PAD_PIECES:
            @pl.when((n & piece) != 0)
            def _():
                go(piece, row)
            row = row + (n & piece)
        return carry
    lax.fori_loop(0, N_EXPERTS, per_expert, 0)

    def per_block(b, carry):
        go(BM, b * BM)
        return carry
    lax.fori_loop(hi_ref[N_EXPERTS - 1] // BM, n_blocks, per_block, 0)


def _dispatch_kernel(lo_ref, hi_ref, dest_ref, h_ref, xb_hbm, zeros, sem, *, n_blocks):
    i = pl.program_id(0)

    @pl.when(i == 0)
    def _():
        zeros[...] = jnp.zeros_like(zeros)
        _zero_fill(lo_ref, hi_ref, xb_hbm, zeros, sem, n_blocks, wait=False)

    def body(t, carry):
        src = h_ref.at[pl.ds(t * RT, RT), :]
        for kk in range(TOP_K):
            d = dest_ref[0, 0, kk * TM + t]
            pltpu.make_async_copy(src, xb_hbm.at[pl.ds(d * RT, RT), :],
                                  sem.at[0]).start(priority=kk % 2)
        return carry
    lax.fori_loop(0, TM, body, 0, unroll=4)
    for kk in range(TOP_K):
        pltpu.make_async_copy(h_ref, xb_hbm.at[pl.ds(0, TM * RT), :], sem.at[0]).wait()

    @pl.when(i == pl.num_programs(0) - 1)
    def _():
        _zero_fill(lo_ref, hi_ref, xb_hbm, zeros, sem, n_blocks, wait=True)


def _dispatch(h_rt, dest, pad_lo, pad_hi, n_blocks):
    n_tiles = dest.shape[0]
    dest3 = dest[:, :TOP_K, :].reshape(n_tiles, 1, TOP_K * TM)
    grid_spec = pltpu.PrefetchScalarGridSpec(
        num_scalar_prefetch=2,
        grid=(n_tiles,),
        in_specs=[pl.BlockSpec((1, 1, TOP_K * TM), lambda i, lo, hi: (i, 0, 0),
                               memory_space=pltpu.SMEM),
                  pl.BlockSpec((TM * RT, LANES), lambda i, lo, hi: (i, 0))],
        out_specs=pl.BlockSpec(memory_space=pl.ANY),
        scratch_shapes=[pltpu.VMEM((BM * RT, LANES), F32),
                        pltpu.SemaphoreType.DMA((2,))],
    )
    xb = pl.pallas_call(
        functools.partial(_dispatch_kernel, n_blocks=n_blocks),
        grid_spec=grid_spec,
        out_shape=jax.ShapeDtypeStruct((n_blocks * BM * RT, LANES), F32),
        compiler_params=_cparams(("arbitrary",), VMEM_LIMIT, disable_bounds_checks=True),
        name="dispatch",
    )(pad_lo, pad_hi, dest3, h_rt)
    return xb, dest3


ROW_STEPS = tuple(range(BM // 4, BM + 1, BM // 4))


def _expert_kernel(bexp_ref, nused_ref, first_ref, wslot_ref, next_ref, valid_ref, xb_ref, w1_hbm, b1_ref,
                   w2_hbm, b2_ref, y_ref, w1s, w2s, w1b, w2b, sem, *, layer):
    i = pl.program_id(0)
    del nused_ref

    def weight_copies(e, slot):
        return (pltpu.make_async_copy(w1_hbm.at[layer, e], w1s.at[slot], sem.at[0, slot]),
                pltpu.make_async_copy(w2_hbm.at[layer, e], w2s.at[slot], sem.at[1, slot]))

    @pl.when(i == 0)
    def _():
        for cp in weight_copies(bexp_ref[0], 0):
            cp.start()

    @pl.when(first_ref[i] == 1)
    def _():
        slot = wslot_ref[i]
        for cp in weight_copies(bexp_ref[i], slot):
            cp.wait()
        w1b[...] = w1s[slot].astype(BF16)
        w2b[...] = w2s[slot].astype(BF16)

        @pl.when(next_ref[i] >= 0)
        def _():
            for cp in weight_copies(next_ref[i], 1 - slot):
                cp.start()

    valid = valid_ref[i]
    for lo, rows in zip((0,) + ROW_STEPS[:-1], ROW_STEPS):
        @pl.when(jnp.logical_and(valid > lo, valid <= rows))
        def _():
            x = jnp.concatenate(_from_row_tiled(xb_ref, rows), axis=1).astype(BF16)
            u = jnp.dot(x, w1b[...], preferred_element_type=F32) + b1_ref[...]
            glu = jnp.minimum(u[:, :D_FF], SWIGLU_LIMIT)
            lin = jnp.clip(u[:, D_FF:], -SWIGLU_LIMIT, SWIGLU_LIMIT)
            act = glu * _sigmoid(SWIGLU_ALPHA * glu) * (lin + 1.0)
            y = jnp.dot(act.astype(BF16), w2b[...], preferred_element_type=F32) + b2_ref[...]
            _to_row_tiled(y_ref.at[pl.ds(0, rows * RT), :], y)
            if rows < BM:
                y_ref[pl.ds(rows * RT, (BM - rows) * RT), :] = jnp.zeros(((BM - rows) * RT, LANES), F32)

    @pl.when(valid == 0)
    def _():
        y_ref[...] = jnp.zeros_like(y_ref)


def _experts(xb, block_exp, n_used, counts, pad_lo, layer, w1, b1, w2, b2):
    n_blocks = xb.shape[0] // (BM * RT)
    used = jnp.arange(n_blocks, dtype=jnp.int32) < n_used[0]
    changed = jnp.concatenate([jnp.ones((1,), bool), block_exp[1:] != block_exp[:-1]])
    first = jnp.logical_and(used, changed).astype(jnp.int32)
    wslot = (jnp.cumsum(first) - 1) % 2
    ar = jnp.arange(N_EXPERTS, dtype=jnp.int32)
    later = jnp.min(jnp.where((ar[None, :] > ar[:, None]) & (counts > 0)[None, :], ar[None, :],
                              N_EXPERTS), axis=1)
    of_block = block_exp[:, None] == ar[None, :]
    next_exp = jnp.sum(jnp.where(of_block, jnp.where(later == N_EXPERTS, -1, later)[None, :], 0), axis=1)
    starts = jnp.arange(n_blocks, dtype=jnp.int32) * BM
    valid = jnp.clip(jnp.sum(jnp.where(of_block, pad_lo[None, :], 0), axis=1) - starts, 0, BM)
    smap = lambda f: (lambda i, be, nu, fi, ws, nx, va: f(i, be, nu))
    grid_spec = pltpu.PrefetchScalarGridSpec(
        num_scalar_prefetch=6,
        grid=(n_blocks,),
        in_specs=[
            pl.BlockSpec((BM * RT, LANES), smap(lambda i, be, nu: (jnp.minimum(i, nu[0] - 1), 0))),
            pl.BlockSpec(memory_space=pl.ANY),
            pl.BlockSpec((None, 1, 2 * D_FF), smap(lambda i, be, nu: (layer * N_EXPERTS + be[i], 0, 0))),
            pl.BlockSpec(memory_space=pl.ANY),
            pl.BlockSpec((None, 1, D), smap(lambda i, be, nu: (layer * N_EXPERTS + be[i], 0, 0))),
        ],
        out_specs=pl.BlockSpec((BM * RT, LANES), smap(lambda i, be, nu: (i, 0))),
        scratch_shapes=[pltpu.VMEM((2, D, 2 * D_FF), F32),
                        pltpu.VMEM((2, D_FF, D), F32),
                        pltpu.VMEM((D, 2 * D_FF), BF16),
                        pltpu.VMEM((D_FF, D), BF16),
                        pltpu.SemaphoreType.DMA((2, 2))],
    )
    return pl.pallas_call(
        functools.partial(_expert_kernel, layer=layer),
        grid_spec=grid_spec,
        out_shape=jax.ShapeDtypeStruct(xb.shape, F32),
        compiler_params=_cparams(("arbitrary",), VMEM_LIMIT),
        name="experts",
    )(block_exp, n_used, first, wslot.astype(jnp.int32), next_exp.astype(jnp.int32),
      valid.astype(jnp.int32), xb, w1,
      b1.reshape(-1, 1, 2 * D_FF), w2, b2.reshape(-1, 1, D))


def _combine_kernel(dest_ref, dest_next_ref, y_hbm, gate_ref, x_ref, mod_ref, fg_ref, o_ref, buf, sem,
                    *, final_norm):
    i = pl.program_id(0)
    n = pl.num_programs(0)
    slot = i % 2
    rows = TOP_K * TM

    def gather(dest, to_slot):
        def body(p, carry):
            for half in range(2):
                r = 2 * p + half
                pltpu.make_async_copy(y_hbm.at[pl.ds(dest[0, 0, r] * RT, RT), :],
                                      buf.at[to_slot, pl.ds(r * RT, RT), :],
                                      sem.at[to_slot]).start(priority=half)
            return carry
        lax.fori_loop(0, rows // 2, body, 0, unroll=4)

    @pl.when(i == 0)
    def _():
        gather(dest_ref, 0)

    @pl.when(i + 1 < n)
    def _():
        gather(dest_next_ref, 1 - slot)

    pltpu.make_async_copy(y_hbm.at[pl.ds(0, rows * RT), :], buf.at[slot], sem.at[slot]).wait()
    eye = (lax.broadcasted_iota(jnp.int32, (8, 8), 0)
           == lax.broadcasted_iota(jnp.int32, (8, 8), 1)).astype(F32)
    gate = lax.dot_general(gate_ref[...], eye, _TN, precision=HIGHEST,
                           preferred_element_type=F32)
    cur = buf.at[slot]
    outs = []
    for j in range(RT):
        acc = None
        for kk in range(TOP_K):
            piece = gate[:, kk:kk + 1] * cur[pl.ds(kk * TM * RT + j, TM, stride=RT), :]
            acc = piece if acc is None else acc + piece
        outs.append(acc)
    out = x_ref[...] + mod_ref[5:6, :] * jnp.concatenate(outs, axis=1)
    if final_norm:
        out = _rms(out, fg_ref[...])
    o_ref[...] = out


def _combine(yb, dest3, gate, x_all, mods, layer, final_g, n_tiles, final_norm):
    last = n_tiles - 1
    tile = pl.BlockSpec((TM, D), lambda i: (i, 0))
    return pl.pallas_call(
        functools.partial(_combine_kernel, final_norm=final_norm),
        grid=(n_tiles,),
        in_specs=[pl.BlockSpec((1, 1, TOP_K * TM), lambda i: (i, 0, 0), memory_space=pltpu.SMEM),
                  pl.BlockSpec((1, 1, TOP_K * TM), lambda i: (jnp.minimum(i + 1, last), 0, 0),
                               memory_space=pltpu.SMEM),
                  pl.BlockSpec(memory_space=pl.ANY),
                  pl.BlockSpec((None, 8, TM), lambda i: (i, 0, 0)),
                  tile,
                  pl.BlockSpec((None, 6, D), lambda i: (layer * 8 + _tile_cond(i), 0, 0)),
                  pl.BlockSpec((1, D), lambda i: (0, 0))],
        out_specs=tile,
        out_shape=jax.ShapeDtypeStruct((n_tiles * TM, D), F32),
        scratch_shapes=[pltpu.VMEM((2, TOP_K * TM * RT, LANES), F32),
                        pltpu.SemaphoreType.DMA((2,))],
        compiler_params=_cparams(("arbitrary",), VMEM_LIMIT, disable_bounds_checks=True),
        name="combine",
    )(dest3, dest3, yb, gate, x_all, mods, final_g.reshape(1, D))


def _moe(x_all, mods, layer, norm_g, w_r, b_r, w1, b1, w2, b2, final_g, n_tiles, final_norm):
    n = n_tiles * TM
    h_rt, e, rank, gate, cnt = _route(x_all, mods, layer, norm_g, w_r, b_r, n_tiles)
    counts = cnt[:, 0].astype(jnp.int32)
    padded = (counts + BM - 1) // BM * BM
    pad_end = jnp.cumsum(padded)
    pad_start = pad_end - padded
    n_blocks = -(-(n * TOP_K) // BM) + N_EXPERTS
    starts = jnp.arange(n_blocks, dtype=jnp.int32) * BM
    block_exp = jnp.minimum(jnp.sum((pad_end[None, :] <= starts[:, None]).astype(jnp.int32), axis=1),
                            N_EXPERTS - 1)
    n_used = pad_end[-1:] // BM
    dest = _dest(pad_start, e, rank)
    xb, dest3 = _dispatch(h_rt, dest, pad_start + counts, pad_end, n_blocks)
    yb = _experts(xb, block_exp, n_used, counts, pad_start + counts, layer, w1, b1, w2, b2)
    return _combine(yb, dest3, gate, x_all, mods, layer, final_g, n_tiles, final_norm)


def kernel(x, c, ctx, c_ctx, mod_w, mod_b, norm1_g, norm2_g, fourier_w_in, fourier_w_out,
           hgrn_w_in, hgrn_lower_bounds, hgrn_norm_g, hgrn_w_out, router_w, router_b,
           expert_w1, expert_b1, expert_w2, expert_b2, final_norm_g):
    assert x.shape == (BATCH, SEQ, D) and ctx.shape == (BATCH, CTX_LEN, D)
    cond8 = jnp.zeros((8, D), F32).at[:BATCH].set(c).at[BATCH].set(c_ctx)
    mods = _adaln(cond8, mod_w, mod_b)
    x_all = jnp.concatenate([x.reshape(NLAT, D), ctx.reshape(NCTX, D)], axis=0)
    dch, mpos, mrow = _fourier_constants()
    experts = (expert_w1, expert_b1, expert_w2, expert_b2)

    vr, vi = _fourier_in(x_all, mods, 0, norm1_g[0], fourier_w_in[0], dch, mpos)
    x_all = _fourier_out(vr, vi, x_all, mods, 0, fourier_w_out[0], mrow)
    x_all = _moe(x_all, mods, 0, norm2_g[0], router_w[0], router_b[0], *experts,
                 final_norm_g, ALL_TILES, False)

    q, ff, fb, v, gs = _hgrn_in(x_all, mods, 1, norm1_g[1], hgrn_w_in[0], hgrn_lower_bounds)
    o_fw, o_bw = _scan(q, ff, fb, v)
    x_lat = _hgrn_out(o_fw, o_bw, gs, x_all, mods, 1, hgrn_norm_g[0], hgrn_w_out[0], LAT_TILES)
    out = _moe(x_lat, mods, 1, norm2_g[1], router_w[1], router_b[1], *experts,
               final_norm_g, LAT_TILES, True)
    return out.reshape(BATCH, SEQ, D)
```

```python
import functools

import numpy as np
import jax
import jax.numpy as jnp
from jax import lax
from jax.experimental import pallas as pl
from jax.experimental.pallas import tpu as pltpu

F32 = jnp.float32
BF16 = jnp.bfloat16
HIGHEST = lax.Precision.HIGHEST

D = 1024
BATCH = 2
SEQ = 8192
CTX_LEN = 256
GRID_W = 64
GRID_H = SEQ // GRID_W
NLAT = BATCH * SEQ
NCTX = BATCH * CTX_LEN
NTOK = NLAT + NCTX
TM = 256
LAT_TILES = NLAT // TM
ALL_TILES = NTOK // TM
TILES_PER_BATCH = SEQ // TM
FGROUPS = 4
FGDIM = D // FGROUPS
HEADS = 8
HDIM = D // HEADS
CHUNK = 128
N_EXPERTS = 32
TOP_K = 4
D_FF = 1024
SWIGLU_ALPHA = 1.702
SWIGLU_LIMIT = 7.0
BM = 512
LANES = 128
RT = D // LANES
NORM_EPS = 1e-6
VMEM_LIMIT = 56 * 1024 * 1024

_NT = (((1,), (1,)), ((), ()))
_TN = (((0,), (0,)), ((), ()))


def _cparams(sem, vmem=None, **kw):
    return pltpu.CompilerParams(dimension_semantics=sem, vmem_limit_bytes=vmem, **kw)


def _sigmoid(x):
    return 1.0 / (1.0 + jnp.exp(-x))


def _rms(x, g):
    return x * lax.rsqrt(jnp.mean(x * x, axis=-1, keepdims=True) + NORM_EPS) * g


def _tile_cond(i):
    return jnp.where(i < LAT_TILES, i // TILES_PER_BATCH, 2)


def _adaln_kernel(cond_ref, w_ref, b_ref, o_ref):
    c = cond_ref[...]
    s = c * _sigmoid(c)
    o_ref[...] = jnp.dot(s, w_ref[...], precision=HIGHEST,
                         preferred_element_type=F32) + b_ref[...]


def _adaln(cond8, mod_w, mod_b):
    depth = mod_w.shape[0]
    nb = 1536
    out = pl.pallas_call(
        _adaln_kernel,
        grid=(depth, 6 * D // nb),
        in_specs=[pl.BlockSpec((8, D), lambda l, j: (0, 0)),
                  pl.BlockSpec((None, D, nb), lambda l, j: (l, 0, j)),
                  pl.BlockSpec((None, 1, nb), lambda l, j: (l, 0, j))],
        out_specs=pl.BlockSpec((None, 8, nb), lambda l, j: (l, 0, j)),
        out_shape=jax.ShapeDtypeStruct((depth, 8, 6 * D), F32),
        compiler_params=_cparams(("arbitrary", "arbitrary"), VMEM_LIMIT),
        name="adaln",
    )(cond8, mod_w, mod_b.reshape(depth, 1, 6 * D))
    return out.reshape(depth * 8, 6, D)


def _dft_cs(n):
    k = np.arange(n)
    ang = 2.0 * np.pi * np.outer(k, k) / n
    s = 1.0 / np.sqrt(n)
    return np.cos(ang) * s, np.sin(ang) * s


def _fourier_constants():
    cd, sd = _dft_cs(FGDIM)
    dch = np.concatenate([cd, sd], axis=1)
    cc, sc = _dft_cs(GRID_W)
    eye = np.eye(TM // GRID_W)
    kc, ks = np.kron(eye, cc), np.kron(eye, sc)
    m_lat = np.block([[kc, -ks], [ks, kc]])
    cp, sp = _dft_cs(CTX_LEN)
    m_ctx = np.block([[cp, -sp], [sp, cp]])
    mpos = np.stack([m_lat, m_ctx])
    cr, sr = _dft_cs(GRID_H)
    mrow = np.concatenate([cr, -sr], axis=1)
    return (jnp.asarray(dch, BF16), jnp.asarray(mpos, BF16), jnp.asarray(mrow, BF16))


def _fourier_in_kernel(x_ref, mod_ref, g_ref, win_ref, dch_ref, mpos_ref, vr_ref, vi_ref):
    x = x_ref[...]
    h = _rms(x, g_ref[...]) * (1.0 + mod_ref[1:2, :]) + mod_ref[0:1, :]
    u = jnp.dot(h.astype(BF16), win_ref[...], preferred_element_type=F32).astype(BF16)
    parts = [jnp.dot(u[:, g * FGDIM:(g + 1) * FGDIM], dch_ref[...],
                     preferred_element_type=F32) for g in range(FGROUPS)]
    uc = jnp.concatenate([p[:, :FGDIM] for p in parts], axis=1)
    us = jnp.concatenate([p[:, FGDIM:] for p in parts], axis=1)
    st = jnp.concatenate([uc, us], axis=0).astype(BF16)
    v = jnp.dot(mpos_ref[...], st, preferred_element_type=F32)
    vr_ref[...] = v[:TM]
    vi_ref[...] = v[TM:]


def _fourier_in(x_all, mods, layer, norm_g, w_in, dch, mpos):
    tile = pl.BlockSpec((TM, D), lambda i: (i, 0))
    return pl.pallas_call(
        _fourier_in_kernel,
        grid=(ALL_TILES,),
        in_specs=[tile,
                  pl.BlockSpec((None, 6, D), lambda i: (layer * 8 + _tile_cond(i), 0, 0)),
                  pl.BlockSpec((1, D), lambda i: (0, 0)),
                  pl.BlockSpec((D, D), lambda i: (0, 0)),
                  pl.BlockSpec((FGDIM, 2 * FGDIM), lambda i: (0, 0)),
                  pl.BlockSpec((None, 2 * TM, 2 * TM), lambda i: (jnp.where(i < LAT_TILES, 0, 1), 0, 0))],
        out_specs=[tile, tile],
        out_shape=[jax.ShapeDtypeStruct((NTOK, D), F32)] * 2,
        compiler_params=_cparams(("arbitrary",), VMEM_LIMIT),
        name="fourier_in",
    )(x_all, mods, norm_g.reshape(1, D), w_in.astype(BF16), dch, mpos)


CB = 8


def _fourier_out_lat_kernel(vr_ref, vi_ref, x_ref, mrow_ref, wout_ref, mod_ref, o_ref):
    g1 = mod_ref[2:3, :]
    yf = []
    for c in range(CB):
        st = jnp.concatenate([vr_ref[:, c, :], vi_ref[:, c, :]], axis=0).astype(BF16)
        yf.append(jnp.dot(mrow_ref[...], st, preferred_element_type=F32).astype(BF16))
    y = jnp.dot(jnp.concatenate(yf, axis=0), wout_ref[...], preferred_element_type=F32)
    for c in range(CB):
        o_ref[:, c, :] = x_ref[:, c, :] + g1 * y[c * GRID_H:(c + 1) * GRID_H, :]


def _fourier_out_ctx_kernel(yr_ref, x_ref, wout_ref, mod_ref, o_ref):
    y = jnp.dot(yr_ref[...].astype(BF16), wout_ref[...], preferred_element_type=F32)
    o_ref[...] = x_ref[...] + mod_ref[2:3, :] * y


def _fourier_out(vr, vi, x_all, mods, layer, w_out, mrow):
    wout = w_out.astype(BF16)
    rows = NTOK // GRID_W
    v3 = lambda a: a.reshape(rows, GRID_W, D)
    blk = pl.BlockSpec((GRID_H, CB, D), lambda b, c: (b, c, 0))
    x_new = pl.pallas_call(
        _fourier_out_lat_kernel,
        grid=(BATCH, GRID_W // CB),
        in_specs=[blk, blk, blk,
                  pl.BlockSpec((GRID_H, 2 * GRID_H), lambda b, c: (0, 0)),
                  pl.BlockSpec((D, D), lambda b, c: (0, 0)),
                  pl.BlockSpec((None, 6, D), lambda b, c: (layer * 8 + b, 0, 0))],
        out_specs=blk,
        out_shape=jax.ShapeDtypeStruct((rows, GRID_W, D), F32),
        input_output_aliases={2: 0},
        compiler_params=_cparams(("arbitrary", "arbitrary"), VMEM_LIMIT),
        name="fourier_out_lat",
    )(v3(vr), v3(vi), v3(x_all), mrow, wout, mods).reshape(NTOK, D)
    ctile = pl.BlockSpec((TM, D), lambda i: (LAT_TILES + i, 0))
    return pl.pallas_call(
        _fourier_out_ctx_kernel,
        grid=(NCTX // TM,),
        in_specs=[ctile, ctile,
                  pl.BlockSpec((D, D), lambda i: (0, 0)),
                  pl.BlockSpec((None, 6, D), lambda i: (layer * 8 + 2, 0, 0))],
        out_specs=ctile,
        out_shape=jax.ShapeDtypeStruct((NTOK, D), F32),
        input_output_aliases={1: 0},
        compiler_params=_cparams(("arbitrary",), VMEM_LIMIT),
        name="fourier_out_ctx",
    )(vr, x_new, wout, mods)


def _hgrn_in_kernel(x_ref, mod_ref, g_ref, win_ref, hlb_ref, q_ref, ff_ref, fb_ref, v_ref, gs_ref,
                    *, layer):
    x = x_ref[...]
    h = (_rms(x, g_ref[...]) * (1.0 + mod_ref[1:2, :]) + mod_ref[0:1, :]).astype(BF16)
    raw = [hlb_ref[l] for l in range(hlb_ref.shape[0])]
    mx = functools.reduce(jnp.maximum, raw)
    ex = [jnp.exp(r - mx) for r in raw]
    den = functools.reduce(lambda a, b: a + b, ex)
    soft = [e / den for e in ex]
    lb = functools.reduce(lambda a, b: a + b, soft[:layer + 1]) - soft[0]

    def proj(j):
        return jnp.dot(h, win_ref[:, j * D:(j + 1) * D], preferred_element_type=F32)

    def per_head(ref, val):
        for hd in range(HEADS):
            ref[hd] = val[:, hd * HDIM:(hd + 1) * HDIM]

    q = proj(0)
    per_head(q_ref, q * _sigmoid(q))
    per_head(ff_ref, lb[0:1, :] + (1.0 - lb[0:1, :]) * _sigmoid(proj(1)))
    per_head(fb_ref, lb[1:2, :] + (1.0 - lb[1:2, :]) * _sigmoid(proj(2)))
    per_head(v_ref, proj(3))
    g = proj(4)
    gs_ref[...] = (g * _sigmoid(g)).astype(BF16)


def _hgrn_in(x_all, mods, layer, norm_g, w_in, hlb):
    tile = pl.BlockSpec((TM, D), lambda i: (i, 0))
    depth = hlb.shape[0]
    return pl.pallas_call(
        functools.partial(_hgrn_in_kernel, layer=layer),
        grid=(ALL_TILES,),
        in_specs=[tile,
                  pl.BlockSpec((None, 6, D), lambda i: (layer * 8 + _tile_cond(i), 0, 0)),
                  pl.BlockSpec((1, D), lambda i: (0, 0)),
                  pl.BlockSpec((D, 5 * D), lambda i: (0, 0)),
                  pl.BlockSpec((depth, 2, D), lambda i: (0, 0, 0))],
        out_specs=[pl.BlockSpec((HEADS, TM, HDIM), lambda i: (0, i, 0))] * 4 + [tile],
        out_shape=[jax.ShapeDtypeStruct((HEADS, NTOK, HDIM), F32)] * 4
                  + [jax.ShapeDtypeStruct((NTOK, D), BF16)],
        compiler_params=_cparams(("arbitrary",), VMEM_LIMIT),
        name="hgrn_in",
    )(x_all, mods, norm_g.reshape(1, D), w_in.astype(BF16), hlb)


N_LEVELS = 7
SUB = 8
NGRP = CHUNK // SUB
FINE_LEVELS = 3


def _scan_pair_kernel(qf_ref, ff_ref, vf_ref, qb_ref, fb_ref, vb_ref, of_ref, ob_ref, sf_ref, sb_ref,
                      rf_ref, rb_ref):
    @pl.when(pl.program_id(1) == 0)
    def _():
        sf_ref[...] = jnp.zeros_like(sf_ref)
        sb_ref[...] = jnp.zeros_like(sb_ref)

    _scan_chunk(qf_ref, ff_ref, vf_ref, of_ref, sf_ref, rf_ref, rev=False)
    _scan_chunk(qb_ref, fb_ref, vb_ref, ob_ref, sb_ref, rb_ref, rev=True)


PG = CHUNK // SUB


def _scan_chunk(q_ref, fg_ref, v_ref, o_ref, s_ref, relay_ref, *, rev):
    heads = [slice(h * HDIM, (h + 1) * HDIM) for h in range(HEADS)]

    def score(qs, ks, h):
        return jnp.dot(qs[:, heads[h]], ks[:, heads[h]].T.astype(BF16), preferred_element_type=F32)

    def by_residue(ref):
        return jnp.concatenate(
            [jnp.concatenate([ref[h, pl.ds(r, PG, stride=SUB), :] for r in range(SUB)], axis=0)
             for h in range(HEADS)], axis=1)

    def residue_groups(a):
        return [a[r * PG:(r + 1) * PG, :] for r in range(SUB)]

    q_p = by_residue(q_ref)
    fg_p = by_residue(fg_ref)
    k_p = 1.0 - fg_p
    v_p = by_residue(v_ref).astype(BF16)
    key = lax.broadcasted_iota(jnp.int32, (PG, CHUNK), 1)
    same_a = (key & (PG - 1)) == lax.broadcasted_iota(jnp.int32, (PG, CHUNK), 0)
    key_r = key >> (PG.bit_length() - 1)
    zero_p = jnp.zeros((PG, D), F32)
    sc_p = [[None] * SUB for _ in range(HEADS)]
    qd = q_p.astype(BF16)
    for h in range(HEADS):
        sc = score(qd, k_p, h)
        for r in range(SUB):
            sc_p[h][r] = jnp.where(same_a & (key_r == r), sc[r * PG:(r + 1) * PG, :], 0.0)
    qr_p = residue_groups(fg_p * q_p)
    kr_p = residue_groups(k_p)
    tot_p = residue_groups(fg_p)
    for l in range(FINE_LEVELS):
        bit = 1 << l
        is_far = [((r & bit) == 0) == rev for r in range(SUB)]
        far_groups = [r for r in range(SUB) if is_far[r]]
        qb = jnp.concatenate([qr_p[r] for r in far_groups], axis=0).astype(BF16)
        kb = jnp.concatenate([zero_p if is_far[r] else kr_p[r] for r in range(SUB)], axis=0)
        keep = [same_a & ((key_r >> (l + 1)) == (r >> (l + 1))) for r in far_groups]
        for h in range(HEADS):
            sc = score(qb, kb, h)
            for i, r in enumerate(far_groups):
                sc_p[h][r] = sc_p[h][r] + jnp.where(keep[i], sc[i * PG:(i + 1) * PG, :], 0.0)
        sib_p = [tot_p[r ^ bit] for r in range(SUB)]
        qr_p = [qr_p[r] * sib_p[r] if is_far[r] else qr_p[r] for r in range(SUB)]
        kr_p = [kr_p[r] if is_far[r] else kr_p[r] * sib_p[r] for r in range(SUB)]
        tot_p = [tot_p[r] * sib_p[r] for r in range(SUB)]
    for x, grp in enumerate((qr_p, kr_p, tot_p)):
        for h in range(HEADS):
            for r in range(SUB):
                relay_ref[x, h, pl.ds(r, PG, stride=SUB), :] = grp[r][:, heads[h]]
    qr, kr, tot = (jnp.concatenate([relay_ref[x, h] for h in range(HEADS)], axis=1)
                   for x in range(3))

    def groups(a):
        return [a[b * SUB:(b + 1) * SUB, :] for b in range(NGRP)]

    qr_g, kr_g, tot_g = (groups(a) for a in (qr, kr, tot))
    sc_g = [[None] * NGRP for _ in range(HEADS)]
    lane = lax.broadcasted_iota(jnp.int32, (SUB, CHUNK), 1)
    zero_g = jnp.zeros((SUB, D), F32)
    for l in range(FINE_LEVELS, N_LEVELS):
        bit = 1 << (l - FINE_LEVELS)
        is_far = [((b & bit) == 0) == rev for b in range(NGRP)]
        far_groups = [b for b in range(NGRP) if is_far[b]]
        qb = jnp.concatenate([qr_g[b] for b in far_groups], axis=0).astype(BF16)
        kb = jnp.concatenate([zero_g if is_far[b] else kr_g[b] for b in range(NGRP)], axis=0)
        span = 2 << l
        keep = [None if span == CHUNK else
                (lane >= b * SUB // span * span) & (lane < b * SUB // span * span + span)
                for b in far_groups]
        for h in range(HEADS):
            sc = score(qb, kb, h)
            for i, b in enumerate(far_groups):
                piece = sc[i * SUB:(i + 1) * SUB, :]
                if keep[i] is not None:
                    piece = jnp.where(keep[i], piece, 0.0)
                sc_g[h][b] = piece if sc_g[h][b] is None else sc_g[h][b] + piece
        sib_g = [tot_g[b ^ bit] for b in range(NGRP)]
        qr_g = [qr_g[b] * sib_g[b] if is_far[b] else qr_g[b] for b in range(NGRP)]
        kr_g = [kr_g[b] if is_far[b] else kr_g[b] * sib_g[b] for b in range(NGRP)]
        if l < N_LEVELS - 1:
            tot_g = [tot_g[b] * sib_g[b] for b in range(NGRP)]
        else:
            tot_row = tot_g[0][0:1, :] * sib_g[0][0:1, :]
    qin = jnp.concatenate(qr_g, axis=0).astype(BF16)
    kst = jnp.concatenate(kr_g, axis=0).astype(BF16)
    vb = jnp.concatenate([v_ref[h] for h in range(HEADS)], axis=1).astype(BF16)
    zero_s = jnp.zeros((SUB, CHUNK), F32)
    for h in range(HEADS):
        sl = heads[h]
        st = s_ref[h]
        sc = jnp.concatenate([zero_s if g is None else g for g in sc_g[h]], axis=0).astype(BF16)
        o_rows = (jnp.dot(sc, vb[:, sl], preferred_element_type=F32)
                  + jnp.dot(qin[:, sl], st.T.astype(BF16), preferred_element_type=F32))
        o_res = jnp.dot(jnp.concatenate(sc_p[h], axis=0).astype(BF16), v_p[:, sl],
                        preferred_element_type=F32)
        for r in range(SUB):
            relay_ref[3, h, pl.ds(r, PG, stride=SUB), :] = o_res[r * PG:(r + 1) * PG, :]
        o_ref[:, sl] = (o_rows + relay_ref[3, h]).astype(BF16)
        s_ref[h] = st * tot_row[:, sl] + lax.dot_general(vb[:, sl], kst[:, sl], _TN,
                                                         preferred_element_type=F32)


LAT_CHUNKS = SEQ // CHUNK
CTX_CHUNKS = CTX_LEN // CHUNK
SCAN_STEPS = CTX_CHUNKS + LAT_CHUNKS


def _scan(q, f_fw, f_bw, v):
    def idx_fw(b, s):
        return (jnp.where(s < CTX_CHUNKS, NLAT // CHUNK + CTX_CHUNKS * b + s,
                          LAT_CHUNKS * b + (s - CTX_CHUNKS)), 0)

    def idx_bw(b, s):
        return (jnp.where(s < CTX_CHUNKS, NLAT // CHUNK + CTX_CHUNKS * b + (CTX_CHUNKS - 1 - s),
                          LAT_CHUNKS * b + (SCAN_STEPS - 1 - s)), 0)
    fw = pl.BlockSpec((CHUNK, D), idx_fw)
    bw = pl.BlockSpec((CHUNK, D), idx_bw)
    fw_in = pl.BlockSpec((HEADS, CHUNK, HDIM), lambda b, s: (0, idx_fw(b, s)[0], 0))
    bw_in = pl.BlockSpec((HEADS, CHUNK, HDIM), lambda b, s: (0, idx_bw(b, s)[0], 0))
    state = pltpu.VMEM((HEADS, HDIM, HDIM), F32)
    relay = pltpu.VMEM((4, HEADS, CHUNK, HDIM), F32)
    return pl.pallas_call(
        _scan_pair_kernel,
        grid=(BATCH, SCAN_STEPS),
        in_specs=[fw_in, fw_in, fw_in, bw_in, bw_in, bw_in],
        out_specs=[fw, bw],
        out_shape=[jax.ShapeDtypeStruct((NTOK, D), BF16)] * 2,
        scratch_shapes=[state, state, relay, relay],
        compiler_params=_cparams(("arbitrary", "arbitrary"), VMEM_LIMIT),
        name="scan",
    )(q, f_fw, v, q, f_bw, v)


def _hgrn_out_kernel(of_ref, ob_ref, gs_ref, x_ref, ng_ref, wout_ref, mod_ref, o_ref):
    o = of_ref[...].astype(F32) + ob_ref[...].astype(F32)
    parts = []
    for h in range(HEADS):
        oh = o[:, h * HDIM:(h + 1) * HDIM]
        parts.append(oh * lax.rsqrt(jnp.mean(oh * oh, axis=-1, keepdims=True) + NORM_EPS))
    on = jnp.concatenate(parts, axis=1) * ng_ref[...]
    y = jnp.dot((on * gs_ref[...].astype(F32)).astype(BF16), wout_ref[...],
                preferred_element_type=F32)
    o_ref[...] = x_ref[...] + mod_ref[2:3, :] * y


def _hgrn_out(o_fw, o_bw, gs, x_all, mods, layer, norm_g, w_out, n_tiles):
    tile = pl.BlockSpec((2 * TM, D), lambda i: (i, 0))
    return pl.pallas_call(
        _hgrn_out_kernel,
        grid=(n_tiles // 2,),
        in_specs=[tile, tile, tile, tile,
                  pl.BlockSpec((1, D), lambda i: (0, 0)),
                  pl.BlockSpec((D, D), lambda i: (0, 0)),
                  pl.BlockSpec((None, 6, D), lambda i: (layer * 8 + _tile_cond(2 * i), 0, 0))],
        out_specs=tile,
        out_shape=jax.ShapeDtypeStruct((n_tiles * TM, D), F32),
        compiler_params=_cparams(("arbitrary",), VMEM_LIMIT),
        name="hgrn_out",
    )(o_fw, o_bw, gs, x_all, norm_g.reshape(1, D), w_out.astype(BF16), mods)


def _to_row_tiled(ref, val):
    for j in range(RT):
        ref[pl.ds(j, val.shape[0], stride=RT), :] = val[:, j * LANES:(j + 1) * LANES]


def _from_row_tiled(ref, n, base=0):
    return [ref[pl.ds(base + j, n, stride=RT), :] for j in range(RT)]


def _route_kernel(x_ref, mod_ref, g_ref, wrt_ref, brt_ref, h_ref, e_ref, rank_ref, gate_ref, cnt_ref):
    @pl.when(pl.program_id(0) == 0)
    def _():
        cnt_ref[...] = jnp.zeros_like(cnt_ref)

    h = _rms(x_ref[...], g_ref[...]) * (1.0 + mod_ref[4:5, :]) + mod_ref[3:4, :]
    _to_row_tiled(h_ref, h)
    w = wrt_ref[...]
    w_hi = w.astype(BF16)
    w_lo = (w - w_hi.astype(F32)).astype(BF16)
    h_hi = h.astype(BF16)
    h_lo = (h - h_hi.astype(F32)).astype(BF16)
    part = lax.dot_general(jnp.concatenate([w_hi, w_lo], axis=0), h_hi, _NT,
                           preferred_element_type=F32)
    logits = (part[:N_EXPERTS] + part[N_EXPERTS:]
              + lax.dot_general(w_hi, h_lo, _NT, preferred_element_type=F32)
              + brt_ref[:, 0:1])
    row = lax.broadcasted_iota(jnp.int32, (N_EXPERTS, TM), 0).astype(F32)
    vals = logits
    sel = jnp.zeros((N_EXPERTS, TM), F32)
    tops, idxs = [], []
    for _ in range(TOP_K):
        m = jnp.max(vals, axis=0, keepdims=True)
        idx = jnp.min(jnp.where(vals == m, row, float(N_EXPERTS)), axis=0, keepdims=True)
        hit = row == idx
        vals = jnp.where(hit, -jnp.inf, vals)
        sel = jnp.where(hit, 1.0, sel)
        tops.append(m)
        idxs.append(idx)
    ex = [jnp.exp(m - tops[0]) for m in tops]
    den = ex[0] + ex[1] + ex[2] + ex[3]
    r = lax.broadcasted_iota(jnp.int32, (TM, TM), 0)
    c = lax.broadcasted_iota(jnp.int32, (TM, TM), 1)
    before = jnp.where(r < c, 1.0, 0.0).astype(BF16)
    pref = jnp.dot(sel.astype(BF16), before, preferred_element_type=F32) + cnt_ref[:, 0:1]
    slot = lax.broadcasted_iota(jnp.int32, (8, TM), 0)
    e_out = jnp.zeros((8, TM), F32)
    rank_out = jnp.zeros((8, TM), F32)
    gate_out = jnp.zeros((8, TM), F32)
    for kk in range(TOP_K):
        rank = jnp.sum(jnp.where(row == idxs[kk], pref, 0.0), axis=0, keepdims=True)
        e_out = jnp.where(slot == kk, idxs[kk], e_out)
        rank_out = jnp.where(slot == kk, rank, rank_out)
        gate_out = jnp.where(slot == kk, ex[kk] / den, gate_out)
    e_ref[...] = e_out.astype(jnp.int32)
    rank_ref[...] = rank_out.astype(jnp.int32)
    gate_ref[...] = gate_out
    cnt_ref[...] += jnp.sum(sel, axis=1, keepdims=True)


def _route(x_all, mods, layer, norm_g, w_r, b_r, n_tiles):
    tile = pl.BlockSpec((TM, D), lambda i: (i, 0))
    small = pl.BlockSpec((None, 8, TM), lambda i: (i, 0, 0))
    n = n_tiles * TM
    return pl.pallas_call(
        _route_kernel,
        grid=(n_tiles,),
        in_specs=[tile,
                  pl.BlockSpec((None, 6, D), lambda i: (layer * 8 + _tile_cond(i), 0, 0)),
                  pl.BlockSpec((1, D), lambda i: (0, 0)),
                  pl.BlockSpec((N_EXPERTS, D), lambda i: (0, 0)),
                  pl.BlockSpec((N_EXPERTS, LANES), lambda i: (0, 0))],
        out_specs=[pl.BlockSpec((TM * RT, LANES), lambda i: (i, 0)), small, small, small,
                   pl.BlockSpec((N_EXPERTS, LANES), lambda i: (0, 0))],
        out_shape=[jax.ShapeDtypeStruct((n * RT, LANES), F32),
                   jax.ShapeDtypeStruct((n_tiles, 8, TM), jnp.int32),
                   jax.ShapeDtypeStruct((n_tiles, 8, TM), jnp.int32),
                   jax.ShapeDtypeStruct((n_tiles, 8, TM), F32),
                   jax.ShapeDtypeStruct((N_EXPERTS, LANES), F32)],
        compiler_params=_cparams(("arbitrary",), VMEM_LIMIT),
        name="route",
    )(x_all, mods, norm_g.reshape(1, D), w_r.T, jnp.broadcast_to(b_r[:, None], (N_EXPERTS, LANES)))


def _dest_kernel(ps_ref, e_ref, rank_ref, d_ref):
    e = e_ref[...]
    acc = rank_ref[...]
    for j in range(N_EXPERTS):
        acc = acc + jnp.where(e == j, ps_ref[j], 0)
    d_ref[...] = acc


def _dest(pad_start, e, rank):
    full = pl.BlockSpec(e.shape, lambda i, ps: (0, 0, 0))
    return pl.pallas_call(
        _dest_kernel,
        grid_spec=pltpu.PrefetchScalarGridSpec(num_scalar_prefetch=1, grid=(1,),
                                               in_specs=[full, full], out_specs=full),
        out_shape=jax.ShapeDtypeStruct(e.shape, jnp.int32),
        name="dest",
    )(pad_start, e, rank)


PAD_PIECES = tuple(1 << p for p in reversed(range(BM.bit_length() - 1)))


def _zero_fill(lo_ref, hi_ref, xb_hbm, zeros, sem, n_blocks, wait):
    def go(rows, row0):
        cp = pltpu.make_async_copy(zeros.at[pl.ds(0, rows * RT), :],
                                   xb_hbm.at[pl.ds(row0 * RT, rows * RT), :], sem.at[1])
        cp.wait() if wait else cp.start()

    def per_expert(e, carry):
        row = lo_ref[e]
        n = hi_ref[e] - row
        for piece in PAD_PIECES:
            @pl.when((n & piece) != 0)
            def _():
                go(piece, row)
            row = row + (n & piece)
        return carry
    lax.fori_loop(0, N_EXPERTS, per_expert, 0)

    def per_block(b, carry):
        go(BM, b * BM)
        return carry
    lax.fori_loop(hi_ref[N_EXPERTS - 1] // BM, n_blocks, per_block, 0)


def _dispatch_kernel(lo_ref, hi_ref, dest_ref, h_ref, xb_hbm, zeros, sem, *, n_blocks):
    i = pl.program_id(0)

    @pl.when(i == 0)
    def _():
        zeros[...] = jnp.zeros_like(zeros)
        _zero_fill(lo_ref, hi_ref, xb_hbm, zeros, sem, n_blocks, wait=False)

    def body(t, carry):
        src = h_ref.at[pl.ds(t * RT, RT), :]
        for kk in range(TOP_K):
            d = dest_ref[0, 0, kk * TM + t]
            pltpu.make_async_copy(src, xb_hbm.at[pl.ds(d * RT, RT), :],
                                  sem.at[0]).start(priority=kk % 2)
        return carry
    lax.fori_loop(0, TM, body, 0, unroll=4)
    for kk in range(TOP_K):
        pltpu.make_async_copy(h_ref, xb_hbm.at[pl.ds(0, TM * RT), :], sem.at[0]).wait()

    @pl.when(i == pl.num_programs(0) - 1)
    def _():
        _zero_fill(lo_ref, hi_ref, xb_hbm, zeros, sem, n_blocks, wait=True)


def _dispatch(h_rt, dest, pad_lo, pad_hi, n_blocks):
    n_tiles = dest.shape[0]
    dest3 = dest[:, :TOP_K, :].reshape(n_tiles, 1, TOP_K * TM)
    grid_spec = pltpu.PrefetchScalarGridSpec(
        num_scalar_prefetch=2,
        grid=(n_tiles,),
        in_specs=[pl.BlockSpec((1, 1, TOP_K * TM), lambda i, lo, hi: (i, 0, 0),
                               memory_space=pltpu.SMEM),
                  pl.BlockSpec((TM * RT, LANES), lambda i, lo, hi: (i, 0))],
        out_specs=pl.BlockSpec(memory_space=pl.ANY),
        scratch_shapes=[pltpu.VMEM((BM * RT, LANES), F32),
                        pltpu.SemaphoreType.DMA((2,))],
    )
    xb = pl.pallas_call(
        functools.partial(_dispatch_kernel, n_blocks=n_blocks),
        grid_spec=grid_spec,
        out_shape=jax.ShapeDtypeStruct((n_blocks * BM * RT, LANES), F32),
        compiler_params=_cparams(("arbitrary",), VMEM_LIMIT, disable_bounds_checks=True),
        name="dispatch",
    )(pad_lo, pad_hi, dest3, h_rt)
    return xb, dest3


ROW_STEPS = tuple(range(BM // 4, BM + 1, BM // 4))


def _expert_kernel(bexp_ref, nused_ref, first_ref, wslot_ref, next_ref, valid_ref, xb_ref, w1_hbm, b1_ref,
                   w2_hbm, b2_ref, y_ref, w1s, w2s, w1b, w2b, sem, *, layer):
    i = pl.program_id(0)
    del nused_ref

    def weight_copies(e, slot):
        return (pltpu.make_async_copy(w1_hbm.at[layer, e], w1s.at[slot], sem.at[0, slot]),
                pltpu.make_async_copy(w2_hbm.at[layer, e], w2s.at[slot], sem.at[1, slot]))

    @pl.when(i == 0)
    def _():
        for cp in weight_copies(bexp_ref[0], 0):
            cp.start()

    @pl.when(first_ref[i] == 1)
    def _():
        slot = wslot_ref[i]
        for cp in weight_copies(bexp_ref[i], slot):
            cp.wait()
        w1b[...] = w1s[slot].astype(BF16)
        w2b[...] = w2s[slot].astype(BF16)

        @pl.when(next_ref[i] >= 0)
        def _():
            for cp in weight_copies(next_ref[i], 1 - slot):
                cp.start()

    valid = valid_ref[i]
    for lo, rows in zip((0,) + ROW_STEPS[:-1], ROW_STEPS):
        @pl.when(jnp.logical_and(valid > lo, valid <= rows))
        def _():
            x = jnp.concatenate(_from_row_tiled(xb_ref, rows), axis=1).astype(BF16)
            u = jnp.dot(x, w1b[...], preferred_element_type=F32) + b1_ref[...]
            glu = jnp.minimum(u[:, :D_FF], SWIGLU_LIMIT)
            lin = jnp.clip(u[:, D_FF:], -SWIGLU_LIMIT, SWIGLU_LIMIT)
            act = glu * _sigmoid(SWIGLU_ALPHA * glu) * (lin + 1.0)
            y = jnp.dot(act.astype(BF16), w2b[...], preferred_element_type=F32) + b2_ref[...]
            _to_row_tiled(y_ref.at[pl.ds(0, rows * RT), :], y)
            if rows < BM:
                y_ref[pl.ds(rows * RT, (BM - rows) * RT), :] = jnp.zeros(((BM - rows) * RT, LANES), F32)

    @pl.when(valid == 0)
    def _():
        y_ref[...] = jnp.zeros_like(y_ref)


def _experts(xb, block_exp, n_used, counts, pad_lo, layer, w1, b1, w2, b2):
    n_blocks = xb.shape[0] // (BM * RT)
    used = jnp.arange(n_blocks, dtype=jnp.int32) < n_used[0]
    changed = jnp.concatenate([jnp.ones((1,), bool), block_exp[1:] != block_exp[:-1]])
    first = jnp.logical_and(used, changed).astype(jnp.int32)
    wslot = (jnp.cumsum(first) - 1) % 2
    ar = jnp.arange(N_EXPERTS, dtype=jnp.int32)
    later = jnp.min(jnp.where((ar[None, :] > ar[:, None]) & (counts > 0)[None, :], ar[None, :],
                              N_EXPERTS), axis=1)
    of_block = block_exp[:, None] == ar[None, :]
    next_exp = jnp.sum(jnp.where(of_block, jnp.where(later == N_EXPERTS, -1, later)[None, :], 0), axis=1)
    starts = jnp.arange(n_blocks, dtype=jnp.int32) * BM
    valid = jnp.clip(jnp.sum(jnp.where(of_block, pad_lo[None, :], 0), axis=1) - starts, 0, BM)
    smap = lambda f: (lambda i, be, nu, fi, ws, nx, va: f(i, be, nu))
    grid_spec = pltpu.PrefetchScalarGridSpec(
        num_scalar_prefetch=6,
        grid=(n_blocks,),
        in_specs=[
            pl.BlockSpec((BM * RT, LANES), smap(lambda i, be, nu: (jnp.minimum(i, nu[0] - 1), 0))),
            pl.BlockSpec(memory_space=pl.ANY),
            pl.BlockSpec((None, 1, 2 * D_FF), smap(lambda i, be, nu: (layer * N_EXPERTS + be[i], 0, 0))),
            pl.BlockSpec(memory_space=pl.ANY),
            pl.BlockSpec((None, 1, D), smap(lambda i, be, nu: (layer * N_EXPERTS + be[i], 0, 0))),
        ],
        out_specs=pl.BlockSpec((BM * RT, LANES), smap(lambda i, be, nu: (i, 0))),
        scratch_shapes=[pltpu.VMEM((2, D, 2 * D_FF), F32),
                        pltpu.VMEM((2, D_FF, D), F32),
                        pltpu.VMEM((D, 2 * D_FF), BF16),
                        pltpu.VMEM((D_FF, D), BF16),
                        pltpu.SemaphoreType.DMA((2, 2))],
    )
    return pl.pallas_call(
        functools.partial(_expert_kernel, layer=layer),
        grid_spec=grid_spec,
        out_shape=jax.ShapeDtypeStruct(xb.shape, F32),
        compiler_params=_cparams(("arbitrary",), VMEM_LIMIT),
        name="experts",
    )(block_exp, n_used, first, wslot.astype(jnp.int32), next_exp.astype(jnp.int32),
      valid.astype(jnp.int32), xb, w1,
      b1.reshape(-1, 1, 2 * D_FF), w2, b2.reshape(-1, 1, D))


def _combine_kernel(dest_ref, dest_next_ref, y_hbm, gate_ref, x_ref, mod_ref, fg_ref, o_ref, buf, sem,
                    *, final_norm):
    i = pl.program_id(0)
    n = pl.num_programs(0)
    slot = i % 2
    rows = TOP_K * TM

    def gather(dest, to_slot):
        def body(p, carry):
            for half in range(2):
                r = 2 * p + half
                pltpu.make_async_copy(y_hbm.at[pl.ds(dest[0, 0, r] * RT, RT), :],
                                      buf.at[to_slot, pl.ds(r * RT, RT), :],
                                      sem.at[to_slot]).start(priority=half)
            return carry
        lax.fori_loop(0, rows // 2, body, 0, unroll=4)

    @pl.when(i == 0)
    def _():
        gather(dest_ref, 0)

    @pl.when(i + 1 < n)
    def _():
        gather(dest_next_ref, 1 - slot)

    pltpu.make_async_copy(y_hbm.at[pl.ds(0, rows * RT), :], buf.at[slot], sem.at[slot]).wait()
    eye = (lax.broadcasted_iota(jnp.int32, (8, 8), 0)
           == lax.broadcasted_iota(jnp.int32, (8, 8), 1)).astype(F32)
    gate = lax.dot_general(gate_ref[...], eye, _TN, precision=HIGHEST,
                           preferred_element_type=F32)
    cur = buf.at[slot]
    outs = []
    for j in range(RT):
        acc = None
        for kk in range(TOP_K):
            piece = gate[:, kk:kk + 1] * cur[pl.ds(kk * TM * RT + j, TM, stride=RT), :]
            acc = piece if acc is None else acc + piece
        outs.append(acc)
    out = x_ref[...] + mod_ref[5:6, :] * jnp.concatenate(outs, axis=1)
    if final_norm:
        out = _rms(out, fg_ref[...])
    o_ref[...] = out


def _combine(yb, dest3, gate, x_all, mods, layer, final_g, n_tiles, final_norm):
    last = n_tiles - 1
    tile = pl.BlockSpec((TM, D), lambda i: (i, 0))
    return pl.pallas_call(
        functools.partial(_combine_kernel, final_norm=final_norm),
        grid=(n_tiles,),
        in_specs=[pl.BlockSpec((1, 1, TOP_K * TM), lambda i: (i, 0, 0), memory_space=pltpu.SMEM),
                  pl.BlockSpec((1, 1, TOP_K * TM), lambda i: (jnp.minimum(i + 1, last), 0, 0),
                               memory_space=pltpu.SMEM),
                  pl.BlockSpec(memory_space=pl.ANY),
                  pl.BlockSpec((None, 8, TM), lambda i: (i, 0, 0)),
                  tile,
                  pl.BlockSpec((None, 6, D), lambda i: (layer * 8 + _tile_cond(i), 0, 0)),
                  pl.BlockSpec((1, D), lambda i: (0, 0))],
        out_specs=tile,
        out_shape=jax.ShapeDtypeStruct((n_tiles * TM, D), F32),
        scratch_shapes=[pltpu.VMEM((2, TOP_K * TM * RT, LANES), F32),
                        pltpu.SemaphoreType.DMA((2,))],
        compiler_params=_cparams(("arbitrary",), VMEM_LIMIT, disable_bounds_checks=True),
        name="combine",
    )(dest3, dest3, yb, gate, x_all, mods, final_g.reshape(1, D))


def _moe(x_all, mods, layer, norm_g, w_r, b_r, w1, b1, w2, b2, final_g, n_tiles, final_norm):
    n = n_tiles * TM
    h_rt, e, rank, gate, cnt = _route(x_all, mods, layer, norm_g, w_r, b_r, n_tiles)
    counts = cnt[:, 0].astype(jnp.int32)
    padded = (counts + BM - 1) // BM * BM
    pad_end = jnp.cumsum(padded)
    pad_start = pad_end - padded
    n_blocks = -(-(n * TOP_K) // BM) + N_EXPERTS
    starts = jnp.arange(n_blocks, dtype=jnp.int32) * BM
    block_exp = jnp.minimum(jnp.sum((pad_end[None, :] <= starts[:, None]).astype(jnp.int32), axis=1),
                            N_EXPERTS - 1)
    n_used = pad_end[-1:] // BM
    dest = _dest(pad_start, e, rank)
    xb, dest3 = _dispatch(h_rt, dest, pad_start + counts, pad_end, n_blocks)
    yb = _experts(xb, block_exp, n_used, counts, pad_start + counts, layer, w1, b1, w2, b2)
    return _combine(yb, dest3, gate, x_all, mods, layer, final_g, n_tiles, final_norm)


def kernel(x, c, ctx, c_ctx, mod_w, mod_b, norm1_g, norm2_g, fourier_w_in, fourier_w_out,
           hgrn_w_in, hgrn_lower_bounds, hgrn_norm_g, hgrn_w_out, router_w, router_b,
           expert_w1, expert_b1, expert_w2, expert_b2, final_norm_g):
    assert x.shape == (BATCH, SEQ, D) and ctx.shape == (BATCH, CTX_LEN, D)
    cond8 = jnp.zeros((8, D), F32).at[:BATCH].set(c).at[BATCH].set(c_ctx)
    mods = _adaln(cond8, mod_w, mod_b)
    x_all = jnp.concatenate([x.reshape(NLAT, D), ctx.reshape(NCTX, D)], axis=0)
    dch, mpos, mrow = _fourier_constants()
    experts = (expert_w1, expert_b1, expert_w2, expert_b2)

    vr, vi = _fourier_in(x_all, mods, 0, norm1_g[0], fourier_w_in[0], dch, mpos)
    x_all = _fourier_out(vr, vi, x_all, mods, 0, fourier_w_out[0], mrow)
    x_all = _moe(x_all, mods, 0, norm2_g[0], router_w[0], router_b[0], *experts,
                 final_norm_g, ALL_TILES, False)

    q, ff, fb, v, gs = _hgrn_in(x_all, mods, 1, norm1_g[1], hgrn_w_in[0], hgrn_lower_bounds)
    o_fw, o_bw = _scan(q, ff, fb, v)
    x_lat = _hgrn_out(o_fw, o_bw, gs, x_all, mods, 1, hgrn_norm_g[0], hgrn_w_out[0], LAT_TILES)
    out = _moe(x_lat, mods, 1, norm2_g[1], router_w[1], router_b[1], *experts,
               final_norm_g, LAT_TILES, True)
    return out.reshape(BATCH, SEQ, D)
```

```python
import functools

import numpy as np
import jax
import jax.numpy as jnp
from jax import lax
from jax.experimental import pallas as pl
from jax.experimental.pallas import tpu as pltpu

F32 = jnp.float32
BF16 = jnp.bfloat16
HIGHEST = lax.Precision.HIGHEST

D = 1024
BATCH = 2
SEQ = 8192
CTX_LEN = 256
GRID_W = 64
GRID_H = SEQ // GRID_W
NLAT = BATCH * SEQ
NCTX = BATCH * CTX_LEN
NTOK = NLAT + NCTX
TM = 256
LAT_TILES = NLAT // TM
ALL_TILES = NTOK // TM
TILES_PER_BATCH = SEQ // TM
FGROUPS = 4
FGDIM = D // FGROUPS
HEADS = 8
HDIM = D // HEADS
CHUNK = 128
N_EXPERTS = 32
TOP_K = 4
D_FF = 1024
SWIGLU_ALPHA = 1.702
SWIGLU_LIMIT = 7.0
BM = 512
LANES = 128
RT = D // LANES
NORM_EPS = 1e-6
VMEM_LIMIT = 56 * 1024 * 1024

_NT = (((1,), (1,)), ((), ()))
_TN = (((0,), (0,)), ((), ()))


def _cparams(sem, vmem=None, **kw):
    return pltpu.CompilerParams(dimension_semantics=sem, vmem_limit_bytes=vmem, **kw)


def _sigmoid(x):
    return 1.0 / (1.0 + jnp.exp(-x))


def _rms(x, g):
    return x * lax.rsqrt(jnp.mean(x * x, axis=-1, keepdims=True) + NORM_EPS) * g


def _tile_cond(i):
    return jnp.where(i < LAT_TILES, i // TILES_PER_BATCH, 2)


def _adaln_kernel(cond_ref, w_ref, b_ref, o_ref):
    c = cond_ref[...]
    s = c * _sigmoid(c)
    o_ref[...] = jnp.dot(s, w_ref[...], precision=HIGHEST,
                         preferred_element_type=F32) + b_ref[...]


def _adaln(cond8, mod_w, mod_b):
    depth = mod_w.shape[0]
    nb = 1536
    out = pl.pallas_call(
        _adaln_kernel,
        grid=(depth, 6 * D // nb),
        in_specs=[pl.BlockSpec((8, D), lambda l, j: (0, 0)),
                  pl.BlockSpec((None, D, nb), lambda l, j: (l, 0, j)),
                  pl.BlockSpec((None, 1, nb), lambda l, j: (l, 0, j))],
        out_specs=pl.BlockSpec((None, 8, nb), lambda l, j: (l, 0, j)),
        out_shape=jax.ShapeDtypeStruct((depth, 8, 6 * D), F32),
        compiler_params=_cparams(("arbitrary", "arbitrary"), VMEM_LIMIT),
        name="adaln",
    )(cond8, mod_w, mod_b.reshape(depth, 1, 6 * D))
    return out.reshape(depth * 8, 6, D)


def _dft_cs(n):
    k = np.arange(n)
    ang = 2.0 * np.pi * np.outer(k, k) / n
    s = 1.0 / np.sqrt(n)
    return np.cos(ang) * s, np.sin(ang) * s


def _fourier_constants():
    cd, sd = _dft_cs(FGDIM)
    dch = np.concatenate([cd, sd], axis=1)
    cc, sc = _dft_cs(GRID_W)
    eye = np.eye(TM // GRID_W)
    kc, ks = np.kron(eye, cc), np.kron(eye, sc)
    m_lat = np.block([[kc, -ks], [ks, kc]])
    cp, sp = _dft_cs(CTX_LEN)
    m_ctx = np.block([[cp, -sp], [sp, cp]])
    mpos = np.stack([m_lat, m_ctx])
    cr, sr = _dft_cs(GRID_H)
    mrow = np.concatenate([cr, -sr], axis=1)
    return (jnp.asarray(dch, BF16), jnp.asarray(mpos, BF16), jnp.asarray(mrow, BF16))


def _fourier_in_kernel(x_ref, mod_ref, g_ref, win_ref, dch_ref, mpos_ref, vr_ref, vi_ref):
    x = x_ref[...]
    h = _rms(x, g_ref[...]) * (1.0 + mod_ref[1:2, :]) + mod_ref[0:1, :]
    u = jnp.dot(h.astype(BF16), win_ref[...], preferred_element_type=F32).astype(BF16)
    parts = [jnp.dot(u[:, g * FGDIM:(g + 1) * FGDIM], dch_ref[...],
                     preferred_element_type=F32) for g in range(FGROUPS)]
    uc = jnp.concatenate([p[:, :FGDIM] for p in parts], axis=1).astype(BF16)
    us = jnp.concatenate([p[:, FGDIM:] for p in parts], axis=1).astype(BF16)
    for half in range(x.shape[0] // TM):
        rows = slice(half * TM, (half + 1) * TM)
        v = jnp.dot(mpos_ref[...], jnp.concatenate([uc[rows], us[rows]], axis=0),
                    preferred_element_type=F32)
        vr_ref[rows, :] = v[:TM]
        vi_ref[rows, :] = v[TM:]


def _fourier_in(x_all, mods, layer, norm_g, w_in, dch, mpos):
    tile = pl.BlockSpec((2 * TM, D), lambda i: (i, 0))
    return pl.pallas_call(
        _fourier_in_kernel,
        grid=(ALL_TILES // 2,),
        in_specs=[tile,
                  pl.BlockSpec((None, 6, D), lambda i: (layer * 8 + _tile_cond(2 * i), 0, 0)),
                  pl.BlockSpec((1, D), lambda i: (0, 0)),
                  pl.BlockSpec((D, D), lambda i: (0, 0)),
                  pl.BlockSpec((FGDIM, 2 * FGDIM), lambda i: (0, 0)),
                  pl.BlockSpec((None, 2 * TM, 2 * TM),
                               lambda i: (jnp.where(2 * i < LAT_TILES, 0, 1), 0, 0))],
        out_specs=[tile, tile],
        out_shape=[jax.ShapeDtypeStruct((NTOK, D), F32)] * 2,
        compiler_params=_cparams(("arbitrary",), VMEM_LIMIT),
        name="fourier_in",
    )(x_all, mods, norm_g.reshape(1, D), w_in.astype(BF16), dch, mpos)


CB = 8


def _fourier_out_lat_kernel(vr_ref, vi_ref, x_ref, mrow_ref, wout_ref, mod_ref, o_ref):
    g1 = mod_ref[2:3, :]
    yf = []
    for c in range(CB):
        st = jnp.concatenate([vr_ref[:, c, :], vi_ref[:, c, :]], axis=0).astype(BF16)
        yf.append(jnp.dot(mrow_ref[...], st, preferred_element_type=F32).astype(BF16))
    y = jnp.dot(jnp.concatenate(yf, axis=0), wout_ref[...], preferred_element_type=F32)
    for c in range(CB):
        o_ref[:, c, :] = x_ref[:, c, :] + g1 * y[c * GRID_H:(c + 1) * GRID_H, :]


def _fourier_out_ctx_kernel(yr_ref, x_ref, wout_ref, mod_ref, o_ref):
    y = jnp.dot(yr_ref[...].astype(BF16), wout_ref[...], preferred_element_type=F32)
    o_ref[...] = x_ref[...] + mod_ref[2:3, :] * y


def _fourier_out(vr, vi, x_all, mods, layer, w_out, mrow):
    wout = w_out.astype(BF16)
    rows = NTOK // GRID_W
    v3 = lambda a: a.reshape(rows, GRID_W, D)
    blk = pl.BlockSpec((GRID_H, CB, D), lambda b, c: (b, c, 0))
    x_new = pl.pallas_call(
        _fourier_out_lat_kernel,
        grid=(BATCH, GRID_W // CB),
        in_specs=[blk, blk, blk,
                  pl.BlockSpec((GRID_H, 2 * GRID_H), lambda b, c: (0, 0)),
                  pl.BlockSpec((D, D), lambda b, c: (0, 0)),
                  pl.BlockSpec((None, 6, D), lambda b, c: (layer * 8 + b, 0, 0))],
        out_specs=blk,
        out_shape=jax.ShapeDtypeStruct((rows, GRID_W, D), F32),
        input_output_aliases={2: 0},
        compiler_params=_cparams(("arbitrary", "arbitrary"), VMEM_LIMIT),
        name="fourier_out_lat",
    )(v3(vr), v3(vi), v3(x_all), mrow, wout, mods).reshape(NTOK, D)
    ctile = pl.BlockSpec((TM, D), lambda i: (LAT_TILES + i, 0))
    return pl.pallas_call(
        _fourier_out_ctx_kernel,
        grid=(NCTX // TM,),
        in_specs=[ctile, ctile,
                  pl.BlockSpec((D, D), lambda i: (0, 0)),
                  pl.BlockSpec((None, 6, D), lambda i: (layer * 8 + 2, 0, 0))],
        out_specs=ctile,
        out_shape=jax.ShapeDtypeStruct((NTOK, D), F32),
        input_output_aliases={1: 0},
        compiler_params=_cparams(("arbitrary",), VMEM_LIMIT),
        name="fourier_out_ctx",
    )(vr, x_new, wout, mods)


def _hgrn_in_kernel(x_ref, mod_ref, g_ref, win_ref, hlb_ref, q_ref, ff_ref, fb_ref, v_ref, gs_ref,
                    *, layer):
    x = x_ref[...]
    h = (_rms(x, g_ref[...]) * (1.0 + mod_ref[1:2, :]) + mod_ref[0:1, :]).astype(BF16)
    raw = [hlb_ref[l] for l in range(hlb_ref.shape[0])]
    mx = functools.reduce(jnp.maximum, raw)
    ex = [jnp.exp(r - mx) for r in raw]
    den = functools.reduce(lambda a, b: a + b, ex)
    soft = [e / den for e in ex]
    lb = functools.reduce(lambda a, b: a + b, soft[:layer + 1]) - soft[0]

    def proj(j):
        return jnp.dot(h, win_ref[:, j * D:(j + 1) * D], preferred_element_type=F32)

    def per_head(ref, val):
        for hd in range(HEADS):
            ref[hd] = val[:, hd * HDIM:(hd + 1) * HDIM]

    q = proj(0)
    per_head(q_ref, q * _sigmoid(q))
    per_head(ff_ref, lb[0:1, :] + (1.0 - lb[0:1, :]) * _sigmoid(proj(1)))
    per_head(fb_ref, lb[1:2, :] + (1.0 - lb[1:2, :]) * _sigmoid(proj(2)))
    per_head(v_ref, proj(3))
    g = proj(4)
    gs_ref[...] = (g * _sigmoid(g)).astype(BF16)


def _hgrn_in(x_all, mods, layer, norm_g, w_in, hlb):
    tile = pl.BlockSpec((TM, D), lambda i: (i, 0))
    depth = hlb.shape[0]
    return pl.pallas_call(
        functools.partial(_hgrn_in_kernel, layer=layer),
        grid=(ALL_TILES,),
        in_specs=[tile,
                  pl.BlockSpec((None, 6, D), lambda i: (layer * 8 + _tile_cond(i), 0, 0)),
                  pl.BlockSpec((1, D), lambda i: (0, 0)),
                  pl.BlockSpec((D, 5 * D), lambda i: (0, 0)),
                  pl.BlockSpec((depth, 2, D), lambda i: (0, 0, 0))],
        out_specs=[pl.BlockSpec((HEADS, TM, HDIM), lambda i: (0, i, 0))] * 4 + [tile],
        out_shape=[jax.ShapeDtypeStruct((HEADS, NTOK, HDIM), F32)] * 4
                  + [jax.ShapeDtypeStruct((NTOK, D), BF16)],
        compiler_params=_cparams(("arbitrary",), VMEM_LIMIT),
        name="hgrn_in",
    )(x_all, mods, norm_g.reshape(1, D), w_in.astype(BF16), hlb)


N_LEVELS = 7
SUB = 8
NGRP = CHUNK // SUB
FINE_LEVELS = 3


def _scan_pair_kernel(qf_ref, ff_ref, vf_ref, qb_ref, fb_ref, vb_ref, of_ref, ob_ref, sf_ref, sb_ref,
                      rf_ref, rb_ref):
    @pl.when(pl.program_id(1) == 0)
    def _():
        sf_ref[...] = jnp.zeros_like(sf_ref)
        sb_ref[...] = jnp.zeros_like(sb_ref)

    _scan_chunk(qf_ref, ff_ref, vf_ref, of_ref, sf_ref, rf_ref, rev=False)
    _scan_chunk(qb_ref, fb_ref, vb_ref, ob_ref, sb_ref, rb_ref, rev=True)


PG = CHUNK // SUB


def _scan_chunk(q_ref, fg_ref, v_ref, o_ref, s_ref, relay_ref, *, rev):
    heads = [slice(h * HDIM, (h + 1) * HDIM) for h in range(HEADS)]

    def score(qs, ks, h):
        return jnp.dot(qs[:, heads[h]], ks[:, heads[h]].T.astype(BF16), preferred_element_type=F32)

    def by_residue(ref):
        return jnp.concatenate(
            [jnp.concatenate([ref[h, pl.ds(r, PG, stride=SUB), :] for r in range(SUB)], axis=0)
             for h in range(HEADS)], axis=1)

    def residue_groups(a):
        return [a[r * PG:(r + 1) * PG, :] for r in range(SUB)]

    q_p = by_residue(q_ref)
    fg_p = by_residue(fg_ref)
    k_p = 1.0 - fg_p
    v_p = by_residue(v_ref).astype(BF16)
    key = lax.broadcasted_iota(jnp.int32, (PG, CHUNK), 1)
    same_a = (key & (PG - 1)) == lax.broadcasted_iota(jnp.int32, (PG, CHUNK), 0)
    key_r = key >> (PG.bit_length() - 1)
    zero_p = jnp.zeros((PG, D), F32)
    sc_p = [[None] * SUB for _ in range(HEADS)]
    qd = q_p.astype(BF16)
    for h in range(HEADS):
        sc = score(qd, k_p, h)
        for r in range(SUB):
            sc_p[h][r] = jnp.where(same_a & (key_r == r), sc[r * PG:(r + 1) * PG, :], 0.0)
    qr_p = residue_groups(fg_p * q_p)
    kr_p = residue_groups(k_p)
    tot_p = residue_groups(fg_p)
    for l in range(FINE_LEVELS):
        bit = 1 << l
        is_far = [((r & bit) == 0) == rev for r in range(SUB)]
        far_groups = [r for r in range(SUB) if is_far[r]]
        qb = jnp.concatenate([qr_p[r] for r in far_groups], axis=0).astype(BF16)
        kb = jnp.concatenate([zero_p if is_far[r] else kr_p[r] for r in range(SUB)], axis=0)
        keep = [same_a & ((key_r >> (l + 1)) == (r >> (l + 1))) for r in far_groups]
        for h in range(HEADS):
            sc = score(qb, kb, h)
            for i, r in enumerate(far_groups):
                sc_p[h][r] = sc_p[h][r] + jnp.where(keep[i], sc[i * PG:(i + 1) * PG, :], 0.0)
        sib_p = [tot_p[r ^ bit] for r in range(SUB)]
        qr_p = [qr_p[r] * sib_p[r] if is_far[r] else qr_p[r] for r in range(SUB)]
        kr_p = [kr_p[r] if is_far[r] else kr_p[r] * sib_p[r] for r in range(SUB)]
        tot_p = [tot_p[r] * sib_p[r] for r in range(SUB)]
    for x, grp in enumerate((qr_p, kr_p, tot_p)):
        for h in range(HEADS):
            for r in range(SUB):
                relay_ref[x, h, pl.ds(r, PG, stride=SUB), :] = grp[r][:, heads[h]]
    qr, kr, tot = (jnp.concatenate([relay_ref[x, h] for h in range(HEADS)], axis=1)
                   for x in range(3))

    def groups(a):
        return [a[b * SUB:(b + 1) * SUB, :] for b in range(NGRP)]

    qr_g, kr_g, tot_g = (groups(a) for a in (qr, kr, tot))
    sc_g = [[None] * NGRP for _ in range(HEADS)]
    lane = lax.broadcasted_iota(jnp.int32, (SUB, CHUNK), 1)
    zero_g = jnp.zeros((SUB, D), F32)
    for l in range(FINE_LEVELS, N_LEVELS):
        bit = 1 << (l - FINE_LEVELS)
        is_far = [((b & bit) == 0) == rev for b in range(NGRP)]
        far_groups = [b for b in range(NGRP) if is_far[b]]
        qb = jnp.concatenate([qr_g[b] for b in far_groups], axis=0).astype(BF16)
        kb = jnp.concatenate([zero_g if is_far[b] else kr_g[b] for b in range(NGRP)], axis=0)
        span = 2 << l
        keep = [None if span == CHUNK else
                (lane >= b * SUB // span * span) & (lane < b * SUB // span * span + span)
                for b in far_groups]
        for h in range(HEADS):
            sc = score(qb, kb, h)
            for i, b in enumerate(far_groups):
                piece = sc[i * SUB:(i + 1) * SUB, :]
                if keep[i] is not None:
                    piece = jnp.where(keep[i], piece, 0.0)
                sc_g[h][b] = piece if sc_g[h][b] is None else sc_g[h][b] + piece
        sib_g = [tot_g[b ^ bit] for b in range(NGRP)]
        qr_g = [qr_g[b] * sib_g[b] if is_far[b] else qr_g[b] for b in range(NGRP)]
        kr_g = [kr_g[b] if is_far[b] else kr_g[b] * sib_g[b] for b in range(NGRP)]
        if l < N_LEVELS - 1:
            tot_g = [tot_g[b] * sib_g[b] for b in range(NGRP)]
        else:
            tot_row = tot_g[0][0:1, :] * sib_g[0][0:1, :]
    qin = jnp.concatenate(qr_g, axis=0).astype(BF16)
    kst = jnp.concatenate(kr_g, axis=0).astype(BF16)
    vb = jnp.concatenate([v_ref[h] for h in range(HEADS)], axis=1).astype(BF16)
    zero_s = jnp.zeros((SUB, CHUNK), F32)
    for h in range(HEADS):
        sl = heads[h]
        st = s_ref[h]
        sc = jnp.concatenate([zero_s if g is None else g for g in sc_g[h]], axis=0).astype(BF16)
        o_rows = (jnp.dot(sc, vb[:, sl], preferred_element_type=F32)
                  + jnp.dot(qin[:, sl], st.T.astype(BF16), preferred_element_type=F32))
        o_res = jnp.dot(jnp.concatenate(sc_p[h], axis=0).astype(BF16), v_p[:, sl],
                        preferred_element_type=F32)
        for r in range(SUB):
            relay_ref[3, h, pl.ds(r, PG, stride=SUB), :] = o_res[r * PG:(r + 1) * PG, :]
        o_ref[:, sl] = (o_rows + relay_ref[3, h]).astype(BF16)
        s_ref[h] = st * tot_row[:, sl] + lax.dot_general(vb[:, sl], kst[:, sl], _TN,
                                                         preferred_element_type=F32)


LAT_CHUNKS = SEQ // CHUNK
CTX_CHUNKS = CTX_LEN // CHUNK
SCAN_STEPS = CTX_CHUNKS + LAT_CHUNKS


def _scan(q, f_fw, f_bw, v):
    def idx_fw(b, s):
        return (jnp.where(s < CTX_CHUNKS, NLAT // CHUNK + CTX_CHUNKS * b + s,
                          LAT_CHUNKS * b + (s - CTX_CHUNKS)), 0)

    def idx_bw(b, s):
        return (jnp.where(s < CTX_CHUNKS, NLAT // CHUNK + CTX_CHUNKS * b + (CTX_CHUNKS - 1 - s),
                          LAT_CHUNKS * b + (SCAN_STEPS - 1 - s)), 0)
    fw = pl.BlockSpec((CHUNK, D), idx_fw)
    bw = pl.BlockSpec((CHUNK, D), idx_bw)
    fw_in = pl.BlockSpec((HEADS, CHUNK, HDIM), lambda b, s: (0, idx_fw(b, s)[0], 0))
    bw_in = pl.BlockSpec((HEADS, CHUNK, HDIM), lambda b, s: (0, idx_bw(b, s)[0], 0))
    state = pltpu.VMEM((HEADS, HDIM, HDIM), F32)
    relay = pltpu.VMEM((4, HEADS, CHUNK, HDIM), F32)
    return pl.pallas_call(
        _scan_pair_kernel,
        grid=(BATCH, SCAN_STEPS),
        in_specs=[fw_in, fw_in, fw_in, bw_in, bw_in, bw_in],
        out_specs=[fw, bw],
        out_shape=[jax.ShapeDtypeStruct((NTOK, D), BF16)] * 2,
        scratch_shapes=[state, state, relay, relay],
        compiler_params=_cparams(("arbitrary", "arbitrary"), VMEM_LIMIT),
        name="scan",
    )(q, f_fw, v, q, f_bw, v)


def _hgrn_out_kernel(of_ref, ob_ref, gs_ref, x_ref, ng_ref, wout_ref, mod_ref, o_ref):
    o = of_ref[...].astype(F32) + ob_ref[...].astype(F32)
    parts = []
    for h in range(HEADS):
        oh = o[:, h * HDIM:(h + 1) * HDIM]
        parts.append(oh * lax.rsqrt(jnp.mean(oh * oh, axis=-1, keepdims=True) + NORM_EPS))
    on = jnp.concatenate(parts, axis=1) * ng_ref[...]
    y = jnp.dot((on * gs_ref[...].astype(F32)).astype(BF16), wout_ref[...],
                preferred_element_type=F32)
    o_ref[...] = x_ref[...] + mod_ref[2:3, :] * y


def _hgrn_out(o_fw, o_bw, gs, x_all, mods, layer, norm_g, w_out, n_tiles):
    tile = pl.BlockSpec((2 * TM, D), lambda i: (i, 0))
    return pl.pallas_call(
        _hgrn_out_kernel,
        grid=(n_tiles // 2,),
        in_specs=[tile, tile, tile, tile,
                  pl.BlockSpec((1, D), lambda i: (0, 0)),
                  pl.BlockSpec((D, D), lambda i: (0, 0)),
                  pl.BlockSpec((None, 6, D), lambda i: (layer * 8 + _tile_cond(2 * i), 0, 0))],
        out_specs=tile,
        out_shape=jax.ShapeDtypeStruct((n_tiles * TM, D), F32),
        compiler_params=_cparams(("arbitrary",), VMEM_LIMIT),
        name="hgrn_out",
    )(o_fw, o_bw, gs, x_all, norm_g.reshape(1, D), w_out.astype(BF16), mods)


def _to_row_tiled(ref, val):
    for j in range(RT):
        ref[pl.ds(j, val.shape[0], stride=RT), :] = val[:, j * LANES:(j + 1) * LANES]


def _from_row_tiled(ref, n, base=0):
    return [ref[pl.ds(base + j, n, stride=RT), :] for j in range(RT)]


def _route_kernel(x_ref, mod_ref, g_ref, wrt_ref, brt_ref, h_ref, e_ref, rank_ref, gate_ref, cnt_ref):
    @pl.when(pl.program_id(0) == 0)
    def _():
        cnt_ref[...] = jnp.zeros_like(cnt_ref)

    h = _rms(x_ref[...], g_ref[...]) * (1.0 + mod_ref[4:5, :]) + mod_ref[3:4, :]
    _to_row_tiled(h_ref, h)
    w = wrt_ref[...]
    w_hi = w.astype(BF16)
    w_lo = (w - w_hi.astype(F32)).astype(BF16)
    h_hi = h.astype(BF16)
    h_lo = (h - h_hi.astype(F32)).astype(BF16)
    part = lax.dot_general(jnp.concatenate([w_hi, w_lo], axis=0), h_hi, _NT,
                           preferred_element_type=F32)
    logits = (part[:N_EXPERTS] + part[N_EXPERTS:]
              + lax.dot_general(w_hi, h_lo, _NT, preferred_element_type=F32)
              + brt_ref[:, 0:1])
    row = lax.broadcasted_iota(jnp.int32, (N_EXPERTS, TM), 0).astype(F32)
    vals = logits
    sel = jnp.zeros((N_EXPERTS, TM), F32)
    tops, idxs = [], []
    for _ in range(TOP_K):
        m = jnp.max(vals, axis=0, keepdims=True)
        idx = jnp.min(jnp.where(vals == m, row, float(N_EXPERTS)), axis=0, keepdims=True)
        hit = row == idx
        vals = jnp.where(hit, -jnp.inf, vals)
        sel = jnp.where(hit, 1.0, sel)
        tops.append(m)
        idxs.append(idx)
    ex = [jnp.exp(m - tops[0]) for m in tops]
    den = ex[0] + ex[1] + ex[2] + ex[3]
    r = lax.broadcasted_iota(jnp.int32, (TM, TM), 0)
    c = lax.broadcasted_iota(jnp.int32, (TM, TM), 1)
    before = jnp.where(r < c, 1.0, 0.0).astype(BF16)
    pref = jnp.dot(sel.astype(BF16), before, preferred_element_type=F32) + cnt_ref[:, 0:1]
    slot = lax.broadcasted_iota(jnp.int32, (8, TM), 0)
    e_out = jnp.zeros((8, TM), F32)
    rank_out = jnp.zeros((8, TM), F32)
    gate_out = jnp.zeros((8, TM), F32)
    for kk in range(TOP_K):
        rank = jnp.sum(jnp.where(row == idxs[kk], pref, 0.0), axis=0, keepdims=True)
        e_out = jnp.where(slot == kk, idxs[kk], e_out)
        rank_out = jnp.where(slot == kk, rank, rank_out)
        gate_out = jnp.where(slot == kk, ex[kk] / den, gate_out)
    e_ref[...] = e_out.astype(jnp.int32)
    rank_ref[...] = rank_out.astype(jnp.int32)
    gate_ref[...] = gate_out
    cnt_ref[...] += jnp.sum(sel, axis=1, keepdims=True)


def _route(x_all, mods, layer, norm_g, w_r, b_r, n_tiles):
    tile = pl.BlockSpec((TM, D), lambda i: (i, 0))
    small = pl.BlockSpec((None, 8, TM), lambda i: (i, 0, 0))
    n = n_tiles * TM
    return pl.pallas_call(
        _route_kernel,
        grid=(n_tiles,),
        in_specs=[tile,
                  pl.BlockSpec((None, 6, D), lambda i: (layer * 8 + _tile_cond(i), 0, 0)),
                  pl.BlockSpec((1, D), lambda i: (0, 0)),
                  pl.BlockSpec((N_EXPERTS, D), lambda i: (0, 0)),
                  pl.BlockSpec((N_EXPERTS, LANES), lambda i: (0, 0))],
        out_specs=[pl.BlockSpec((TM * RT, LANES), lambda i: (i, 0)), small, small, small,
                   pl.BlockSpec((N_EXPERTS, LANES), lambda i: (0, 0))],
        out_shape=[jax.ShapeDtypeStruct((n * RT, LANES), F32),
                   jax.ShapeDtypeStruct((n_tiles, 8, TM), jnp.int32),
                   jax.ShapeDtypeStruct((n_tiles, 8, TM), jnp.int32),
                   jax.ShapeDtypeStruct((n_tiles, 8, TM), F32),
                   jax.ShapeDtypeStruct((N_EXPERTS, LANES), F32)],
        compiler_params=_cparams(("arbitrary",), VMEM_LIMIT),
        name="route",
    )(x_all, mods, norm_g.reshape(1, D), w_r.T, jnp.broadcast_to(b_r[:, None], (N_EXPERTS, LANES)))


def _dest_kernel(ps_ref, e_ref, rank_ref, d_ref):
    e = e_ref[...]
    acc = rank_ref[...]
    for j in range(N_EXPERTS):
        acc = acc + jnp.where(e == j, ps_ref[j], 0)
    d_ref[...] = acc


def _dest(pad_start, e, rank):
    full = pl.BlockSpec(e.shape, lambda i, ps: (0, 0, 0))
    return pl.pallas_call(
        _dest_kernel,
        grid_spec=pltpu.PrefetchScalarGridSpec(num_scalar_prefetch=1, grid=(1,),
                                               in_specs=[full, full], out_specs=full),
        out_shape=jax.ShapeDtypeStruct(e.shape, jnp.int32),
        name="dest",
    )(pad_start, e, rank)


PAD_PIECES = tuple(1 << p for p in reversed(range(BM.bit_length() - 1)))


def _zero_fill(lo_ref, hi_ref, xb_hbm, zeros, sem, n_blocks, wait):
    def go(rows, row0):
        cp = pltpu.make_async_copy(zeros.at[pl.ds(0, rows * RT), :],
                                   xb_hbm.at[pl.ds(row0 * RT, rows * RT), :], sem.at[1])
        cp.wait() if wait else cp.start()

    def per_expert(e, carry):
        row = lo_ref[e]
        n = hi_ref[e] - row
        for piece in PAD_PIECES:
            @pl.when((n & piece) != 0)
            def _():
                go(piece, row)
            row = row + (n & piece)
        return carry
    lax.fori_loop(0, N_EXPERTS, per_expert, 0)

    def per_block(b, carry):
        go(BM, b * BM)
        return carry
    lax.fori_loop(hi_ref[N_EXPERTS - 1] // BM, n_blocks, per_block, 0)


def _dispatch_kernel(lo_ref, hi_ref, dest_ref, h_ref, xb_hbm, zeros, sem, *, n_blocks):
    i = pl.program_id(0)

    @pl.when(i == 0)
    def _():
        zeros[...] = jnp.zeros_like(zeros)
        _zero_fill(lo_ref, hi_ref, xb_hbm, zeros, sem, n_blocks, wait=False)

    def body(t, carry):
        src = h_ref.at[pl.ds(t * RT, RT), :]
        for kk in range(TOP_K):
            d = dest_ref[0, 0, kk * TM + t]
            pltpu.make_async_copy(src, xb_hbm.at[pl.ds(d * RT, RT), :],
                                  sem.at[0]).start(priority=kk % 2)
        return carry
    lax.fori_loop(0, TM, body, 0, unroll=4)
    for kk in range(TOP_K):
        pltpu.make_async_copy(h_ref, xb_hbm.at[pl.ds(0, TM * RT), :], sem.at[0]).wait()

    @pl.when(i == pl.num_programs(0) - 1)
    def _():
        _zero_fill(lo_ref, hi_ref, xb_hbm, zeros, sem, n_blocks, wait=True)


def _dispatch(h_rt, dest, pad_lo, pad_hi, n_blocks):
    n_tiles = dest.shape[0]
    dest3 = dest[:, :TOP_K, :].reshape(n_tiles, 1, TOP_K * TM)
    grid_spec = pltpu.PrefetchScalarGridSpec(
        num_scalar_prefetch=2,
        grid=(n_tiles,),
        in_specs=[pl.BlockSpec((1, 1, TOP_K * TM), lambda i, lo, hi: (i, 0, 0),
                               memory_space=pltpu.SMEM),
                  pl.BlockSpec((TM * RT, LANES), lambda i, lo, hi: (i, 0))],
        out_specs=pl.BlockSpec(memory_space=pl.ANY),
        scratch_shapes=[pltpu.VMEM((BM * RT, LANES), F32),
                        pltpu.SemaphoreType.DMA((2,))],
    )
    xb = pl.pallas_call(
        functools.partial(_dispatch_kernel, n_blocks=n_blocks),
        grid_spec=grid_spec,
        out_shape=jax.ShapeDtypeStruct((n_blocks * BM * RT, LANES), F32),
        compiler_params=_cparams(("arbitrary",), VMEM_LIMIT, disable_bounds_checks=True),
        name="dispatch",
    )(pad_lo, pad_hi, dest3, h_rt)
    return xb, dest3


ROW_STEPS = tuple(range(BM // 4, BM + 1, BM // 4))


def _expert_kernel(bexp_ref, nused_ref, first_ref, wslot_ref, next_ref, valid_ref, xb_ref, w1_hbm, b1_ref,
                   w2_hbm, b2_ref, y_ref, w1s, w2s, w1b, w2b, sem, *, layer):
    i = pl.program_id(0)
    del nused_ref

    def weight_copies(e, slot):
        return (pltpu.make_async_copy(w1_hbm.at[layer, e], w1s.at[slot], sem.at[0, slot]),
                pltpu.make_async_copy(w2_hbm.at[layer, e], w2s.at[slot], sem.at[1, slot]))

    @pl.when(i == 0)
    def _():
        for cp in weight_copies(bexp_ref[0], 0):
            cp.start()

    @pl.when(first_ref[i] == 1)
    def _():
        slot = wslot_ref[i]
        for cp in weight_copies(bexp_ref[i], slot):
            cp.wait()
        w1b[...] = w1s[slot].astype(BF16)
        w2b[...] = w2s[slot].astype(BF16)

        @pl.when(next_ref[i] >= 0)
        def _():
            for cp in weight_copies(next_ref[i], 1 - slot):
                cp.start()

    valid = valid_ref[i]
    for lo, rows in zip((0,) + ROW_STEPS[:-1], ROW_STEPS):
        @pl.when(jnp.logical_and(valid > lo, valid <= rows))
        def _():
            x = jnp.concatenate(_from_row_tiled(xb_ref, rows), axis=1).astype(BF16)
            u = jnp.dot(x, w1b[...], preferred_element_type=F32) + b1_ref[...]
            glu = jnp.minimum(u[:, :D_FF], SWIGLU_LIMIT)
            lin = jnp.clip(u[:, D_FF:], -SWIGLU_LIMIT, SWIGLU_LIMIT)
            act = glu * _sigmoid(SWIGLU_ALPHA * glu) * (lin + 1.0)
            y = jnp.dot(act.astype(BF16), w2b[...], preferred_element_type=F32) + b2_ref[...]
            _to_row_tiled(y_ref.at[pl.ds(0, rows * RT), :], y)
            if rows < BM:
                y_ref[pl.ds(rows * RT, (BM - rows) * RT), :] = jnp.zeros(((BM - rows) * RT, LANES), F32)

    @pl.when(valid == 0)
    def _():
        y_ref[...] = jnp.zeros_like(y_ref)


def _experts(xb, block_exp, n_used, counts, pad_lo, layer, w1, b1, w2, b2):
    n_blocks = xb.shape[0] // (BM * RT)
    used = jnp.arange(n_blocks, dtype=jnp.int32) < n_used[0]
    changed = jnp.concatenate([jnp.ones((1,), bool), block_exp[1:] != block_exp[:-1]])
    first = jnp.logical_and(used, changed).astype(jnp.int32)
    wslot = (jnp.cumsum(first) - 1) % 2
    ar = jnp.arange(N_EXPERTS, dtype=jnp.int32)
    later = jnp.min(jnp.where((ar[None, :] > ar[:, None]) & (counts > 0)[None, :], ar[None, :],
                              N_EXPERTS), axis=1)
    of_block = block_exp[:, None] == ar[None, :]
    next_exp = jnp.sum(jnp.where(of_block, jnp.where(later == N_EXPERTS, -1, later)[None, :], 0), axis=1)
    starts = jnp.arange(n_blocks, dtype=jnp.int32) * BM
    valid = jnp.clip(jnp.sum(jnp.where(of_block, pad_lo[None, :], 0), axis=1) - starts, 0, BM)
    smap = lambda f: (lambda i, be, nu, fi, ws, nx, va: f(i, be, nu))
    grid_spec = pltpu.PrefetchScalarGridSpec(
        num_scalar_prefetch=6,
        grid=(n_blocks,),
        in_specs=[
            pl.BlockSpec((BM * RT, LANES), smap(lambda i, be, nu: (jnp.minimum(i, nu[0] - 1), 0))),
            pl.BlockSpec(memory_space=pl.ANY),
            pl.BlockSpec((None, 1, 2 * D_FF), smap(lambda i, be, nu: (layer * N_EXPERTS + be[i], 0, 0))),
            pl.BlockSpec(memory_space=pl.ANY),
            pl.BlockSpec((None, 1, D), smap(lambda i, be, nu: (layer * N_EXPERTS + be[i], 0, 0))),
        ],
        out_specs=pl.BlockSpec((BM * RT, LANES), smap(lambda i, be, nu: (i, 0))),
        scratch_shapes=[pltpu.VMEM((2, D, 2 * D_FF), F32),
                        pltpu.VMEM((2, D_FF, D), F32),
                        pltpu.VMEM((D, 2 * D_FF), BF16),
                        pltpu.VMEM((D_FF, D), BF16),
                        pltpu.SemaphoreType.DMA((2, 2))],
    )
    return pl.pallas_call(
        functools.partial(_expert_kernel, layer=layer),
        grid_spec=grid_spec,
        out_shape=jax.ShapeDtypeStruct(xb.shape, F32),
        compiler_params=_cparams(("arbitrary",), VMEM_LIMIT),
        name="experts",
    )(block_exp, n_used, first, wslot.astype(jnp.int32), next_exp.astype(jnp.int32),
      valid.astype(jnp.int32), xb, w1,
      b1.reshape(-1, 1, 2 * D_FF), w2, b2.reshape(-1, 1, D))


def _combine_kernel(dest_ref, dest_next_ref, y_hbm, gate_ref, x_ref, mod_ref, fg_ref, o_ref, buf, sem,
                    *, final_norm):
    i = pl.program_id(0)
    n = pl.num_programs(0)
    slot = i % 2
    rows = TOP_K * TM

    def gather(dest, to_slot):
        def body(p, carry):
            for half in range(2):
                r = 2 * p + half
                pltpu.make_async_copy(y_hbm.at[pl.ds(dest[0, 0, r] * RT, RT), :],
                                      buf.at[to_slot, pl.ds(r * RT, RT), :],
                                      sem.at[to_slot]).start(priority=half)
            return carry
        lax.fori_loop(0, rows // 2, body, 0, unroll=4)

    @pl.when(i == 0)
    def _():
        gather(dest_ref, 0)

    @pl.when(i + 1 < n)
    def _():
        gather(dest_next_ref, 1 - slot)

    pltpu.make_async_copy(y_hbm.at[pl.ds(0, rows * RT), :], buf.at[slot], sem.at[slot]).wait()
    eye = (lax.broadcasted_iota(jnp.int32, (8, 8), 0)
           == lax.broadcasted_iota(jnp.int32, (8, 8), 1)).astype(F32)
    gate = lax.dot_general(gate_ref[...], eye, _TN, precision=HIGHEST,
                           preferred_element_type=F32)
    cur = buf.at[slot]
    outs = []
    for j in range(RT):
        acc = None
        for kk in range(TOP_K):
            piece = gate[:, kk:kk + 1] * cur[pl.ds(kk * TM * RT + j, TM, stride=RT), :]
            acc = piece if acc is None else acc + piece
        outs.append(acc)
    out = x_ref[...] + mod_ref[5:6, :] * jnp.concatenate(outs, axis=1)
    if final_norm:
        out = _rms(out, fg_ref[...])
    o_ref[...] = out


def _combine(yb, dest3, gate, x_all, mods, layer, final_g, n_tiles, final_norm):
    last = n_tiles - 1
    tile = pl.BlockSpec((TM, D), lambda i: (i, 0))
    return pl.pallas_call(
        functools.partial(_combine_kernel, final_norm=final_norm),
        grid=(n_tiles,),
        in_specs=[pl.BlockSpec((1, 1, TOP_K * TM), lambda i: (i, 0, 0), memory_space=pltpu.SMEM),
                  pl.BlockSpec((1, 1, TOP_K * TM), lambda i: (jnp.minimum(i + 1, last), 0, 0),
                               memory_space=pltpu.SMEM),
                  pl.BlockSpec(memory_space=pl.ANY),
                  pl.BlockSpec((None, 8, TM), lambda i: (i, 0, 0)),
                  tile,
                  pl.BlockSpec((None, 6, D), lambda i: (layer * 8 + _tile_cond(i), 0, 0)),
                  pl.BlockSpec((1, D), lambda i: (0, 0))],
        out_specs=tile,
        out_shape=jax.ShapeDtypeStruct((n_tiles * TM, D), F32),
        scratch_shapes=[pltpu.VMEM((2, TOP_K * TM * RT, LANES), F32),
                        pltpu.SemaphoreType.DMA((2,))],
        compiler_params=_cparams(("arbitrary",), VMEM_LIMIT, disable_bounds_checks=True),
        name="combine",
    )(dest3, dest3, yb, gate, x_all, mods, final_g.reshape(1, D))


def _moe(x_all, mods, layer, norm_g, w_r, b_r, w1, b1, w2, b2, final_g, n_tiles, final_norm):
    n = n_tiles * TM
    h_rt, e, rank, gate, cnt = _route(x_all, mods, layer, norm_g, w_r, b_r, n_tiles)
    counts = cnt[:, 0].astype(jnp.int32)
    padded = (counts + BM - 1) // BM * BM
    pad_end = jnp.cumsum(padded)
    pad_start = pad_end - padded
    n_blocks = -(-(n * TOP_K) // BM) + N_EXPERTS
    starts = jnp.arange(n_blocks, dtype=jnp.int32) * BM
    block_exp = jnp.minimum(jnp.sum((pad_end[None, :] <= starts[:, None]).astype(jnp.int32), axis=1),
                            N_EXPERTS - 1)
    n_used = pad_end[-1:] // BM
    dest = _dest(pad_start, e, rank)
    xb, dest3 = _dispatch(h_rt, dest, pad_start + counts, pad_end, n_blocks)
    yb = _experts(xb, block_exp, n_used, counts, pad_start + counts, layer, w1, b1, w2, b2)
    return _combine(yb, dest3, gate, x_all, mods, layer, final_g, n_tiles, final_norm)


def kernel(x, c, ctx, c_ctx, mod_w, mod_b, norm1_g, norm2_g, fourier_w_in, fourier_w_out,
           hgrn_w_in, hgrn_lower_bounds, hgrn_norm_g, hgrn_w_out, router_w, router_b,
           expert_w1, expert_b1, expert_w2, expert_b2, final_norm_g):
    assert x.shape == (BATCH, SEQ, D) and ctx.shape == (BATCH, CTX_LEN, D)
    cond8 = jnp.zeros((8, D), F32).at[:BATCH].set(c).at[BATCH].set(c_ctx)
    mods = _adaln(cond8, mod_w, mod_b)
    x_all = jnp.concatenate([x.reshape(NLAT, D), ctx.reshape(NCTX, D)], axis=0)
    dch, mpos, mrow = _fourier_constants()
    experts = (expert_w1, expert_b1, expert_w2, expert_b2)

    vr, vi = _fourier_in(x_all, mods, 0, norm1_g[0], fourier_w_in[0], dch, mpos)
    x_all = _fourier_out(vr, vi, x_all, mods, 0, fourier_w_out[0], mrow)
    x_all = _moe(x_all, mods, 0, norm2_g[0], router_w[0], router_b[0], *experts,
                 final_norm_g, ALL_TILES, False)

    q, ff, fb, v, gs = _hgrn_in(x_all, mods, 1, norm1_g[1], hgrn_w_in[0], hgrn_lower_bounds)
    o_fw, o_bw = _scan(q, ff, fb, v)
    x_lat = _hgrn_out(o_fw, o_bw, gs, x_all, mods, 1, hgrn_norm_g[0], hgrn_w_out[0], LAT_TILES)
    out = _moe(x_lat, mods, 1, norm2_g[1], router_w[1], router_b[1], *experts,
               final_norm_g, LAT_TILES, True)
    return out.reshape(BATCH, SEQ, D)
```

```python
import functools

import numpy as np
import jax
import jax.numpy as jnp
from jax import lax
from jax.experimental import pallas as pl
from jax.experimental.pallas import tpu as pltpu

F32 = jnp.float32
BF16 = jnp.bfloat16
HIGHEST = lax.Precision.HIGHEST

D = 1024
BATCH = 2
SEQ = 8192
CTX_LEN = 256
GRID_W = 64
GRID_H = SEQ // GRID_W
NLAT = BATCH * SEQ
NCTX = BATCH * CTX_LEN
NTOK = NLAT + NCTX
TM = 256
LAT_TILES = NLAT // TM
ALL_TILES = NTOK // TM
TILES_PER_BATCH = SEQ // TM
FGROUPS = 4
FGDIM = D // FGROUPS
HEADS = 8
HDIM = D // HEADS
CHUNK = 128
N_EXPERTS = 32
TOP_K = 4
D_FF = 1024
SWIGLU_ALPHA = 1.702
SWIGLU_LIMIT = 7.0
BM = 512
LANES = 128
RT = D // LANES
NORM_EPS = 1e-6
VMEM_LIMIT = 56 * 1024 * 1024

_NT = (((1,), (1,)), ((), ()))
_TN = (((0,), (0,)), ((), ()))


def _cparams(sem, vmem=None, **kw):
    return pltpu.CompilerParams(dimension_semantics=sem, vmem_limit_bytes=vmem, **kw)


def _sigmoid(x):
    return 1.0 / (1.0 + jnp.exp(-x))


def _rms(x, g):
    return x * lax.rsqrt(jnp.mean(x * x, axis=-1, keepdims=True) + NORM_EPS) * g


def _tile_cond(i):
    return jnp.where(i < LAT_TILES, i // TILES_PER_BATCH, 2)


def _adaln_kernel(cond_ref, w_ref, b_ref, o_ref):
    c = cond_ref[...]
    s = c * _sigmoid(c)
    o_ref[...] = jnp.dot(s, w_ref[...], precision=HIGHEST,
                         preferred_element_type=F32) + b_ref[...]


def _adaln(cond8, mod_w, mod_b):
    depth = mod_w.shape[0]
    nb = 1536
    out = pl.pallas_call(
        _adaln_kernel,
        grid=(depth, 6 * D // nb),
        in_specs=[pl.BlockSpec((8, D), lambda l, j: (0, 0)),
                  pl.BlockSpec((None, D, nb), lambda l, j: (l, 0, j)),
                  pl.BlockSpec((None, 1, nb), lambda l, j: (l, 0, j))],
        out_specs=pl.BlockSpec((None, 8, nb), lambda l, j: (l, 0, j)),
        out_shape=jax.ShapeDtypeStruct((depth, 8, 6 * D), F32),
        compiler_params=_cparams(("arbitrary", "arbitrary"), VMEM_LIMIT),
        name="adaln",
    )(cond8, mod_w, mod_b.reshape(depth, 1, 6 * D))
    return out.reshape(depth * 8, 6, D)


def _dft_cs(n):
    k = np.arange(n)
    ang = 2.0 * np.pi * np.outer(k, k) / n
    s = 1.0 / np.sqrt(n)
    return np.cos(ang) * s, np.sin(ang) * s


def _fourier_constants():
    cd, sd = _dft_cs(FGDIM)
    dch = np.concatenate([cd, sd], axis=1)
    cc, sc = _dft_cs(GRID_W)
    eye = np.eye(TM // GRID_W)
    kc, ks = np.kron(eye, cc), np.kron(eye, sc)
    m_lat = np.block([[kc, -ks], [ks, kc]])
    cp, sp = _dft_cs(CTX_LEN)
    m_ctx = np.block([[cp, -sp], [sp, cp]])
    mpos = np.stack([m_lat, m_ctx])
    cr, sr = _dft_cs(GRID_H)
    mrow = np.concatenate([cr, -sr], axis=1)
    return (jnp.asarray(dch, BF16), jnp.asarray(mpos, BF16), jnp.asarray(mrow, BF16))


def _fourier_in_kernel(x_ref, mod_ref, g_ref, win_ref, dch_ref, mpos_ref, vr_ref, vi_ref):
    x = x_ref[...]
    h = _rms(x, g_ref[...]) * (1.0 + mod_ref[1:2, :]) + mod_ref[0:1, :]
    u = jnp.dot(h.astype(BF16), win_ref[...], preferred_element_type=F32).astype(BF16)
    parts = [jnp.dot(u[:, g * FGDIM:(g + 1) * FGDIM], dch_ref[...],
                     preferred_element_type=F32) for g in range(FGROUPS)]
    uc = jnp.concatenate([p[:, :FGDIM] for p in parts], axis=1).astype(BF16)
    us = jnp.concatenate([p[:, FGDIM:] for p in parts], axis=1).astype(BF16)
    for half in range(x.shape[0] // TM):
        rows = slice(half * TM, (half + 1) * TM)
        v = jnp.dot(mpos_ref[...], jnp.concatenate([uc[rows], us[rows]], axis=0),
                    preferred_element_type=F32)
        vr_ref[rows, :] = v[:TM]
        vi_ref[rows, :] = v[TM:]


def _fourier_in(x_all, mods, layer, norm_g, w_in, dch, mpos):
    tile = pl.BlockSpec((2 * TM, D), lambda i: (i, 0))
    return pl.pallas_call(
        _fourier_in_kernel,
        grid=(ALL_TILES // 2,),
        in_specs=[tile,
                  pl.BlockSpec((None, 6, D), lambda i: (layer * 8 + _tile_cond(2 * i), 0, 0)),
                  pl.BlockSpec((1, D), lambda i: (0, 0)),
                  pl.BlockSpec((D, D), lambda i: (0, 0)),
                  pl.BlockSpec((FGDIM, 2 * FGDIM), lambda i: (0, 0)),
                  pl.BlockSpec((None, 2 * TM, 2 * TM),
                               lambda i: (jnp.where(2 * i < LAT_TILES, 0, 1), 0, 0))],
        out_specs=[tile, tile],
        out_shape=[jax.ShapeDtypeStruct((NTOK, D), F32)] * 2,
        compiler_params=_cparams(("arbitrary",), VMEM_LIMIT),
        name="fourier_in",
    )(x_all, mods, norm_g.reshape(1, D), w_in.astype(BF16), dch, mpos)


CB = 8


def _fourier_out_lat_kernel(vr_ref, vi_ref, x_ref, mrow_ref, wout_ref, mod_ref, o_ref):
    g1 = mod_ref[2:3, :]
    yf = []
    for c in range(CB):
        st = jnp.concatenate([vr_ref[:, c, :], vi_ref[:, c, :]], axis=0).astype(BF16)
        yf.append(jnp.dot(mrow_ref[...], st, preferred_element_type=F32).astype(BF16))
    y = jnp.dot(jnp.concatenate(yf, axis=0), wout_ref[...], preferred_element_type=F32)
    for c in range(CB):
        o_ref[:, c, :] = x_ref[:, c, :] + g1 * y[c * GRID_H:(c + 1) * GRID_H, :]


def _fourier_out_ctx_kernel(yr_ref, x_ref, wout_ref, mod_ref, o_ref):
    y = jnp.dot(yr_ref[...].astype(BF16), wout_ref[...], preferred_element_type=F32)
    o_ref[...] = x_ref[...] + mod_ref[2:3, :] * y


def _fourier_out(vr, vi, x_all, mods, layer, w_out, mrow):
    wout = w_out.astype(BF16)
    rows = NTOK // GRID_W
    v3 = lambda a: a.reshape(rows, GRID_W, D)
    blk = pl.BlockSpec((GRID_H, CB, D), lambda b, c: (b, c, 0))
    x_new = pl.pallas_call(
        _fourier_out_lat_kernel,
        grid=(BATCH, GRID_W // CB),
        in_specs=[blk, blk, blk,
                  pl.BlockSpec((GRID_H, 2 * GRID_H), lambda b, c: (0, 0)),
                  pl.BlockSpec((D, D), lambda b, c: (0, 0)),
                  pl.BlockSpec((None, 6, D), lambda b, c: (layer * 8 + b, 0, 0))],
        out_specs=blk,
        out_shape=jax.ShapeDtypeStruct((rows, GRID_W, D), F32),
        input_output_aliases={2: 0},
        compiler_params=_cparams(("arbitrary", "arbitrary"), VMEM_LIMIT),
        name="fourier_out_lat",
    )(v3(vr), v3(vi), v3(x_all), mrow, wout, mods).reshape(NTOK, D)
    ctile = pl.BlockSpec((TM, D), lambda i: (LAT_TILES + i, 0))
    return pl.pallas_call(
        _fourier_out_ctx_kernel,
        grid=(NCTX // TM,),
        in_specs=[ctile, ctile,
                  pl.BlockSpec((D, D), lambda i: (0, 0)),
                  pl.BlockSpec((None, 6, D), lambda i: (layer * 8 + 2, 0, 0))],
        out_specs=ctile,
        out_shape=jax.ShapeDtypeStruct((NTOK, D), F32),
        input_output_aliases={1: 0},
        compiler_params=_cparams(("arbitrary",), VMEM_LIMIT),
        name="fourier_out_ctx",
    )(vr, x_new, wout, mods)


def _hgrn_in_kernel(x_ref, mod_ref, g_ref, win_ref, hlb_ref, q_ref, ff_ref, fb_ref, v_ref, gs_ref,
                    *, layer):
    x = x_ref[...]
    h = (_rms(x, g_ref[...]) * (1.0 + mod_ref[1:2, :]) + mod_ref[0:1, :]).astype(BF16)
    raw = [hlb_ref[l] for l in range(hlb_ref.shape[0])]
    mx = functools.reduce(jnp.maximum, raw)
    ex = [jnp.exp(r - mx) for r in raw]
    den = functools.reduce(lambda a, b: a + b, ex)
    soft = [e / den for e in ex]
    lb = functools.reduce(lambda a, b: a + b, soft[:layer + 1]) - soft[0]

    def proj(j):
        return jnp.dot(h, win_ref[:, j * D:(j + 1) * D], preferred_element_type=F32)

    def per_head(ref, val):
        for hd in range(HEADS):
            ref[hd] = val[:, hd * HDIM:(hd + 1) * HDIM]

    q = proj(0)
    per_head(q_ref, q * _sigmoid(q))
    per_head(ff_ref, lb[0:1, :] + (1.0 - lb[0:1, :]) * _sigmoid(proj(1)))
    per_head(fb_ref, lb[1:2, :] + (1.0 - lb[1:2, :]) * _sigmoid(proj(2)))
    per_head(v_ref, proj(3))
    g = proj(4)
    gs_ref[...] = (g * _sigmoid(g)).astype(BF16)


def _hgrn_in(x_all, mods, layer, norm_g, w_in, hlb):
    tile = pl.BlockSpec((TM, D), lambda i: (i, 0))
    depth = hlb.shape[0]
    return pl.pallas_call(
        functools.partial(_hgrn_in_kernel, layer=layer),
        grid=(ALL_TILES,),
        in_specs=[tile,
                  pl.BlockSpec((None, 6, D), lambda i: (layer * 8 + _tile_cond(i), 0, 0)),
                  pl.BlockSpec((1, D), lambda i: (0, 0)),
                  pl.BlockSpec((D, 5 * D), lambda i: (0, 0)),
                  pl.BlockSpec((depth, 2, D), lambda i: (0, 0, 0))],
        out_specs=[pl.BlockSpec((HEADS, TM, HDIM), lambda i: (0, i, 0))] * 4 + [tile],
        out_shape=[jax.ShapeDtypeStruct((HEADS, NTOK, HDIM), F32)] * 4
                  + [jax.ShapeDtypeStruct((NTOK, D), BF16)],
        compiler_params=_cparams(("arbitrary",), VMEM_LIMIT),
        name="hgrn_in",
    )(x_all, mods, norm_g.reshape(1, D), w_in.astype(BF16), hlb)


N_LEVELS = 7
SUB = 8
NGRP = CHUNK // SUB
FINE_LEVELS = 3


def _scan_pair_kernel(qf_ref, ff_ref, vf_ref, qb_ref, fb_ref, vb_ref, of_ref, ob_ref, sf_ref, sb_ref,
                      rf_ref, rb_ref):
    @pl.when(pl.program_id(1) == 0)
    def _():
        sf_ref[...] = jnp.zeros_like(sf_ref)
        sb_ref[...] = jnp.zeros_like(sb_ref)

    _scan_chunk(qf_ref, ff_ref, vf_ref, of_ref, sf_ref, rf_ref, rev=False)
    _scan_chunk(qb_ref, fb_ref, vb_ref, ob_ref, sb_ref, rb_ref, rev=True)


PG = CHUNK // SUB


def _scan_chunk(q_ref, fg_ref, v_ref, o_ref, s_ref, relay_ref, *, rev):
    heads = [slice(h * HDIM, (h + 1) * HDIM) for h in range(HEADS)]

    def score(qs, ks, h):
        return jnp.dot(qs[:, heads[h]], ks[:, heads[h]].T.astype(BF16), preferred_element_type=F32)

    def by_residue(ref):
        return jnp.concatenate(
            [jnp.concatenate([ref[h, pl.ds(r, PG, stride=SUB), :] for r in range(SUB)], axis=0)
             for h in range(HEADS)], axis=1)

    def residue_groups(a):
        return [a[r * PG:(r + 1) * PG, :] for r in range(SUB)]

    q_p = by_residue(q_ref)
    fg_p = by_residue(fg_ref)
    k_p = 1.0 - fg_p
    v_p = by_residue(v_ref).astype(BF16)
    key = lax.broadcasted_iota(jnp.int32, (PG, CHUNK), 1)
    same_a = (key & (PG - 1)) == lax.broadcasted_iota(jnp.int32, (PG, CHUNK), 0)
    key_r = key >> (PG.bit_length() - 1)
    zero_p = jnp.zeros((PG, D), F32)
    sc_p = [[None] * SUB for _ in range(HEADS)]
    qd = q_p.astype(BF16)
    for h in range(HEADS):
        sc = score(qd, k_p, h)
        for r in range(SUB):
            sc_p[h][r] = jnp.where(same_a & (key_r == r), sc[r * PG:(r + 1) * PG, :], 0.0)
    qr_p = residue_groups(fg_p * q_p)
    kr_p = residue_groups(k_p)
    tot_p = residue_groups(fg_p)
    for l in range(FINE_LEVELS):
        bit = 1 << l
        is_far = [((r & bit) == 0) == rev for r in range(SUB)]
        far_groups = [r for r in range(SUB) if is_far[r]]
        qb = jnp.concatenate([qr_p[r] for r in far_groups], axis=0).astype(BF16)
        kb = jnp.concatenate([zero_p if is_far[r] else kr_p[r] for r in range(SUB)], axis=0)
        keep = [same_a & ((key_r >> (l + 1)) == (r >> (l + 1))) for r in far_groups]
        for h in range(HEADS):
            sc = score(qb, kb, h)
            for i, r in enumerate(far_groups):
                sc_p[h][r] = sc_p[h][r] + jnp.where(keep[i], sc[i * PG:(i + 1) * PG, :], 0.0)
        sib_p = [tot_p[r ^ bit] for r in range(SUB)]
        qr_p = [qr_p[r] * sib_p[r] if is_far[r] else qr_p[r] for r in range(SUB)]
        kr_p = [kr_p[r] if is_far[r] else kr_p[r] * sib_p[r] for r in range(SUB)]
        tot_p = [tot_p[r] * sib_p[r] for r in range(SUB)]
    for x, grp in enumerate((qr_p, kr_p, tot_p)):
        for h in range(HEADS):
            for r in range(SUB):
                relay_ref[x, h, pl.ds(r, PG, stride=SUB), :] = grp[r][:, heads[h]]
    qr, kr, tot = (jnp.concatenate([relay_ref[x, h] for h in range(HEADS)], axis=1)
                   for x in range(3))

    def groups(a):
        return [a[b * SUB:(b + 1) * SUB, :] for b in range(NGRP)]

    qr_g, kr_g, tot_g = (groups(a) for a in (qr, kr, tot))
    sc_g = [[None] * NGRP for _ in range(HEADS)]
    lane = lax.broadcasted_iota(jnp.int32, (SUB, CHUNK), 1)
    zero_g = jnp.zeros((SUB, D), F32)
    for l in range(FINE_LEVELS, N_LEVELS):
        bit = 1 << (l - FINE_LEVELS)
        is_far = [((b & bit) == 0) == rev for b in range(NGRP)]
        far_groups = [b for b in range(NGRP) if is_far[b]]
        qb = jnp.concatenate([qr_g[b] for b in far_groups], axis=0).astype(BF16)
        kb = jnp.concatenate([zero_g if is_far[b] else kr_g[b] for b in range(NGRP)], axis=0)
        span = 2 << l
        keep = [None if span == CHUNK else
                (lane >= b * SUB // span * span) & (lane < b * SUB // span * span + span)
                for b in far_groups]
        for h in range(HEADS):
            sc = score(qb, kb, h)
            for i, b in enumerate(far_groups):
                piece = sc[i * SUB:(i + 1) * SUB, :]
                if keep[i] is not None:
                    piece = jnp.where(keep[i], piece, 0.0)
                sc_g[h][b] = piece if sc_g[h][b] is None else sc_g[h][b] + piece
        sib_g = [tot_g[b ^ bit] for b in range(NGRP)]
        qr_g = [qr_g[b] * sib_g[b] if is_far[b] else qr_g[b] for b in range(NGRP)]
        kr_g = [kr_g[b] if is_far[b] else kr_g[b] * sib_g[b] for b in range(NGRP)]
        if l < N_LEVELS - 1:
            tot_g = [tot_g[b] * sib_g[b] for b in range(NGRP)]
        else:
            tot_row = tot_g[0][0:1, :] * sib_g[0][0:1, :]
    qin = jnp.concatenate(qr_g, axis=0).astype(BF16)
    kst = jnp.concatenate(kr_g, axis=0).astype(BF16)
    vb = jnp.concatenate([v_ref[h] for h in range(HEADS)], axis=1).astype(BF16)
    zero_s = jnp.zeros((SUB, CHUNK), F32)
    for h in range(HEADS):
        sl = heads[h]
        st = s_ref[h]
        sc = jnp.concatenate([zero_s if g is None else g for g in sc_g[h]], axis=0).astype(BF16)
        o_rows = (jnp.dot(sc, vb[:, sl], preferred_element_type=F32)
                  + jnp.dot(qin[:, sl], st.T.astype(BF16), preferred_element_type=F32))
        o_res = jnp.dot(jnp.concatenate(sc_p[h], axis=0).astype(BF16), v_p[:, sl],
                        preferred_element_type=F32)
        for r in range(SUB):
            relay_ref[3, h, pl.ds(r, PG, stride=SUB), :] = o_res[r * PG:(r + 1) * PG, :]
        o_ref[:, sl] = (o_rows + relay_ref[3, h]).astype(BF16)
        s_ref[h] = st * tot_row[:, sl] + lax.dot_general(vb[:, sl], kst[:, sl], _TN,
                                                         preferred_element_type=F32)


LAT_CHUNKS = SEQ // CHUNK
CTX_CHUNKS = CTX_LEN // CHUNK
SCAN_STEPS = CTX_CHUNKS + LAT_CHUNKS


def _scan(q, f_fw, f_bw, v):
    def idx_fw(b, s):
        return (jnp.where(s < CTX_CHUNKS, NLAT // CHUNK + CTX_CHUNKS * b + s,
                          LAT_CHUNKS * b + (s - CTX_CHUNKS)), 0)

    def idx_bw(b, s):
        return (jnp.where(s < CTX_CHUNKS, NLAT // CHUNK + CTX_CHUNKS * b + (CTX_CHUNKS - 1 - s),
                          LAT_CHUNKS * b + (SCAN_STEPS - 1 - s)), 0)
    fw = pl.BlockSpec((CHUNK, D), idx_fw)
    bw = pl.BlockSpec((CHUNK, D), idx_bw)
    fw_in = pl.BlockSpec((HEADS, CHUNK, HDIM), lambda b, s: (0, idx_fw(b, s)[0], 0))
    bw_in = pl.BlockSpec((HEADS, CHUNK, HDIM), lambda b, s: (0, idx_bw(b, s)[0], 0))
    state = pltpu.VMEM((HEADS, HDIM, HDIM), F32)
    relay = pltpu.VMEM((4, HEADS, CHUNK, HDIM), F32)
    return pl.pallas_call(
        _scan_pair_kernel,
        grid=(BATCH, SCAN_STEPS),
        in_specs=[fw_in, fw_in, fw_in, bw_in, bw_in, bw_in],
        out_specs=[fw, bw],
        out_shape=[jax.ShapeDtypeStruct((NTOK, D), BF16)] * 2,
        scratch_shapes=[state, state, relay, relay],
        compiler_params=_cparams(("arbitrary", "arbitrary"), VMEM_LIMIT),
        name="scan",
    )(q, f_fw, v, q, f_bw, v)


def _hgrn_out_kernel(of_ref, ob_ref, gs_ref, x_ref, ng_ref, wout_ref, mod_ref, o_ref):
    o = of_ref[...].astype(F32) + ob_ref[...].astype(F32)
    parts = []
    for h in range(HEADS):
        oh = o[:, h * HDIM:(h + 1) * HDIM]
        parts.append(oh * lax.rsqrt(jnp.mean(oh * oh, axis=-1, keepdims=True) + NORM_EPS))
    on = jnp.concatenate(parts, axis=1) * ng_ref[...]
    y = jnp.dot((on * gs_ref[...].astype(F32)).astype(BF16), wout_ref[...],
                preferred_element_type=F32)
    o_ref[...] = x_ref[...] + mod_ref[2:3, :] * y


def _hgrn_out(o_fw, o_bw, gs, x_all, mods, layer, norm_g, w_out, n_tiles):
    tile = pl.BlockSpec((2 * TM, D), lambda i: (i, 0))
    return pl.pallas_call(
        _hgrn_out_kernel,
        grid=(n_tiles // 2,),
        in_specs=[tile, tile, tile, tile,
                  pl.BlockSpec((1, D), lambda i: (0, 0)),
                  pl.BlockSpec((D, D), lambda i: (0, 0)),
                  pl.BlockSpec((None, 6, D), lambda i: (layer * 8 + _tile_cond(2 * i), 0, 0))],
        out_specs=tile,
        out_shape=jax.ShapeDtypeStruct((n_tiles * TM, D), F32),
        compiler_params=_cparams(("arbitrary",), VMEM_LIMIT),
        name="hgrn_out",
    )(o_fw, o_bw, gs, x_all, norm_g.reshape(1, D), w_out.astype(BF16), mods)


def _to_row_tiled(ref, val):
    for j in range(RT):
        ref[pl.ds(j, val.shape[0], stride=RT), :] = val[:, j * LANES:(j + 1) * LANES]


def _from_row_tiled(ref, n, base=0):
    return [ref[pl.ds(base + j, n, stride=RT), :] for j in range(RT)]


def _route_kernel(x_ref, mod_ref, g_ref, wrt_ref, brt_ref, h_ref, e_ref, rank_ref, gate_ref, cnt_ref):
    @pl.when(pl.program_id(0) == 0)
    def _():
        cnt_ref[...] = jnp.zeros_like(cnt_ref)

    h = _rms(x_ref[...], g_ref[...]) * (1.0 + mod_ref[4:5, :]) + mod_ref[3:4, :]
    _to_row_tiled(h_ref, h)
    w = wrt_ref[...]
    w_hi = w.astype(BF16)
    w_lo = (w - w_hi.astype(F32)).astype(BF16)
    h_hi = h.astype(BF16)
    h_lo = (h - h_hi.astype(F32)).astype(BF16)
    part = lax.dot_general(jnp.concatenate([w_hi, w_lo], axis=0), h_hi, _NT,
                           preferred_element_type=F32)
    logits = (part[:N_EXPERTS] + part[N_EXPERTS:]
              + lax.dot_general(w_hi, h_lo, _NT, preferred_element_type=F32)
              + brt_ref[:, 0:1])
    row = lax.broadcasted_iota(jnp.int32, (N_EXPERTS, TM), 0).astype(F32)
    vals = logits
    sel = jnp.zeros((N_EXPERTS, TM), F32)
    tops, idxs = [], []
    for _ in range(TOP_K):
        m = jnp.max(vals, axis=0, keepdims=True)
        idx = jnp.min(jnp.where(vals == m, row, float(N_EXPERTS)), axis=0, keepdims=True)
        hit = row == idx
        vals = jnp.where(hit, -jnp.inf, vals)
        sel = jnp.where(hit, 1.0, sel)
        tops.append(m)
        idxs.append(idx)
    ex = [jnp.exp(m - tops[0]) for m in tops]
    den = ex[0] + ex[1] + ex[2] + ex[3]
    r = lax.broadcasted_iota(jnp.int32, (TM, TM), 0)
    c = lax.broadcasted_iota(jnp.int32, (TM, TM), 1)
    before = jnp.where(r < c, 1.0, 0.0).astype(BF16)
    pref = jnp.dot(sel.astype(BF16), before, preferred_element_type=F32) + cnt_ref[:, 0:1]
    slot = lax.broadcasted_iota(jnp.int32, (8, TM), 0)
    e_out = jnp.zeros((8, TM), F32)
    rank_out = jnp.zeros((8, TM), F32)
    gate_out = jnp.zeros((8, TM), F32)
    for kk in range(TOP_K):
        rank = jnp.sum(jnp.where(row == idxs[kk], pref, 0.0), axis=0, keepdims=True)
        e_out = jnp.where(slot == kk, idxs[kk], e_out)
        rank_out = jnp.where(slot == kk, rank, rank_out)
        gate_out = jnp.where(slot == kk, ex[kk] / den, gate_out)
    e_ref[...] = e_out.astype(jnp.int32)
    rank_ref[...] = rank_out.astype(jnp.int32)
    gate_ref[...] = gate_out
    cnt_ref[...] += jnp.sum(sel, axis=1, keepdims=True)


def _route(x_all, mods, layer, norm_g, w_r, b_r, n_tiles):
    tile = pl.BlockSpec((TM, D), lambda i: (i, 0))
    small = pl.BlockSpec((None, 8, TM), lambda i: (i, 0, 0))
    n = n_tiles * TM
    return pl.pallas_call(
        _route_kernel,
        grid=(n_tiles,),
        in_specs=[tile,
                  pl.BlockSpec((None, 6, D), lambda i: (layer * 8 + _tile_cond(i), 0, 0)),
                  pl.BlockSpec((1, D), lambda i: (0, 0)),
                  pl.BlockSpec((N_EXPERTS, D), lambda i: (0, 0)),
                  pl.BlockSpec((N_EXPERTS, LANES), lambda i: (0, 0))],
        out_specs=[pl.BlockSpec((TM * RT, LANES), lambda i: (i, 0)), small, small, small,
                   pl.BlockSpec((N_EXPERTS, LANES), lambda i: (0, 0))],
        out_shape=[jax.ShapeDtypeStruct((n * RT, LANES), F32),
                   jax.ShapeDtypeStruct((n_tiles, 8, TM), jnp.int32),
                   jax.ShapeDtypeStruct((n_tiles, 8, TM), jnp.int32),
                   jax.ShapeDtypeStruct((n_tiles, 8, TM), F32),
                   jax.ShapeDtypeStruct((N_EXPERTS, LANES), F32)],
        compiler_params=_cparams(("arbitrary",), VMEM_LIMIT),
        name="route",
    )(x_all, mods, norm_g.reshape(1, D), w_r.T, jnp.broadcast_to(b_r[:, None], (N_EXPERTS, LANES)))


def _dest_kernel(ps_ref, e_ref, rank_ref, d_ref):
    e = e_ref[...]
    acc = rank_ref[...]
    for j in range(N_EXPERTS):
        acc = acc + jnp.where(e == j, ps_ref[j], 0)
    d_ref[...] = acc


def _dest(pad_start, e, rank):
    full = pl.BlockSpec(e.shape, lambda i, ps: (0, 0, 0))
    return pl.pallas_call(
        _dest_kernel,
        grid_spec=pltpu.PrefetchScalarGridSpec(num_scalar_prefetch=1, grid=(1,),
                                               in_specs=[full, full], out_specs=full),
        out_shape=jax.ShapeDtypeStruct(e.shape, jnp.int32),
        name="dest",
    )(pad_start, e, rank)


PAD_PIECES = tuple(1 << p for p in reversed(range(BM.bit_length() - 1)))


def _zero_fill(lo_ref, hi_ref, xb_hbm, zeros, sem, n_blocks, wait):
    def go(rows, row0):
        cp = pltpu.make_async_copy(zeros.at[pl.ds(0, rows * RT), :],
                                   xb_hbm.at[pl.ds(row0 * RT, rows * RT), :], sem.at[1])
        cp.wait() if wait else cp.start()

    def per_expert(e, carry):
        row = lo_ref[e]
        n = hi_ref[e] - row
        for piece in PAD_PIECES:
            @pl.when((n & piece) != 0)
            def _():
                go(piece, row)
            row = row + (n & piece)
        return carry
    lax.fori_loop(0, N_EXPERTS, per_expert, 0)

    def per_block(b, carry):
        go(BM, b * BM)
        return carry
    lax.fori_loop(hi_ref[N_EXPERTS - 1] // BM, n_blocks, per_block, 0)


def _dispatch_kernel(lo_ref, hi_ref, dest_ref, h_ref, xb_hbm, zeros, sem, *, n_blocks):
    i = pl.program_id(0)

    @pl.when(i == 0)
    def _():
        zeros[...] = jnp.zeros_like(zeros)
        _zero_fill(lo_ref, hi_ref, xb_hbm, zeros, sem, n_blocks, wait=False)

    def body(t, carry):
        src = h_ref.at[pl.ds(t * RT, RT), :]
        for kk in range(TOP_K):
            d = dest_ref[0, 0, kk * TM + t]
            pltpu.make_async_copy(src, xb_hbm.at[pl.ds(d * RT, RT), :],
                                  sem.at[0]).start(priority=kk % 2)
        return carry
    lax.fori_loop(0, TM, body, 0, unroll=4)
    for kk in range(TOP_K):
        pltpu.make_async_copy(h_ref, xb_hbm.at[pl.ds(0, TM * RT), :], sem.at[0]).wait()

    @pl.when(i == pl.num_programs(0) - 1)
    def _():
        _zero_fill(lo_ref, hi_ref, xb_hbm, zeros, sem, n_blocks, wait=True)


def _dispatch(h_rt, dest, pad_lo, pad_hi, n_blocks):
    n_tiles = dest.shape[0]
    dest3 = dest[:, :TOP_K, :].reshape(n_tiles, 1, TOP_K * TM)
    grid_spec = pltpu.PrefetchScalarGridSpec(
        num_scalar_prefetch=2,
        grid=(n_tiles,),
        in_specs=[pl.BlockSpec((1, 1, TOP_K * TM), lambda i, lo, hi: (i, 0, 0),
                               memory_space=pltpu.SMEM),
                  pl.BlockSpec((TM * RT, LANES), lambda i, lo, hi: (i, 0))],
        out_specs=pl.BlockSpec(memory_space=pl.ANY),
        scratch_shapes=[pltpu.VMEM((BM * RT, LANES), F32),
                        pltpu.SemaphoreType.DMA((2,))],
    )
    xb = pl.pallas_call(
        functools.partial(_dispatch_kernel, n_blocks=n_blocks),
        grid_spec=grid_spec,
        out_shape=jax.ShapeDtypeStruct((n_blocks * BM * RT, LANES), F32),
        compiler_params=_cparams(("arbitrary",), VMEM_LIMIT, disable_bounds_checks=True),
        name="dispatch",
    )(pad_lo, pad_hi, dest3, h_rt)
    return xb, dest3


ROW_STEPS = tuple(range(BM // 4, BM + 1, BM // 4))


def _expert_kernel(bexp_ref, nused_ref, first_ref, wslot_ref, next_ref, valid_ref, xb_ref, w1_hbm, b1_ref,
                   w2_hbm, b2_ref, y_ref, w1s, w2s, w1b, w2b, sem, *, layer):
    i = pl.program_id(0)
    del nused_ref

    def weight_copies(e, slot):
        return (pltpu.make_async_copy(w1_hbm.at[layer, e], w1s.at[slot], sem.at[0, slot]),
                pltpu.make_async_copy(w2_hbm.at[layer, e], w2s.at[slot], sem.at[1, slot]))

    @pl.when(i == 0)
    def _():
        for cp in weight_copies(bexp_ref[0], 0):
            cp.start()

    @pl.when(first_ref[i] == 1)
    def _():
        slot = wslot_ref[i]
        for cp in weight_copies(bexp_ref[i], slot):
            cp.wait()
        w1b[...] = w1s[slot].astype(BF16)
        w2b[...] = w2s[slot].astype(BF16)

        @pl.when(next_ref[i] >= 0)
        def _():
            for cp in weight_copies(next_ref[i], 1 - slot):
                cp.start()

    valid = valid_ref[i]
    for lo, rows in zip((0,) + ROW_STEPS[:-1], ROW_STEPS):
        @pl.when(jnp.logical_and(valid > lo, valid <= rows))
        def _():
            x = jnp.concatenate(_from_row_tiled(xb_ref, rows), axis=1).astype(BF16)
            u = jnp.dot(x, w1b[...], preferred_element_type=F32) + b1_ref[...]
            glu = jnp.minimum(u[:, :D_FF], SWIGLU_LIMIT)
            lin = jnp.clip(u[:, D_FF:], -SWIGLU_LIMIT, SWIGLU_LIMIT)
            act = glu * _sigmoid(SWIGLU_ALPHA * glu) * (lin + 1.0)
            y = jnp.dot(act.astype(BF16), w2b[...], preferred_element_type=F32) + b2_ref[...]
            _to_row_tiled(y_ref.at[pl.ds(0, rows * RT), :], y)
            if rows < BM:
                y_ref[pl.ds(rows * RT, (BM - rows) * RT), :] = jnp.zeros(((BM - rows) * RT, LANES), F32)

    @pl.when(valid == 0)
    def _():
        y_ref[...] = jnp.zeros_like(y_ref)


def _experts(xb, block_exp, n_used, counts, pad_lo, layer, w1, b1, w2, b2):
    n_blocks = xb.shape[0] // (BM * RT)
    used = jnp.arange(n_blocks, dtype=jnp.int32) < n_used[0]
    changed = jnp.concatenate([jnp.ones((1,), bool), block_exp[1:] != block_exp[:-1]])
    first = jnp.logical_and(used, changed).astype(jnp.int32)
    wslot = (jnp.cumsum(first) - 1) % 2
    ar = jnp.arange(N_EXPERTS, dtype=jnp.int32)
    later = jnp.min(jnp.where((ar[None, :] > ar[:, None]) & (counts > 0)[None, :], ar[None, :],
                              N_EXPERTS), axis=1)
    of_block = block_exp[:, None] == ar[None, :]
    next_exp = jnp.sum(jnp.where(of_block, jnp.where(later == N_EXPERTS, -1, later)[None, :], 0), axis=1)
    starts = jnp.arange(n_blocks, dtype=jnp.int32) * BM
    valid = jnp.clip(jnp.sum(jnp.where(of_block, pad_lo[None, :], 0), axis=1) - starts, 0, BM)
    smap = lambda f: (lambda i, be, nu, fi, ws, nx, va: f(i, be, nu))
    grid_spec = pltpu.PrefetchScalarGridSpec(
        num_scalar_prefetch=6,
        grid=(n_blocks,),
        in_specs=[
            pl.BlockSpec((BM * RT, LANES), smap(lambda i, be, nu: (jnp.minimum(i, nu[0] - 1), 0))),
            pl.BlockSpec(memory_space=pl.ANY),
            pl.BlockSpec((None, 1, 2 * D_FF), smap(lambda i, be, nu: (layer * N_EXPERTS + be[i], 0, 0))),
            pl.BlockSpec(memory_space=pl.ANY),
            pl.BlockSpec((None, 1, D), smap(lambda i, be, nu: (layer * N_EXPERTS + be[i], 0, 0))),
        ],
        out_specs=pl.BlockSpec((BM * RT, LANES), smap(lambda i, be, nu: (i, 0))),
        scratch_shapes=[pltpu.VMEM((2, D, 2 * D_FF), F32),
                        pltpu.VMEM((2, D_FF, D), F32),
                        pltpu.VMEM((D, 2 * D_FF), BF16),
                        pltpu.VMEM((D_FF, D), BF16),
                        pltpu.SemaphoreType.DMA((2, 2))],
    )
    return pl.pallas_call(
        functools.partial(_expert_kernel, layer=layer),
        grid_spec=grid_spec,
        out_shape=jax.ShapeDtypeStruct(xb.shape, F32),
        compiler_params=_cparams(("arbitrary",), VMEM_LIMIT),
        name="experts",
    )(block_exp, n_used, first, wslot.astype(jnp.int32), next_exp.astype(jnp.int32),
      valid.astype(jnp.int32), xb, w1,
      b1.reshape(-1, 1, 2 * D_FF), w2, b2.reshape(-1, 1, D))


def _combine_kernel(dest_ref, dest_next_ref, y_hbm, gate_ref, x_ref, mod_ref, fg_ref, o_ref, buf, sem,
                    *, final_norm):
    i = pl.program_id(0)
    n = pl.num_programs(0)
    slot = i % 2
    rows = TOP_K * TM

    def gather(dest, to_slot):
        def body(p, carry):
            for half in range(2):
                r = 2 * p + half
                pltpu.make_async_copy(y_hbm.at[pl.ds(dest[0, 0, r] * RT, RT), :],
                                      buf.at[to_slot, pl.ds(r * RT, RT), :],
                                      sem.at[to_slot]).start(priority=half)
            return carry
        lax.fori_loop(0, rows // 2, body, 0, unroll=8)

    @pl.when(i == 0)
    def _():
        gather(dest_ref, 0)

    @pl.when(i + 1 < n)
    def _():
        gather(dest_next_ref, 1 - slot)

    pltpu.make_async_copy(y_hbm.at[pl.ds(0, rows * RT), :], buf.at[slot], sem.at[slot]).wait()
    eye = (lax.broadcasted_iota(jnp.int32, (8, 8), 0)
           == lax.broadcasted_iota(jnp.int32, (8, 8), 1)).astype(F32)
    gate = lax.dot_general(gate_ref[...], eye, _TN, precision=HIGHEST,
                           preferred_element_type=F32)
    cur = buf.at[slot]
    outs = []
    for j in range(RT):
        acc = None
        for kk in range(TOP_K):
            piece = gate[:, kk:kk + 1] * cur[pl.ds(kk * TM * RT + j, TM, stride=RT), :]
            acc = piece if acc is None else acc + piece
        outs.append(acc)
    out = x_ref[...] + mod_ref[5:6, :] * jnp.concatenate(outs, axis=1)
    if final_norm:
        out = _rms(out, fg_ref[...])
    o_ref[...] = out


def _combine(yb, dest3, gate, x_all, mods, layer, final_g, n_tiles, final_norm):
    last = n_tiles - 1
    tile = pl.BlockSpec((TM, D), lambda i: (i, 0))
    return pl.pallas_call(
        functools.partial(_combine_kernel, final_norm=final_norm),
        grid=(n_tiles,),
        in_specs=[pl.BlockSpec((1, 1, TOP_K * TM), lambda i: (i, 0, 0), memory_space=pltpu.SMEM),
                  pl.BlockSpec((1, 1, TOP_K * TM), lambda i: (jnp.minimum(i + 1, last), 0, 0),
                               memory_space=pltpu.SMEM),
                  pl.BlockSpec(memory_space=pl.ANY),
                  pl.BlockSpec((None, 8, TM), lambda i: (i, 0, 0)),
                  tile,
                  pl.BlockSpec((None, 6, D), lambda i: (layer * 8 + _tile_cond(i), 0, 0)),
                  pl.BlockSpec((1, D), lambda i: (0, 0))],
        out_specs=tile,
        out_shape=jax.ShapeDtypeStruct((n_tiles * TM, D), F32),
        scratch_shapes=[pltpu.VMEM((2, TOP_K * TM * RT, LANES), F32),
                        pltpu.SemaphoreType.DMA((2,))],
        compiler_params=_cparams(("arbitrary",), VMEM_LIMIT, disable_bounds_checks=True),
        name="combine",
    )(dest3, dest3, yb, gate, x_all, mods, final_g.reshape(1, D))


def _moe(x_all, mods, layer, norm_g, w_r, b_r, w1, b1, w2, b2, final_g, n_tiles, final_norm):
    n = n_tiles * TM
    h_rt, e, rank, gate, cnt = _route(x_all, mods, layer, norm_g, w_r, b_r, n_tiles)
    counts = cnt[:, 0].astype(jnp.int32)
    padded = (counts + BM - 1) // BM * BM
    pad_end = jnp.cumsum(padded)
    pad_start = pad_end - padded
    n_blocks = -(-(n * TOP_K) // BM) + N_EXPERTS
    starts = jnp.arange(n_blocks, dtype=jnp.int32) * BM
    block_exp = jnp.minimum(jnp.sum((pad_end[None, :] <= starts[:, None]).astype(jnp.int32), axis=1),
                            N_EXPERTS - 1)
    n_used = pad_end[-1:] // BM
    dest = _dest(pad_start, e, rank)
    xb, dest3 = _dispatch(h_rt, dest, pad_start + counts, pad_end, n_blocks)
    yb = _experts(xb, block_exp, n_used, counts, pad_start + counts, layer, w1, b1, w2, b2)
    return _combine(yb, dest3, gate, x_all, mods, layer, final_g, n_tiles, final_norm)


def kernel(x, c, ctx, c_ctx, mod_w, mod_b, norm1_g, norm2_g, fourier_w_in, fourier_w_out,
           hgrn_w_in, hgrn_lower_bounds, hgrn_norm_g, hgrn_w_out, router_w, router_b,
           expert_w1, expert_b1, expert_w2, expert_b2, final_norm_g):
    assert x.shape == (BATCH, SEQ, D) and ctx.shape == (BATCH, CTX_LEN, D)
    cond8 = jnp.zeros((8, D), F32).at[:BATCH].set(c).at[BATCH].set(c_ctx)
    mods = _adaln(cond8, mod_w, mod_b)
    x_all = jnp.concatenate([x.reshape(NLAT, D), ctx.reshape(NCTX, D)], axis=0)
    dch, mpos, mrow = _fourier_constants()
    experts = (expert_w1, expert_b1, expert_w2, expert_b2)

    vr, vi = _fourier_in(x_all, mods, 0, norm1_g[0], fourier_w_in[0], dch, mpos)
    x_all = _fourier_out(vr, vi, x_all, mods, 0, fourier_w_out[0], mrow)
    x_all = _moe(x_all, mods, 0, norm2_g[0], router_w[0], router_b[0], *experts,
                 final_norm_g, ALL_TILES, False)

    q, ff, fb, v, gs = _hgrn_in(x_all, mods, 1, norm1_g[1], hgrn_w_in[0], hgrn_lower_bounds)
    o_fw, o_bw = _scan(q, ff, fb, v)
    x_lat = _hgrn_out(o_fw, o_bw, gs, x_all, mods, 1, hgrn_norm_g[0], hgrn_w_out[0], LAT_TILES)
    out = _moe(x_lat, mods, 1, norm2_g[1], router_w[1], router_b[1], *experts,
               final_norm_g, LAT_TILES, True)
    return out.reshape(BATCH, SEQ, D)
```
